```python
import math
import jax, jax.numpy as jnp
from jax import lax
import numpy as np

D_MODEL = 2048
BATCH = 4
SEQ = 4096
DEPTH = 2

HEAD_DIM = 128
Q_BLOCK = 128
ROPE_THETA = 10000.0
FOX_HEADS = 8
FOX_WIDTH = FOX_HEADS * HEAD_DIM
FORGET_BIAS_INIT = 2.0
SWA_Q_HEADS = 8
SWA_KV_HEADS = 2
SWA_WINDOW = 128
SWA_Q_WIDTH = SWA_Q_HEADS * HEAD_DIM
SWA_KV_WIDTH = SWA_KV_HEADS * HEAD_DIM
EVEN_MIX_WIDTH = FOX_WIDTH + SWA_Q_WIDTH
MLA_HEADS = 16
MLA_Q_RANK = 512
MLA_KV_RANK = 512
MLA_NOPE_DIM = 128
MLA_ROPE_DIM = 64
MLA_V_DIM = 128
D_FF_DENSE = 5632
N_EXPERTS = 8
TOP_K = 2
D_FF_EXPERT = 7168
MOE_BLOCK = 128
LN_EPS = 1e-5
RMS_EPS = 1e-6
DEEPNORM_ALPHA = (2 * DEPTH) ** 0.25
DEEPNORM_BETA = (8 * DEPTH) ** -0.25

kernel_name = 'hybrid_fox_swa_mla_moe_deepnorm'


def _split(t, sizes):
    return jnp.split(t, [int(v) for v in np.cumsum(sizes)[:-1]], axis=-1)


def layer_norm(x, g, b):
    xf = x.astype(jnp.float32)
    mu = jnp.mean(xf, axis=-1, keepdims=True)
    var = jnp.mean(jnp.square(xf - mu), axis=-1, keepdims=True)
    return ((xf - mu) * lax.rsqrt(var + LN_EPS) * g.astype(jnp.float32) + b.astype(jnp.float32)).astype(x.dtype)


def rms_norm(x, g):
    xf = x.astype(jnp.float32)
    ms = jnp.mean(jnp.square(xf), axis=-1, keepdims=True)
    return (xf * lax.rsqrt(ms + RMS_EPS) * g.astype(jnp.float32)).astype(x.dtype)


def rope(x, positions):
    d = x.shape[-1]
    half = d // 2
    inv_freq = ROPE_THETA ** (-2.0 * jnp.arange(half, dtype=jnp.float32) / d)
    ang = positions.astype(jnp.float32)[..., None] * inv_freq
    cos = jnp.cos(ang)[:, :, None, :]
    sin = jnp.sin(ang)[:, :, None, :]
    xf = x.astype(jnp.float32)
    x1, x2 = xf[..., :half], xf[..., half:]
    return jnp.concatenate([x1 * cos - x2 * sin, x2 * cos + x1 * sin], axis=-1).astype(x.dtype)


def causal_block_attention(q, k, v, scale, cum_log_f=None):
    B, S, H, _ = q.shape
    n_blocks = S // Q_BLOCK
    kpos = jnp.arange(S)
    if cum_log_f is not None:
        c_keys = jnp.transpose(cum_log_f, (0, 2, 1))[:, :, None, :]

    def one_block(i):
        start = i * Q_BLOCK
        qb = lax.dynamic_slice_in_dim(q, start, Q_BLOCK, axis=1)
        s = jnp.einsum('bqhd,bkhd->bhqk', qb, k).astype(jnp.float32) * scale
        if cum_log_f is not None:
            cq = lax.dynamic_slice_in_dim(cum_log_f, start, Q_BLOCK, axis=1)
            s = s + (jnp.transpose(cq, (0, 2, 1))[..., None] - c_keys)
        qpos = start + jnp.arange(Q_BLOCK)
        causal = qpos[:, None] >= kpos[None, :]
        s = jnp.where(causal, s, -jnp.inf)
        p = jax.nn.softmax(s, axis=-1)
        return jnp.einsum('bhqk,bkhd->bqhd', p.astype(v.dtype), v)

    out = lax.map(one_block, jnp.arange(n_blocks))
    return jnp.transpose(out, (1, 0, 2, 3, 4)).reshape(B, S, H, v.shape[-1])


def sliding_window_sink_attention(q, k, v, sinks, scale):
    B, S, Hq, d = q.shape
    Hkv = k.shape[2]
    G = Hq // Hkv
    W = SWA_WINDOW
    nb = S // W
    qb = q.reshape(B, nb, W, Hkv, G, d)

    def band(t):
        tb = t.reshape(B, nb, W, Hkv, d)
        prev = jnp.concatenate([jnp.zeros_like(tb[:, :1]), tb[:, :-1]], axis=1)
        return jnp.concatenate([prev, tb], axis=2)

    kb, vb = band(k), band(v)
    s = jnp.einsum('bnqkgd,bnjkd->bnkgqj', qb, kb).astype(jnp.float32) * scale
    qi = jnp.arange(W)[:, None]
    ji = jnp.arange(2 * W)[None, :]
    in_window = (ji > qi) & (ji <= qi + W)
    key_exists = (jnp.arange(nb) > 0)[:, None, None] | (ji >= W)[None]
    mask = in_window[None] & key_exists
    s = jnp.where(mask[None, :, None, None], s, -jnp.inf)
    sink = sinks.astype(jnp.float32).reshape(1, 1, Hkv, G, 1, 1)
    m = jnp.maximum(jnp.max(s, axis=-1, keepdims=True), sink)
    p = jnp.exp(s - m)
    denom = jnp.sum(p, axis=-1, keepdims=True) + jnp.exp(sink - m)
    o = jnp.einsum('bnkgqj,bnjkd->bnqkgd', (p / denom).astype(v.dtype), vb)
    return o.reshape(B, S, Hq, d)


def even_mixer(x, positions, w_in0, b_forget, sinks_b, w_out0):
    B, S, _ = x.shape
    q_a, k_a, v_a, f_a, q_b, k_b, v_b = _split(
        x @ w_in0, [FOX_WIDTH, FOX_WIDTH, FOX_WIDTH, FOX_HEADS, SWA_Q_WIDTH, SWA_KV_WIDTH, SWA_KV_WIDTH])
    heads = lambda t, h: t.reshape(B, S, h, HEAD_DIM)
    log_f = jax.nn.log_sigmoid((f_a + b_forget).astype(jnp.float32))
    cum_log_f = jnp.cumsum(log_f, axis=1)
    o_a = causal_block_attention(heads(q_a, FOX_HEADS), heads(k_a, FOX_HEADS), heads(v_a, FOX_HEADS),
                                 HEAD_DIM ** -0.5, cum_log_f)
    qr = rope(heads(q_b, SWA_Q_HEADS), positions)
    kr = rope(heads(k_b, SWA_KV_HEADS), positions)
    o_b = sliding_window_sink_attention(qr, kr, heads(v_b, SWA_KV_HEADS), sinks_b, HEAD_DIM ** -0.5)
    o = jnp.concatenate([o_a.reshape(B, S, FOX_WIDTH), o_b.reshape(B, S, SWA_Q_WIDTH)], axis=-1)
    return o @ w_out0


def mla_mixer(x, positions, w_in1, q_norm_g, w_uq, kv_norm_g, w_ukv, w_out1):
    B, S, _ = x.shape
    c_q, c_kv, k_pe = _split(x @ w_in1, [MLA_Q_RANK, MLA_KV_RANK, MLA_ROPE_DIM])
    q = (rms_norm(c_q, q_norm_g) @ w_uq).reshape(B, S, MLA_HEADS, MLA_NOPE_DIM + MLA_ROPE_DIM)
    q_nope, q_pe = q[..., :MLA_NOPE_DIM], rope(q[..., MLA_NOPE_DIM:], positions)
    k_pe = rope(k_pe[:, :, None, :], positions)
    kv = (rms_norm(c_kv, kv_norm_g) @ w_ukv).reshape(B, S, MLA_HEADS, MLA_NOPE_DIM + MLA_V_DIM)
    k_nope, v = kv[..., :MLA_NOPE_DIM], kv[..., MLA_NOPE_DIM:]
    q_full = jnp.concatenate([q_nope, q_pe], axis=-1)
    k_full = jnp.concatenate([k_nope, jnp.broadcast_to(k_pe, (B, S, MLA_HEADS, MLA_ROPE_DIM))], axis=-1)
    o = causal_block_attention(q_full, k_full, v, (MLA_NOPE_DIM + MLA_ROPE_DIM) ** -0.5)
    return o.reshape(B, S, MLA_HEADS * MLA_V_DIM) @ w_out1


def swiglu(x, w_gate, w_up, w_down):
    return (jax.nn.silu(x @ w_gate) * (x @ w_up)) @ w_down


def moe_swiglu(x, w_router, w_moe_gate, w_moe_up, w_moe_down):
    B, S, D = x.shape
    xt = x.reshape(B * S, D)
    N = xt.shape[0]
    logits = (xt @ w_router).astype(jnp.float32)
    top_val, top_idx = lax.top_k(logits, TOP_K)
    gates = jax.nn.softmax(top_val, axis=-1)
    A = N * TOP_K
    expert_of = top_idx.reshape(A).astype(jnp.int32)
    token_of = jnp.repeat(jnp.arange(N, dtype=jnp.int32), TOP_K)
    gate_of = gates.reshape(A)
    order = jnp.argsort(expert_of)
    e_sorted = expert_of[order]
    counts = jnp.bincount(expert_of, length=N_EXPERTS)
    group_start = jnp.cumsum(counts) - counts
    padded = ((counts + MOE_BLOCK - 1) // MOE_BLOCK) * MOE_BLOCK
    padded_end = jnp.cumsum(padded)
    padded_start = padded_end - padded
    dest = padded_start[e_sorted] + (jnp.arange(A) - group_start[e_sorted])
    n_blocks = -(-A // MOE_BLOCK) + N_EXPERTS
    n_rows = n_blocks * MOE_BLOCK
    row_token = jnp.zeros((n_rows,), jnp.int32).at[dest].set(token_of[order])
    row_gate = jnp.zeros((n_rows,), jnp.float32).at[dest].set(gate_of[order])
    block_start = jnp.arange(n_blocks, dtype=padded_end.dtype) * MOE_BLOCK
    block_expert = jnp.minimum(jnp.searchsorted(padded_end, block_start, side='right'), N_EXPERTS - 1)

    def one_block(args):
        tok, e = args
        return swiglu(xt[tok], w_moe_gate[e], w_moe_up[e], w_moe_down[e])

    y_rows = lax.map(one_block, (row_token.reshape(n_blocks, MOE_BLOCK), block_expert))
    y_rows = y_rows.reshape(n_rows, D) * row_gate[:, None].astype(y_rows.dtype)
    y = jnp.zeros_like(xt).at[row_token].add(y_rows)
    return y.reshape(B, S, D)


def setup_inputs(seed: int = 0) -> dict:
    key = jax.random.key(seed)
    ks = iter(jax.random.split(key, 40))
    f32 = jnp.float32
    nrm = lambda shape, scale: jax.random.normal(next(ks), shape, f32) * scale
    proj = lambda shape, fan_in, g=1.0: nrm(shape, g * fan_in ** -0.5)
    gain = lambda n: 1.0 + nrm((n,), 0.02)
    bias = lambda n: nrm((n,), 0.02)
    beta = DEEPNORM_BETA

    x = nrm((BATCH, SEQ, D_MODEL), 1.0)
    start = jax.random.randint(next(ks), (BATCH, 1), 0, 1024, dtype=jnp.int32)
    positions = start + jnp.arange(SEQ, dtype=jnp.int32)[None, :]
    w_in0 = jnp.concatenate([
        proj((D_MODEL, 2 * FOX_WIDTH), D_MODEL),
        proj((D_MODEL, FOX_WIDTH), D_MODEL, beta),
        proj((D_MODEL, FOX_HEADS), D_MODEL),
        proj((D_MODEL, SWA_Q_WIDTH + SWA_KV_WIDTH), D_MODEL),
        proj((D_MODEL, SWA_KV_WIDTH), D_MODEL, beta)], axis=1)
    b_forget = FORGET_BIAS_INIT + nrm((FOX_HEADS,), 0.5)
    sinks_b = nrm((SWA_Q_HEADS,), 0.5)
    w_out0 = proj((EVEN_MIX_WIDTH, D_MODEL), EVEN_MIX_WIDTH, beta)
    ln0_mix_g, ln0_mix_b = gain(D_MODEL), bias(D_MODEL)
    w_ffn_gate = proj((D_MODEL, D_FF_DENSE), D_MODEL, beta)
    w_ffn_up = proj((D_MODEL, D_FF_DENSE), D_MODEL, beta)
    w_ffn_down = proj((D_FF_DENSE, D_MODEL), D_FF_DENSE, beta)
    ln0_ffn_g, ln0_ffn_b = gain(D_MODEL), bias(D_MODEL)
    w_in1 = proj((D_MODEL, MLA_Q_RANK + MLA_KV_RANK + MLA_ROPE_DIM), D_MODEL)
    q_norm_g = gain(MLA_Q_RANK)
    w_uq = proj((MLA_Q_RANK, MLA_HEADS * (MLA_NOPE_DIM + MLA_ROPE_DIM)), MLA_Q_RANK)
    kv_norm_g = gain(MLA_KV_RANK)
    w_ukv = jnp.concatenate([
        proj((MLA_KV_RANK, MLA_HEADS, MLA_NOPE_DIM), MLA_KV_RANK),
        proj((MLA_KV_RANK, MLA_HEADS, MLA_V_DIM), MLA_KV_RANK, beta)], axis=-1
    ).reshape(MLA_KV_RANK, MLA_HEADS * (MLA_NOPE_DIM + MLA_V_DIM))
    w_out1 = proj((MLA_HEADS * MLA_V_DIM, D_MODEL), MLA_HEADS * MLA_V_DIM, beta)
    ln1_mix_g, ln1_mix_b = gain(D_MODEL), bias(D_MODEL)
    w_router = proj((D_MODEL, N_EXPERTS), D_MODEL)
    w_moe_gate = proj((N_EXPERTS, D_MODEL, D_FF_EXPERT), D_MODEL, beta)
    w_moe_up = proj((N_EXPERTS, D_MODEL, D_FF_EXPERT), D_MODEL, beta)
    w_moe_down = proj((N_EXPERTS, D_FF_EXPERT, D_MODEL), D_FF_EXPERT, beta)
    ln1_ffn_g, ln1_ffn_b = gain(D_MODEL), bias(D_MODEL)
    return {'x': x, 'positions': positions,
            'w_in0': w_in0, 'b_forget': b_forget, 'sinks_b': sinks_b, 'w_out0': w_out0,
            'ln0_mix_g': ln0_mix_g, 'ln0_mix_b': ln0_mix_b,
            'w_ffn_gate': w_ffn_gate, 'w_ffn_up': w_ffn_up, 'w_ffn_down': w_ffn_down,
            'ln0_ffn_g': ln0_ffn_g, 'ln0_ffn_b': ln0_ffn_b,
            'w_in1': w_in1, 'q_norm_g': q_norm_g, 'w_uq': w_uq, 'kv_norm_g': kv_norm_g,
            'w_ukv': w_ukv, 'w_out1': w_out1, 'ln1_mix_g': ln1_mix_g, 'ln1_mix_b': ln1_mix_b,
            'w_router': w_router, 'w_moe_gate': w_moe_gate, 'w_moe_up': w_moe_up,
            'w_moe_down': w_moe_down, 'ln1_ffn_g': ln1_ffn_g, 'ln1_ffn_b': ln1_ffn_b}


def reference(x, positions, w_in0, b_forget, sinks_b, w_out0, ln0_mix_g, ln0_mix_b,
              w_ffn_gate, w_ffn_up, w_ffn_down, ln0_ffn_g, ln0_ffn_b,
              w_in1, q_norm_g, w_uq, kv_norm_g, w_ukv, w_out1, ln1_mix_g, ln1_mix_b,
              w_router, w_moe_gate, w_moe_up, w_moe_down, ln1_ffn_g, ln1_ffn_b):
    h = x
    for layer in range(DEPTH):
        if layer % 2 == 0:
            h = layer_norm(DEEPNORM_ALPHA * h + even_mixer(h, positions, w_in0, b_forget, sinks_b, w_out0),
                           ln0_mix_g, ln0_mix_b)
            h = layer_norm(DEEPNORM_ALPHA * h + swiglu(h, w_ffn_gate, w_ffn_up, w_ffn_down),
                           ln0_ffn_g, ln0_ffn_b)
        else:
            h = layer_norm(DEEPNORM_ALPHA * h + mla_mixer(h, positions, w_in1, q_norm_g, w_uq,
                                                          kv_norm_g, w_ukv, w_out1),
                           ln1_mix_g, ln1_mix_b)
            h = layer_norm(DEEPNORM_ALPHA * h + moe_swiglu(h, w_router, w_moe_gate, w_moe_up, w_moe_down),
                           ln1_ffn_g, ln1_ffn_b)
    return h
```

```python
import functools

import jax
import jax.numpy as jnp
from jax import lax
from jax.experimental import pallas as pl
from jax.experimental.pallas import tpu as pltpu

F32 = jnp.float32
BF16 = jnp.bfloat16

LANE = 128
HEAD_DIM = 128
SWA_WINDOW = 128
MLA_NOPE_DIM = 128
MLA_ROPE_DIM = 64
MLA_V_DIM = 128
MLA_QK_PAD = 256
ROPE_THETA = 10000.0
TOP_K = 2
LN_EPS = 1e-5
RMS_EPS = 1e-6
DEPTH = 2
DEEPNORM_ALPHA = (2 * DEPTH) ** 0.25

NT_DIMS = (((1,), (1,)), ((), ()))


def _tile(n, pref):
    if n <= pref:
        return n
    t = (pref // LANE) * LANE
    while n % t:
        t -= LANE
    return t


def _cparams(*sem):
    return pltpu.CompilerParams(dimension_semantics=sem)


def _dot(a, b):
    return jnp.dot(a, b, preferred_element_type=F32)


def _layer_norm(y, g, b):
    mu = jnp.mean(y, axis=-1, keepdims=True)
    d = y - mu
    var = jnp.mean(d * d, axis=-1, keepdims=True)
    return d * lax.rsqrt(var + LN_EPS) * g + b


def _silu(g):
    return g / (1.0 + jnp.exp(-g))


def _rope_table_kernel(pos_ref, freq_ref, cos_a_ref, sin_a_ref, tab_m_ref):
    pos = pos_ref[...]
    ang_a = pos * freq_ref[0:1, :]
    lane = lax.broadcasted_iota(jnp.int32, ang_a.shape, 1)
    cos_a_ref[...] = jnp.cos(ang_a)
    sin_a_ref[...] = jnp.where(lane < HEAD_DIM // 2, -jnp.sin(ang_a), jnp.sin(ang_a))
    ang_m = pos * freq_ref[1:2, :]
    tab_m_ref[...] = jnp.where(lane < MLA_ROPE_DIM, jnp.cos(ang_m), jnp.sin(ang_m))


def _rope_tables(positions):
    n = positions.size
    pos = positions.astype(F32).reshape(n, 1)
    half_a = HEAD_DIM // 2
    half_m = MLA_ROPE_DIM // 2
    inv_a = ROPE_THETA ** (-2.0 * jnp.arange(half_a, dtype=F32) / HEAD_DIM)
    inv_m = ROPE_THETA ** (-2.0 * jnp.arange(half_m, dtype=F32) / MLA_ROPE_DIM)
    freq = jnp.stack([jnp.tile(inv_a, 2), jnp.tile(inv_m, 4)])
    tm = _tile(n, 1024)
    out = jax.ShapeDtypeStruct((n, LANE), F32)
    row = pl.BlockSpec((tm, LANE), lambda i: (i, 0))
    return pl.pallas_call(
        _rope_table_kernel, grid=(n // tm,),
        in_specs=[pl.BlockSpec((tm, 1), lambda i: (i, 0)), pl.BlockSpec((2, LANE), lambda i: (0, 0))],
        out_specs=[row, row, row], out_shape=[out, out, out],
        compiler_params=_cparams("parallel"), name="rope_tables")(pos, freq)


def _mm_kernel(x_ref, w_ref, o_ref):
    o_ref[...] = _dot(x_ref[...], w_ref[...]).astype(o_ref.dtype)


def _matmul(x, w, tm, tn, name):
    m, k = x.shape
    n = w.shape[1]
    tm, tn = _tile(m, tm), _tile(n, tn)
    return pl.pallas_call(
        _mm_kernel, grid=(m // tm, n // tn),
        in_specs=[pl.BlockSpec((tm, k), lambda i, j: (i, 0)), pl.BlockSpec((k, tn), lambda i, j: (0, j))],
        out_specs=pl.BlockSpec((tm, tn), lambda i, j: (i, j)),
        out_shape=jax.ShapeDtypeStruct((m, n), BF16),
        compiler_params=_cparams("parallel", "arbitrary"), name=name)(x, w)


def _mm_rope_kernel(x_ref, w_ref, cos_ref, sin_ref, o_ref):
    acc = _dot(x_ref[...], w_ref[...])
    cos, sin = cos_ref[...], sin_ref[...]
    for c in range(acc.shape[1] // HEAD_DIM):
        a = acc[:, c * HEAD_DIM:(c + 1) * HEAD_DIM]
        o_ref[:, c * HEAD_DIM:(c + 1) * HEAD_DIM] = (
            a * cos + pltpu.roll(a, HEAD_DIM // 2, 1) * sin).astype(o_ref.dtype)


def _matmul_rope(x, w, cos, sin, tm, tn, name):
    m, k = x.shape
    n = w.shape[1]
    tm, tn = _tile(m, tm), _tile(n, tn)
    return pl.pallas_call(
        _mm_rope_kernel, grid=(m // tm, n // tn),
        in_specs=[pl.BlockSpec((tm, k), lambda i, j: (i, 0)), pl.BlockSpec((k, tn), lambda i, j: (0, j)),
                  pl.BlockSpec((tm, LANE), lambda i, j: (i, 0)), pl.BlockSpec((tm, LANE), lambda i, j: (i, 0))],
        out_specs=pl.BlockSpec((tm, tn), lambda i, j: (i, j)),
        out_shape=jax.ShapeDtypeStruct((m, n), BF16),
        compiler_params=_cparams("parallel", "arbitrary"), name=name)(x, w, cos, sin)


def _split3(x):
    hi = x.astype(BF16)
    r = x - hi.astype(F32)
    mid = r.astype(BF16)
    lo = (r - mid.astype(F32)).astype(BF16)
    return hi, mid, lo


def _fox_gate_kernel(x_ref, wf_ref, bf_ref, o_ref, carry_ref, *, n_heads):
    @pl.when(pl.program_id(1) == 0)
    def _():
        carry_ref[...] = jnp.zeros_like(carry_ref)

    f = lax.dot_general(wf_ref[...], x_ref[...], NT_DIMS, preferred_element_type=F32)
    z = f + bf_ref[...]
    log_f = jnp.minimum(z, 0.0) - jnp.log1p(jnp.exp(-jnp.abs(z)))
    tc = z.shape[1]
    src = lax.broadcasted_iota(jnp.int32, (tc, tc), 0)
    dst = lax.broadcasted_iota(jnp.int32, (tc, tc), 1)
    tri = jnp.where(src <= dst, 1.0, 0.0).astype(BF16)
    hi, mid, lo = _split3(log_f)
    cum = _dot(hi, tri) + _dot(mid, tri) + _dot(lo, tri) + carry_ref[:, 0:1]
    o_ref[0] = cum[:n_heads]
    carry_ref[...] = jnp.broadcast_to(cum[:, tc - 1:tc], carry_ref.shape)


def _fox_cum_log_forget(x_bf, w_f, b_forget, batch, seq):
    n, d = x_bf.shape
    n_heads = w_f.shape[1]
    rows = 16
    wf_t = jnp.zeros((rows, d), BF16).at[:n_heads].set(w_f.T.astype(BF16))
    bf = jnp.zeros((rows, 1), F32).at[:n_heads, 0].set(b_forget.astype(F32))
    tc = _tile(seq, 512)
    nc = seq // tc
    return pl.pallas_call(
        functools.partial(_fox_gate_kernel, n_heads=n_heads), grid=(batch, nc),
        in_specs=[pl.BlockSpec((tc, d), lambda b, s: (b * nc + s, 0)),
                  pl.BlockSpec((rows, d), lambda b, s: (0, 0)),
                  pl.BlockSpec((rows, 1), lambda b, s: (0, 0))],
        out_specs=pl.BlockSpec((1, n_heads, tc), lambda b, s: (b, 0, s)),
        out_shape=jax.ShapeDtypeStruct((batch, n_heads, seq), F32),
        scratch_shapes=[pltpu.VMEM((rows, LANE), F32)],
        compiler_params=_cparams("parallel", "arbitrary"), name="fox_gate")(x_bf, wf_t, bf)


def _flash_kernel(*refs, scale, tk, has_bias):
    if has_bias:
        q_ref, k_ref, v_ref, ck_ref, o_ref = refs
    else:
        q_ref, k_ref, v_ref, o_ref = refs
    qi = pl.program_id(2)
    q = q_ref[...]
    tq = q.shape[0]
    dv = v_ref.shape[1]

    def scores(j):
        start = pl.multiple_of(j * tk, tk)
        s = lax.dot_general(q, k_ref[pl.ds(start, tk), :], NT_DIMS, preferred_element_type=F32) * scale
        if has_bias:
            s = s - ck_ref[0, 0, j]
        return s, start

    def update(carry, s, start):
        m, l, acc = carry
        m_new = jnp.maximum(m, jnp.max(s, axis=-1, keepdims=True))
        a = jnp.exp(m - m_new)
        p = jnp.exp(s - m_new)
        l = a * l + jnp.sum(p, axis=-1, keepdims=True)
        acc = a * acc + _dot(p.astype(BF16), v_ref[pl.ds(start, tk), :])
        return m_new, l, acc

    def body(j, carry):
        s, start = scores(j)
        return update(carry, s, start)

    init = (jnp.full((tq, 1), -jnp.inf, F32), jnp.zeros((tq, 1), F32), jnp.zeros((tq, dv), F32))
    carry = lax.fori_loop(0, qi, body, init)
    s, start = scores(qi)
    row = lax.broadcasted_iota(jnp.int32, s.shape, 0)
    col = lax.broadcasted_iota(jnp.int32, s.shape, 1)
    s = jnp.where(row >= col, s, -jnp.inf)
    _, l, acc = update(carry, s, start)
    o_ref[...] = (acc / l).astype(o_ref.dtype)


def _causal_attention(q_arr, k_arr, v_arr, q_col, k_col, v_col, dk, dv, n_heads, batch, seq, scale,
                      ck=None, name="attn"):
    t = _tile(seq, 512)
    nq = seq // t
    in_specs = [pl.BlockSpec((t, dk), lambda b, h, i: (b * nq + i, q_col + h)),
                pl.BlockSpec((seq, dk), lambda b, h, i: (b, k_col + h)),
                pl.BlockSpec((seq, dv), lambda b, h, i: (b, v_col + h))]
    args = [q_arr, k_arr, v_arr]
    if ck is not None:
        args.append(ck.reshape(batch, n_heads, nq, 1, t))
        in_specs.append(pl.BlockSpec((1, 1, nq, 1, t), lambda b, h, i: (b, h, 0, 0, 0)))
    return pl.pallas_call(
        functools.partial(_flash_kernel, scale=scale, tk=t, has_bias=ck is not None),
        grid=(batch, n_heads, nq), in_specs=in_specs,
        out_specs=pl.BlockSpec((t, dv), lambda b, h, i: (b * nq + i, h)),
        out_shape=jax.ShapeDtypeStruct((batch * seq, n_heads * dv), BF16),
        compiler_params=_cparams("parallel", "parallel", "arbitrary"), name=name)(*args)


def _swa_kernel(sink_ref, q_ref, kp_ref, kc_ref, vp_ref, vc_ref, o_ref, *, scale, n_q, n_kv):
    w = SWA_WINDOW
    qi = lax.broadcasted_iota(jnp.int32, (w, 2 * w), 0)
    ji = lax.broadcasted_iota(jnp.int32, (w, 2 * w), 1)
    has_prev = pl.program_id(1) > 0
    mask = (ji > qi) & (ji <= qi + w) & ((ji >= w) | has_prev)
    group = n_q // n_kv
    for kvh in range(n_kv):
        cols = slice(kvh * HEAD_DIM, (kvh + 1) * HEAD_DIM)
        k = jnp.concatenate([kp_ref[:, cols], kc_ref[:, cols]], axis=0)
        v = jnp.concatenate([vp_ref[:, cols], vc_ref[:, cols]], axis=0)
        for g in range(group):
            h = kvh * group + g
            hc = slice(h * HEAD_DIM, (h + 1) * HEAD_DIM)
            s = lax.dot_general(q_ref[:, hc], k, NT_DIMS, preferred_element_type=F32) * scale
            s = jnp.where(mask, s, -jnp.inf)
            sink = sink_ref[h]
            m = jnp.maximum(jnp.max(s, axis=-1, keepdims=True), sink)
            p = jnp.exp(s - m)
            denom = jnp.sum(p, axis=-1, keepdims=True) + jnp.exp(sink - m)
            o_ref[:, hc] = (_dot(p.astype(BF16), v) / denom).astype(o_ref.dtype)


def _swa_attention(qk, v, sinks, n_q, n_kv, batch, seq, scale):
    w = SWA_WINDOW
    nb = seq // w
    qw, kw = n_q * HEAD_DIM, n_kv * HEAD_DIM
    k_col = qw // kw
    cur = lambda b, n: b * nb + n
    prev = lambda b, n: b * nb + jnp.maximum(n - 1, 0)
    return pl.pallas_call(
        functools.partial(_swa_kernel, scale=scale, n_q=n_q, n_kv=n_kv), grid=(batch, nb),
        in_specs=[pl.BlockSpec(memory_space=pltpu.SMEM),
                  pl.BlockSpec((w, qw), lambda b, n: (cur(b, n), 0)),
                  pl.BlockSpec((w, kw), lambda b, n: (prev(b, n), k_col)),
                  pl.BlockSpec((w, kw), lambda b, n: (cur(b, n), k_col)),
                  pl.BlockSpec((w, kw), lambda b, n: (prev(b, n), 0)),
                  pl.BlockSpec((w, kw), lambda b, n: (cur(b, n), 0))],
        out_specs=pl.BlockSpec((w, qw), lambda b, n: (cur(b, n), 0)),
        out_shape=jax.ShapeDtypeStruct((batch * seq, qw), BF16),
        compiler_params=_cparams("parallel", "arbitrary"), name="swa_attn")(
            sinks.astype(F32), qk, qk, qk, v, v)


def _mm_res_ln_kernel(*refs, n_pairs, emit_bf16):
    a_refs, w_refs = refs[:n_pairs], refs[n_pairs:2 * n_pairs]
    res_ref, g_ref, b_ref = refs[2 * n_pairs:2 * n_pairs + 3]
    outs = refs[2 * n_pairs + 3:]
    acc = _dot(a_refs[0][...], w_refs[0][...])
    for a_ref, w_ref in zip(a_refs[1:], w_refs[1:]):
        acc = acc + _dot(a_ref[...], w_ref[...])
    y = _layer_norm(DEEPNORM_ALPHA * res_ref[...] + acc, g_ref[...], b_ref[...])
    outs[0][...] = y
    if emit_bf16:
        outs[1][...] = y.astype(BF16)


def _matmul_residual_ln(pairs, res, g, b, emit_bf16, name):
    m, d = res.shape
    tm = _tile(m, 256)
    row = lambda i: (i, 0)
    fixed = lambda i: (0, 0)
    in_specs = ([pl.BlockSpec((tm, a.shape[1]), row) for a, _ in pairs]
                + [pl.BlockSpec(w.shape, fixed) for _, w in pairs]
                + [pl.BlockSpec((tm, d), row), pl.BlockSpec((1, d), fixed), pl.BlockSpec((1, d), fixed)])
    out_specs = [pl.BlockSpec((tm, d), row)]
    out_shape = [jax.ShapeDtypeStruct((m, d), F32)]
    if emit_bf16:
        out_specs.append(pl.BlockSpec((tm, d), row))
        out_shape.append(jax.ShapeDtypeStruct((m, d), BF16))
    args = [a for a, _ in pairs] + [w for _, w in pairs] + [res, g.reshape(1, d), b.reshape(1, d)]
    return pl.pallas_call(
        functools.partial(_mm_res_ln_kernel, n_pairs=len(pairs), emit_bf16=emit_bf16),
        grid=(m // tm,), in_specs=in_specs, out_specs=out_specs, out_shape=out_shape,
        compiler_params=_cparams("parallel"), name=name)(*args)


def _ffn_ln_kernel(x_ref, wg_ref, wu_ref, wd_ref, res_ref, g_ref, b_ref, o_ref, obf_ref, acc_ref):
    j = pl.program_id(1)

    @pl.when(j == 0)
    def _():
        acc_ref[...] = jnp.zeros_like(acc_ref)

    x = x_ref[...]
    a = _silu(_dot(x, wg_ref[...])) * _dot(x, wu_ref[...])
    acc_ref[...] += _dot(a.astype(BF16), wd_ref[...])

    @pl.when(j == pl.num_programs(1) - 1)
    def _():
        y = _layer_norm(DEEPNORM_ALPHA * res_ref[...] + acc_ref[...], g_ref[...], b_ref[...])
        o_ref[...] = y
        obf_ref[...] = y.astype(BF16)


def _swiglu_residual_ln(x_bf, wg, wu, wd, res, g, b):
    m, d = x_bf.shape
    f = wg.shape[1]
    tm, tf = _tile(m, 512), _tile(f, 512)
    row = lambda i, j: (i, 0)
    fixed = lambda i, j: (0, 0)
    return pl.pallas_call(
        _ffn_ln_kernel, grid=(m // tm, f // tf),
        in_specs=[pl.BlockSpec((tm, d), row),
                  pl.BlockSpec((d, tf), lambda i, j: (0, j)), pl.BlockSpec((d, tf), lambda i, j: (0, j)),
                  pl.BlockSpec((tf, d), lambda i, j: (j, 0)),
                  pl.BlockSpec((tm, d), row), pl.BlockSpec((1, d), fixed), pl.BlockSpec((1, d), fixed)],
        out_specs=[pl.BlockSpec((tm, d), row), pl.BlockSpec((tm, d), row)],
        out_shape=[jax.ShapeDtypeStruct((m, d), F32), jax.ShapeDtypeStruct((m, d), BF16)],
        scratch_shapes=[pltpu.VMEM((tm, d), F32)],
        compiler_params=_cparams("parallel", "arbitrary"), name="ffn_ln")(
            x_bf, wg, wu, wd, res, g.reshape(1, d), b.reshape(1, d))


def _rms_norm(c, g):
    ms = jnp.mean(c * c, axis=-1, keepdims=True)
    return c * lax.rsqrt(ms + RMS_EPS) * g


def _rope64(pair, tab):
    w = pair * tab
    lane = lax.broadcasted_iota(jnp.int32, w.shape, 1)
    return jnp.where(lane < MLA_ROPE_DIM, w + pltpu.roll(w, MLA_ROPE_DIM, 1), 0.0)


def _mla_down_kernel(x_ref, w_ref, gq_ref, gkv_ref, tab_ref, cq_ref, ckv_ref, kpe_ref, *, q_rank, kv_rank):
    acc = _dot(x_ref[...], w_ref[...])
    cq_ref[...] = _rms_norm(acc[:, :q_rank], gq_ref[...]).astype(BF16)
    ckv_ref[...] = _rms_norm(acc[:, q_rank:q_rank + kv_rank], gkv_ref[...]).astype(BF16)
    kpe_ref[...] = _rope64(acc[:, q_rank + kv_rank:], tab_ref[...]).astype(BF16)


def _mla_down(x_bf, w, gq, gkv, tab, q_rank, kv_rank):
    m, d = x_bf.shape
    n = w.shape[1]
    tm = _tile(m, 512)
    row = lambda i: (i, 0)
    fixed = lambda i: (0, 0)
    return pl.pallas_call(
        functools.partial(_mla_down_kernel, q_rank=q_rank, kv_rank=kv_rank), grid=(m // tm,),
        in_specs=[pl.BlockSpec((tm, d), row), pl.BlockSpec((d, n), fixed),
                  pl.BlockSpec((1, q_rank), fixed), pl.BlockSpec((1, kv_rank), fixed),
                  pl.BlockSpec((tm, LANE), row)],
        out_specs=[pl.BlockSpec((tm, q_rank), row), pl.BlockSpec((tm, kv_rank), row),
                   pl.BlockSpec((tm, LANE), row)],
        out_shape=[jax.ShapeDtypeStruct((m, q_rank), BF16), jax.ShapeDtypeStruct((m, kv_rank), BF16),
                   jax.ShapeDtypeStruct((m, LANE), BF16)],
        compiler_params=_cparams("parallel"), name="mla_down")(
            x_bf, w, gq.reshape(1, q_rank).astype(F32), gkv.reshape(1, kv_rank).astype(F32), tab)


def _mla_q_kernel(c_ref, w_ref, tab_ref, o_ref):
    acc = _dot(c_ref[...], w_ref[...])
    o_ref[:, :MLA_NOPE_DIM] = acc[:, :MLA_NOPE_DIM].astype(o_ref.dtype)
    o_ref[:, MLA_NOPE_DIM:] = _rope64(acc[:, MLA_NOPE_DIM:], tab_ref[...]).astype(o_ref.dtype)


def _mla_q_up(cq, w, tab, n_heads):
    m, r = cq.shape
    tm = _tile(m, 1024)
    return pl.pallas_call(
        _mla_q_kernel, grid=(m // tm, n_heads),
        in_specs=[pl.BlockSpec((tm, r), lambda i, h: (i, 0)), pl.BlockSpec((r, MLA_QK_PAD), lambda i, h: (0, h)),
                  pl.BlockSpec((tm, LANE), lambda i, h: (i, 0))],
        out_specs=pl.BlockSpec((tm, MLA_QK_PAD), lambda i, h: (i, h)),
        out_shape=jax.ShapeDtypeStruct((m, n_heads * MLA_QK_PAD), BF16),
        compiler_params=_cparams("parallel", "arbitrary"), name="mla_q_up")(cq, w, tab)


def _mla_kv_kernel(c_ref, w_ref, kpe_ref, k_ref, v_ref):
    acc = _dot(c_ref[...], w_ref[...])
    k_ref[:, :MLA_NOPE_DIM] = acc[:, :MLA_NOPE_DIM].astype(k_ref.dtype)
    k_ref[:, MLA_NOPE_DIM:] = kpe_ref[...]
    v_ref[...] = acc[:, MLA_NOPE_DIM:].astype(v_ref.dtype)


def _mla_kv_up(ckv, w, kpe, n_heads):
    m, r = ckv.shape
    tm = _tile(m, 1024)
    wn = MLA_NOPE_DIM + MLA_V_DIM
    return pl.pallas_call(
        _mla_kv_kernel, grid=(m // tm, n_heads),
        in_specs=[pl.BlockSpec((tm, r), lambda i, h: (i, 0)), pl.BlockSpec((r, wn), lambda i, h: (0, h)),
                  pl.BlockSpec((tm, LANE), lambda i, h: (i, 0))],
        out_specs=[pl.BlockSpec((tm, MLA_QK_PAD), lambda i, h: (i, h)),
                   pl.BlockSpec((tm, MLA_V_DIM), lambda i, h: (i, h))],
        out_shape=[jax.ShapeDtypeStruct((m, n_heads * MLA_QK_PAD), BF16),
                   jax.ShapeDtypeStruct((m, n_heads * MLA_V_DIM), BF16)],
        compiler_params=_cparams("parallel", "arbitrary"), name="mla_kv_up")(ckv, w, kpe)


def _router_kernel(h_ref, w_ref, idx_ref, gate_ref, *, n_experts):
    x = h_ref[...]
    xh = x.astype(BF16)
    xl = (x - xh.astype(F32)).astype(BF16)
    w = w_ref[...]
    wh = w.astype(BF16)
    wl = (w - wh.astype(F32)).astype(BF16)
    logits = _dot(xh, wh) + _dot(xh, wl) + _dot(xl, wh)
    lane = lax.broadcasted_iota(jnp.int32, logits.shape, 1)
    lane_f = lane.astype(F32)
    l1 = jnp.where(lane < n_experts, logits, -jnp.inf)
    v1 = jnp.max(l1, axis=-1, keepdims=True)
    i1 = jnp.min(jnp.where(l1 == v1, lane_f, float(LANE)), axis=-1, keepdims=True)
    l2 = jnp.where(lane_f == i1, -jnp.inf, l1)
    v2 = jnp.max(l2, axis=-1, keepdims=True)
    i2 = jnp.min(jnp.where(l2 == v2, lane_f, float(LANE)), axis=-1, keepdims=True)
    e2 = jnp.exp(v2 - v1)
    idx_ref[...] = jnp.where(lane == 0, i1, jnp.where(lane == 1, i2, 0.0)).astype(jnp.int32)
    gate_ref[...] = jnp.where(lane == 0, 1.0 / (1.0 + e2), jnp.where(lane == 1, e2 / (1.0 + e2), 0.0))


def _router(h, w_router):
    m, d = h.shape
    e = w_router.shape[1]
    w_pad = jnp.zeros((d, LANE), F32).at[:, :e].set(w_router.astype(F32))
    tm = _tile(m, 512)
    row = lambda i: (i, 0)
    return pl.pallas_call(
        functools.partial(_router_kernel, n_experts=e), grid=(m // tm,),
        in_specs=[pl.BlockSpec((tm, d), row), pl.BlockSpec((d, LANE), lambda i: (0, 0))],
        out_specs=[pl.BlockSpec((tm, LANE), row), pl.BlockSpec((tm, LANE), row)],
        out_shape=[jax.ShapeDtypeStruct((m, LANE), jnp.int32), jax.ShapeDtypeStruct((m, LANE), F32)],
        compiler_params=_cparams("parallel"), name="router")(h, w_pad)


def _gather_row_copy(src_hbm, buf, sem, src_row, dst_row):
    return pltpu.make_async_copy(src_hbm.at[pl.ds(src_row, 1), :], buf.at[pl.ds(dst_row, 1), :], sem)


def _gather_rows_kernel(tok_ref, x_hbm, o_ref, buf, sem):
    rows = buf.shape[0]
    base = pl.program_id(0) * rows

    def start(r, _):
        _gather_row_copy(x_hbm, buf, sem, tok_ref[base + r], r).start()
        return 0

    def wait(r, _):
        _gather_row_copy(x_hbm, buf, sem, 0, r).wait()
        return 0

    lax.fori_loop(0, rows, start, 0)
    lax.fori_loop(0, rows, wait, 0)
    o_ref[...] = buf[...].astype(o_ref.dtype)


def _gather_rows(h, row_token, rows_per_step):
    n_rows = row_token.shape[0]
    d = h.shape[1]
    return pl.pallas_call(
        _gather_rows_kernel,
        grid_spec=pltpu.PrefetchScalarGridSpec(
            num_scalar_prefetch=1, grid=(n_rows // rows_per_step,),
            in_specs=[pl.BlockSpec(memory_space=pl.ANY)],
            out_specs=pl.BlockSpec((rows_per_step, d), lambda i, tok: (i, 0)),
            scratch_shapes=[pltpu.VMEM((rows_per_step, d), F32), pltpu.SemaphoreType.DMA(())]),
        out_shape=jax.ShapeDtypeStruct((n_rows, d), BF16),
        compiler_params=_cparams("arbitrary"), name="moe_gather")(row_token, h)


def _moe_ffn_kernel(te_ref, tv_ref, x_ref, wg_ref, wu_ref, wd_ref, o_ref):
    i, j = pl.program_id(0), pl.program_id(1)

    @pl.when(j == 0)
    def _():
        o_ref[...] = jnp.zeros_like(o_ref)

    @pl.when(tv_ref[i] == 1)
    def _():
        x = x_ref[...]
        a = _silu(_dot(x, wg_ref[0])) * _dot(x, wu_ref[0])
        o_ref[...] += _dot(a.astype(BF16), wd_ref[0])


def _moe_ffn(xg, wg, wu, wd, tile_expert, tile_valid, tm):
    n_rows, d = xg.shape
    f = wg.shape[2]
    tf = _tile(f, 512)
    nj = f // tf
    col = lambda i, j, te, tv: jnp.where(tv[i] == 1, j, nj - 1)
    return pl.pallas_call(
        _moe_ffn_kernel,
        grid_spec=pltpu.PrefetchScalarGridSpec(
            num_scalar_prefetch=2, grid=(n_rows // tm, nj),
            in_specs=[pl.BlockSpec((tm, d), lambda i, j, te, tv: (i, 0)),
                      pl.BlockSpec((1, d, tf), lambda i, j, te, tv: (te[i], 0, col(i, j, te, tv))),
                      pl.BlockSpec((1, d, tf), lambda i, j, te, tv: (te[i], 0, col(i, j, te, tv))),
                      pl.BlockSpec((1, tf, d), lambda i, j, te, tv: (te[i], col(i, j, te, tv), 0))],
            out_specs=pl.BlockSpec((tm, d), lambda i, j, te, tv: (i, 0))),
        out_shape=jax.ShapeDtypeStruct((n_rows, d), F32),
        compiler_params=_cparams("arbitrary", "arbitrary"), name="moe_ffn")(
            tile_expert, tile_valid, xg, wg, wu, wd)


def _combine_ln_kernel(pos_ref, h_ref, gate_ref, g_ref, b_ref, y_hbm, o_ref, buf0, buf1, sem):
    rows = buf0.shape[0]
    base = pl.program_id(0) * rows

    def start(r, _):
        n = base + r
        _gather_row_copy(y_hbm, buf0, sem.at[0], pos_ref[TOP_K * n], r).start()
        _gather_row_copy(y_hbm, buf1, sem.at[1], pos_ref[TOP_K * n + 1], r).start()
        return 0

    def wait(r, _):
        _gather_row_copy(y_hbm, buf0, sem.at[0], 0, r).wait()
        _gather_row_copy(y_hbm, buf1, sem.at[1], 0, r).wait()
        return 0

    lax.fori_loop(0, rows, start, 0)
    lax.fori_loop(0, rows, wait, 0)
    gate = gate_ref[...]
    y = buf0[...] * gate[:, 0:1] + buf1[...] * gate[:, 1:2]
    o_ref[...] = _layer_norm(DEEPNORM_ALPHA * h_ref[...] + y, g_ref[...], b_ref[...])


def _moe_combine_ln(y_rows, pos, h, gates, g, b):
    m, d = h.shape
    tm = _tile(m, 256)
    row = lambda i, p: (i, 0)
    fixed = lambda i, p: (0, 0)
    return pl.pallas_call(
        _combine_ln_kernel,
        grid_spec=pltpu.PrefetchScalarGridSpec(
            num_scalar_prefetch=1, grid=(m // tm,),
            in_specs=[pl.BlockSpec((tm, d), row), pl.BlockSpec((tm, LANE), row),
                      pl.BlockSpec((1, d), fixed), pl.BlockSpec((1, d), fixed),
                      pl.BlockSpec(memory_space=pl.ANY)],
            out_specs=pl.BlockSpec((tm, d), row),
            scratch_shapes=[pltpu.VMEM((tm, d), F32), pltpu.VMEM((tm, d), F32),
                            pltpu.SemaphoreType.DMA((2,))]),
        out_shape=jax.ShapeDtypeStruct((m, d), F32),
        compiler_params=_cparams("arbitrary"), name="moe_combine_ln")(
            pos, h, gates, g.reshape(1, d), b.reshape(1, d), y_rows)


def _moe_routing(top_idx, n_experts, tm):
    n_assign = top_idx.size
    expert_of = top_idx.reshape(n_assign)
    onehot = (expert_of[:, None] == jnp.arange(n_experts, dtype=jnp.int32)[None, :]).astype(jnp.int32)
    csum = jnp.cumsum(onehot, axis=0)
    rank = jnp.sum(onehot * (csum - 1), axis=1)
    counts = csum[-1]
    padded = ((counts + tm - 1) // tm) * tm
    padded_end = jnp.cumsum(padded)
    padded_start = padded_end - padded
    pos = (padded_start[expert_of] + rank).astype(jnp.int32)
    n_tiles = -(-n_assign // tm) + n_experts
    row_token = jnp.zeros((n_tiles * tm,), jnp.int32).at[pos].set(
        jnp.arange(n_assign, dtype=jnp.int32) // TOP_K)
    tile_start = jnp.arange(n_tiles, dtype=jnp.int32) * tm
    tile_valid = (tile_start < padded_end[-1]).astype(jnp.int32)
    tile_expert = jnp.minimum(jnp.searchsorted(padded_end, tile_start, side='right'), n_experts - 1)
    last_expert = jnp.max(jnp.where(tile_valid == 1, tile_expert, 0))
    tile_expert = jnp.where(tile_valid == 1, tile_expert, last_expert).astype(jnp.int32)
    return pos, row_token, tile_expert, tile_valid


def _rotate_half_cols(w):
    half = w.shape[-1] // 2
    return jnp.concatenate([-w[..., half:], w[..., :half]], axis=-1)


def kernel(x, positions, w_in0, b_forget, sinks_b, w_out0, ln0_mix_g, ln0_mix_b, w_ffn_gate, w_ffn_up, w_ffn_down, ln0_ffn_g, ln0_ffn_b, w_in1, q_norm_g, w_uq, kv_norm_g, w_ukv, w_out1, ln1_mix_g, ln1_mix_b, w_router, w_moe_gate, w_moe_up, w_moe_down, ln1_ffn_g, ln1_ffn_b):
    batch, seq, d = x.shape
    n = batch * seq
    fox_heads = b_forget.shape[0]
    fox_w = fox_heads * HEAD_DIM
    swa_q = sinks_b.shape[0]
    swa_qw = swa_q * HEAD_DIM
    swa_kvw = (w_in0.shape[1] - 3 * fox_w - fox_heads - swa_qw) // 2
    swa_kv = swa_kvw // HEAD_DIM
    q_rank, kv_rank = q_norm_g.shape[0], kv_norm_g.shape[0]
    mla_heads = w_uq.shape[1] // (MLA_NOPE_DIM + MLA_ROPE_DIM)
    n_experts = w_router.shape[1]

    xf = x.reshape(n, d)
    x_bf = xf.astype(BF16)
    cos_a, sin_a, tab_m = _rope_tables(positions)

    c0 = 3 * fox_w
    c1 = c0 + fox_heads
    c2 = c1 + swa_qw + swa_kvw
    qkv_a = _matmul(x_bf, w_in0[:, :c0].astype(BF16), 2048, 512, "proj_fox")
    qk_b = _matmul_rope(x_bf, w_in0[:, c1:c2].astype(BF16), cos_a, sin_a, 2048, 256, "proj_swa_qk")
    v_b = _matmul(x_bf, w_in0[:, c2:].astype(BF16), 2048, 512, "proj_swa_v")
    cum_log_f = _fox_cum_log_forget(x_bf, w_in0[:, c0:c1], b_forget, batch, seq)
    o_a = _causal_attention(qkv_a, qkv_a, qkv_a, 0, fox_heads, 2 * fox_heads, HEAD_DIM, HEAD_DIM,
                            fox_heads, batch, seq, HEAD_DIM ** -0.5, ck=cum_log_f, name="fox_attn")
    o_b = _swa_attention(qk_b, v_b, sinks_b, swa_q, swa_kv, batch, seq, HEAD_DIM ** -0.5)
    w_out0_bf = w_out0.astype(BF16)
    h1, h1_bf = _matmul_residual_ln([(o_a, w_out0_bf[:fox_w]), (o_b, w_out0_bf[fox_w:])], xf,
                                    ln0_mix_g, ln0_mix_b, True, "out0_ln")
    h2, h2_bf = _swiglu_residual_ln(h1_bf, w_ffn_gate.astype(BF16), w_ffn_up.astype(BF16),
                                    w_ffn_down.astype(BF16), h1, ln0_ffn_g, ln0_ffn_b)

    r0 = q_rank + kv_rank
    w_in1_x = jnp.concatenate([w_in1, _rotate_half_cols(w_in1[:, r0:])], axis=1).astype(BF16)
    cq, ckv, kpe = _mla_down(h2_bf, w_in1_x, q_norm_g, kv_norm_g, tab_m, q_rank, kv_rank)
    w_uq_h = w_uq.reshape(q_rank, mla_heads, MLA_NOPE_DIM + MLA_ROPE_DIM)
    w_uq_x = jnp.concatenate([w_uq_h, _rotate_half_cols(w_uq_h[..., MLA_NOPE_DIM:])], axis=-1)
    q_m = _mla_q_up(cq, w_uq_x.reshape(q_rank, mla_heads * MLA_QK_PAD).astype(BF16), tab_m, mla_heads)
    k_m, v_m = _mla_kv_up(ckv, w_ukv.astype(BF16), kpe, mla_heads)
    o_m = _causal_attention(q_m, k_m, v_m, 0, 0, 0, MLA_QK_PAD, MLA_V_DIM, mla_heads, batch, seq,
                            (MLA_NOPE_DIM + MLA_ROPE_DIM) ** -0.5, name="mla_attn")
    (h3,) = _matmul_residual_ln([(o_m, w_out1.astype(BF16))], h2, ln1_mix_g, ln1_mix_b, False, "out1_ln")

    idx, gates = _router(h3, w_router)
    moe_tm = _tile(n * TOP_K, 512)
    pos, row_token, tile_expert, tile_valid = _moe_routing(idx[:, :TOP_K], n_experts, moe_tm)
    xg = _gather_rows(h3, row_token, _tile(moe_tm, 256))
    y_rows = _moe_ffn(xg, w_moe_gate.astype(BF16), w_moe_up.astype(BF16), w_moe_down.astype(BF16),
                      tile_expert, tile_valid, moe_tm)
    out = _moe_combine_ln(y_rows, pos, h3, gates, ln1_ffn_g, ln1_ffn_b)
    return out.reshape(batch, seq, d)
```

```python
import functools
import math

import jax
import jax.numpy as jnp
from jax import lax
from jax.experimental import pallas as pl
from jax.experimental.pallas import tpu as pltpu

F32 = jnp.float32
BF16 = jnp.bfloat16

LANE = 128
HEAD_DIM = 128
SWA_WINDOW = 128
MLA_NOPE_DIM = 128
MLA_ROPE_DIM = 64
MLA_V_DIM = 128
MLA_QK_PAD = 256
ROPE_THETA = 10000.0
TOP_K = 2
LN_EPS = 1e-5
RMS_EPS = 1e-6
DEPTH = 2
DEEPNORM_ALPHA = (2 * DEPTH) ** 0.25
LOG2_E = math.log2(math.e)

NT_DIMS = (((1,), (1,)), ((), ()))


def _tile(n, pref):
    if n <= pref:
        return n
    t = (pref // LANE) * LANE
    while n % t:
        t -= LANE
    return t


def _cparams(*sem):
    return pltpu.CompilerParams(dimension_semantics=sem)


def _dot(a, b):
    return jnp.dot(a, b, preferred_element_type=F32)


def _layer_norm(y, g, b):
    mu = jnp.mean(y, axis=-1, keepdims=True)
    d = y - mu
    var = jnp.mean(d * d, axis=-1, keepdims=True)
    return d * lax.rsqrt(var + LN_EPS) * g + b


def _silu(g):
    return g / (1.0 + jnp.exp(-g))


def _rope_table_kernel(pos_ref, freq_ref, cos_a_ref, sin_a_ref, tab_m_ref):
    pos = pos_ref[...]
    ang_a = pos * freq_ref[0:1, :]
    lane = lax.broadcasted_iota(jnp.int32, ang_a.shape, 1)
    cos_a_ref[...] = jnp.cos(ang_a)
    sin_a_ref[...] = jnp.where(lane < HEAD_DIM // 2, -jnp.sin(ang_a), jnp.sin(ang_a))
    ang_m = pos * freq_ref[1:2, :]
    tab_m_ref[...] = jnp.where(lane < MLA_ROPE_DIM, jnp.cos(ang_m), jnp.sin(ang_m))


def _rope_tables(positions):
    n = positions.size
    pos = positions.astype(F32).reshape(n, 1)
    half_a = HEAD_DIM // 2
    half_m = MLA_ROPE_DIM // 2
    inv_a = ROPE_THETA ** (-2.0 * jnp.arange(half_a, dtype=F32) / HEAD_DIM)
    inv_m = ROPE_THETA ** (-2.0 * jnp.arange(half_m, dtype=F32) / MLA_ROPE_DIM)
    freq = jnp.stack([jnp.tile(inv_a, 2), jnp.tile(inv_m, 4)])
    tm = _tile(n, 1024)
    out = jax.ShapeDtypeStruct((n, LANE), F32)
    row = pl.BlockSpec((tm, LANE), lambda i: (i, 0))
    return pl.pallas_call(
        _rope_table_kernel, grid=(n // tm,),
        in_specs=[pl.BlockSpec((tm, 1), lambda i: (i, 0)), pl.BlockSpec((2, LANE), lambda i: (0, 0))],
        out_specs=[row, row, row], out_shape=[out, out, out],
        compiler_params=_cparams("parallel"), name="rope_tables")(pos, freq)


def _mm_kernel(x_ref, w_ref, o_ref, *, scaled_cols, col_scale):
    acc = _dot(x_ref[...], w_ref[...])
    if scaled_cols:
        first_col = pl.program_id(1) * o_ref.shape[1]
        acc = acc * jnp.where(first_col < scaled_cols, col_scale, 1.0)
    o_ref[...] = acc.astype(o_ref.dtype)


def _matmul(x, w, tm, tn, name, scaled_cols=0, col_scale=1.0):
    m, k = x.shape
    n = w.shape[1]
    tm, tn = _tile(m, tm), _tile(n, tn)
    assert scaled_cols % tn == 0
    return pl.pallas_call(
        functools.partial(_mm_kernel, scaled_cols=scaled_cols, col_scale=col_scale),
        grid=(m // tm, n // tn),
        in_specs=[pl.BlockSpec((tm, k), lambda i, j: (i, 0)), pl.BlockSpec((k, tn), lambda i, j: (0, j))],
        out_specs=pl.BlockSpec((tm, tn), lambda i, j: (i, j)),
        out_shape=jax.ShapeDtypeStruct((m, n), BF16),
        compiler_params=_cparams("parallel", "arbitrary"), name=name)(x, w)


def _mm_rope_kernel(x_ref, w_ref, cos_ref, sin_ref, o_ref):
    acc = _dot(x_ref[...], w_ref[...])
    cos, sin = cos_ref[...], sin_ref[...]
    for c in range(acc.shape[1] // HEAD_DIM):
        a = acc[:, c * HEAD_DIM:(c + 1) * HEAD_DIM]
        o_ref[:, c * HEAD_DIM:(c + 1) * HEAD_DIM] = (
            a * cos + pltpu.roll(a, HEAD_DIM // 2, 1) * sin).astype(o_ref.dtype)


def _matmul_rope(x, w, cos, sin, tm, tn, name):
    m, k = x.shape
    n = w.shape[1]
    tm, tn = _tile(m, tm), _tile(n, tn)
    return pl.pallas_call(
        _mm_rope_kernel, grid=(m // tm, n // tn),
        in_specs=[pl.BlockSpec((tm, k), lambda i, j: (i, 0)), pl.BlockSpec((k, tn), lambda i, j: (0, j)),
                  pl.BlockSpec((tm, LANE), lambda i, j: (i, 0)), pl.BlockSpec((tm, LANE), lambda i, j: (i, 0))],
        out_specs=pl.BlockSpec((tm, tn), lambda i, j: (i, j)),
        out_shape=jax.ShapeDtypeStruct((m, n), BF16),
        compiler_params=_cparams("parallel", "arbitrary"), name=name)(x, w, cos, sin)


def _split3(x):
    hi = x.astype(BF16)
    r = x - hi.astype(F32)
    mid = r.astype(BF16)
    lo = (r - mid.astype(F32)).astype(BF16)
    return hi, mid, lo


def _fox_gate_kernel(x_ref, wf_ref, bf_ref, o_ref, carry_ref, *, n_heads):
    @pl.when(pl.program_id(1) == 0)
    def _():
        carry_ref[...] = jnp.zeros_like(carry_ref)

    f = lax.dot_general(wf_ref[...], x_ref[...], NT_DIMS, preferred_element_type=F32)
    z = f + bf_ref[...]
    log_f = jnp.minimum(z, 0.0) - jnp.log1p(jnp.exp(-jnp.abs(z)))
    tc = z.shape[1]
    src = lax.broadcasted_iota(jnp.int32, (tc, tc), 0)
    dst = lax.broadcasted_iota(jnp.int32, (tc, tc), 1)
    tri = jnp.where(src <= dst, 1.0, 0.0).astype(BF16)
    hi, mid, lo = _split3(log_f)
    cum = _dot(hi, tri) + _dot(mid, tri) + _dot(lo, tri) + carry_ref[:, 0:1]
    o_ref[0] = cum[:n_heads] * LOG2_E
    carry_ref[...] = jnp.broadcast_to(cum[:, tc - 1:tc], carry_ref.shape)


def _fox_cum_log_forget(x_bf, w_f, b_forget, batch, seq):
    n, d = x_bf.shape
    n_heads = w_f.shape[1]
    rows = 16
    wf_t = jnp.zeros((rows, d), BF16).at[:n_heads].set(w_f.T.astype(BF16))
    bf = jnp.zeros((rows, 1), F32).at[:n_heads, 0].set(b_forget.astype(F32))
    tc = _tile(seq, 512)
    nc = seq // tc
    return pl.pallas_call(
        functools.partial(_fox_gate_kernel, n_heads=n_heads), grid=(batch, nc),
        in_specs=[pl.BlockSpec((tc, d), lambda b, s: (b * nc + s, 0)),
                  pl.BlockSpec((rows, d), lambda b, s: (0, 0)),
                  pl.BlockSpec((rows, 1), lambda b, s: (0, 0))],
        out_specs=pl.BlockSpec((1, n_heads, tc), lambda b, s: (b, 0, s)),
        out_shape=jax.ShapeDtypeStruct((batch, n_heads, seq), F32),
        scratch_shapes=[pltpu.VMEM((rows, LANE), F32)],
        compiler_params=_cparams("parallel", "arbitrary"), name="fox_gate")(x_bf, wf_t, bf)


def _flash_kernel(*refs, tk, dk, dv, heads, has_bias):
    if has_bias:
        q_ref, k_ref, v_ref, ck_ref, o_ref = refs
    else:
        q_ref, k_ref, v_ref, o_ref = refs
    qi = pl.program_id(2)
    tq = q_ref.shape[0]
    qs = [q_ref[:, g * dk:(g + 1) * dk] for g in range(heads)]

    def step(g, state, j, masked):
        m, l, acc = state
        rows = pl.ds(pl.multiple_of(j * tk, tk), tk)
        s = lax.dot_general(qs[g], k_ref[rows, g * dk:(g + 1) * dk], NT_DIMS, preferred_element_type=F32)
        if has_bias:
            s = s - ck_ref[0, g, j]
        if masked:
            row = lax.broadcasted_iota(jnp.int32, s.shape, 0)
            col = lax.broadcasted_iota(jnp.int32, s.shape, 1)
            s = jnp.where(row >= col, s, -jnp.inf)
        m_new = jnp.maximum(m, jnp.max(s, axis=-1, keepdims=True))
        a = jnp.exp2(m - m_new)
        p = jnp.exp2(s - m_new)
        l = a * l + jnp.sum(p, axis=-1, keepdims=True)
        acc = a * acc + _dot(p.astype(BF16), v_ref[rows, g * dv:(g + 1) * dv])
        return m_new, l, acc

    def body(j, carry):
        return tuple(step(g, carry[g], j, False) for g in range(heads))

    init = (jnp.full((tq, 1), -jnp.inf, F32), jnp.zeros((tq, 1), F32), jnp.zeros((tq, dv), F32))
    carry = lax.fori_loop(0, qi, body, (init,) * heads)
    for g in range(heads):
        _, l, acc = step(g, carry[g], qi, True)
        o_ref[:, g * dv:(g + 1) * dv] = (acc / l).astype(o_ref.dtype)


def _causal_attention(q_arr, k_arr, v_arr, q_col, k_col, v_col, dk, dv, n_heads, batch, seq,
                      ck=None, name="attn"):
    heads = 2
    assert n_heads % heads == 0 and q_col % heads == 0 and k_col % heads == 0 and v_col % heads == 0
    t = _tile(seq, 512)
    nq = seq // t
    in_specs = [pl.BlockSpec((t, heads * dk), lambda b, h, i: (b * nq + i, q_col // heads + h)),
                pl.BlockSpec((seq, heads * dk), lambda b, h, i: (b, k_col // heads + h)),
                pl.BlockSpec((seq, heads * dv), lambda b, h, i: (b, v_col // heads + h))]
    args = [q_arr, k_arr, v_arr]
    if ck is not None:
        args.append(ck.reshape(batch, n_heads, nq, 1, t))
        in_specs.append(pl.BlockSpec((1, heads, nq, 1, t), lambda b, h, i: (b, h, 0, 0, 0)))
    return pl.pallas_call(
        functools.partial(_flash_kernel, tk=t, dk=dk, dv=dv, heads=heads, has_bias=ck is not None),
        grid=(batch, n_heads // heads, nq), in_specs=in_specs,
        out_specs=pl.BlockSpec((t, heads * dv), lambda b, h, i: (b * nq + i, h)),
        out_shape=jax.ShapeDtypeStruct((batch * seq, n_heads * dv), BF16),
        compiler_params=_cparams("parallel", "parallel", "arbitrary"), name=name)(*args)


def _swa_kernel(sink_ref, q_ref, kp_ref, kc_ref, vp_ref, vc_ref, o_ref, *, scale, n_q, n_kv):
    w = SWA_WINDOW
    qi = lax.broadcasted_iota(jnp.int32, (w, 2 * w), 0)
    ji = lax.broadcasted_iota(jnp.int32, (w, 2 * w), 1)
    has_prev = pl.program_id(1) > 0
    mask = (ji > qi) & (ji <= qi + w) & ((ji >= w) | has_prev)
    group = n_q // n_kv
    for kvh in range(n_kv):
        cols = slice(kvh * HEAD_DIM, (kvh + 1) * HEAD_DIM)
        k = jnp.concatenate([kp_ref[:, cols], kc_ref[:, cols]], axis=0)
        v = jnp.concatenate([vp_ref[:, cols], vc_ref[:, cols]], axis=0)
        for g in range(group):
            h = kvh * group + g
            hc = slice(h * HEAD_DIM, (h + 1) * HEAD_DIM)
            s = lax.dot_general(q_ref[:, hc], k, NT_DIMS, preferred_element_type=F32) * scale
            s = jnp.where(mask, s, -jnp.inf)
            sink = sink_ref[h]
            m = jnp.maximum(jnp.max(s, axis=-1, keepdims=True), sink)
            p = jnp.exp(s - m)
            denom = jnp.sum(p, axis=-1, keepdims=True) + jnp.exp(sink - m)
            o_ref[:, hc] = (_dot(p.astype(BF16), v) / denom).astype(o_ref.dtype)


def _swa_attention(qk, v, sinks, n_q, n_kv, batch, seq, scale):
    w = SWA_WINDOW
    nb = seq // w
    qw, kw = n_q * HEAD_DIM, n_kv * HEAD_DIM
    k_col = qw // kw
    cur = lambda b, n: b * nb + n
    prev = lambda b, n: b * nb + jnp.maximum(n - 1, 0)
    return pl.pallas_call(
        functools.partial(_swa_kernel, scale=scale, n_q=n_q, n_kv=n_kv), grid=(batch, nb),
        in_specs=[pl.BlockSpec(memory_space=pltpu.SMEM),
                  pl.BlockSpec((w, qw), lambda b, n: (cur(b, n), 0)),
                  pl.BlockSpec((w, kw), lambda b, n: (prev(b, n), k_col)),
                  pl.BlockSpec((w, kw), lambda b, n: (cur(b, n), k_col)),
                  pl.BlockSpec((w, kw), lambda b, n: (prev(b, n), 0)),
                  pl.BlockSpec((w, kw), lambda b, n: (cur(b, n), 0))],
        out_specs=pl.BlockSpec((w, qw), lambda b, n: (cur(b, n), 0)),
        out_shape=jax.ShapeDtypeStruct((batch * seq, qw), BF16),
        compiler_params=_cparams("parallel", "arbitrary"), name="swa_attn")(
            sinks.astype(F32), qk, qk, qk, v, v)


def _mm_res_ln_kernel(*refs, n_pairs, emit_bf16):
    a_refs, w_refs = refs[:n_pairs], refs[n_pairs:2 * n_pairs]
    res_ref, g_ref, b_ref = refs[2 * n_pairs:2 * n_pairs + 3]
    outs = refs[2 * n_pairs + 3:]
    acc = _dot(a_refs[0][...], w_refs[0][...])
    for a_ref, w_ref in zip(a_refs[1:], w_refs[1:]):
        acc = acc + _dot(a_ref[...], w_ref[...])
    y = _layer_norm(DEEPNORM_ALPHA * res_ref[...] + acc, g_ref[...], b_ref[...])
    outs[0][...] = y
    if emit_bf16:
        outs[1][...] = y.astype(BF16)


def _matmul_residual_ln(pairs, res, g, b, emit_bf16, name):
    m, d = res.shape
    tm = _tile(m, 256)
    row = lambda i: (i, 0)
    fixed = lambda i: (0, 0)
    in_specs = ([pl.BlockSpec((tm, a.shape[1]), row) for a, _ in pairs]
                + [pl.BlockSpec(w.shape, fixed) for _, w in pairs]
                + [pl.BlockSpec((tm, d), row), pl.BlockSpec((1, d), fixed), pl.BlockSpec((1, d), fixed)])
    out_specs = [pl.BlockSpec((tm, d), row)]
    out_shape = [jax.ShapeDtypeStruct((m, d), F32)]
    if emit_bf16:
        out_specs.append(pl.BlockSpec((tm, d), row))
        out_shape.append(jax.ShapeDtypeStruct((m, d), BF16))
    args = [a for a, _ in pairs] + [w for _, w in pairs] + [res, g.reshape(1, d), b.reshape(1, d)]
    return pl.pallas_call(
        functools.partial(_mm_res_ln_kernel, n_pairs=len(pairs), emit_bf16=emit_bf16),
        grid=(m // tm,), in_specs=in_specs, out_specs=out_specs, out_shape=out_shape,
        compiler_params=_cparams("parallel"), name=name)(*args)


def _ffn_ln_kernel(x_ref, wg_ref, wu_ref, wd_ref, res_ref, g_ref, b_ref, o_ref, obf_ref, acc_ref):
    j = pl.program_id(1)

    @pl.when(j == 0)
    def _():
        acc_ref[...] = jnp.zeros_like(acc_ref)

    x = x_ref[...]
    a = _silu(_dot(x, wg_ref[...])) * _dot(x, wu_ref[...])
    acc_ref[...] += _dot(a.astype(BF16), wd_ref[...])

    @pl.when(j == pl.num_programs(1) - 1)
    def _():
        y = _layer_norm(DEEPNORM_ALPHA * res_ref[...] + acc_ref[...], g_ref[...], b_ref[...])
        o_ref[...] = y
        obf_ref[...] = y.astype(BF16)


def _swiglu_residual_ln(x_bf, wg, wu, wd, res, g, b):
    m, d = x_bf.shape
    f = wg.shape[1]
    tm, tf = _tile(m, 512), _tile(f, 512)
    row = lambda i, j: (i, 0)
    fixed = lambda i, j: (0, 0)
    return pl.pallas_call(
        _ffn_ln_kernel, grid=(m // tm, f // tf),
        in_specs=[pl.BlockSpec((tm, d), row),
                  pl.BlockSpec((d, tf), lambda i, j: (0, j)), pl.BlockSpec((d, tf), lambda i, j: (0, j)),
                  pl.BlockSpec((tf, d), lambda i, j: (j, 0)),
                  pl.BlockSpec((tm, d), row), pl.BlockSpec((1, d), fixed), pl.BlockSpec((1, d), fixed)],
        out_specs=[pl.BlockSpec((tm, d), row), pl.BlockSpec((tm, d), row)],
        out_shape=[jax.ShapeDtypeStruct((m, d), F32), jax.ShapeDtypeStruct((m, d), BF16)],
        scratch_shapes=[pltpu.VMEM((tm, d), F32)],
        compiler_params=_cparams("parallel", "arbitrary"), name="ffn_ln")(
            x_bf, wg, wu, wd, res, g.reshape(1, d), b.reshape(1, d))


def _rms_norm(c, g):
    ms = jnp.mean(c * c, axis=-1, keepdims=True)
    return c * lax.rsqrt(ms + RMS_EPS) * g


def _rope64(pair, tab):
    w = pair * tab
    lane = lax.broadcasted_iota(jnp.int32, w.shape, 1)
    return jnp.where(lane < MLA_ROPE_DIM, w + pltpu.roll(w, MLA_ROPE_DIM, 1), 0.0)


def _mla_proj_kernel(x_ref, win_ref, gq_ref, gkv_ref, tab_ref, wq_ref, wkv_ref, q_ref, k_ref, v_ref, *,
                     q_rank, kv_rank, n_heads, q_scale):
    acc = _dot(x_ref[...], win_ref[...])
    cq = _rms_norm(acc[:, :q_rank], gq_ref[...]).astype(BF16)
    ckv = _rms_norm(acc[:, q_rank:q_rank + kv_rank], gkv_ref[...]).astype(BF16)
    tab = tab_ref[...]
    k_pe = _rope64(acc[:, q_rank + kv_rank:], tab).astype(BF16)
    wkv_w = MLA_NOPE_DIM + MLA_V_DIM
    for h in range(n_heads):
        q0 = h * MLA_QK_PAD
        qh = _dot(cq, wq_ref[:, q0:q0 + MLA_QK_PAD]) * q_scale
        q_ref[:, q0:q0 + MLA_NOPE_DIM] = qh[:, :MLA_NOPE_DIM].astype(BF16)
        q_ref[:, q0 + MLA_NOPE_DIM:q0 + MLA_QK_PAD] = _rope64(qh[:, MLA_NOPE_DIM:], tab).astype(BF16)
        kvh = _dot(ckv, wkv_ref[:, h * wkv_w:(h + 1) * wkv_w])
        k_ref[:, q0:q0 + MLA_NOPE_DIM] = kvh[:, :MLA_NOPE_DIM].astype(BF16)
        k_ref[:, q0 + MLA_NOPE_DIM:q0 + MLA_QK_PAD] = k_pe
        v_ref[:, h * MLA_V_DIM:(h + 1) * MLA_V_DIM] = kvh[:, MLA_NOPE_DIM:].astype(BF16)


def _mla_projections(x_bf, w_in, gq, gkv, tab, w_q, w_kv, q_rank, kv_rank, n_heads, q_scale):
    m, d = x_bf.shape
    tm = _tile(m, 512)
    row = lambda i: (i, 0)
    fixed = lambda i: (0, 0)
    resident = lambda shape: pl.BlockSpec(shape, fixed, pipeline_mode=pl.Buffered(1))
    qk_w, v_w = n_heads * MLA_QK_PAD, n_heads * MLA_V_DIM
    return pl.pallas_call(
        functools.partial(_mla_proj_kernel, q_rank=q_rank, kv_rank=kv_rank, n_heads=n_heads, q_scale=q_scale),
        grid=(m // tm,),
        in_specs=[pl.BlockSpec((tm, d), row), resident(w_in.shape),
                  pl.BlockSpec((1, q_rank), fixed), pl.BlockSpec((1, kv_rank), fixed),
                  pl.BlockSpec((tm, LANE), row), resident(w_q.shape), resident(w_kv.shape)],
        out_specs=[pl.BlockSpec((tm, qk_w), row), pl.BlockSpec((tm, qk_w), row), pl.BlockSpec((tm, v_w), row)],
        out_shape=[jax.ShapeDtypeStruct((m, qk_w), BF16), jax.ShapeDtypeStruct((m, qk_w), BF16),
                   jax.ShapeDtypeStruct((m, v_w), BF16)],
        compiler_params=_cparams("parallel"), name="mla_proj")(
            x_bf, w_in, gq.reshape(1, q_rank).astype(F32), gkv.reshape(1, kv_rank).astype(F32), tab, w_q, w_kv)


def _router_kernel(h_ref, w_ref, idx_ref, gate_ref, *, n_experts):
    x = h_ref[...]
    xh = x.astype(BF16)
    xl = (x - xh.astype(F32)).astype(BF16)
    w = w_ref[...]
    wh = w.astype(BF16)
    wl = (w - wh.astype(F32)).astype(BF16)
    logits = _dot(xh, wh) + _dot(xh, wl) + _dot(xl, wh)
    lane = lax.broadcasted_iota(jnp.int32, logits.shape, 1)
    lane_f = lane.astype(F32)
    l1 = jnp.where(lane < n_experts, logits, -jnp.inf)
    v1 = jnp.max(l1, axis=-1, keepdims=True)
    i1 = jnp.min(jnp.where(l1 == v1, lane_f, float(LANE)), axis=-1, keepdims=True)
    l2 = jnp.where(lane_f == i1, -jnp.inf, l1)
    v2 = jnp.max(l2, axis=-1, keepdims=True)
    i2 = jnp.min(jnp.where(l2 == v2, lane_f, float(LANE)), axis=-1, keepdims=True)
    e2 = jnp.exp(v2 - v1)
    idx_ref[...] = jnp.where(lane == 0, i1, jnp.where(lane == 1, i2, 0.0)).astype(jnp.int32)
    gate_ref[...] = jnp.where(lane == 0, 1.0 / (1.0 + e2), jnp.where(lane == 1, e2 / (1.0 + e2), 0.0))


def _router(h, w_router):
    m, d = h.shape
    e = w_router.shape[1]
    w_pad = jnp.zeros((d, LANE), F32).at[:, :e].set(w_router.astype(F32))
    tm = _tile(m, 512)
    row = lambda i: (i, 0)
    return pl.pallas_call(
        functools.partial(_router_kernel, n_experts=e), grid=(m // tm,),
        in_specs=[pl.BlockSpec((tm, d), row), pl.BlockSpec((d, LANE), lambda i: (0, 0))],
        out_specs=[pl.BlockSpec((tm, LANE), row), pl.BlockSpec((tm, LANE), row)],
        out_shape=[jax.ShapeDtypeStruct((m, LANE), jnp.int32), jax.ShapeDtypeStruct((m, LANE), F32)],
        compiler_params=_cparams("parallel"), name="router")(h, w_pad)


def _gather_tile_copy(src_hbm, buf, sem, src_row, dst_row):
    return pltpu.make_async_copy(src_hbm.at[src_row], buf.at[dst_row], sem)


def _gather_rows_kernel(tok_ref, x_hbm, o_ref, buf, sem):
    rows, chunks, _ = buf.shape
    base = pl.program_id(0) * rows

    def start(r, _):
        _gather_tile_copy(x_hbm, buf, sem, tok_ref[base + r], r).start()
        return 0

    def wait(r, _):
        _gather_tile_copy(x_hbm, buf, sem, 0, r).wait()
        return 0

    lax.fori_loop(0, rows, start, 0)
    lax.fori_loop(0, rows, wait, 0)
    for c in range(chunks):
        o_ref[:, c * LANE:(c + 1) * LANE] = buf[:, c, :].astype(o_ref.dtype)


def _gather_rows(h_tiles, row_token, rows_per_step):
    n_rows = row_token.shape[0]
    _, chunks, lane = h_tiles.shape
    return pl.pallas_call(
        _gather_rows_kernel,
        grid_spec=pltpu.PrefetchScalarGridSpec(
            num_scalar_prefetch=1, grid=(n_rows // rows_per_step,),
            in_specs=[pl.BlockSpec(memory_space=pl.ANY)],
            out_specs=pl.BlockSpec((rows_per_step, chunks * lane), lambda i, tok: (i, 0)),
            scratch_shapes=[pltpu.VMEM((rows_per_step, chunks, lane), F32), pltpu.SemaphoreType.DMA(())]),
        out_shape=jax.ShapeDtypeStruct((n_rows, chunks * lane), BF16),
        compiler_params=_cparams("arbitrary"), name="moe_gather")(row_token, h_tiles)


def _moe_ffn_kernel(te_ref, tv_ref, x_ref, wg_ref, wu_ref, wd_ref, o_ref, acc_ref):
    i, j = pl.program_id(0), pl.program_id(1)

    @pl.when(j == 0)
    def _():
        acc_ref[...] = jnp.zeros_like(acc_ref)

    @pl.when(tv_ref[i] == 1)
    def _():
        x = x_ref[...]
        a = _silu(_dot(x, wg_ref[0])) * _dot(x, wu_ref[0])
        acc_ref[...] += _dot(a.astype(BF16), wd_ref[0])

    @pl.when(j == pl.num_programs(1) - 1)
    def _():
        for c in range(o_ref.shape[1]):
            o_ref[:, c, :] = acc_ref[:, c * LANE:(c + 1) * LANE]


def _moe_ffn(xg, wg, wu, wd, tile_expert, tile_valid, tm):
    n_rows, d = xg.shape
    f = wg.shape[2]
    tf = _tile(f, 512)
    nj = f // tf
    col = lambda i, j, te, tv: jnp.where(tv[i] == 1, j, nj - 1)
    return pl.pallas_call(
        _moe_ffn_kernel,
        grid_spec=pltpu.PrefetchScalarGridSpec(
            num_scalar_prefetch=2, grid=(n_rows // tm, nj),
            in_specs=[pl.BlockSpec((tm, d), lambda i, j, te, tv: (i, 0)),
                      pl.BlockSpec((1, d, tf), lambda i, j, te, tv: (te[i], 0, col(i, j, te, tv))),
                      pl.BlockSpec((1, d, tf), lambda i, j, te, tv: (te[i], 0, col(i, j, te, tv))),
                      pl.BlockSpec((1, tf, d), lambda i, j, te, tv: (te[i], col(i, j, te, tv), 0))],
            out_specs=pl.BlockSpec((tm, d // LANE, LANE), lambda i, j, te, tv: (i, 0, 0)),
            scratch_shapes=[pltpu.VMEM((tm, d), F32)]),
        out_shape=jax.ShapeDtypeStruct((n_rows, d // LANE, LANE), F32),
        compiler_params=_cparams("arbitrary", "arbitrary"), name="moe_ffn")(
            tile_expert, tile_valid, xg, wg, wu, wd)


def _combine_ln_kernel(pos_ref, h_ref, gate_ref, g_ref, b_ref, y_hbm, o_ref, buf0, buf1, sem):
    rows, chunks, _ = buf0.shape
    base = pl.program_id(0) * rows

    def start(r, _):
        n = base + r
        _gather_tile_copy(y_hbm, buf0, sem.at[0], pos_ref[TOP_K * n], r).start()
        _gather_tile_copy(y_hbm, buf1, sem.at[1], pos_ref[TOP_K * n + 1], r).start()
        return 0

    def wait(r, _):
        _gather_tile_copy(y_hbm, buf0, sem.at[0], 0, r).wait()
        _gather_tile_copy(y_hbm, buf1, sem.at[1], 0, r).wait()
        return 0

    lax.fori_loop(0, rows, start, 0)
    lax.fori_loop(0, rows, wait, 0)
    gate = gate_ref[...]
    g0, g1 = gate[:, 0:1], gate[:, 1:2]
    for c in range(chunks):
        cols = slice(c * LANE, (c + 1) * LANE)
        o_ref[:, cols] = DEEPNORM_ALPHA * h_ref[:, cols] + (buf0[:, c, :] * g0 + buf1[:, c, :] * g1)
    o_ref[...] = _layer_norm(o_ref[...], g_ref[...], b_ref[...])


def _moe_combine_ln(y_tiles, pos, h, gates, g, b):
    m, d = h.shape
    tm = _tile(m, 256)
    row = lambda i, p: (i, 0)
    fixed = lambda i, p: (0, 0)
    tile_buf = pltpu.VMEM((tm, d // LANE, LANE), F32)
    return pl.pallas_call(
        _combine_ln_kernel,
        grid_spec=pltpu.PrefetchScalarGridSpec(
            num_scalar_prefetch=1, grid=(m // tm,),
            in_specs=[pl.BlockSpec((tm, d), row), pl.BlockSpec((tm, LANE), row),
                      pl.BlockSpec((1, d), fixed), pl.BlockSpec((1, d), fixed),
                      pl.BlockSpec(memory_space=pl.ANY)],
            out_specs=pl.BlockSpec((tm, d), row),
            scratch_shapes=[tile_buf, tile_buf, pltpu.SemaphoreType.DMA((2,))]),
        out_shape=jax.ShapeDtypeStruct((m, d), F32),
        compiler_params=_cparams("arbitrary"), name="moe_combine_ln")(
            pos, h, gates, g.reshape(1, d), b.reshape(1, d), y_tiles)


def _moe_routing(top_idx, n_experts, tm):
    n_assign = top_idx.size
    expert_of = top_idx.reshape(n_assign)
    onehot = (expert_of[:, None] == jnp.arange(n_experts, dtype=jnp.int32)[None, :]).astype(jnp.int32)
    csum = jnp.cumsum(onehot, axis=0)
    rank = jnp.sum(onehot * (csum - 1), axis=1)
    counts = csum[-1]
    padded = ((counts + tm - 1) // tm) * tm
    padded_end = jnp.cumsum(padded)
    padded_start = padded_end - padded
    pos = (padded_start[expert_of] + rank).astype(jnp.int32)
    n_tiles = -(-n_assign // tm) + n_experts
    row_token = jnp.zeros((n_tiles * tm,), jnp.int32).at[pos].set(
        jnp.arange(n_assign, dtype=jnp.int32) // TOP_K)
    tile_start = jnp.arange(n_tiles, dtype=jnp.int32) * tm
    tile_valid = (tile_start < padded_end[-1]).astype(jnp.int32)
    tile_expert = jnp.minimum(jnp.searchsorted(padded_end, tile_start, side='right'), n_experts - 1)
    last_expert = jnp.max(jnp.where(tile_valid == 1, tile_expert, 0))
    tile_expert = jnp.where(tile_valid == 1, tile_expert, last_expert).astype(jnp.int32)
    return pos, row_token, tile_expert, tile_valid


def _rotate_half_cols(w):
    half = w.shape[-1] // 2
    return jnp.concatenate([-w[..., half:], w[..., :half]], axis=-1)


def kernel(x, positions, w_in0, b_forget, sinks_b, w_out0, ln0_mix_g, ln0_mix_b, w_ffn_gate, w_ffn_up, w_ffn_down, ln0_ffn_g, ln0_ffn_b, w_in1, q_norm_g, w_uq, kv_norm_g, w_ukv, w_out1, ln1_mix_g, ln1_mix_b, w_router, w_moe_gate, w_moe_up, w_moe_down, ln1_ffn_g, ln1_ffn_b):
    batch, seq, d = x.shape
    n = batch * seq
    fox_heads = b_forget.shape[0]
    fox_w = fox_heads * HEAD_DIM
    swa_q = sinks_b.shape[0]
    swa_qw = swa_q * HEAD_DIM
    swa_kvw = (w_in0.shape[1] - 3 * fox_w - fox_heads - swa_qw) // 2
    swa_kv = swa_kvw // HEAD_DIM
    q_rank, kv_rank = q_norm_g.shape[0], kv_norm_g.shape[0]
    mla_heads = w_uq.shape[1] // (MLA_NOPE_DIM + MLA_ROPE_DIM)
    n_experts = w_router.shape[1]

    xf = x.reshape(n, d)
    x_bf = xf.astype(BF16)
    cos_a, sin_a, tab_m = _rope_tables(positions)

    c0 = 3 * fox_w
    c1 = c0 + fox_heads
    c2 = c1 + swa_qw + swa_kvw
    qkv_a = _matmul(x_bf, w_in0[:, :c0].astype(BF16), 2048, _tile(fox_w, 512), "proj_fox",
                    scaled_cols=fox_w, col_scale=HEAD_DIM ** -0.5 * LOG2_E)
    qk_b = _matmul_rope(x_bf, w_in0[:, c1:c2].astype(BF16), cos_a, sin_a, 2048, 256, "proj_swa_qk")
    v_b = _matmul(x_bf, w_in0[:, c2:].astype(BF16), 2048, 512, "proj_swa_v")
    cum_log_f = _fox_cum_log_forget(x_bf, w_in0[:, c0:c1], b_forget, batch, seq)
    o_a = _causal_attention(qkv_a, qkv_a, qkv_a, 0, fox_heads, 2 * fox_heads, HEAD_DIM, HEAD_DIM,
                            fox_heads, batch, seq, ck=cum_log_f, name="fox_attn")
    o_b = _swa_attention(qk_b, v_b, sinks_b, swa_q, swa_kv, batch, seq, HEAD_DIM ** -0.5)
    w_out0_bf = w_out0.astype(BF16)
    h1, h1_bf = _matmul_residual_ln([(o_a, w_out0_bf[:fox_w]), (o_b, w_out0_bf[fox_w:])], xf,
                                    ln0_mix_g, ln0_mix_b, True, "out0_ln")
    h2, h2_bf = _swiglu_residual_ln(h1_bf, w_ffn_gate.astype(BF16), w_ffn_up.astype(BF16),
                                    w_ffn_down.astype(BF16), h1, ln0_ffn_g, ln0_ffn_b)

    r0 = q_rank + kv_rank
    w_in1_x = jnp.concatenate([w_in1, _rotate_half_cols(w_in1[:, r0:])], axis=1).astype(BF16)
    w_uq_h = w_uq.reshape(q_rank, mla_heads, MLA_NOPE_DIM + MLA_ROPE_DIM)
    w_uq_x = jnp.concatenate([w_uq_h, _rotate_half_cols(w_uq_h[..., MLA_NOPE_DIM:])], axis=-1)
    q_m, k_m, v_m = _mla_projections(
        h2_bf, w_in1_x, q_norm_g, kv_norm_g, tab_m,
        w_uq_x.reshape(q_rank, mla_heads * MLA_QK_PAD).astype(BF16), w_ukv.astype(BF16),
        q_rank, kv_rank, mla_heads, (MLA_NOPE_DIM + MLA_ROPE_DIM) ** -0.5 * LOG2_E)
    o_m = _causal_attention(q_m, k_m, v_m, 0, 0, 0, MLA_QK_PAD, MLA_V_DIM, mla_heads, batch, seq,
                            name="mla_attn")
    (h3,) = _matmul_residual_ln([(o_m, w_out1.astype(BF16))], h2, ln1_mix_g, ln1_mix_b, False, "out1_ln")

    idx, gates = _router(h3, w_router)
    moe_tm = _tile(n * TOP_K, 512)
    pos, row_token, tile_expert, tile_valid = _moe_routing(idx[:, :TOP_K], n_experts, moe_tm)
    xg = _gather_rows(h3.reshape(n, d // LANE, LANE), row_token, _tile(moe_tm, 256))
    y_rows = _moe_ffn(xg, w_moe_gate.astype(BF16), w_moe_up.astype(BF16), w_moe_down.astype(BF16),
                      tile_expert, tile_valid, moe_tm)
    out = _moe_combine_ln(y_rows, pos, h3, gates, ln1_ffn_g, ln1_ffn_b)
    return out.reshape(batch, seq, d)
```

```python
import functools
import math

import jax
import jax.numpy as jnp
from jax import lax
from jax.experimental import pallas as pl
from jax.experimental.pallas import tpu as pltpu

F32 = jnp.float32
BF16 = jnp.bfloat16

LANE = 128
HEAD_DIM = 128
SWA_WINDOW = 128
MLA_NOPE_DIM = 128
MLA_ROPE_DIM = 64
MLA_V_DIM = 128
MLA_QK_PAD = 256
ROPE_THETA = 10000.0
TOP_K = 2
LN_EPS = 1e-5
RMS_EPS = 1e-6
DEPTH = 2
DEEPNORM_ALPHA = (2 * DEPTH) ** 0.25
LOG2_E = math.log2(math.e)

NT_DIMS = (((1,), (1,)), ((), ()))


def _tile(n, pref):
    if n <= pref:
        return n
    t = (pref // LANE) * LANE
    while n % t:
        t -= LANE
    return t


def _cparams(*sem):
    return pltpu.CompilerParams(dimension_semantics=sem)


def _dot(a, b):
    return jnp.dot(a, b, preferred_element_type=F32)


def _layer_norm(y, g, b):
    mu = jnp.mean(y, axis=-1, keepdims=True)
    d = y - mu
    var = jnp.mean(d * d, axis=-1, keepdims=True)
    return d * lax.rsqrt(var + LN_EPS) * g + b


def _silu(g):
    return g / (1.0 + jnp.exp(-g))


def _rope_table_kernel(pos_ref, freq_ref, cos_a_ref, sin_a_ref, tab_m_ref):
    pos = pos_ref[...]
    ang_a = pos * freq_ref[0:1, :]
    lane = lax.broadcasted_iota(jnp.int32, ang_a.shape, 1)
    cos_a_ref[...] = jnp.cos(ang_a)
    sin_a_ref[...] = jnp.where(lane < HEAD_DIM // 2, -jnp.sin(ang_a), jnp.sin(ang_a))
    ang_m = pos * freq_ref[1:2, :]
    tab_m_ref[...] = jnp.where(lane < MLA_ROPE_DIM, jnp.cos(ang_m), jnp.sin(ang_m))


def _rope_tables(positions):
    n = positions.size
    pos = positions.astype(F32).reshape(n, 1)
    half_a = HEAD_DIM // 2
    half_m = MLA_ROPE_DIM // 2
    inv_a = ROPE_THETA ** (-2.0 * jnp.arange(half_a, dtype=F32) / HEAD_DIM)
    inv_m = ROPE_THETA ** (-2.0 * jnp.arange(half_m, dtype=F32) / MLA_ROPE_DIM)
    freq = jnp.stack([jnp.tile(inv_a, 2), jnp.tile(inv_m, 4)])
    tm = _tile(n, 1024)
    out = jax.ShapeDtypeStruct((n, LANE), F32)
    row = pl.BlockSpec((tm, LANE), lambda i: (i, 0))
    return pl.pallas_call(
        _rope_table_kernel, grid=(n // tm,),
        in_specs=[pl.BlockSpec((tm, 1), lambda i: (i, 0)), pl.BlockSpec((2, LANE), lambda i: (0, 0))],
        out_specs=[row, row, row], out_shape=[out, out, out],
        compiler_params=_cparams("parallel"), name="rope_tables")(pos, freq)


def _mm_kernel(x_ref, w_ref, o_ref, *, scaled_cols, col_scale):
    acc = _dot(x_ref[...], w_ref[...])
    if scaled_cols:
        first_col = pl.program_id(1) * o_ref.shape[1]
        acc = acc * jnp.where(first_col < scaled_cols, col_scale, 1.0)
    o_ref[...] = acc.astype(o_ref.dtype)


def _matmul(x, w, tm, tn, name, scaled_cols=0, col_scale=1.0):
    m, k = x.shape
    n = w.shape[1]
    tm, tn = _tile(m, tm), _tile(n, tn)
    assert scaled_cols % tn == 0
    return pl.pallas_call(
        functools.partial(_mm_kernel, scaled_cols=scaled_cols, col_scale=col_scale),
        grid=(m // tm, n // tn),
        in_specs=[pl.BlockSpec((tm, k), lambda i, j: (i, 0)), pl.BlockSpec((k, tn), lambda i, j: (0, j))],
        out_specs=pl.BlockSpec((tm, tn), lambda i, j: (i, j)),
        out_shape=jax.ShapeDtypeStruct((m, n), BF16),
        compiler_params=_cparams("parallel", "arbitrary"), name=name)(x, w)


def _mm_rope_kernel(x_ref, w_ref, cos_ref, sin_ref, o_ref):
    acc = _dot(x_ref[...], w_ref[...])
    cos, sin = cos_ref[...], sin_ref[...]
    for c in range(acc.shape[1] // HEAD_DIM):
        a = acc[:, c * HEAD_DIM:(c + 1) * HEAD_DIM]
        o_ref[:, c * HEAD_DIM:(c + 1) * HEAD_DIM] = (
            a * cos + pltpu.roll(a, HEAD_DIM // 2, 1) * sin).astype(o_ref.dtype)


def _matmul_rope(x, w, cos, sin, tm, tn, name):
    m, k = x.shape
    n = w.shape[1]
    tm, tn = _tile(m, tm), _tile(n, tn)
    return pl.pallas_call(
        _mm_rope_kernel, grid=(m // tm, n // tn),
        in_specs=[pl.BlockSpec((tm, k), lambda i, j: (i, 0)), pl.BlockSpec((k, tn), lambda i, j: (0, j)),
                  pl.BlockSpec((tm, LANE), lambda i, j: (i, 0)), pl.BlockSpec((tm, LANE), lambda i, j: (i, 0))],
        out_specs=pl.BlockSpec((tm, tn), lambda i, j: (i, j)),
        out_shape=jax.ShapeDtypeStruct((m, n), BF16),
        compiler_params=_cparams("parallel", "arbitrary"), name=name)(x, w, cos, sin)


def _split3(x):
    hi = x.astype(BF16)
    r = x - hi.astype(F32)
    mid = r.astype(BF16)
    lo = (r - mid.astype(F32)).astype(BF16)
    return hi, mid, lo


def _fox_gate_kernel(x_ref, wf_ref, bf_ref, o_ref, carry_ref, *, n_heads):
    @pl.when(pl.program_id(1) == 0)
    def _():
        carry_ref[...] = jnp.zeros_like(carry_ref)

    f = lax.dot_general(wf_ref[...], x_ref[...], NT_DIMS, preferred_element_type=F32)
    z = f + bf_ref[...]
    log_f = jnp.minimum(z, 0.0) - jnp.log1p(jnp.exp(-jnp.abs(z)))
    tc = z.shape[1]
    src = lax.broadcasted_iota(jnp.int32, (tc, tc), 0)
    dst = lax.broadcasted_iota(jnp.int32, (tc, tc), 1)
    tri = jnp.where(src <= dst, 1.0, 0.0).astype(BF16)
    hi, mid, lo = _split3(log_f)
    cum = _dot(hi, tri) + _dot(mid, tri) + _dot(lo, tri) + carry_ref[:, 0:1]
    o_ref[0] = cum[:n_heads] * LOG2_E
    carry_ref[...] = jnp.broadcast_to(cum[:, tc - 1:tc], carry_ref.shape)


def _fox_cum_log_forget(x_bf, w_f, b_forget, batch, seq):
    n, d = x_bf.shape
    n_heads = w_f.shape[1]
    rows = 16
    wf_t = jnp.zeros((rows, d), BF16).at[:n_heads].set(w_f.T.astype(BF16))
    bf = jnp.zeros((rows, 1), F32).at[:n_heads, 0].set(b_forget.astype(F32))
    tc = _tile(seq, 512)
    nc = seq // tc
    return pl.pallas_call(
        functools.partial(_fox_gate_kernel, n_heads=n_heads), grid=(batch, nc),
        in_specs=[pl.BlockSpec((tc, d), lambda b, s: (b * nc + s, 0)),
                  pl.BlockSpec((rows, d), lambda b, s: (0, 0)),
                  pl.BlockSpec((rows, 1), lambda b, s: (0, 0))],
        out_specs=pl.BlockSpec((1, n_heads, tc), lambda b, s: (b, 0, s)),
        out_shape=jax.ShapeDtypeStruct((batch, n_heads, seq), F32),
        scratch_shapes=[pltpu.VMEM((rows, LANE), F32)],
        compiler_params=_cparams("parallel", "arbitrary"), name="fox_gate")(x_bf, wf_t, bf)


def _flash_kernel(*refs, tk, dk, dv, heads, has_bias):
    if has_bias:
        q_ref, k_ref, v_ref, ck_ref, o_ref = refs
    else:
        q_ref, k_ref, v_ref, o_ref = refs
    qi = pl.program_id(2)
    tq = q_ref.shape[0]
    qs = [q_ref[:, g * dk:(g + 1) * dk] for g in range(heads)]

    def step(g, state, j, masked):
        m, l, acc = state
        rows = pl.ds(pl.multiple_of(j * tk, tk), tk)
        s = lax.dot_general(qs[g], k_ref[rows, g * dk:(g + 1) * dk], NT_DIMS, preferred_element_type=F32)
        if has_bias:
            s = s - ck_ref[0, g, j]
        if masked:
            row = lax.broadcasted_iota(jnp.int32, s.shape, 0)
            col = lax.broadcasted_iota(jnp.int32, s.shape, 1)
            s = jnp.where(row >= col, s, -jnp.inf)
        m_new = jnp.maximum(m, jnp.max(s, axis=-1, keepdims=True))
        a = jnp.exp2(m - m_new)
        p = jnp.exp2(s - m_new)
        l = a * l + jnp.sum(p, axis=-1, keepdims=True)
        acc = a * acc + _dot(p.astype(BF16), v_ref[rows, g * dv:(g + 1) * dv])
        return m_new, l, acc

    def body(j, carry):
        return tuple(step(g, carry[g], j, False) for g in range(heads))

    init = (jnp.full((tq, 1), -jnp.inf, F32), jnp.zeros((tq, 1), F32), jnp.zeros((tq, dv), F32))
    carry = lax.fori_loop(0, qi, body, (init,) * heads)
    for g in range(heads):
        _, l, acc = step(g, carry[g], qi, True)
        o_ref[:, g * dv:(g + 1) * dv] = (acc / l).astype(o_ref.dtype)


def _causal_attention(q_arr, k_arr, v_arr, q_col, k_col, v_col, dk, dv, n_heads, batch, seq,
                      ck=None, name="attn"):
    heads = 2
    assert n_heads % heads == 0 and q_col % heads == 0 and k_col % heads == 0 and v_col % heads == 0
    t = _tile(seq, 512)
    nq = seq // t
    in_specs = [pl.BlockSpec((t, heads * dk), lambda b, h, i: (b * nq + i, q_col // heads + h)),
                pl.BlockSpec((seq, heads * dk), lambda b, h, i: (b, k_col // heads + h)),
                pl.BlockSpec((seq, heads * dv), lambda b, h, i: (b, v_col // heads + h))]
    args = [q_arr, k_arr, v_arr]
    if ck is not None:
        args.append(ck.reshape(batch, n_heads, nq, 1, t))
        in_specs.append(pl.BlockSpec((1, heads, nq, 1, t), lambda b, h, i: (b, h, 0, 0, 0)))
    return pl.pallas_call(
        functools.partial(_flash_kernel, tk=t, dk=dk, dv=dv, heads=heads, has_bias=ck is not None),
        grid=(batch, n_heads // heads, nq), in_specs=in_specs,
        out_specs=pl.BlockSpec((t, heads * dv), lambda b, h, i: (b * nq + i, h)),
        out_shape=jax.ShapeDtypeStruct((batch * seq, n_heads * dv), BF16),
        compiler_params=_cparams("parallel", "parallel", "arbitrary"), name=name)(*args)


def _swa_kernel(sink_ref, q_ref, kp_ref, kc_ref, vp_ref, vc_ref, o_ref, *, scale, n_q, n_kv):
    w = SWA_WINDOW
    qi = lax.broadcasted_iota(jnp.int32, (w, 2 * w), 0)
    ji = lax.broadcasted_iota(jnp.int32, (w, 2 * w), 1)
    has_prev = pl.program_id(1) > 0
    mask = (ji > qi) & (ji <= qi + w) & ((ji >= w) | has_prev)
    group = n_q // n_kv
    for kvh in range(n_kv):
        cols = slice(kvh * HEAD_DIM, (kvh + 1) * HEAD_DIM)
        k = jnp.concatenate([kp_ref[:, cols], kc_ref[:, cols]], axis=0)
        v = jnp.concatenate([vp_ref[:, cols], vc_ref[:, cols]], axis=0)
        for g in range(group):
            h = kvh * group + g
            hc = slice(h * HEAD_DIM, (h + 1) * HEAD_DIM)
            s = lax.dot_general(q_ref[:, hc], k, NT_DIMS, preferred_element_type=F32) * scale
            s = jnp.where(mask, s, -jnp.inf)
            sink = sink_ref[h]
            m = jnp.maximum(jnp.max(s, axis=-1, keepdims=True), sink)
            p = jnp.exp(s - m)
            denom = jnp.sum(p, axis=-1, keepdims=True) + jnp.exp(sink - m)
            o_ref[:, hc] = (_dot(p.astype(BF16), v) / denom).astype(o_ref.dtype)


def _swa_attention(qk, v, sinks, n_q, n_kv, batch, seq, scale):
    w = SWA_WINDOW
    nb = seq // w
    qw, kw = n_q * HEAD_DIM, n_kv * HEAD_DIM
    k_col = qw // kw
    cur = lambda b, n: b * nb + n
    prev = lambda b, n: b * nb + jnp.maximum(n - 1, 0)
    return pl.pallas_call(
        functools.partial(_swa_kernel, scale=scale, n_q=n_q, n_kv=n_kv), grid=(batch, nb),
        in_specs=[pl.BlockSpec(memory_space=pltpu.SMEM),
                  pl.BlockSpec((w, qw), lambda b, n: (cur(b, n), 0)),
                  pl.BlockSpec((w, kw), lambda b, n: (prev(b, n), k_col)),
                  pl.BlockSpec((w, kw), lambda b, n: (cur(b, n), k_col)),
                  pl.BlockSpec((w, kw), lambda b, n: (prev(b, n), 0)),
                  pl.BlockSpec((w, kw), lambda b, n: (cur(b, n), 0))],
        out_specs=pl.BlockSpec((w, qw), lambda b, n: (cur(b, n), 0)),
        out_shape=jax.ShapeDtypeStruct((batch * seq, qw), BF16),
        compiler_params=_cparams("parallel", "arbitrary"), name="swa_attn")(
            sinks.astype(F32), qk, qk, qk, v, v)


def _mm_res_ln_kernel(*refs, n_pairs, emit_bf16):
    a_refs, w_refs = refs[:n_pairs], refs[n_pairs:2 * n_pairs]
    res_ref, g_ref, b_ref = refs[2 * n_pairs:2 * n_pairs + 3]
    outs = refs[2 * n_pairs + 3:]
    acc = _dot(a_refs[0][...], w_refs[0][...])
    for a_ref, w_ref in zip(a_refs[1:], w_refs[1:]):
        acc = acc + _dot(a_ref[...], w_ref[...])
    y = _layer_norm(DEEPNORM_ALPHA * res_ref[...] + acc, g_ref[...], b_ref[...])
    outs[0][...] = y
    if emit_bf16:
        outs[1][...] = y.astype(BF16)


def _matmul_residual_ln(pairs, res, g, b, emit_bf16, name):
    m, d = res.shape
    tm = _tile(m, 256)
    row = lambda i: (i, 0)
    fixed = lambda i: (0, 0)
    in_specs = ([pl.BlockSpec((tm, a.shape[1]), row) for a, _ in pairs]
                + [pl.BlockSpec(w.shape, fixed) for _, w in pairs]
                + [pl.BlockSpec((tm, d), row), pl.BlockSpec((1, d), fixed), pl.BlockSpec((1, d), fixed)])
    out_specs = [pl.BlockSpec((tm, d), row)]
    out_shape = [jax.ShapeDtypeStruct((m, d), F32)]
    if emit_bf16:
        out_specs.append(pl.BlockSpec((tm, d), row))
        out_shape.append(jax.ShapeDtypeStruct((m, d), BF16))
    args = [a for a, _ in pairs] + [w for _, w in pairs] + [res, g.reshape(1, d), b.reshape(1, d)]
    return pl.pallas_call(
        functools.partial(_mm_res_ln_kernel, n_pairs=len(pairs), emit_bf16=emit_bf16),
        grid=(m // tm,), in_specs=in_specs, out_specs=out_specs, out_shape=out_shape,
        compiler_params=_cparams("parallel"), name=name)(*args)


def _ffn_ln_kernel(x_ref, wg_ref, wu_ref, wd_ref, res_ref, g_ref, b_ref, o_ref, obf_ref, acc_ref):
    j = pl.program_id(1)

    @pl.when(j == 0)
    def _():
        acc_ref[...] = jnp.zeros_like(acc_ref)

    x = x_ref[...]
    a = _silu(_dot(x, wg_ref[...])) * _dot(x, wu_ref[...])
    acc_ref[...] += _dot(a.astype(BF16), wd_ref[...])

    @pl.when(j == pl.num_programs(1) - 1)
    def _():
        y = _layer_norm(DEEPNORM_ALPHA * res_ref[...] + acc_ref[...], g_ref[...], b_ref[...])
        o_ref[...] = y
        obf_ref[...] = y.astype(BF16)


def _swiglu_residual_ln(x_bf, wg, wu, wd, res, g, b):
    m, d = x_bf.shape
    f = wg.shape[1]
    tm, tf = _tile(m, 512), _tile(f, 512)
    row = lambda i, j: (i, 0)
    fixed = lambda i, j: (0, 0)
    return pl.pallas_call(
        _ffn_ln_kernel, grid=(m // tm, f // tf),
        in_specs=[pl.BlockSpec((tm, d), row),
                  pl.BlockSpec((d, tf), lambda i, j: (0, j)), pl.BlockSpec((d, tf), lambda i, j: (0, j)),
                  pl.BlockSpec((tf, d), lambda i, j: (j, 0)),
                  pl.BlockSpec((tm, d), row), pl.BlockSpec((1, d), fixed), pl.BlockSpec((1, d), fixed)],
        out_specs=[pl.BlockSpec((tm, d), row), pl.BlockSpec((tm, d), row)],
        out_shape=[jax.ShapeDtypeStruct((m, d), F32), jax.ShapeDtypeStruct((m, d), BF16)],
        scratch_shapes=[pltpu.VMEM((tm, d), F32)],
        compiler_params=_cparams("parallel", "arbitrary"), name="ffn_ln")(
            x_bf, wg, wu, wd, res, g.reshape(1, d), b.reshape(1, d))


def _rms_norm(c, g):
    ms = jnp.mean(c * c, axis=-1, keepdims=True)
    return c * lax.rsqrt(ms + RMS_EPS) * g


def _rope64(pair, tab):
    w = pair * tab
    lane = lax.broadcasted_iota(jnp.int32, w.shape, 1)
    return jnp.where(lane < MLA_ROPE_DIM, w + pltpu.roll(w, MLA_ROPE_DIM, 1), 0.0)


def _mla_proj_kernel(x_ref, win_ref, gq_ref, gkv_ref, tab_ref, wq_ref, wkv_ref, q_ref, k_ref, v_ref, *,
                     q_rank, kv_rank, n_heads, q_scale):
    acc = _dot(x_ref[...], win_ref[...])
    cq = _rms_norm(acc[:, :q_rank], gq_ref[...]).astype(BF16)
    ckv = _rms_norm(acc[:, q_rank:q_rank + kv_rank], gkv_ref[...]).astype(BF16)
    tab = tab_ref[...]
    k_pe = _rope64(acc[:, q_rank + kv_rank:], tab).astype(BF16)
    wkv_w = MLA_NOPE_DIM + MLA_V_DIM
    for h in range(n_heads):
        q0 = h * MLA_QK_PAD
        qh = _dot(cq, wq_ref[:, q0:q0 + MLA_QK_PAD]) * q_scale
        q_ref[:, q0:q0 + MLA_NOPE_DIM] = qh[:, :MLA_NOPE_DIM].astype(BF16)
        q_ref[:, q0 + MLA_NOPE_DIM:q0 + MLA_QK_PAD] = _rope64(qh[:, MLA_NOPE_DIM:], tab).astype(BF16)
        kvh = _dot(ckv, wkv_ref[:, h * wkv_w:(h + 1) * wkv_w])
        k_ref[:, q0:q0 + MLA_NOPE_DIM] = kvh[:, :MLA_NOPE_DIM].astype(BF16)
        k_ref[:, q0 + MLA_NOPE_DIM:q0 + MLA_QK_PAD] = k_pe
        v_ref[:, h * MLA_V_DIM:(h + 1) * MLA_V_DIM] = kvh[:, MLA_NOPE_DIM:].astype(BF16)


def _mla_projections(x_bf, w_in, gq, gkv, tab, w_q, w_kv, q_rank, kv_rank, n_heads, q_scale):
    m, d = x_bf.shape
    tm = _tile(m, 512)
    row = lambda i: (i, 0)
    fixed = lambda i: (0, 0)
    resident = lambda shape: pl.BlockSpec(shape, fixed, pipeline_mode=pl.Buffered(1))
    qk_w, v_w = n_heads * MLA_QK_PAD, n_heads * MLA_V_DIM
    return pl.pallas_call(
        functools.partial(_mla_proj_kernel, q_rank=q_rank, kv_rank=kv_rank, n_heads=n_heads, q_scale=q_scale),
        grid=(m // tm,),
        in_specs=[pl.BlockSpec((tm, d), row), resident(w_in.shape),
                  pl.BlockSpec((1, q_rank), fixed), pl.BlockSpec((1, kv_rank), fixed),
                  pl.BlockSpec((tm, LANE), row), resident(w_q.shape), resident(w_kv.shape)],
        out_specs=[pl.BlockSpec((tm, qk_w), row), pl.BlockSpec((tm, qk_w), row), pl.BlockSpec((tm, v_w), row)],
        out_shape=[jax.ShapeDtypeStruct((m, qk_w), BF16), jax.ShapeDtypeStruct((m, qk_w), BF16),
                   jax.ShapeDtypeStruct((m, v_w), BF16)],
        compiler_params=_cparams("parallel"), name="mla_proj")(
            x_bf, w_in, gq.reshape(1, q_rank).astype(F32), gkv.reshape(1, kv_rank).astype(F32), tab, w_q, w_kv)


def _router_kernel(h_ref, w_ref, idx_ref, gate_ref, *, n_experts):
    x = h_ref[...]
    xh = x.astype(BF16)
    xl = (x - xh.astype(F32)).astype(BF16)
    w = w_ref[...]
    wh = w.astype(BF16)
    wl = (w - wh.astype(F32)).astype(BF16)
    logits = _dot(xh, wh) + _dot(xh, wl) + _dot(xl, wh)
    lane = lax.broadcasted_iota(jnp.int32, logits.shape, 1)
    lane_f = lane.astype(F32)
    l1 = jnp.where(lane < n_experts, logits, -jnp.inf)
    v1 = jnp.max(l1, axis=-1, keepdims=True)
    i1 = jnp.min(jnp.where(l1 == v1, lane_f, float(LANE)), axis=-1, keepdims=True)
    l2 = jnp.where(lane_f == i1, -jnp.inf, l1)
    v2 = jnp.max(l2, axis=-1, keepdims=True)
    i2 = jnp.min(jnp.where(l2 == v2, lane_f, float(LANE)), axis=-1, keepdims=True)
    e2 = jnp.exp(v2 - v1)
    idx_ref[...] = jnp.where(lane == 0, i1, jnp.where(lane == 1, i2, 0.0)).astype(jnp.int32)
    gate_ref[...] = jnp.where(lane == 0, 1.0 / (1.0 + e2), jnp.where(lane == 1, e2 / (1.0 + e2), 0.0))


def _router(h, w_router):
    m, d = h.shape
    e = w_router.shape[1]
    w_pad = jnp.zeros((d, LANE), F32).at[:, :e].set(w_router.astype(F32))
    tm = _tile(m, 512)
    row = lambda i: (i, 0)
    return pl.pallas_call(
        functools.partial(_router_kernel, n_experts=e), grid=(m // tm,),
        in_specs=[pl.BlockSpec((tm, d), row), pl.BlockSpec((d, LANE), lambda i: (0, 0))],
        out_specs=[pl.BlockSpec((tm, LANE), row), pl.BlockSpec((tm, LANE), row)],
        out_shape=[jax.ShapeDtypeStruct((m, LANE), jnp.int32), jax.ShapeDtypeStruct((m, LANE), F32)],
        compiler_params=_cparams("parallel"), name="router")(h, w_pad)


GATHER_ISSUE_UNROLL = 8


def _row_copy(src_hbm, buf, sem, src_row, dst_row):
    return pltpu.make_async_copy(src_hbm.at[pl.ds(src_row, 1), :], buf.at[pl.ds(dst_row, 1), :], sem)


def _wait_all_rows(src_hbm, buf, sem):
    pltpu.make_async_copy(src_hbm.at[pl.ds(0, buf.shape[0]), :], buf, sem).wait()


def _gather_rows_kernel(tok_ref, x_hbm, o_ref, buf, sem):
    rows = buf.shape[1]
    i, n = pl.program_id(0), pl.num_programs(0)

    def issue(step, slot):
        def start(r, _):
            _row_copy(x_hbm, buf.at[slot], sem.at[slot], tok_ref[step * rows + r], r).start()
            return 0
        lax.fori_loop(0, rows, start, 0, unroll=GATHER_ISSUE_UNROLL)

    @pl.when(i == 0)
    def _():
        issue(0, 0)

    @pl.when(i + 1 < n)
    def _():
        issue(i + 1, (i + 1) % 2)

    slot = i % 2
    _wait_all_rows(x_hbm, buf.at[slot], sem.at[slot])
    o_ref[...] = buf[slot].astype(o_ref.dtype)


def _gather_rows(h, row_token, rows_per_step):
    n_rows = row_token.shape[0]
    d = h.shape[1]
    return pl.pallas_call(
        _gather_rows_kernel,
        grid_spec=pltpu.PrefetchScalarGridSpec(
            num_scalar_prefetch=1, grid=(n_rows // rows_per_step,),
            in_specs=[pl.BlockSpec(memory_space=pl.ANY)],
            out_specs=pl.BlockSpec((rows_per_step, d), lambda i, tok: (i, 0)),
            scratch_shapes=[pltpu.VMEM((2, rows_per_step, d), F32), pltpu.SemaphoreType.DMA((2,))]),
        out_shape=jax.ShapeDtypeStruct((n_rows, d), BF16),
        compiler_params=_cparams("arbitrary"), name="moe_gather")(row_token, h)


def _moe_ffn_kernel(te_ref, tv_ref, x_ref, wg_ref, wu_ref, wd_ref, o_ref):
    i, j = pl.program_id(0), pl.program_id(1)

    @pl.when(j == 0)
    def _():
        o_ref[...] = jnp.zeros_like(o_ref)

    @pl.when(tv_ref[i] == 1)
    def _():
        x = x_ref[...]
        a = _silu(_dot(x, wg_ref[0])) * _dot(x, wu_ref[0])
        o_ref[...] += _dot(a.astype(BF16), wd_ref[0])


def _moe_ffn(xg, wg, wu, wd, tile_expert, tile_valid, tm):
    n_rows, d = xg.shape
    f = wg.shape[2]
    tf = _tile(f, 512)
    nj = f // tf
    col = lambda i, j, te, tv: jnp.where(tv[i] == 1, j, nj - 1)
    return pl.pallas_call(
        _moe_ffn_kernel,
        grid_spec=pltpu.PrefetchScalarGridSpec(
            num_scalar_prefetch=2, grid=(n_rows // tm, nj),
            in_specs=[pl.BlockSpec((tm, d), lambda i, j, te, tv: (i, 0)),
                      pl.BlockSpec((1, d, tf), lambda i, j, te, tv: (te[i], 0, col(i, j, te, tv))),
                      pl.BlockSpec((1, d, tf), lambda i, j, te, tv: (te[i], 0, col(i, j, te, tv))),
                      pl.BlockSpec((1, tf, d), lambda i, j, te, tv: (te[i], col(i, j, te, tv), 0))],
            out_specs=pl.BlockSpec((tm, d), lambda i, j, te, tv: (i, 0))),
        out_shape=jax.ShapeDtypeStruct((n_rows, d), F32),
        compiler_params=_cparams("arbitrary", "arbitrary"), name="moe_ffn")(
            tile_expert, tile_valid, xg, wg, wu, wd)


def _combine_ln_kernel(pos_ref, h_ref, gate_ref, g_ref, b_ref, y_hbm, o_ref, buf, sem):
    rows = buf.shape[2]
    i, n = pl.program_id(0), pl.num_programs(0)

    def issue(step, slot):
        def start(r, _):
            for k in range(TOP_K):
                src = pos_ref[TOP_K * (step * rows + r) + k]
                _row_copy(y_hbm, buf.at[slot, k], sem.at[slot, k], src, r).start()
            return 0
        lax.fori_loop(0, rows, start, 0, unroll=GATHER_ISSUE_UNROLL // TOP_K)

    @pl.when(i == 0)
    def _():
        issue(0, 0)

    @pl.when(i + 1 < n)
    def _():
        issue(i + 1, (i + 1) % 2)

    slot = i % 2
    gate = gate_ref[...]
    y = None
    for k in range(TOP_K):
        _wait_all_rows(y_hbm, buf.at[slot, k], sem.at[slot, k])
        term = buf[slot, k] * gate[:, k:k + 1]
        y = term if y is None else y + term
    o_ref[...] = _layer_norm(DEEPNORM_ALPHA * h_ref[...] + y, g_ref[...], b_ref[...])


def _moe_combine_ln(y_rows, pos, h, gates, g, b):
    m, d = h.shape
    tm = _tile(m, 256)
    row = lambda i, p: (i, 0)
    fixed = lambda i, p: (0, 0)
    return pl.pallas_call(
        _combine_ln_kernel,
        grid_spec=pltpu.PrefetchScalarGridSpec(
            num_scalar_prefetch=1, grid=(m // tm,),
            in_specs=[pl.BlockSpec((tm, d), row), pl.BlockSpec((tm, LANE), row),
                      pl.BlockSpec((1, d), fixed), pl.BlockSpec((1, d), fixed),
                      pl.BlockSpec(memory_space=pl.ANY)],
            out_specs=pl.BlockSpec((tm, d), row),
            scratch_shapes=[pltpu.VMEM((2, TOP_K, tm, d), F32), pltpu.SemaphoreType.DMA((2, TOP_K))]),
        out_shape=jax.ShapeDtypeStruct((m, d), F32),
        compiler_params=_cparams("arbitrary"), name="moe_combine_ln")(
            pos, h, gates, g.reshape(1, d), b.reshape(1, d), y_rows)


def _moe_routing(top_idx, n_experts, tm):
    n_assign = top_idx.size
    expert_of = top_idx.reshape(n_assign)
    onehot = (expert_of[:, None] == jnp.arange(n_experts, dtype=jnp.int32)[None, :]).astype(jnp.int32)
    csum = jnp.cumsum(onehot, axis=0)
    rank = jnp.sum(onehot * (csum - 1), axis=1)
    counts = csum[-1]
    padded = ((counts + tm - 1) // tm) * tm
    padded_end = jnp.cumsum(padded)
    padded_start = padded_end - padded
    pos = (padded_start[expert_of] + rank).astype(jnp.int32)
    n_tiles = -(-n_assign // tm) + n_experts
    row_token = jnp.zeros((n_tiles * tm,), jnp.int32).at[pos].set(
        jnp.arange(n_assign, dtype=jnp.int32) // TOP_K)
    tile_start = jnp.arange(n_tiles, dtype=jnp.int32) * tm
    tile_valid = (tile_start < padded_end[-1]).astype(jnp.int32)
    tile_expert = jnp.minimum(jnp.searchsorted(padded_end, tile_start, side='right'), n_experts - 1)
    last_expert = jnp.max(jnp.where(tile_valid == 1, tile_expert, 0))
    tile_expert = jnp.where(tile_valid == 1, tile_expert, last_expert).astype(jnp.int32)
    return pos, row_token, tile_expert, tile_valid


def _rotate_half_cols(w):
    half = w.shape[-1] // 2
    return jnp.concatenate([-w[..., half:], w[..., :half]], axis=-1)


def kernel(x, positions, w_in0, b_forget, sinks_b, w_out0, ln0_mix_g, ln0_mix_b, w_ffn_gate, w_ffn_up, w_ffn_down, ln0_ffn_g, ln0_ffn_b, w_in1, q_norm_g, w_uq, kv_norm_g, w_ukv, w_out1, ln1_mix_g, ln1_mix_b, w_router, w_moe_gate, w_moe_up, w_moe_down, ln1_ffn_g, ln1_ffn_b):
    batch, seq, d = x.shape
    n = batch * seq
    fox_heads = b_forget.shape[0]
    fox_w = fox_heads * HEAD_DIM
    swa_q = sinks_b.shape[0]
    swa_qw = swa_q * HEAD_DIM
    swa_kvw = (w_in0.shape[1] - 3 * fox_w - fox_heads - swa_qw) // 2
    swa_kv = swa_kvw // HEAD_DIM
    q_rank, kv_rank = q_norm_g.shape[0], kv_norm_g.shape[0]
    mla_heads = w_uq.shape[1] // (MLA_NOPE_DIM + MLA_ROPE_DIM)
    n_experts = w_router.shape[1]

    xf = x.reshape(n, d)
    x_bf = xf.astype(BF16)
    cos_a, sin_a, tab_m = _rope_tables(positions)

    c0 = 3 * fox_w
    c1 = c0 + fox_heads
    c2 = c1 + swa_qw + swa_kvw
    qkv_a = _matmul(x_bf, w_in0[:, :c0].astype(BF16), 2048, _tile(fox_w, 512), "proj_fox",
                    scaled_cols=fox_w, col_scale=HEAD_DIM ** -0.5 * LOG2_E)
    qk_b = _matmul_rope(x_bf, w_in0[:, c1:c2].astype(BF16), cos_a, sin_a, 2048, 256, "proj_swa_qk")
    v_b = _matmul(x_bf, w_in0[:, c2:].astype(BF16), 2048, 512, "proj_swa_v")
    cum_log_f = _fox_cum_log_forget(x_bf, w_in0[:, c0:c1], b_forget, batch, seq)
    o_a = _causal_attention(qkv_a, qkv_a, qkv_a, 0, fox_heads, 2 * fox_heads, HEAD_DIM, HEAD_DIM,
                            fox_heads, batch, seq, ck=cum_log_f, name="fox_attn")
    o_b = _swa_attention(qk_b, v_b, sinks_b, swa_q, swa_kv, batch, seq, HEAD_DIM ** -0.5)
    w_out0_bf = w_out0.astype(BF16)
    h1, h1_bf = _matmul_residual_ln([(o_a, w_out0_bf[:fox_w]), (o_b, w_out0_bf[fox_w:])], xf,
                                    ln0_mix_g, ln0_mix_b, True, "out0_ln")
    h2, h2_bf = _swiglu_residual_ln(h1_bf, w_ffn_gate.astype(BF16), w_ffn_up.astype(BF16),
                                    w_ffn_down.astype(BF16), h1, ln0_ffn_g, ln0_ffn_b)

    r0 = q_rank + kv_rank
    w_in1_x = jnp.concatenate([w_in1, _rotate_half_cols(w_in1[:, r0:])], axis=1).astype(BF16)
    w_uq_h = w_uq.reshape(q_rank, mla_heads, MLA_NOPE_DIM + MLA_ROPE_DIM)
    w_uq_x = jnp.concatenate([w_uq_h, _rotate_half_cols(w_uq_h[..., MLA_NOPE_DIM:])], axis=-1)
    q_m, k_m, v_m = _mla_projections(
        h2_bf, w_in1_x, q_norm_g, kv_norm_g, tab_m,
        w_uq_x.reshape(q_rank, mla_heads * MLA_QK_PAD).astype(BF16), w_ukv.astype(BF16),
        q_rank, kv_rank, mla_heads, (MLA_NOPE_DIM + MLA_ROPE_DIM) ** -0.5 * LOG2_E)
    o_m = _causal_attention(q_m, k_m, v_m, 0, 0, 0, MLA_QK_PAD, MLA_V_DIM, mla_heads, batch, seq,
                            name="mla_attn")
    (h3,) = _matmul_residual_ln([(o_m, w_out1.astype(BF16))], h2, ln1_mix_g, ln1_mix_b, False, "out1_ln")

    idx, gates = _router(h3, w_router)
    moe_tm = _tile(n * TOP_K, 512)
    pos, row_token, tile_expert, tile_valid = _moe_routing(idx[:, :TOP_K], n_experts, moe_tm)
    xg = _gather_rows(h3, row_token, _tile(moe_tm, 256))
    y_rows = _moe_ffn(xg, w_moe_gate.astype(BF16), w_moe_up.astype(BF16), w_moe_down.astype(BF16),
                      tile_expert, tile_valid, moe_tm)
    out = _moe_combine_ln(y_rows, pos, h3, gates, ln1_ffn_g, ln1_ffn_b)
    return out.reshape(batch, seq, d)
```

```python
import functools
import math

import jax
import jax.numpy as jnp
from jax import lax
from jax.experimental import pallas as pl
from jax.experimental.pallas import tpu as pltpu

F32 = jnp.float32
BF16 = jnp.bfloat16

LANE = 128
HEAD_DIM = 128
SWA_WINDOW = 128
MLA_NOPE_DIM = 128
MLA_ROPE_DIM = 64
MLA_V_DIM = 128
MLA_QK_PAD = 256
ROPE_THETA = 10000.0
TOP_K = 2
LN_EPS = 1e-5
RMS_EPS = 1e-6
DEPTH = 2
DEEPNORM_ALPHA = (2 * DEPTH) ** 0.25
LOG2_E = math.log2(math.e)

NT_DIMS = (((1,), (1,)), ((), ()))


def _tile(n, pref):
    if n <= pref:
        return n
    t = (pref // LANE) * LANE
    while n % t:
        t -= LANE
    return t


def _cparams(*sem):
    return pltpu.CompilerParams(dimension_semantics=sem)


def _dot(a, b):
    return jnp.dot(a, b, preferred_element_type=F32)


def _layer_norm(y, g, b):
    mu = jnp.mean(y, axis=-1, keepdims=True)
    d = y - mu
    var = jnp.mean(d * d, axis=-1, keepdims=True)
    return d * lax.rsqrt(var + LN_EPS) * g + b


def _silu(g):
    return g / (1.0 + jnp.exp(-g))


def _rope_table_kernel(pos_ref, freq_ref, cos_a_ref, sin_a_ref, tab_m_ref):
    pos = pos_ref[...]
    ang_a = pos * freq_ref[0:1, :]
    lane = lax.broadcasted_iota(jnp.int32, ang_a.shape, 1)
    cos_a_ref[...] = jnp.cos(ang_a)
    sin_a_ref[...] = jnp.where(lane < HEAD_DIM // 2, -jnp.sin(ang_a), jnp.sin(ang_a))
    ang_m = pos * freq_ref[1:2, :]
    tab_m_ref[...] = jnp.where(lane < MLA_ROPE_DIM, jnp.cos(ang_m), jnp.sin(ang_m))


def _rope_tables(positions):
    n = positions.size
    pos = positions.astype(F32).reshape(n, 1)
    half_a = HEAD_DIM // 2
    half_m = MLA_ROPE_DIM // 2
    inv_a = ROPE_THETA ** (-2.0 * jnp.arange(half_a, dtype=F32) / HEAD_DIM)
    inv_m = ROPE_THETA ** (-2.0 * jnp.arange(half_m, dtype=F32) / MLA_ROPE_DIM)
    freq = jnp.stack([jnp.tile(inv_a, 2), jnp.tile(inv_m, 4)])
    tm = _tile(n, 1024)
    out = jax.ShapeDtypeStruct((n, LANE), F32)
    row = pl.BlockSpec((tm, LANE), lambda i: (i, 0))
    return pl.pallas_call(
        _rope_table_kernel, grid=(n // tm,),
        in_specs=[pl.BlockSpec((tm, 1), lambda i: (i, 0)), pl.BlockSpec((2, LANE), lambda i: (0, 0))],
        out_specs=[row, row, row], out_shape=[out, out, out],
        compiler_params=_cparams("parallel"), name="rope_tables")(pos, freq)


def _mm_kernel(x_ref, w_ref, o_ref, *, scaled_cols, col_scale):
    acc = _dot(x_ref[...], w_ref[...])
    if scaled_cols:
        first_col = pl.program_id(1) * o_ref.shape[1]
        acc = acc * jnp.where(first_col < scaled_cols, col_scale, 1.0)
    o_ref[...] = acc.astype(o_ref.dtype)


def _matmul(x, w, tm, tn, name, scaled_cols=0, col_scale=1.0):
    m, k = x.shape
    n = w.shape[1]
    tm, tn = _tile(m, tm), _tile(n, tn)
    assert scaled_cols % tn == 0
    return pl.pallas_call(
        functools.partial(_mm_kernel, scaled_cols=scaled_cols, col_scale=col_scale),
        grid=(m // tm, n // tn),
        in_specs=[pl.BlockSpec((tm, k), lambda i, j: (i, 0)), pl.BlockSpec((k, tn), lambda i, j: (0, j))],
        out_specs=pl.BlockSpec((tm, tn), lambda i, j: (i, j)),
        out_shape=jax.ShapeDtypeStruct((m, n), BF16),
        compiler_params=_cparams("parallel", "arbitrary"), name=name)(x, w)


def _mm_rope_kernel(x_ref, w_ref, cos_ref, sin_ref, o_ref):
    acc = _dot(x_ref[...], w_ref[...])
    cos, sin = cos_ref[...], sin_ref[...]
    for c in range(acc.shape[1] // HEAD_DIM):
        a = acc[:, c * HEAD_DIM:(c + 1) * HEAD_DIM]
        o_ref[:, c * HEAD_DIM:(c + 1) * HEAD_DIM] = (
            a * cos + pltpu.roll(a, HEAD_DIM // 2, 1) * sin).astype(o_ref.dtype)


def _matmul_rope(x, w, cos, sin, tm, tn, name):
    m, k = x.shape
    n = w.shape[1]
    tm, tn = _tile(m, tm), _tile(n, tn)
    return pl.pallas_call(
        _mm_rope_kernel, grid=(m // tm, n // tn),
        in_specs=[pl.BlockSpec((tm, k), lambda i, j: (i, 0)), pl.BlockSpec((k, tn), lambda i, j: (0, j)),
                  pl.BlockSpec((tm, LANE), lambda i, j: (i, 0)), pl.BlockSpec((tm, LANE), lambda i, j: (i, 0))],
        out_specs=pl.BlockSpec((tm, tn), lambda i, j: (i, j)),
        out_shape=jax.ShapeDtypeStruct((m, n), BF16),
        compiler_params=_cparams("parallel", "arbitrary"), name=name)(x, w, cos, sin)


def _split3(x):
    hi = x.astype(BF16)
    r = x - hi.astype(F32)
    mid = r.astype(BF16)
    lo = (r - mid.astype(F32)).astype(BF16)
    return hi, mid, lo


def _fox_gate_kernel(x_ref, wf_ref, bf_ref, o_ref, carry_ref, *, n_heads):
    @pl.when(pl.program_id(1) == 0)
    def _():
        carry_ref[...] = jnp.zeros_like(carry_ref)

    f = lax.dot_general(wf_ref[...], x_ref[...], NT_DIMS, preferred_element_type=F32)
    z = f + bf_ref[...]
    log_f = jnp.minimum(z, 0.0) - jnp.log1p(jnp.exp(-jnp.abs(z)))
    tc = z.shape[1]
    src = lax.broadcasted_iota(jnp.int32, (tc, tc), 0)
    dst = lax.broadcasted_iota(jnp.int32, (tc, tc), 1)
    tri = jnp.where(src <= dst, 1.0, 0.0).astype(BF16)
    hi, mid, lo = _split3(log_f)
    cum = _dot(hi, tri) + _dot(mid, tri) + _dot(lo, tri) + carry_ref[:, 0:1]
    o_ref[0] = cum[:n_heads] * LOG2_E
    carry_ref[...] = jnp.broadcast_to(cum[:, tc - 1:tc], carry_ref.shape)


def _fox_cum_log_forget(x_bf, w_f, b_forget, batch, seq):
    n, d = x_bf.shape
    n_heads = w_f.shape[1]
    rows = 16
    wf_t = jnp.zeros((rows, d), BF16).at[:n_heads].set(w_f.T.astype(BF16))
    bf = jnp.zeros((rows, 1), F32).at[:n_heads, 0].set(b_forget.astype(F32))
    tc = _tile(seq, 512)
    nc = seq // tc
    return pl.pallas_call(
        functools.partial(_fox_gate_kernel, n_heads=n_heads), grid=(batch, nc),
        in_specs=[pl.BlockSpec((tc, d), lambda b, s: (b * nc + s, 0)),
                  pl.BlockSpec((rows, d), lambda b, s: (0, 0)),
                  pl.BlockSpec((rows, 1), lambda b, s: (0, 0))],
        out_specs=pl.BlockSpec((1, n_heads, tc), lambda b, s: (b, 0, s)),
        out_shape=jax.ShapeDtypeStruct((batch, n_heads, seq), F32),
        scratch_shapes=[pltpu.VMEM((rows, LANE), F32)],
        compiler_params=_cparams("parallel", "arbitrary"), name="fox_gate")(x_bf, wf_t, bf)


def _flash_kernel(*refs, tk, dk, dv, heads, has_bias):
    if has_bias:
        q_ref, k_ref, v_ref, ck_ref, o_ref = refs
    else:
        q_ref, k_ref, v_ref, o_ref = refs
    qi = pl.program_id(2)
    tq = q_ref.shape[0]
    qs = [q_ref[:, g * dk:(g + 1) * dk] for g in range(heads)]

    def step(g, state, j, masked):
        m, l, acc = state
        rows = pl.ds(pl.multiple_of(j * tk, tk), tk)
        s = lax.dot_general(qs[g], k_ref[rows, g * dk:(g + 1) * dk], NT_DIMS, preferred_element_type=F32)
        if has_bias:
            s = s - ck_ref[0, g, j]
        if masked:
            row = lax.broadcasted_iota(jnp.int32, s.shape, 0)
            col = lax.broadcasted_iota(jnp.int32, s.shape, 1)
            s = jnp.where(row >= col, s, -jnp.inf)
        m_new = jnp.maximum(m, jnp.max(s, axis=-1, keepdims=True))
        a = jnp.exp2(m - m_new)
        p = jnp.exp2(s - m_new)
        l = a * l + jnp.sum(p, axis=-1, keepdims=True)
        acc = a * acc + _dot(p.astype(BF16), v_ref[rows, g * dv:(g + 1) * dv])
        return m_new, l, acc

    def body(j, carry):
        return tuple(step(g, carry[g], j, False) for g in range(heads))

    init = (jnp.full((tq, 1), -jnp.inf, F32), jnp.zeros((tq, 1), F32), jnp.zeros((tq, dv), F32))
    carry = lax.fori_loop(0, qi, body, (init,) * heads)
    for g in range(heads):
        _, l, acc = step(g, carry[g], qi, True)
        o_ref[:, g * dv:(g + 1) * dv] = (acc / l).astype(o_ref.dtype)


def _causal_attention(q_arr, k_arr, v_arr, q_col, k_col, v_col, dk, dv, n_heads, batch, seq,
                      ck=None, name="attn"):
    heads = 4
    assert n_heads % heads == 0 and q_col % heads == 0 and k_col % heads == 0 and v_col % heads == 0
    t = _tile(seq, 512)
    nq = seq // t
    in_specs = [pl.BlockSpec((t, heads * dk), lambda b, h, i: (b * nq + i, q_col // heads + h)),
                pl.BlockSpec((seq, heads * dk), lambda b, h, i: (b, k_col // heads + h)),
                pl.BlockSpec((seq, heads * dv), lambda b, h, i: (b, v_col // heads + h))]
    args = [q_arr, k_arr, v_arr]
    if ck is not None:
        args.append(ck.reshape(batch, n_heads, nq, 1, t))
        in_specs.append(pl.BlockSpec((1, heads, nq, 1, t), lambda b, h, i: (b, h, 0, 0, 0)))
    return pl.pallas_call(
        functools.partial(_flash_kernel, tk=t, dk=dk, dv=dv, heads=heads, has_bias=ck is not None),
        grid=(batch, n_heads // heads, nq), in_specs=in_specs,
        out_specs=pl.BlockSpec((t, heads * dv), lambda b, h, i: (b * nq + i, h)),
        out_shape=jax.ShapeDtypeStruct((batch * seq, n_heads * dv), BF16),
        compiler_params=_cparams("parallel", "parallel", "arbitrary"), name=name)(*args)


def _swa_kernel(sink_ref, q_ref, kp_ref, kc_ref, vp_ref, vc_ref, o_ref, *, scale, n_q, n_kv):
    w = SWA_WINDOW
    qi = lax.broadcasted_iota(jnp.int32, (w, 2 * w), 0)
    ji = lax.broadcasted_iota(jnp.int32, (w, 2 * w), 1)
    has_prev = pl.program_id(1) > 0
    mask = (ji > qi) & (ji <= qi + w) & ((ji >= w) | has_prev)
    group = n_q // n_kv
    for kvh in range(n_kv):
        cols = slice(kvh * HEAD_DIM, (kvh + 1) * HEAD_DIM)
        k = jnp.concatenate([kp_ref[:, cols], kc_ref[:, cols]], axis=0)
        v = jnp.concatenate([vp_ref[:, cols], vc_ref[:, cols]], axis=0)
        for g in range(group):
            h = kvh * group + g
            hc = slice(h * HEAD_DIM, (h + 1) * HEAD_DIM)
            s = lax.dot_general(q_ref[:, hc], k, NT_DIMS, preferred_element_type=F32) * scale
            s = jnp.where(mask, s, -jnp.inf)
            sink = sink_ref[h]
            m = jnp.maximum(jnp.max(s, axis=-1, keepdims=True), sink)
            p = jnp.exp(s - m)
            denom = jnp.sum(p, axis=-1, keepdims=True) + jnp.exp(sink - m)
            o_ref[:, hc] = (_dot(p.astype(BF16), v) / denom).astype(o_ref.dtype)


def _swa_attention(qk, v, sinks, n_q, n_kv, batch, seq, scale):
    w = SWA_WINDOW
    nb = seq // w
    qw, kw = n_q * HEAD_DIM, n_kv * HEAD_DIM
    k_col = qw // kw
    cur = lambda b, n: b * nb + n
    prev = lambda b, n: b * nb + jnp.maximum(n - 1, 0)
    return pl.pallas_call(
        functools.partial(_swa_kernel, scale=scale, n_q=n_q, n_kv=n_kv), grid=(batch, nb),
        in_specs=[pl.BlockSpec(memory_space=pltpu.SMEM),
                  pl.BlockSpec((w, qw), lambda b, n: (cur(b, n), 0)),
                  pl.BlockSpec((w, kw), lambda b, n: (prev(b, n), k_col)),
                  pl.BlockSpec((w, kw), lambda b, n: (cur(b, n), k_col)),
                  pl.BlockSpec((w, kw), lambda b, n: (prev(b, n), 0)),
                  pl.BlockSpec((w, kw), lambda b, n: (cur(b, n), 0))],
        out_specs=pl.BlockSpec((w, qw), lambda b, n: (cur(b, n), 0)),
        out_shape=jax.ShapeDtypeStruct((batch * seq, qw), BF16),
        compiler_params=_cparams("parallel", "arbitrary"), name="swa_attn")(
            sinks.astype(F32), qk, qk, qk, v, v)


def _mm_res_ln_kernel(*refs, n_pairs):
    a_refs, w_refs = refs[:n_pairs], refs[n_pairs:2 * n_pairs]
    res_ref, g_ref, b_ref, o_ref = refs[2 * n_pairs:]
    tm = o_ref.shape[0]
    sub = _tile(tm, 256)
    for r in range(tm // sub):
        rows = slice(r * sub, (r + 1) * sub)
        acc = _dot(a_refs[0][rows, :], w_refs[0][...])
        for a_ref, w_ref in zip(a_refs[1:], w_refs[1:]):
            acc = acc + _dot(a_ref[rows, :], w_ref[...])
        o_ref[rows, :] = _layer_norm(DEEPNORM_ALPHA * res_ref[rows, :] + acc, g_ref[...], b_ref[...])


def _matmul_residual_ln(pairs, res, g, b, name):
    m, d = res.shape
    tm = _tile(m, 512)
    row = lambda i: (i, 0)
    fixed = lambda i: (0, 0)
    in_specs = ([pl.BlockSpec((tm, a.shape[1]), row) for a, _ in pairs]
                + [pl.BlockSpec(w.shape, fixed, pipeline_mode=pl.Buffered(1)) for _, w in pairs]
                + [pl.BlockSpec((tm, d), row), pl.BlockSpec((1, d), fixed), pl.BlockSpec((1, d), fixed)])
    args = [a for a, _ in pairs] + [w for _, w in pairs] + [res, g.reshape(1, d), b.reshape(1, d)]
    return pl.pallas_call(
        functools.partial(_mm_res_ln_kernel, n_pairs=len(pairs)),
        grid=(m // tm,), in_specs=in_specs, out_specs=pl.BlockSpec((tm, d), row),
        out_shape=jax.ShapeDtypeStruct((m, d), F32),
        compiler_params=_cparams("parallel"), name=name)(*args)


def _ffn_ln_kernel(h_ref, wg_ref, wu_ref, wd_ref, g_ref, b_ref, o_ref, x_ref):
    j = pl.program_id(1)

    @pl.when(j == 0)
    def _():
        x_ref[...] = h_ref[...].astype(BF16)
        o_ref[...] = jnp.zeros_like(o_ref)

    x = x_ref[...]
    a = _silu(_dot(x, wg_ref[...])) * _dot(x, wu_ref[...])
    o_ref[...] += _dot(a.astype(BF16), wd_ref[...])

    @pl.when(j == pl.num_programs(1) - 1)
    def _():
        o_ref[...] = _layer_norm(DEEPNORM_ALPHA * h_ref[...] + o_ref[...], g_ref[...], b_ref[...])


def _swiglu_residual_ln(h, wg, wu, wd, g, b):
    m, d = h.shape
    f = wg.shape[1]
    tm, tf = _tile(m, 1024), _tile(f, 512)
    row = lambda i, j: (i, 0)
    fixed = lambda i, j: (0, 0)
    return pl.pallas_call(
        _ffn_ln_kernel, grid=(m // tm, f // tf),
        in_specs=[pl.BlockSpec((tm, d), row, pipeline_mode=pl.Buffered(1)),
                  pl.BlockSpec((d, tf), lambda i, j: (0, j)), pl.BlockSpec((d, tf), lambda i, j: (0, j)),
                  pl.BlockSpec((tf, d), lambda i, j: (j, 0)),
                  pl.BlockSpec((1, d), fixed), pl.BlockSpec((1, d), fixed)],
        out_specs=pl.BlockSpec((tm, d), row),
        out_shape=jax.ShapeDtypeStruct((m, d), F32),
        scratch_shapes=[pltpu.VMEM((tm, d), BF16)],
        compiler_params=_cparams("parallel", "arbitrary"), name="ffn_ln")(
            h, wg, wu, wd, g.reshape(1, d), b.reshape(1, d))


def _rms_norm(c, g):
    ms = jnp.mean(c * c, axis=-1, keepdims=True)
    return c * lax.rsqrt(ms + RMS_EPS) * g


def _rope64(pair, tab):
    w = pair * tab
    lane = lax.broadcasted_iota(jnp.int32, w.shape, 1)
    return jnp.where(lane < MLA_ROPE_DIM, w + pltpu.roll(w, MLA_ROPE_DIM, 1), 0.0)


def _mla_proj_kernel(x_ref, win_ref, gq_ref, gkv_ref, tab_ref, wq_ref, wkv_ref, q_ref, k_ref, v_ref, *,
                     q_rank, kv_rank, n_heads, q_scale):
    acc = _dot(x_ref[...].astype(BF16), win_ref[...])
    cq = _rms_norm(acc[:, :q_rank], gq_ref[...]).astype(BF16)
    ckv = _rms_norm(acc[:, q_rank:q_rank + kv_rank], gkv_ref[...]).astype(BF16)
    tab = tab_ref[...]
    k_pe = _rope64(acc[:, q_rank + kv_rank:], tab).astype(BF16)
    wkv_w = MLA_NOPE_DIM + MLA_V_DIM
    for h in range(n_heads):
        q0 = h * MLA_QK_PAD
        qh = _dot(cq, wq_ref[:, q0:q0 + MLA_QK_PAD]) * q_scale
        q_ref[:, q0:q0 + MLA_NOPE_DIM] = qh[:, :MLA_NOPE_DIM].astype(BF16)
        q_ref[:, q0 + MLA_NOPE_DIM:q0 + MLA_QK_PAD] = _rope64(qh[:, MLA_NOPE_DIM:], tab).astype(BF16)
        kvh = _dot(ckv, wkv_ref[:, h * wkv_w:(h + 1) * wkv_w])
        k_ref[:, q0:q0 + MLA_NOPE_DIM] = kvh[:, :MLA_NOPE_DIM].astype(BF16)
        k_ref[:, q0 + MLA_NOPE_DIM:q0 + MLA_QK_PAD] = k_pe
        v_ref[:, h * MLA_V_DIM:(h + 1) * MLA_V_DIM] = kvh[:, MLA_NOPE_DIM:].astype(BF16)


def _mla_projections(x_bf, w_in, gq, gkv, tab, w_q, w_kv, q_rank, kv_rank, n_heads, q_scale):
    m, d = x_bf.shape
    tm = _tile(m, 512)
    row = lambda i: (i, 0)
    fixed = lambda i: (0, 0)
    resident = lambda shape: pl.BlockSpec(shape, fixed, pipeline_mode=pl.Buffered(1))
    qk_w, v_w = n_heads * MLA_QK_PAD, n_heads * MLA_V_DIM
    return pl.pallas_call(
        functools.partial(_mla_proj_kernel, q_rank=q_rank, kv_rank=kv_rank, n_heads=n_heads, q_scale=q_scale),
        grid=(m // tm,),
        in_specs=[pl.BlockSpec((tm, d), row), resident(w_in.shape),
                  pl.BlockSpec((1, q_rank), fixed), pl.BlockSpec((1, kv_rank), fixed),
                  pl.BlockSpec((tm, LANE), row), resident(w_q.shape), resident(w_kv.shape)],
        out_specs=[pl.BlockSpec((tm, qk_w), row), pl.BlockSpec((tm, qk_w), row), pl.BlockSpec((tm, v_w), row)],
        out_shape=[jax.ShapeDtypeStruct((m, qk_w), BF16), jax.ShapeDtypeStruct((m, qk_w), BF16),
                   jax.ShapeDtypeStruct((m, v_w), BF16)],
        compiler_params=_cparams("parallel"), name="mla_proj")(
            x_bf, w_in, gq.reshape(1, q_rank).astype(F32), gkv.reshape(1, kv_rank).astype(F32), tab, w_q, w_kv)


def _router_kernel(h_ref, w_ref, idx_ref, gate_ref, *, n_experts):
    x = h_ref[...]
    xh = x.astype(BF16)
    xl = (x - xh.astype(F32)).astype(BF16)
    w = w_ref[...]
    wh = w.astype(BF16)
    wl = (w - wh.astype(F32)).astype(BF16)
    logits = _dot(xh, wh) + _dot(xh, wl) + _dot(xl, wh)
    lane = lax.broadcasted_iota(jnp.int32, logits.shape, 1)
    lane_f = lane.astype(F32)
    l1 = jnp.where(lane < n_experts, logits, -jnp.inf)
    v1 = jnp.max(l1, axis=-1, keepdims=True)
    i1 = jnp.min(jnp.where(l1 == v1, lane_f, float(LANE)), axis=-1, keepdims=True)
    l2 = jnp.where(lane_f == i1, -jnp.inf, l1)
    v2 = jnp.max(l2, axis=-1, keepdims=True)
    i2 = jnp.min(jnp.where(l2 == v2, lane_f, float(LANE)), axis=-1, keepdims=True)
    e2 = jnp.exp(v2 - v1)
    idx_ref[...] = jnp.where(lane == 0, i1, jnp.where(lane == 1, i2, 0.0)).astype(jnp.int32)
    gate_ref[...] = jnp.where(lane == 0, 1.0 / (1.0 + e2), jnp.where(lane == 1, e2 / (1.0 + e2), 0.0))


def _router(h, w_router):
    m, d = h.shape
    e = w_router.shape[1]
    w_pad = jnp.zeros((d, LANE), F32).at[:, :e].set(w_router.astype(F32))
    tm = _tile(m, 512)
    row = lambda i: (i, 0)
    return pl.pallas_call(
        functools.partial(_router_kernel, n_experts=e), grid=(m // tm,),
        in_specs=[pl.BlockSpec((tm, d), row), pl.BlockSpec((d, LANE), lambda i: (0, 0))],
        out_specs=[pl.BlockSpec((tm, LANE), row), pl.BlockSpec((tm, LANE), row)],
        out_shape=[jax.ShapeDtypeStruct((m, LANE), jnp.int32), jax.ShapeDtypeStruct((m, LANE), F32)],
        compiler_params=_cparams("parallel"), name="router")(h, w_pad)


GATHER_ISSUE_UNROLL = 8


def _row_copy(src_hbm, buf, sem, src_row, dst_row):
    return pltpu.make_async_copy(src_hbm.at[pl.ds(src_row, 1), :], buf.at[pl.ds(dst_row, 1), :], sem)


def _wait_all_rows(src_hbm, buf, sem):
    pltpu.make_async_copy(src_hbm.at[pl.ds(0, buf.shape[0]), :], buf, sem).wait()


def _gather_rows_kernel(tok_ref, x_hbm, o_ref, buf, sem):
    rows = buf.shape[1]
    i, n = pl.program_id(0), pl.num_programs(0)

    def issue(step, slot):
        def start(r, _):
            _row_copy(x_hbm, buf.at[slot], sem.at[slot], tok_ref[step * rows + r], r).start()
            return 0
        lax.fori_loop(0, rows, start, 0, unroll=GATHER_ISSUE_UNROLL)

    @pl.when(i == 0)
    def _():
        issue(0, 0)

    @pl.when(i + 1 < n)
    def _():
        issue(i + 1, (i + 1) % 2)

    slot = i % 2
    _wait_all_rows(x_hbm, buf.at[slot], sem.at[slot])
    o_ref[...] = buf[slot].astype(o_ref.dtype)


def _gather_rows(h, row_token, rows_per_step):
    n_rows = row_token.shape[0]
    d = h.shape[1]
    return pl.pallas_call(
        _gather_rows_kernel,
        grid_spec=pltpu.PrefetchScalarGridSpec(
            num_scalar_prefetch=1, grid=(n_rows // rows_per_step,),
            in_specs=[pl.BlockSpec(memory_space=pl.ANY)],
            out_specs=pl.BlockSpec((rows_per_step, d), lambda i, tok: (i, 0)),
            scratch_shapes=[pltpu.VMEM((2, rows_per_step, d), F32), pltpu.SemaphoreType.DMA((2,))]),
        out_shape=jax.ShapeDtypeStruct((n_rows, d), BF16),
        compiler_params=_cparams("arbitrary"), name="moe_gather")(row_token, h)


def _moe_ffn_kernel(te_ref, tv_ref, x_ref, wg_ref, wu_ref, wd_ref, o_ref):
    i, j = pl.program_id(0), pl.program_id(1)

    @pl.when(j == 0)
    def _():
        o_ref[...] = jnp.zeros_like(o_ref)

    @pl.when(tv_ref[i] == 1)
    def _():
        x = x_ref[...]
        a = _silu(_dot(x, wg_ref[0])) * _dot(x, wu_ref[0])
        o_ref[...] += _dot(a.astype(BF16), wd_ref[0])


def _moe_ffn(xg, wg, wu, wd, tile_expert, tile_valid, tm):
    n_rows, d = xg.shape
    f = wg.shape[2]
    tf = _tile(f, 1024)
    nj = f // tf
    col = lambda i, j, te, tv: jnp.where(tv[i] == 1, j, nj - 1)
    return pl.pallas_call(
        _moe_ffn_kernel,
        grid_spec=pltpu.PrefetchScalarGridSpec(
            num_scalar_prefetch=2, grid=(n_rows // tm, nj),
            in_specs=[pl.BlockSpec((tm, d), lambda i, j, te, tv: (i, 0)),
                      pl.BlockSpec((1, d, tf), lambda i, j, te, tv: (te[i], 0, col(i, j, te, tv))),
                      pl.BlockSpec((1, d, tf), lambda i, j, te, tv: (te[i], 0, col(i, j, te, tv))),
                      pl.BlockSpec((1, tf, d), lambda i, j, te, tv: (te[i], col(i, j, te, tv), 0))],
            out_specs=pl.BlockSpec((tm, d), lambda i, j, te, tv: (i, 0))),
        out_shape=jax.ShapeDtypeStruct((n_rows, d), F32),
        compiler_params=_cparams("arbitrary", "arbitrary"), name="moe_ffn")(
            tile_expert, tile_valid, xg, wg, wu, wd)


def _combine_ln_kernel(pos_ref, h_ref, gate_ref, g_ref, b_ref, y_hbm, o_ref, buf, sem):
    rows = buf.shape[2]
    i, n = pl.program_id(0), pl.num_programs(0)

    def issue(step, slot):
        def start(r, _):
            for k in range(TOP_K):
                src = pos_ref[TOP_K * (step * rows + r) + k]
                _row_copy(y_hbm, buf.at[slot, k], sem.at[slot, k], src, r).start()
            return 0
        lax.fori_loop(0, rows, start, 0, unroll=GATHER_ISSUE_UNROLL // TOP_K)

    @pl.when(i == 0)
    def _():
        issue(0, 0)

    @pl.when(i + 1 < n)
    def _():
        issue(i + 1, (i + 1) % 2)

    slot = i % 2
    gate = gate_ref[...]
    y = None
    for k in range(TOP_K):
        _wait_all_rows(y_hbm, buf.at[slot, k], sem.at[slot, k])
        term = buf[slot, k] * gate[:, k:k + 1]
        y = term if y is None else y + term
    o_ref[...] = _layer_norm(DEEPNORM_ALPHA * h_ref[...] + y, g_ref[...], b_ref[...])


def _moe_combine_ln(y_rows, pos, h, gates, g, b):
    m, d = h.shape
    tm = _tile(m, 256)
    row = lambda i, p: (i, 0)
    fixed = lambda i, p: (0, 0)
    return pl.pallas_call(
        _combine_ln_kernel,
        grid_spec=pltpu.PrefetchScalarGridSpec(
            num_scalar_prefetch=1, grid=(m // tm,),
            in_specs=[pl.BlockSpec((tm, d), row), pl.BlockSpec((tm, LANE), row),
                      pl.BlockSpec((1, d), fixed), pl.BlockSpec((1, d), fixed),
                      pl.BlockSpec(memory_space=pl.ANY)],
            out_specs=pl.BlockSpec((tm, d), row),
            scratch_shapes=[pltpu.VMEM((2, TOP_K, tm, d), F32), pltpu.SemaphoreType.DMA((2, TOP_K))]),
        out_shape=jax.ShapeDtypeStruct((m, d), F32),
        compiler_params=_cparams("arbitrary"), name="moe_combine_ln")(
            pos, h, gates, g.reshape(1, d), b.reshape(1, d), y_rows)


def _moe_routing(top_idx, n_experts, tm):
    n_assign = top_idx.size
    expert_of = top_idx.reshape(n_assign)
    onehot = (expert_of[:, None] == jnp.arange(n_experts, dtype=jnp.int32)[None, :]).astype(jnp.int32)
    csum = jnp.cumsum(onehot, axis=0)
    rank = jnp.sum(onehot * (csum - 1), axis=1)
    counts = csum[-1]
    padded = ((counts + tm - 1) // tm) * tm
    padded_end = jnp.cumsum(padded)
    padded_start = padded_end - padded
    pos = (padded_start[expert_of] + rank).astype(jnp.int32)
    n_tiles = -(-n_assign // tm) + n_experts
    row_token = jnp.zeros((n_tiles * tm,), jnp.int32).at[pos].set(
        jnp.arange(n_assign, dtype=jnp.int32) // TOP_K)
    tile_start = jnp.arange(n_tiles, dtype=jnp.int32) * tm
    tile_valid = (tile_start < padded_end[-1]).astype(jnp.int32)
    tile_expert = jnp.minimum(jnp.searchsorted(padded_end, tile_start, side='right'), n_experts - 1)
    last_expert = jnp.max(jnp.where(tile_valid == 1, tile_expert, 0))
    tile_expert = jnp.where(tile_valid == 1, tile_expert, last_expert).astype(jnp.int32)
    return pos, row_token, tile_expert, tile_valid


def _rotate_half_cols(w):
    half = w.shape[-1] // 2
    return jnp.concatenate([-w[..., half:], w[..., :half]], axis=-1)


def kernel(x, positions, w_in0, b_forget, sinks_b, w_out0, ln0_mix_g, ln0_mix_b, w_ffn_gate, w_ffn_up, w_ffn_down, ln0_ffn_g, ln0_ffn_b, w_in1, q_norm_g, w_uq, kv_norm_g, w_ukv, w_out1, ln1_mix_g, ln1_mix_b, w_router, w_moe_gate, w_moe_up, w_moe_down, ln1_ffn_g, ln1_ffn_b):
    batch, seq, d = x.shape
    n = batch * seq
    fox_heads = b_forget.shape[0]
    fox_w = fox_heads * HEAD_DIM
    swa_q = sinks_b.shape[0]
    swa_qw = swa_q * HEAD_DIM
    swa_kvw = (w_in0.shape[1] - 3 * fox_w - fox_heads - swa_qw) // 2
    swa_kv = swa_kvw // HEAD_DIM
    q_rank, kv_rank = q_norm_g.shape[0], kv_norm_g.shape[0]
    mla_heads = w_uq.shape[1] // (MLA_NOPE_DIM + MLA_ROPE_DIM)
    n_experts = w_router.shape[1]

    xf = x.reshape(n, d)
    x_bf = xf.astype(BF16)
    cos_a, sin_a, tab_m = _rope_tables(positions)

    c0 = 3 * fox_w
    c1 = c0 + fox_heads
    c2 = c1 + swa_qw + swa_kvw
    qkv_a = _matmul(x_bf, w_in0[:, :c0].astype(BF16), 2048, _tile(fox_w, 512), "proj_fox",
                    scaled_cols=fox_w, col_scale=HEAD_DIM ** -0.5 * LOG2_E)
    qk_b = _matmul_rope(x_bf, w_in0[:, c1:c2].astype(BF16), cos_a, sin_a, 2048, 256, "proj_swa_qk")
    v_b = _matmul(x_bf, w_in0[:, c2:].astype(BF16), 2048, 512, "proj_swa_v")
    cum_log_f = _fox_cum_log_forget(x_bf, w_in0[:, c0:c1], b_forget, batch, seq)
    o_a = _causal_attention(qkv_a, qkv_a, qkv_a, 0, fox_heads, 2 * fox_heads, HEAD_DIM, HEAD_DIM,
                            fox_heads, batch, seq, ck=cum_log_f, name="fox_attn")
    o_b = _swa_attention(qk_b, v_b, sinks_b, swa_q, swa_kv, batch, seq, HEAD_DIM ** -0.5)
    w_out0_bf = w_out0.astype(BF16)
    h1 = _matmul_residual_ln([(o_a, w_out0_bf[:fox_w]), (o_b, w_out0_bf[fox_w:])], xf,
                             ln0_mix_g, ln0_mix_b, "out0_ln")
    h2 = _swiglu_residual_ln(h1, w_ffn_gate.astype(BF16), w_ffn_up.astype(BF16), w_ffn_down.astype(BF16),
                             ln0_ffn_g, ln0_ffn_b)

    r0 = q_rank + kv_rank
    w_in1_x = jnp.concatenate([w_in1, _rotate_half_cols(w_in1[:, r0:])], axis=1).astype(BF16)
    w_uq_h = w_uq.reshape(q_rank, mla_heads, MLA_NOPE_DIM + MLA_ROPE_DIM)
    w_uq_x = jnp.concatenate([w_uq_h, _rotate_half_cols(w_uq_h[..., MLA_NOPE_DIM:])], axis=-1)
    q_m, k_m, v_m = _mla_projections(
        h2, w_in1_x, q_norm_g, kv_norm_g, tab_m,
        w_uq_x.reshape(q_rank, mla_heads * MLA_QK_PAD).astype(BF16), w_ukv.astype(BF16),
        q_rank, kv_rank, mla_heads, (MLA_NOPE_DIM + MLA_ROPE_DIM) ** -0.5 * LOG2_E)
    o_m = _causal_attention(q_m, k_m, v_m, 0, 0, 0, MLA_QK_PAD, MLA_V_DIM, mla_heads, batch, seq,
                            name="mla_attn")
    h3 = _matmul_residual_ln([(o_m, w_out1.astype(BF16))], h2, ln1_mix_g, ln1_mix_b, "out1_ln")

    idx, gates = _router(h3, w_router)
    moe_tm = _tile(n * TOP_K, 512)
    pos, row_token, tile_expert, tile_valid = _moe_routing(idx[:, :TOP_K], n_experts, moe_tm)
    xg = _gather_rows(h3, row_token, _tile(moe_tm, 256))
    y_rows = _moe_ffn(xg, w_moe_gate.astype(BF16), w_moe_up.astype(BF16), w_moe_down.astype(BF16),
                      tile_expert, tile_valid, moe_tm)
    out = _moe_combine_ln(y_rows, pos, h3, gates, ln1_ffn_g, ln1_ffn_b)
    return out.reshape(batch, seq, d)
```

```python
import functools
import math

import jax
import jax.numpy as jnp
from jax import lax
from jax.experimental import pallas as pl
from jax.experimental.pallas import tpu as pltpu

F32 = jnp.float32
BF16 = jnp.bfloat16

LANE = 128
HEAD_DIM = 128
SWA_WINDOW = 128
MLA_NOPE_DIM = 128
MLA_ROPE_DIM = 64
MLA_V_DIM = 128
MLA_QK_PAD = 256
ROPE_THETA = 10000.0
TOP_K = 2
LN_EPS = 1e-5
RMS_EPS = 1e-6
DEPTH = 2
DEEPNORM_ALPHA = (2 * DEPTH) ** 0.25
LOG2_E = math.log2(math.e)

NT_DIMS = (((1,), (1,)), ((), ()))


def _tile(n, pref):
    if n <= pref:
        return n
    t = (pref // LANE) * LANE
    while n % t:
        t -= LANE
    return t


def _cparams(*sem):
    return pltpu.CompilerParams(dimension_semantics=sem)


def _dot(a, b):
    return jnp.dot(a, b, preferred_element_type=F32)


def _layer_norm(y, g, b):
    mu = jnp.mean(y, axis=-1, keepdims=True)
    d = y - mu
    var = jnp.mean(d * d, axis=-1, keepdims=True)
    return d * lax.rsqrt(var + LN_EPS) * g + b


def _silu(g):
    return g / (1.0 + jnp.exp(-g))


def _rope_table_kernel(pos_ref, freq_ref, cos_a_ref, sin_a_ref, tab_m_ref):
    pos = pos_ref[...]
    ang_a = pos * freq_ref[0:1, :]
    lane = lax.broadcasted_iota(jnp.int32, ang_a.shape, 1)
    cos_a_ref[...] = jnp.cos(ang_a)
    sin_a_ref[...] = jnp.where(lane < HEAD_DIM // 2, -jnp.sin(ang_a), jnp.sin(ang_a))
    ang_m = pos * freq_ref[1:2, :]
    tab_m_ref[...] = jnp.where(lane < MLA_ROPE_DIM, jnp.cos(ang_m), jnp.sin(ang_m))


def _rope_tables(positions):
    n = positions.size
    pos = positions.astype(F32).reshape(n, 1)
    half_a = HEAD_DIM // 2
    half_m = MLA_ROPE_DIM // 2
    inv_a = ROPE_THETA ** (-2.0 * jnp.arange(half_a, dtype=F32) / HEAD_DIM)
    inv_m = ROPE_THETA ** (-2.0 * jnp.arange(half_m, dtype=F32) / MLA_ROPE_DIM)
    freq = jnp.stack([jnp.tile(inv_a, 2), jnp.tile(inv_m, 4)])
    tm = _tile(n, 1024)
    out = jax.ShapeDtypeStruct((n, LANE), F32)
    row = pl.BlockSpec((tm, LANE), lambda i: (i, 0))
    return pl.pallas_call(
        _rope_table_kernel, grid=(n // tm,),
        in_specs=[pl.BlockSpec((tm, 1), lambda i: (i, 0)), pl.BlockSpec((2, LANE), lambda i: (0, 0))],
        out_specs=[row, row, row], out_shape=[out, out, out],
        compiler_params=_cparams("parallel"), name="rope_tables")(pos, freq)


def _mm_kernel(x_ref, w_ref, o_ref, *, scaled_cols, col_scale):
    acc = _dot(x_ref[...], w_ref[...])
    if scaled_cols:
        first_col = pl.program_id(1) * o_ref.shape[1]
        acc = acc * jnp.where(first_col < scaled_cols, col_scale, 1.0)
    o_ref[...] = acc.astype(o_ref.dtype)


def _matmul(x, w, tm, tn, name, scaled_cols=0, col_scale=1.0):
    m, k = x.shape
    n = w.shape[1]
    tm, tn = _tile(m, tm), _tile(n, tn)
    assert scaled_cols % tn == 0
    return pl.pallas_call(
        functools.partial(_mm_kernel, scaled_cols=scaled_cols, col_scale=col_scale),
        grid=(m // tm, n // tn),
        in_specs=[pl.BlockSpec((tm, k), lambda i, j: (i, 0)), pl.BlockSpec((k, tn), lambda i, j: (0, j))],
        out_specs=pl.BlockSpec((tm, tn), lambda i, j: (i, j)),
        out_shape=jax.ShapeDtypeStruct((m, n), BF16),
        compiler_params=_cparams("parallel", "arbitrary"), name=name)(x, w)


def _mm_rope_kernel(x_ref, w_ref, cos_ref, sin_ref, o_ref):
    acc = _dot(x_ref[...], w_ref[...])
    cos, sin = cos_ref[...], sin_ref[...]
    for c in range(acc.shape[1] // HEAD_DIM):
        a = acc[:, c * HEAD_DIM:(c + 1) * HEAD_DIM]
        o_ref[:, c * HEAD_DIM:(c + 1) * HEAD_DIM] = (
            a * cos + pltpu.roll(a, HEAD_DIM // 2, 1) * sin).astype(o_ref.dtype)


def _matmul_rope(x, w, cos, sin, tm, tn, name):
    m, k = x.shape
    n = w.shape[1]
    tm, tn = _tile(m, tm), _tile(n, tn)
    return pl.pallas_call(
        _mm_rope_kernel, grid=(m // tm, n // tn),
        in_specs=[pl.BlockSpec((tm, k), lambda i, j: (i, 0)), pl.BlockSpec((k, tn), lambda i, j: (0, j)),
                  pl.BlockSpec((tm, LANE), lambda i, j: (i, 0)), pl.BlockSpec((tm, LANE), lambda i, j: (i, 0))],
        out_specs=pl.BlockSpec((tm, tn), lambda i, j: (i, j)),
        out_shape=jax.ShapeDtypeStruct((m, n), BF16),
        compiler_params=_cparams("parallel", "arbitrary"), name=name)(x, w, cos, sin)


def _split3(x):
    hi = x.astype(BF16)
    r = x - hi.astype(F32)
    mid = r.astype(BF16)
    lo = (r - mid.astype(F32)).astype(BF16)
    return hi, mid, lo


def _fox_gate_kernel(x_ref, wf_ref, bf_ref, o_ref, carry_ref, *, n_heads):
    @pl.when(pl.program_id(1) == 0)
    def _():
        carry_ref[...] = jnp.zeros_like(carry_ref)

    f = lax.dot_general(wf_ref[...], x_ref[...], NT_DIMS, preferred_element_type=F32)
    z = f + bf_ref[...]
    log_f = jnp.minimum(z, 0.0) - jnp.log1p(jnp.exp(-jnp.abs(z)))
    tc = z.shape[1]
    src = lax.broadcasted_iota(jnp.int32, (tc, tc), 0)
    dst = lax.broadcasted_iota(jnp.int32, (tc, tc), 1)
    tri = jnp.where(src <= dst, 1.0, 0.0).astype(BF16)
    hi, mid, lo = _split3(log_f)
    cum = _dot(hi, tri) + _dot(mid, tri) + _dot(lo, tri) + carry_ref[:, 0:1]
    o_ref[0] = cum[:n_heads] * LOG2_E
    carry_ref[...] = jnp.broadcast_to(cum[:, tc - 1:tc], carry_ref.shape)


def _fox_cum_log_forget(x_bf, w_f, b_forget, batch, seq):
    n, d = x_bf.shape
    n_heads = w_f.shape[1]
    rows = 16
    wf_t = jnp.zeros((rows, d), BF16).at[:n_heads].set(w_f.T.astype(BF16))
    bf = jnp.zeros((rows, 1), F32).at[:n_heads, 0].set(b_forget.astype(F32))
    tc = _tile(seq, 512)
    nc = seq // tc
    return pl.pallas_call(
        functools.partial(_fox_gate_kernel, n_heads=n_heads), grid=(batch, nc),
        in_specs=[pl.BlockSpec((tc, d), lambda b, s: (b * nc + s, 0)),
                  pl.BlockSpec((rows, d), lambda b, s: (0, 0)),
                  pl.BlockSpec((rows, 1), lambda b, s: (0, 0))],
        out_specs=pl.BlockSpec((1, n_heads, tc), lambda b, s: (b, 0, s)),
        out_shape=jax.ShapeDtypeStruct((batch, n_heads, seq), F32),
        scratch_shapes=[pltpu.VMEM((rows, LANE), F32)],
        compiler_params=_cparams("parallel", "arbitrary"), name="fox_gate")(x_bf, wf_t, bf)


def _flash_kernel(*refs, tk, dk, dv, heads, has_bias):
    if has_bias:
        q_ref, k_ref, v_ref, ck_ref, o_ref = refs
    else:
        q_ref, k_ref, v_ref, o_ref = refs
    qi = pl.program_id(2)
    tq = q_ref.shape[0]
    qs = [q_ref[:, g * dk:(g + 1) * dk] for g in range(heads)]

    def step(g, state, j, masked):
        m, l, acc = state
        kv_rows = pl.ds(pl.multiple_of(j * tk, tk), tk)
        s = lax.dot_general(qs[g], k_ref[kv_rows, g * dk:(g + 1) * dk], NT_DIMS, preferred_element_type=F32)
        if has_bias:
            s = s - ck_ref[0, g, j]
        if masked:
            row = lax.broadcasted_iota(jnp.int32, s.shape, 0)
            col = lax.broadcasted_iota(jnp.int32, s.shape, 1)
            s = jnp.where(row >= col, s, -jnp.inf)
        m_new = jnp.maximum(m, jnp.max(s, axis=-1, keepdims=True))
        a = jnp.exp2(m - m_new)
        p = jnp.exp2(s - m_new)
        l = a * l + jnp.sum(p, axis=-1, keepdims=True)
        acc = a * acc + _dot(p.astype(BF16), v_ref[kv_rows, g * dv:(g + 1) * dv])
        return m_new, l, acc

    def body(j, carry):
        return tuple(step(g, carry[g], j, False) for g in range(heads))

    init = (jnp.full((tq, 1), -jnp.inf, F32), jnp.zeros((tq, 1), F32), jnp.zeros((tq, dv), F32))
    carry = lax.fori_loop(0, qi, body, (init,) * heads)
    for g in range(heads):
        _, l, acc = step(g, carry[g], qi, True)
        o_ref[:, g * dv:(g + 1) * dv] = (acc / l).astype(o_ref.dtype)


def _causal_attention(q_arr, k_arr, v_arr, q_col, k_col, v_col, dk, dv, n_heads, batch, seq,
                      ck=None, name="attn"):
    heads = 4
    assert n_heads % heads == 0 and q_col % heads == 0 and k_col % heads == 0 and v_col % heads == 0
    t = _tile(seq, 512)
    nq = seq // t
    in_specs = [pl.BlockSpec((t, heads * dk), lambda b, h, i: (b * nq + i, q_col // heads + h)),
                pl.BlockSpec((seq, heads * dk), lambda b, h, i: (b, k_col // heads + h)),
                pl.BlockSpec((seq, heads * dv), lambda b, h, i: (b, v_col // heads + h))]
    args = [q_arr, k_arr, v_arr]
    if ck is not None:
        args.append(ck.reshape(batch, n_heads, nq, 1, t))
        in_specs.append(pl.BlockSpec((1, heads, nq, 1, t), lambda b, h, i: (b, h, 0, 0, 0)))
    return pl.pallas_call(
        functools.partial(_flash_kernel, tk=t, dk=dk, dv=dv, heads=heads, has_bias=ck is not None),
        grid=(batch, n_heads // heads, nq), in_specs=in_specs,
        out_specs=pl.BlockSpec((t, heads * dv), lambda b, h, i: (b * nq + i, h)),
        out_shape=jax.ShapeDtypeStruct((batch * seq, n_heads * dv), BF16),
        compiler_params=_cparams("parallel", "parallel", "arbitrary"), name=name)(*args)


def _swa_kernel(sink_ref, q_ref, kp_ref, kc_ref, vp_ref, vc_ref, o_ref, *, scale, n_q, n_kv):
    w = SWA_WINDOW
    qi = lax.broadcasted_iota(jnp.int32, (w, 2 * w), 0)
    ji = lax.broadcasted_iota(jnp.int32, (w, 2 * w), 1)
    has_prev = pl.program_id(1) > 0
    mask = (ji > qi) & (ji <= qi + w) & ((ji >= w) | has_prev)
    group = n_q // n_kv
    for kvh in range(n_kv):
        cols = slice(kvh * HEAD_DIM, (kvh + 1) * HEAD_DIM)
        k = jnp.concatenate([kp_ref[:, cols], kc_ref[:, cols]], axis=0)
        v = jnp.concatenate([vp_ref[:, cols], vc_ref[:, cols]], axis=0)
        for g in range(group):
            h = kvh * group + g
            hc = slice(h * HEAD_DIM, (h + 1) * HEAD_DIM)
            s = lax.dot_general(q_ref[:, hc], k, NT_DIMS, preferred_element_type=F32) * scale
            s = jnp.where(mask, s, -jnp.inf)
            sink = sink_ref[h]
            m = jnp.maximum(jnp.max(s, axis=-1, keepdims=True), sink)
            p = jnp.exp(s - m)
            denom = jnp.sum(p, axis=-1, keepdims=True) + jnp.exp(sink - m)
            o_ref[:, hc] = (_dot(p.astype(BF16), v) / denom).astype(o_ref.dtype)


def _swa_attention(qk, v, sinks, n_q, n_kv, batch, seq, scale):
    w = SWA_WINDOW
    nb = seq // w
    qw, kw = n_q * HEAD_DIM, n_kv * HEAD_DIM
    k_col = qw // kw
    cur = lambda b, n: b * nb + n
    prev = lambda b, n: b * nb + jnp.maximum(n - 1, 0)
    return pl.pallas_call(
        functools.partial(_swa_kernel, scale=scale, n_q=n_q, n_kv=n_kv), grid=(batch, nb),
        in_specs=[pl.BlockSpec(memory_space=pltpu.SMEM),
                  pl.BlockSpec((w, qw), lambda b, n: (cur(b, n), 0)),
                  pl.BlockSpec((w, kw), lambda b, n: (prev(b, n), k_col)),
                  pl.BlockSpec((w, kw), lambda b, n: (cur(b, n), k_col)),
                  pl.BlockSpec((w, kw), lambda b, n: (prev(b, n), 0)),
                  pl.BlockSpec((w, kw), lambda b, n: (cur(b, n), 0))],
        out_specs=pl.BlockSpec((w, qw), lambda b, n: (cur(b, n), 0)),
        out_shape=jax.ShapeDtypeStruct((batch * seq, qw), BF16),
        compiler_params=_cparams("parallel", "arbitrary"), name="swa_attn")(
            sinks.astype(F32), qk, qk, qk, v, v)


def _mm_res_ln_kernel(*refs, n_pairs):
    a_refs, w_refs = refs[:n_pairs], refs[n_pairs:2 * n_pairs]
    res_ref, g_ref, b_ref, o_ref = refs[2 * n_pairs:]
    tm = o_ref.shape[0]
    sub = _tile(tm, 256)
    for r in range(tm // sub):
        rows = slice(r * sub, (r + 1) * sub)
        acc = _dot(a_refs[0][rows, :], w_refs[0][...])
        for a_ref, w_ref in zip(a_refs[1:], w_refs[1:]):
            acc = acc + _dot(a_ref[rows, :], w_ref[...])
        o_ref[rows, :] = _layer_norm(DEEPNORM_ALPHA * res_ref[rows, :] + acc, g_ref[...], b_ref[...])


def _matmul_residual_ln(pairs, res, g, b, name):
    m, d = res.shape
    tm = _tile(m, 512)
    row = lambda i: (i, 0)
    fixed = lambda i: (0, 0)
    in_specs = ([pl.BlockSpec((tm, a.shape[1]), row) for a, _ in pairs]
                + [pl.BlockSpec(w.shape, fixed, pipeline_mode=pl.Buffered(1)) for _, w in pairs]
                + [pl.BlockSpec((tm, d), row), pl.BlockSpec((1, d), fixed), pl.BlockSpec((1, d), fixed)])
    args = [a for a, _ in pairs] + [w for _, w in pairs] + [res, g.reshape(1, d), b.reshape(1, d)]
    return pl.pallas_call(
        functools.partial(_mm_res_ln_kernel, n_pairs=len(pairs)),
        grid=(m // tm,), in_specs=in_specs, out_specs=pl.BlockSpec((tm, d), row),
        out_shape=jax.ShapeDtypeStruct((m, d), F32),
        compiler_params=_cparams("parallel"), name=name)(*args)


def _swiglu(x, wg, wu, wd):
    a = _silu(_dot(x, wg)) * _dot(x, wu)
    return _dot(a.astype(BF16), wd)


def _ffn_ln_kernel(*refs, has_tail):
    h_ref, wg_ref, wu_ref, wd_ref = refs[:4]
    tail_refs = refs[4:7] if has_tail else ()
    g_ref, b_ref, o_ref, x_ref = refs[-4:]
    j = pl.program_id(1)

    @pl.when(j == 0)
    def _():
        x_ref[...] = h_ref[...].astype(BF16)
        o_ref[...] = jnp.zeros_like(o_ref)

    o_ref[...] += _swiglu(x_ref[...], wg_ref[...], wu_ref[...], wd_ref[...])

    @pl.when(j == pl.num_programs(1) - 1)
    def _():
        y = o_ref[...]
        if has_tail:
            y = y + _swiglu(x_ref[...], *(r[...] for r in tail_refs))
        o_ref[...] = _layer_norm(DEEPNORM_ALPHA * h_ref[...] + y, g_ref[...], b_ref[...])


def _swiglu_residual_ln(h, wg, wu, wd, g, b):
    m, d = h.shape
    f = wg.shape[1]
    tm = _tile(m, 512)
    tf = 1024 if f >= 1024 else f
    nj, tail = f // tf, f % tf
    main = nj * tf
    row = lambda i, j: (i, 0)
    fixed = lambda i, j: (0, 0)
    resident = lambda shape: pl.BlockSpec(shape, fixed, pipeline_mode=pl.Buffered(1))
    in_specs = [pl.BlockSpec((tm, d), row, pipeline_mode=pl.Buffered(1)),
                pl.BlockSpec((d, tf), lambda i, j: (0, j)), pl.BlockSpec((d, tf), lambda i, j: (0, j)),
                pl.BlockSpec((tf, d), lambda i, j: (j, 0))]
    args = [h, wg[:, :main], wu[:, :main], wd[:main]]
    if tail:
        in_specs += [resident((d, tail)), resident((d, tail)), resident((tail, d))]
        args += [wg[:, main:], wu[:, main:], wd[main:]]
    in_specs += [pl.BlockSpec((1, d), fixed), pl.BlockSpec((1, d), fixed)]
    args += [g.reshape(1, d), b.reshape(1, d)]
    return pl.pallas_call(
        functools.partial(_ffn_ln_kernel, has_tail=bool(tail)), grid=(m // tm, nj),
        in_specs=in_specs, out_specs=pl.BlockSpec((tm, d), row),
        out_shape=jax.ShapeDtypeStruct((m, d), F32),
        scratch_shapes=[pltpu.VMEM((tm, d), BF16)],
        compiler_params=_cparams("parallel", "arbitrary"), name="ffn_ln")(*args)


def _rms_norm(c, g):
    ms = jnp.mean(c * c, axis=-1, keepdims=True)
    return c * lax.rsqrt(ms + RMS_EPS) * g


def _rope64(pair, tab):
    w = pair * tab
    lane = lax.broadcasted_iota(jnp.int32, w.shape, 1)
    return jnp.where(lane < MLA_ROPE_DIM, w + pltpu.roll(w, MLA_ROPE_DIM, 1), 0.0)


def _mla_proj_kernel(x_ref, win_ref, gq_ref, gkv_ref, tab_ref, wq_ref, wkv_ref, q_ref, k_ref, v_ref, *,
                     q_rank, kv_rank, n_heads, q_scale):
    acc = _dot(x_ref[...].astype(BF16), win_ref[...])
    cq = _rms_norm(acc[:, :q_rank], gq_ref[...]).astype(BF16)
    ckv = _rms_norm(acc[:, q_rank:q_rank + kv_rank], gkv_ref[...]).astype(BF16)
    tab = tab_ref[...]
    k_pe = _rope64(acc[:, q_rank + kv_rank:], tab).astype(BF16)
    wkv_w = MLA_NOPE_DIM + MLA_V_DIM
    for h in range(n_heads):
        q0 = h * MLA_QK_PAD
        qh = _dot(cq, wq_ref[:, q0:q0 + MLA_QK_PAD]) * q_scale
        q_ref[:, q0:q0 + MLA_NOPE_DIM] = qh[:, :MLA_NOPE_DIM].astype(BF16)
        q_ref[:, q0 + MLA_NOPE_DIM:q0 + MLA_QK_PAD] = _rope64(qh[:, MLA_NOPE_DIM:], tab).astype(BF16)
        kvh = _dot(ckv, wkv_ref[:, h * wkv_w:(h + 1) * wkv_w])
        k_ref[:, q0:q0 + MLA_NOPE_DIM] = kvh[:, :MLA_NOPE_DIM].astype(BF16)
        k_ref[:, q0 + MLA_NOPE_DIM:q0 + MLA_QK_PAD] = k_pe
        v_ref[:, h * MLA_V_DIM:(h + 1) * MLA_V_DIM] = kvh[:, MLA_NOPE_DIM:].astype(BF16)


def _mla_projections(x_bf, w_in, gq, gkv, tab, w_q, w_kv, q_rank, kv_rank, n_heads, q_scale):
    m, d = x_bf.shape
    tm = _tile(m, 512)
    row = lambda i: (i, 0)
    fixed = lambda i: (0, 0)
    resident = lambda shape: pl.BlockSpec(shape, fixed, pipeline_mode=pl.Buffered(1))
    qk_w, v_w = n_heads * MLA_QK_PAD, n_heads * MLA_V_DIM
    return pl.pallas_call(
        functools.partial(_mla_proj_kernel, q_rank=q_rank, kv_rank=kv_rank, n_heads=n_heads, q_scale=q_scale),
        grid=(m // tm,),
        in_specs=[pl.BlockSpec((tm, d), row), resident(w_in.shape),
                  pl.BlockSpec((1, q_rank), fixed), pl.BlockSpec((1, kv_rank), fixed),
                  pl.BlockSpec((tm, LANE), row), resident(w_q.shape), resident(w_kv.shape)],
        out_specs=[pl.BlockSpec((tm, qk_w), row), pl.BlockSpec((tm, qk_w), row), pl.BlockSpec((tm, v_w), row)],
        out_shape=[jax.ShapeDtypeStruct((m, qk_w), BF16), jax.ShapeDtypeStruct((m, qk_w), BF16),
                   jax.ShapeDtypeStruct((m, v_w), BF16)],
        compiler_params=_cparams("parallel"), name="mla_proj")(
            x_bf, w_in, gq.reshape(1, q_rank).astype(F32), gkv.reshape(1, kv_rank).astype(F32), tab, w_q, w_kv)


def _router_kernel(h_ref, w_ref, idx_ref, gate_ref, *, n_experts):
    x = h_ref[...]
    xh = x.astype(BF16)
    xl = (x - xh.astype(F32)).astype(BF16)
    w = w_ref[...]
    wh = w.astype(BF16)
    wl = (w - wh.astype(F32)).astype(BF16)
    logits = _dot(xh, wh) + _dot(xh, wl) + _dot(xl, wh)
    lane = lax.broadcasted_iota(jnp.int32, logits.shape, 1)
    lane_f = lane.astype(F32)
    l1 = jnp.where(lane < n_experts, logits, -jnp.inf)
    v1 = jnp.max(l1, axis=-1, keepdims=True)
    i1 = jnp.min(jnp.where(l1 == v1, lane_f, float(LANE)), axis=-1, keepdims=True)
    l2 = jnp.where(lane_f == i1, -jnp.inf, l1)
    v2 = jnp.max(l2, axis=-1, keepdims=True)
    i2 = jnp.min(jnp.where(l2 == v2, lane_f, float(LANE)), axis=-1, keepdims=True)
    e2 = jnp.exp(v2 - v1)
    idx_ref[...] = jnp.where(lane == 0, i1, jnp.where(lane == 1, i2, 0.0)).astype(jnp.int32)
    gate_ref[...] = jnp.where(lane == 0, 1.0 / (1.0 + e2), jnp.where(lane == 1, e2 / (1.0 + e2), 0.0))


def _router(h, w_router):
    m, d = h.shape
    e = w_router.shape[1]
    w_pad = jnp.zeros((d, LANE), F32).at[:, :e].set(w_router.astype(F32))
    tm = _tile(m, 512)
    row = lambda i: (i, 0)
    return pl.pallas_call(
        functools.partial(_router_kernel, n_experts=e), grid=(m // tm,),
        in_specs=[pl.BlockSpec((tm, d), row), pl.BlockSpec((d, LANE), lambda i: (0, 0))],
        out_specs=[pl.BlockSpec((tm, LANE), row), pl.BlockSpec((tm, LANE), row)],
        out_shape=[jax.ShapeDtypeStruct((m, LANE), jnp.int32), jax.ShapeDtypeStruct((m, LANE), F32)],
        compiler_params=_cparams("parallel"), name="router")(h, w_pad)


GATHER_ISSUE_UNROLL = 8


def _row_copy(src_hbm, buf, sem, src_row, dst_row):
    return pltpu.make_async_copy(src_hbm.at[pl.ds(src_row, 1), :], buf.at[pl.ds(dst_row, 1), :], sem)


def _wait_all_rows(src_hbm, buf, sem):
    pltpu.make_async_copy(src_hbm.at[pl.ds(0, buf.shape[0]), :], buf, sem).wait()


def _gather_rows_kernel(tok_ref, x_hbm, *refs, n_weights, n_slabs):
    w_refs, o_ref, wo_refs = refs[:n_weights], refs[n_weights], refs[n_weights + 1:2 * n_weights + 1]
    buf, sem = refs[2 * n_weights + 1:]
    rows = buf.shape[1]
    i, n = pl.program_id(0), pl.num_programs(0)

    @pl.when(i < n_slabs)
    def _():
        for w_ref, wo_ref in zip(w_refs, wo_refs):
            wo_ref[...] = w_ref[...].astype(wo_ref.dtype)

    def issue(step, slot):
        def start(r, _):
            _row_copy(x_hbm, buf.at[slot], sem.at[slot], tok_ref[step * rows + r], r).start()
            return 0
        lax.fori_loop(0, rows, start, 0, unroll=GATHER_ISSUE_UNROLL)

    @pl.when(i == 0)
    def _():
        issue(0, 0)

    @pl.when(i + 1 < n)
    def _():
        issue(i + 1, (i + 1) % 2)

    slot = i % 2
    _wait_all_rows(x_hbm, buf.at[slot], sem.at[slot])
    o_ref[...] = buf[slot].astype(o_ref.dtype)


BF16_SUBLANES = 16


def _slab_rows(total_rows, max_slabs):
    rows = BF16_SUBLANES
    while total_rows % rows or total_rows // rows > max_slabs:
        rows += BF16_SUBLANES
    return rows


def _gather_rows_cast_weights(h, row_token, rows_per_step, weights):
    n_rows = row_token.shape[0]
    d = h.shape[1]
    n_steps = n_rows // rows_per_step
    flat = [w.reshape(-1, w.shape[-1]) for w in weights]
    slab = [_slab_rows(w.shape[0], n_steps) for w in flat]
    n_slabs = flat[0].shape[0] // slab[0]
    assert all(w.shape[0] // s == n_slabs for w, s in zip(flat, slab))
    slab_spec = lambda w, s: pl.BlockSpec((s, w.shape[1]), lambda i, tok: (jnp.minimum(i, n_slabs - 1), 0))
    outs = pl.pallas_call(
        functools.partial(_gather_rows_kernel, n_weights=len(flat), n_slabs=n_slabs),
        grid_spec=pltpu.PrefetchScalarGridSpec(
            num_scalar_prefetch=1, grid=(n_steps,),
            in_specs=[pl.BlockSpec(memory_space=pl.ANY)] + [slab_spec(w, s) for w, s in zip(flat, slab)],
            out_specs=[pl.BlockSpec((rows_per_step, d), lambda i, tok: (i, 0))]
            + [slab_spec(w, s) for w, s in zip(flat, slab)],
            scratch_shapes=[pltpu.VMEM((2, rows_per_step, d), F32), pltpu.SemaphoreType.DMA((2,))]),
        out_shape=[jax.ShapeDtypeStruct((n_rows, d), BF16)]
        + [jax.ShapeDtypeStruct(w.shape, BF16) for w in flat],
        compiler_params=_cparams("arbitrary"), name="moe_gather")(row_token, h, *flat)
    return outs[0], [o.reshape(w.shape) for o, w in zip(outs[1:], weights)]


def _moe_ffn_kernel(te_ref, tv_ref, x_ref, wg_ref, wu_ref, wd_ref, o_ref):
    i, j = pl.program_id(0), pl.program_id(1)

    @pl.when(j == 0)
    def _():
        o_ref[...] = jnp.zeros_like(o_ref)

    @pl.when(tv_ref[i] == 1)
    def _():
        o_ref[...] += _swiglu(x_ref[...], wg_ref[0], wu_ref[0], wd_ref[0])


def _moe_ffn(xg, wg, wu, wd, tile_expert, tile_valid, tm):
    n_rows, d = xg.shape
    f = wg.shape[2]
    tf = _tile(f, 1024)
    nj = f // tf
    col = lambda i, j, te, tv: jnp.where(tv[i] == 1, j, nj - 1)
    return pl.pallas_call(
        _moe_ffn_kernel,
        grid_spec=pltpu.PrefetchScalarGridSpec(
            num_scalar_prefetch=2, grid=(n_rows // tm, nj),
            in_specs=[pl.BlockSpec((tm, d), lambda i, j, te, tv: (i, 0)),
                      pl.BlockSpec((1, d, tf), lambda i, j, te, tv: (te[i], 0, col(i, j, te, tv))),
                      pl.BlockSpec((1, d, tf), lambda i, j, te, tv: (te[i], 0, col(i, j, te, tv))),
                      pl.BlockSpec((1, tf, d), lambda i, j, te, tv: (te[i], col(i, j, te, tv), 0))],
            out_specs=pl.BlockSpec((tm, d), lambda i, j, te, tv: (i, 0))),
        out_shape=jax.ShapeDtypeStruct((n_rows, d), F32),
        compiler_params=_cparams("arbitrary", "arbitrary"), name="moe_ffn")(
            tile_expert, tile_valid, xg, wg, wu, wd)


def _combine_ln_kernel(pos_ref, h_ref, gate_ref, g_ref, b_ref, y_hbm, o_ref, buf, sem):
    rows = buf.shape[2]
    i, n = pl.program_id(0), pl.num_programs(0)

    def issue(step, slot):
        def start(r, _):
            for k in range(TOP_K):
                src = pos_ref[TOP_K * (step * rows + r) + k]
                _row_copy(y_hbm, buf.at[slot, k], sem.at[slot, k], src, r).start()
            return 0
        lax.fori_loop(0, rows, start, 0, unroll=GATHER_ISSUE_UNROLL // TOP_K)

    @pl.when(i == 0)
    def _():
        issue(0, 0)

    @pl.when(i + 1 < n)
    def _():
        issue(i + 1, (i + 1) % 2)

    slot = i % 2
    gate = gate_ref[...]
    y = None
    for k in range(TOP_K):
        _wait_all_rows(y_hbm, buf.at[slot, k], sem.at[slot, k])
        term = buf[slot, k] * gate[:, k:k + 1]
        y = term if y is None else y + term
    o_ref[...] = _layer_norm(DEEPNORM_ALPHA * h_ref[...] + y, g_ref[...], b_ref[...])


def _moe_combine_ln(y_rows, pos, h, gates, g, b):
    m, d = h.shape
    tm = _tile(m, 256)
    row = lambda i, p: (i, 0)
    fixed = lambda i, p: (0, 0)
    return pl.pallas_call(
        _combine_ln_kernel,
        grid_spec=pltpu.PrefetchScalarGridSpec(
            num_scalar_prefetch=1, grid=(m // tm,),
            in_specs=[pl.BlockSpec((tm, d), row), pl.BlockSpec((tm, LANE), row),
                      pl.BlockSpec((1, d), fixed), pl.BlockSpec((1, d), fixed),
                      pl.BlockSpec(memory_space=pl.ANY)],
            out_specs=pl.BlockSpec((tm, d), row),
            scratch_shapes=[pltpu.VMEM((2, TOP_K, tm, d), F32), pltpu.SemaphoreType.DMA((2, TOP_K))]),
        out_shape=jax.ShapeDtypeStruct((m, d), F32),
        compiler_params=_cparams("arbitrary"), name="moe_combine_ln")(
            pos, h, gates, g.reshape(1, d), b.reshape(1, d), y_rows)


def _moe_routing(top_idx, n_experts, tm):
    n_assign = top_idx.size
    expert_of = top_idx.reshape(n_assign)
    onehot = (expert_of[:, None] == jnp.arange(n_experts, dtype=jnp.int32)[None, :]).astype(jnp.int32)
    csum = jnp.cumsum(onehot, axis=0)
    rank = jnp.sum(onehot * (csum - 1), axis=1)
    counts = csum[-1]
    padded = ((counts + tm - 1) // tm) * tm
    padded_end = jnp.cumsum(padded)
    padded_start = padded_end - padded
    pos = (padded_start[expert_of] + rank).astype(jnp.int32)
    n_tiles = -(-n_assign // tm) + n_experts
    row_token = jnp.zeros((n_tiles * tm,), jnp.int32).at[pos].set(
        jnp.arange(n_assign, dtype=jnp.int32) // TOP_K)
    tile_start = jnp.arange(n_tiles, dtype=jnp.int32) * tm
    tile_valid = (tile_start < padded_end[-1]).astype(jnp.int32)
    tile_expert = jnp.minimum(jnp.searchsorted(padded_end, tile_start, side='right'), n_experts - 1)
    last_expert = jnp.max(jnp.where(tile_valid == 1, tile_expert, 0))
    tile_expert = jnp.where(tile_valid == 1, tile_expert, last_expert).astype(jnp.int32)
    return pos, row_token, tile_expert, tile_valid


def _rotate_half_cols(w):
    half = w.shape[-1] // 2
    return jnp.concatenate([-w[..., half:], w[..., :half]], axis=-1)


def kernel(x, positions, w_in0, b_forget, sinks_b, w_out0, ln0_mix_g, ln0_mix_b, w_ffn_gate, w_ffn_up, w_ffn_down, ln0_ffn_g, ln0_ffn_b, w_in1, q_norm_g, w_uq, kv_norm_g, w_ukv, w_out1, ln1_mix_g, ln1_mix_b, w_router, w_moe_gate, w_moe_up, w_moe_down, ln1_ffn_g, ln1_ffn_b):
    batch, seq, d = x.shape
    n = batch * seq
    fox_heads = b_forget.shape[0]
    fox_w = fox_heads * HEAD_DIM
    swa_q = sinks_b.shape[0]
    swa_qw = swa_q * HEAD_DIM
    swa_kvw = (w_in0.shape[1] - 3 * fox_w - fox_heads - swa_qw) // 2
    swa_kv = swa_kvw // HEAD_DIM
    q_rank, kv_rank = q_norm_g.shape[0], kv_norm_g.shape[0]
    mla_heads = w_uq.shape[1] // (MLA_NOPE_DIM + MLA_ROPE_DIM)
    n_experts = w_router.shape[1]

    xf = x.reshape(n, d)
    x_bf = xf.astype(BF16)
    cos_a, sin_a, tab_m = _rope_tables(positions)

    c0 = 3 * fox_w
    c1 = c0 + fox_heads
    c2 = c1 + swa_qw + swa_kvw
    qkv_a = _matmul(x_bf, w_in0[:, :c0].astype(BF16), 2048, _tile(fox_w, 512), "proj_fox",
                    scaled_cols=fox_w, col_scale=HEAD_DIM ** -0.5 * LOG2_E)
    qk_b = _matmul_rope(x_bf, w_in0[:, c1:c2].astype(BF16), cos_a, sin_a, 2048, 256, "proj_swa_qk")
    v_b = _matmul(x_bf, w_in0[:, c2:].astype(BF16), 2048, 512, "proj_swa_v")
    cum_log_f = _fox_cum_log_forget(x_bf, w_in0[:, c0:c1], b_forget, batch, seq)
    o_a = _causal_attention(qkv_a, qkv_a, qkv_a, 0, fox_heads, 2 * fox_heads, HEAD_DIM, HEAD_DIM,
                            fox_heads, batch, seq, ck=cum_log_f, name="fox_attn")
    o_b = _swa_attention(qk_b, v_b, sinks_b, swa_q, swa_kv, batch, seq, HEAD_DIM ** -0.5)
    w_out0_bf = w_out0.astype(BF16)
    h1 = _matmul_residual_ln([(o_a, w_out0_bf[:fox_w]), (o_b, w_out0_bf[fox_w:])], xf,
                             ln0_mix_g, ln0_mix_b, "out0_ln")
    h2 = _swiglu_residual_ln(h1, w_ffn_gate.astype(BF16), w_ffn_up.astype(BF16), w_ffn_down.astype(BF16),
                             ln0_ffn_g, ln0_ffn_b)

    r0 = q_rank + kv_rank
    w_in1_x = jnp.concatenate([w_in1, _rotate_half_cols(w_in1[:, r0:])], axis=1).astype(BF16)
    w_uq_h = w_uq.reshape(q_rank, mla_heads, MLA_NOPE_DIM + MLA_ROPE_DIM)
    w_uq_x = jnp.concatenate([w_uq_h, _rotate_half_cols(w_uq_h[..., MLA_NOPE_DIM:])], axis=-1)
    q_m, k_m, v_m = _mla_projections(
        h2, w_in1_x, q_norm_g, kv_norm_g, tab_m,
        w_uq_x.reshape(q_rank, mla_heads * MLA_QK_PAD).astype(BF16), w_ukv.astype(BF16),
        q_rank, kv_rank, mla_heads, (MLA_NOPE_DIM + MLA_ROPE_DIM) ** -0.5 * LOG2_E)
    o_m = _causal_attention(q_m, k_m, v_m, 0, 0, 0, MLA_QK_PAD, MLA_V_DIM, mla_heads, batch, seq,
                            name="mla_attn")
    h3 = _matmul_residual_ln([(o_m, w_out1.astype(BF16))], h2, ln1_mix_g, ln1_mix_b, "out1_ln")

    idx, gates = _router(h3, w_router)
    moe_tm = _tile(n * TOP_K, 512)
    pos, row_token, tile_expert, tile_valid = _moe_routing(idx[:, :TOP_K], n_experts, moe_tm)
    xg, (wg_bf, wu_bf, wd_bf) = _gather_rows_cast_weights(
        h3, row_token, _tile(moe_tm, 256), [w_moe_gate, w_moe_up, w_moe_down])
    y_rows = _moe_ffn(xg, wg_bf, wu_bf, wd_bf, tile_expert, tile_valid, moe_tm)
    out = _moe_combine_ln(y_rows, pos, h3, gates, ln1_ffn_g, ln1_ffn_b)
    return out.reshape(batch, seq, d)
```

```python
import functools
import math

import jax
import jax.numpy as jnp
from jax import lax
from jax.experimental import pallas as pl
from jax.experimental.pallas import tpu as pltpu

F32 = jnp.float32
BF16 = jnp.bfloat16

LANE = 128
HEAD_DIM = 128
SWA_WINDOW = 128
MLA_NOPE_DIM = 128
MLA_ROPE_DIM = 64
MLA_V_DIM = 128
MLA_QK_PAD = 256
ROPE_THETA = 10000.0
TOP_K = 2
LN_EPS = 1e-5
RMS_EPS = 1e-6
DEPTH = 2
DEEPNORM_ALPHA = (2 * DEPTH) ** 0.25
LOG2_E = math.log2(math.e)

NT_DIMS = (((1,), (1,)), ((), ()))


def _tile(n, pref):
    if n <= pref:
        return n
    t = (pref // LANE) * LANE
    while n % t:
        t -= LANE
    return t


def _cparams(*sem):
    return pltpu.CompilerParams(dimension_semantics=sem)


def _dot(a, b):
    return jnp.dot(a, b, preferred_element_type=F32)


def _layer_norm(y, g, b):
    mu = jnp.mean(y, axis=-1, keepdims=True)
    d = y - mu
    var = jnp.mean(d * d, axis=-1, keepdims=True)
    return d * lax.rsqrt(var + LN_EPS) * g + b


def _silu(g):
    return g / (1.0 + jnp.exp(-g))


def _rope_table_kernel(pos_ref, freq_ref, cos_a_ref, sin_a_ref, tab_m_ref):
    pos = pos_ref[...]
    ang_a = pos * freq_ref[0:1, :]
    lane = lax.broadcasted_iota(jnp.int32, ang_a.shape, 1)
    cos_a_ref[...] = jnp.cos(ang_a)
    sin_a_ref[...] = jnp.where(lane < HEAD_DIM // 2, -jnp.sin(ang_a), jnp.sin(ang_a))
    ang_m = pos * freq_ref[1:2, :]
    tab_m_ref[...] = jnp.where(lane < MLA_ROPE_DIM, jnp.cos(ang_m), jnp.sin(ang_m))


def _rope_tables(positions):
    n = positions.size
    pos = positions.astype(F32).reshape(n, 1)
    half_a = HEAD_DIM // 2
    half_m = MLA_ROPE_DIM // 2
    inv_a = ROPE_THETA ** (-2.0 * jnp.arange(half_a, dtype=F32) / HEAD_DIM)
    inv_m = ROPE_THETA ** (-2.0 * jnp.arange(half_m, dtype=F32) / MLA_ROPE_DIM)
    freq = jnp.stack([jnp.tile(inv_a, 2), jnp.tile(inv_m, 4)])
    tm = _tile(n, 1024)
    out = jax.ShapeDtypeStruct((n, LANE), F32)
    row = pl.BlockSpec((tm, LANE), lambda i: (i, 0))
    return pl.pallas_call(
        _rope_table_kernel, grid=(n // tm,),
        in_specs=[pl.BlockSpec((tm, 1), lambda i: (i, 0)), pl.BlockSpec((2, LANE), lambda i: (0, 0))],
        out_specs=[row, row, row], out_shape=[out, out, out],
        compiler_params=_cparams("parallel"), name="rope_tables")(pos, freq)


def _mm_kernel(x_ref, w_ref, o_ref, *, scaled_cols, col_scale):
    acc = _dot(x_ref[...], w_ref[...])
    if scaled_cols:
        first_col = pl.program_id(1) * o_ref.shape[1]
        acc = acc * jnp.where(first_col < scaled_cols, col_scale, 1.0)
    o_ref[...] = acc.astype(o_ref.dtype)


def _matmul(x, w, tm, tn, name, scaled_cols=0, col_scale=1.0):
    m, k = x.shape
    n = w.shape[1]
    tm, tn = _tile(m, tm), _tile(n, tn)
    assert scaled_cols % tn == 0
    return pl.pallas_call(
        functools.partial(_mm_kernel, scaled_cols=scaled_cols, col_scale=col_scale),
        grid=(m // tm, n // tn),
        in_specs=[pl.BlockSpec((tm, k), lambda i, j: (i, 0)), pl.BlockSpec((k, tn), lambda i, j: (0, j))],
        out_specs=pl.BlockSpec((tm, tn), lambda i, j: (i, j)),
        out_shape=jax.ShapeDtypeStruct((m, n), BF16),
        compiler_params=_cparams("parallel", "arbitrary"), name=name)(x, w)


def _mm_rope_kernel(x_ref, w_ref, cos_ref, sin_ref, o_ref):
    acc = _dot(x_ref[...], w_ref[...])
    cos, sin = cos_ref[...], sin_ref[...]
    for c in range(acc.shape[1] // HEAD_DIM):
        a = acc[:, c * HEAD_DIM:(c + 1) * HEAD_DIM]
        o_ref[:, c * HEAD_DIM:(c + 1) * HEAD_DIM] = (
            a * cos + pltpu.roll(a, HEAD_DIM // 2, 1) * sin).astype(o_ref.dtype)


def _matmul_rope(x, w, cos, sin, tm, tn, name):
    m, k = x.shape
    n = w.shape[1]
    tm, tn = _tile(m, tm), _tile(n, tn)
    return pl.pallas_call(
        _mm_rope_kernel, grid=(m // tm, n // tn),
        in_specs=[pl.BlockSpec((tm, k), lambda i, j: (i, 0)), pl.BlockSpec((k, tn), lambda i, j: (0, j)),
                  pl.BlockSpec((tm, LANE), lambda i, j: (i, 0)), pl.BlockSpec((tm, LANE), lambda i, j: (i, 0))],
        out_specs=pl.BlockSpec((tm, tn), lambda i, j: (i, j)),
        out_shape=jax.ShapeDtypeStruct((m, n), BF16),
        compiler_params=_cparams("parallel", "arbitrary"), name=name)(x, w, cos, sin)


def _split3(x):
    hi = x.astype(BF16)
    r = x - hi.astype(F32)
    mid = r.astype(BF16)
    lo = (r - mid.astype(F32)).astype(BF16)
    return hi, mid, lo


def _fox_gate_kernel(x_ref, wf_ref, bf_ref, o_ref, carry_ref, *, n_heads):
    @pl.when(pl.program_id(1) == 0)
    def _():
        carry_ref[...] = jnp.zeros_like(carry_ref)

    f = lax.dot_general(wf_ref[...], x_ref[...], NT_DIMS, preferred_element_type=F32)
    z = f + bf_ref[...]
    log_f = jnp.minimum(z, 0.0) - jnp.log1p(jnp.exp(-jnp.abs(z)))
    tc = z.shape[1]
    src = lax.broadcasted_iota(jnp.int32, (tc, tc), 0)
    dst = lax.broadcasted_iota(jnp.int32, (tc, tc), 1)
    tri = jnp.where(src <= dst, 1.0, 0.0).astype(BF16)
    hi, mid, lo = _split3(log_f)
    cum = _dot(hi, tri) + _dot(mid, tri) + _dot(lo, tri) + carry_ref[:, 0:1]
    o_ref[0] = cum[:n_heads] * LOG2_E
    carry_ref[...] = jnp.broadcast_to(cum[:, tc - 1:tc], carry_ref.shape)


def _fox_cum_log_forget(x_bf, w_f, b_forget, batch, seq):
    n, d = x_bf.shape
    n_heads = w_f.shape[1]
    rows = 16
    wf_t = jnp.zeros((rows, d), BF16).at[:n_heads].set(w_f.T.astype(BF16))
    bf = jnp.zeros((rows, 1), F32).at[:n_heads, 0].set(b_forget.astype(F32))
    tc = _tile(seq, 512)
    nc = seq // tc
    return pl.pallas_call(
        functools.partial(_fox_gate_kernel, n_heads=n_heads), grid=(batch, nc),
        in_specs=[pl.BlockSpec((tc, d), lambda b, s: (b * nc + s, 0)),
                  pl.BlockSpec((rows, d), lambda b, s: (0, 0)),
                  pl.BlockSpec((rows, 1), lambda b, s: (0, 0))],
        out_specs=pl.BlockSpec((1, n_heads, tc), lambda b, s: (b, 0, s)),
        out_shape=jax.ShapeDtypeStruct((batch, n_heads, seq), F32),
        scratch_shapes=[pltpu.VMEM((rows, LANE), F32)],
        compiler_params=_cparams("parallel", "arbitrary"), name="fox_gate")(x_bf, wf_t, bf)


def _flash_kernel(*refs, tk, dk, dv, heads, has_bias):
    if has_bias:
        q_ref, k_ref, v_ref, ck_ref, o_ref = refs
    else:
        q_ref, k_ref, v_ref, o_ref = refs
    qi = pl.program_id(2)
    tq = q_ref.shape[0]
    qs = [q_ref[:, g * dk:(g + 1) * dk] for g in range(heads)]

    def step(g, state, j, masked):
        m, l, acc = state
        kv_rows = pl.ds(pl.multiple_of(j * tk, tk), tk)
        s = lax.dot_general(qs[g], k_ref[kv_rows, g * dk:(g + 1) * dk], NT_DIMS, preferred_element_type=F32)
        if has_bias:
            s = s - ck_ref[0, g, j]
        if masked:
            row = lax.broadcasted_iota(jnp.int32, s.shape, 0)
            col = lax.broadcasted_iota(jnp.int32, s.shape, 1)
            s = jnp.where(row >= col, s, -jnp.inf)
        m_new = jnp.maximum(m, jnp.max(s, axis=-1, keepdims=True))
        a = jnp.exp2(m - m_new)
        p = jnp.exp2(s - m_new)
        l = a * l + jnp.sum(p, axis=-1, keepdims=True)
        acc = a * acc + _dot(p.astype(BF16), v_ref[kv_rows, g * dv:(g + 1) * dv])
        return m_new, l, acc

    def body(j, carry):
        return tuple(step(g, carry[g], j, False) for g in range(heads))

    init = (jnp.full((tq, 1), -jnp.inf, F32), jnp.zeros((tq, 1), F32), jnp.zeros((tq, dv), F32))
    carry = lax.fori_loop(0, qi, body, (init,) * heads)
    for g in range(heads):
        _, l, acc = step(g, carry[g], qi, True)
        o_ref[:, g * dv:(g + 1) * dv] = (acc / l).astype(o_ref.dtype)


def _causal_attention(q_arr, k_arr, v_arr, q_col, k_col, v_col, dk, dv, n_heads, batch, seq,
                      ck=None, name="attn"):
    heads = 4
    assert n_heads % heads == 0 and q_col % heads == 0 and k_col % heads == 0 and v_col % heads == 0
    t = _tile(seq, 512)
    nq = seq // t
    in_specs = [pl.BlockSpec((t, heads * dk), lambda b, h, i: (b * nq + i, q_col // heads + h)),
                pl.BlockSpec((seq, heads * dk), lambda b, h, i: (b, k_col // heads + h)),
                pl.BlockSpec((seq, heads * dv), lambda b, h, i: (b, v_col // heads + h))]
    args = [q_arr, k_arr, v_arr]
    if ck is not None:
        args.append(ck.reshape(batch, n_heads, nq, 1, t))
        in_specs.append(pl.BlockSpec((1, heads, nq, 1, t), lambda b, h, i: (b, h, 0, 0, 0)))
    return pl.pallas_call(
        functools.partial(_flash_kernel, tk=t, dk=dk, dv=dv, heads=heads, has_bias=ck is not None),
        grid=(batch, n_heads // heads, nq), in_specs=in_specs,
        out_specs=pl.BlockSpec((t, heads * dv), lambda b, h, i: (b * nq + i, h)),
        out_shape=jax.ShapeDtypeStruct((batch * seq, n_heads * dv), BF16),
        compiler_params=_cparams("parallel", "parallel", "arbitrary"), name=name)(*args)


def _swa_kernel(sink_ref, q_ref, kp_ref, kc_ref, vp_ref, vc_ref, o_ref, *, scale, n_q, n_kv):
    w = SWA_WINDOW
    qi = lax.broadcasted_iota(jnp.int32, (w, 2 * w), 0)
    ji = lax.broadcasted_iota(jnp.int32, (w, 2 * w), 1)
    has_prev = pl.program_id(1) > 0
    mask = (ji > qi) & (ji <= qi + w) & ((ji >= w) | has_prev)
    group = n_q // n_kv
    for kvh in range(n_kv):
        cols = slice(kvh * HEAD_DIM, (kvh + 1) * HEAD_DIM)
        k = jnp.concatenate([kp_ref[:, cols], kc_ref[:, cols]], axis=0)
        v = jnp.concatenate([vp_ref[:, cols], vc_ref[:, cols]], axis=0)
        for g in range(group):
            h = kvh * group + g
            hc = slice(h * HEAD_DIM, (h + 1) * HEAD_DIM)
            s = lax.dot_general(q_ref[:, hc], k, NT_DIMS, preferred_element_type=F32) * scale
            s = jnp.where(mask, s, -jnp.inf)
            sink = sink_ref[h]
            m = jnp.maximum(jnp.max(s, axis=-1, keepdims=True), sink)
            p = jnp.exp(s - m)
            denom = jnp.sum(p, axis=-1, keepdims=True) + jnp.exp(sink - m)
            o_ref[:, hc] = (_dot(p.astype(BF16), v) / denom).astype(o_ref.dtype)


def _swa_attention(qk, v, sinks, n_q, n_kv, batch, seq, scale):
    w = SWA_WINDOW
    nb = seq // w
    qw, kw = n_q * HEAD_DIM, n_kv * HEAD_DIM
    k_col = qw // kw
    cur = lambda b, n: b * nb + n
    prev = lambda b, n: b * nb + jnp.maximum(n - 1, 0)
    return pl.pallas_call(
        functools.partial(_swa_kernel, scale=scale, n_q=n_q, n_kv=n_kv), grid=(batch, nb),
        in_specs=[pl.BlockSpec(memory_space=pltpu.SMEM),
                  pl.BlockSpec((w, qw), lambda b, n: (cur(b, n), 0)),
                  pl.BlockSpec((w, kw), lambda b, n: (prev(b, n), k_col)),
                  pl.BlockSpec((w, kw), lambda b, n: (cur(b, n), k_col)),
                  pl.BlockSpec((w, kw), lambda b, n: (prev(b, n), 0)),
                  pl.BlockSpec((w, kw), lambda b, n: (cur(b, n), 0))],
        out_specs=pl.BlockSpec((w, qw), lambda b, n: (cur(b, n), 0)),
        out_shape=jax.ShapeDtypeStruct((batch * seq, qw), BF16),
        compiler_params=_cparams("parallel", "arbitrary"), name="swa_attn")(
            sinks.astype(F32), qk, qk, qk, v, v)


def _top2_route(x, w, n_experts):
    xh = x.astype(BF16)
    xl = (x - xh.astype(F32)).astype(BF16)
    wh = w.astype(BF16)
    wl = (w - wh.astype(F32)).astype(BF16)
    logits = _dot(xh, wh) + _dot(xh, wl) + _dot(xl, wh)
    lane = lax.broadcasted_iota(jnp.int32, logits.shape, 1)
    lane_f = lane.astype(F32)
    l1 = jnp.where(lane < n_experts, logits, -jnp.inf)
    v1 = jnp.max(l1, axis=-1, keepdims=True)
    i1 = jnp.min(jnp.where(l1 == v1, lane_f, float(LANE)), axis=-1, keepdims=True)
    l2 = jnp.where(lane_f == i1, -jnp.inf, l1)
    v2 = jnp.max(l2, axis=-1, keepdims=True)
    i2 = jnp.min(jnp.where(l2 == v2, lane_f, float(LANE)), axis=-1, keepdims=True)
    e2 = jnp.exp(v2 - v1)
    idx = jnp.where(lane == 0, i1, jnp.where(lane == 1, i2, 0.0)).astype(jnp.int32)
    gate = jnp.where(lane == 0, 1.0 / (1.0 + e2), jnp.where(lane == 1, e2 / (1.0 + e2), 0.0))
    return idx, gate


def _mm_res_ln_kernel(*refs, n_pairs, n_experts):
    a_refs, w_refs = refs[:n_pairs], refs[n_pairs:2 * n_pairs]
    if n_experts:
        res_ref, g_ref, b_ref, wr_ref, o_ref, idx_ref, gate_ref = refs[2 * n_pairs:]
    else:
        res_ref, g_ref, b_ref, o_ref = refs[2 * n_pairs:]
    tm = o_ref.shape[0]
    sub = _tile(tm, 256)
    for r in range(tm // sub):
        rows = slice(r * sub, (r + 1) * sub)
        acc = _dot(a_refs[0][rows, :], w_refs[0][...])
        for a_ref, w_ref in zip(a_refs[1:], w_refs[1:]):
            acc = acc + _dot(a_ref[rows, :], w_ref[...])
        y = _layer_norm(DEEPNORM_ALPHA * res_ref[rows, :] + acc, g_ref[...], b_ref[...])
        o_ref[rows, :] = y
        if n_experts:
            idx_ref[rows, :], gate_ref[rows, :] = _top2_route(y, wr_ref[...], n_experts)


def _matmul_residual_ln(pairs, res, g, b, name, w_router=None):
    m, d = res.shape
    tm = _tile(m, 512)
    row = lambda i: (i, 0)
    fixed = lambda i: (0, 0)
    resident = lambda shape: pl.BlockSpec(shape, fixed, pipeline_mode=pl.Buffered(1))
    in_specs = ([pl.BlockSpec((tm, a.shape[1]), row) for a, _ in pairs] + [resident(w.shape) for _, w in pairs]
                + [pl.BlockSpec((tm, d), row), pl.BlockSpec((1, d), fixed), pl.BlockSpec((1, d), fixed)])
    args = [a for a, _ in pairs] + [w for _, w in pairs] + [res, g.reshape(1, d), b.reshape(1, d)]
    out_specs = [pl.BlockSpec((tm, d), row)]
    out_shape = [jax.ShapeDtypeStruct((m, d), F32)]
    n_experts = 0
    if w_router is not None:
        n_experts = w_router.shape[1]
        in_specs.append(resident((d, LANE)))
        args.append(jnp.zeros((d, LANE), F32).at[:, :n_experts].set(w_router.astype(F32)))
        out_specs += [pl.BlockSpec((tm, LANE), row), pl.BlockSpec((tm, LANE), row)]
        out_shape += [jax.ShapeDtypeStruct((m, LANE), jnp.int32), jax.ShapeDtypeStruct((m, LANE), F32)]
    return pl.pallas_call(
        functools.partial(_mm_res_ln_kernel, n_pairs=len(pairs), n_experts=n_experts),
        grid=(m // tm,), in_specs=in_specs, out_specs=out_specs, out_shape=out_shape,
        compiler_params=_cparams("parallel"), name=name)(*args)


def _swiglu(x, wg, wu, wd):
    a = _silu(_dot(x, wg)) * _dot(x, wu)
    return _dot(a.astype(BF16), wd)


def _ffn_ln_kernel(h_ref, wg_ref, wu_ref, wd_ref, g_ref, b_ref, o_ref, x_ref):
    j = pl.program_id(1)

    @pl.when(j == 0)
    def _():
        x_ref[...] = h_ref[...].astype(BF16)
        o_ref[...] = jnp.zeros_like(o_ref)

    o_ref[...] += _swiglu(x_ref[...], wg_ref[...], wu_ref[...], wd_ref[...])

    @pl.when(j == pl.num_programs(1) - 1)
    def _():
        o_ref[...] = _layer_norm(DEEPNORM_ALPHA * h_ref[...] + o_ref[...], g_ref[...], b_ref[...])


def _swiglu_residual_ln(h, wg, wu, wd, g, b):
    m, d = h.shape
    f = wg.shape[1]
    tm, tf = _tile(m, 512), _tile(f, 512)
    row = lambda i, j: (i, 0)
    fixed = lambda i, j: (0, 0)
    return pl.pallas_call(
        _ffn_ln_kernel, grid=(m // tm, f // tf),
        in_specs=[pl.BlockSpec((tm, d), row),
                  pl.BlockSpec((d, tf), lambda i, j: (0, j)), pl.BlockSpec((d, tf), lambda i, j: (0, j)),
                  pl.BlockSpec((tf, d), lambda i, j: (j, 0)),
                  pl.BlockSpec((1, d), fixed), pl.BlockSpec((1, d), fixed)],
        out_specs=pl.BlockSpec((tm, d), row),
        out_shape=jax.ShapeDtypeStruct((m, d), F32),
        scratch_shapes=[pltpu.VMEM((tm, d), BF16)],
        compiler_params=_cparams("parallel", "arbitrary"), name="ffn_ln")(
            h, wg, wu, wd, g.reshape(1, d), b.reshape(1, d))


def _rms_norm(c, g):
    ms = jnp.mean(c * c, axis=-1, keepdims=True)
    return c * lax.rsqrt(ms + RMS_EPS) * g


def _rope64(pair, tab):
    w = pair * tab
    lane = lax.broadcasted_iota(jnp.int32, w.shape, 1)
    return jnp.where(lane < MLA_ROPE_DIM, w + pltpu.roll(w, MLA_ROPE_DIM, 1), 0.0)


def _mla_proj_kernel(x_ref, win_ref, gq_ref, gkv_ref, tab_ref, wq_ref, wkv_ref, q_ref, k_ref, v_ref, *,
                     q_rank, kv_rank, n_heads, q_scale):
    acc = _dot(x_ref[...].astype(BF16), win_ref[...])
    cq = _rms_norm(acc[:, :q_rank], gq_ref[...]).astype(BF16)
    ckv = _rms_norm(acc[:, q_rank:q_rank + kv_rank], gkv_ref[...]).astype(BF16)
    tab = tab_ref[...]
    k_pe = _rope64(acc[:, q_rank + kv_rank:], tab).astype(BF16)
    wkv_w = MLA_NOPE_DIM + MLA_V_DIM
    for h in range(n_heads):
        q0 = h * MLA_QK_PAD
        qh = _dot(cq, wq_ref[:, q0:q0 + MLA_QK_PAD]) * q_scale
        q_ref[:, q0:q0 + MLA_NOPE_DIM] = qh[:, :MLA_NOPE_DIM].astype(BF16)
        q_ref[:, q0 + MLA_NOPE_DIM:q0 + MLA_QK_PAD] = _rope64(qh[:, MLA_NOPE_DIM:], tab).astype(BF16)
        kvh = _dot(ckv, wkv_ref[:, h * wkv_w:(h + 1) * wkv_w])
        k_ref[:, q0:q0 + MLA_NOPE_DIM] = kvh[:, :MLA_NOPE_DIM].astype(BF16)
        k_ref[:, q0 + MLA_NOPE_DIM:q0 + MLA_QK_PAD] = k_pe
        v_ref[:, h * MLA_V_DIM:(h + 1) * MLA_V_DIM] = kvh[:, MLA_NOPE_DIM:].astype(BF16)


def _mla_projections(x_bf, w_in, gq, gkv, tab, w_q, w_kv, q_rank, kv_rank, n_heads, q_scale):
    m, d = x_bf.shape
    tm = _tile(m, 512)
    row = lambda i: (i, 0)
    fixed = lambda i: (0, 0)
    resident = lambda shape: pl.BlockSpec(shape, fixed, pipeline_mode=pl.Buffered(1))
    qk_w, v_w = n_heads * MLA_QK_PAD, n_heads * MLA_V_DIM
    return pl.pallas_call(
        functools.partial(_mla_proj_kernel, q_rank=q_rank, kv_rank=kv_rank, n_heads=n_heads, q_scale=q_scale),
        grid=(m // tm,),
        in_specs=[pl.BlockSpec((tm, d), row), resident(w_in.shape),
                  pl.BlockSpec((1, q_rank), fixed), pl.BlockSpec((1, kv_rank), fixed),
                  pl.BlockSpec((tm, LANE), row), resident(w_q.shape), resident(w_kv.shape)],
        out_specs=[pl.BlockSpec((tm, qk_w), row), pl.BlockSpec((tm, qk_w), row), pl.BlockSpec((tm, v_w), row)],
        out_shape=[jax.ShapeDtypeStruct((m, qk_w), BF16), jax.ShapeDtypeStruct((m, qk_w), BF16),
                   jax.ShapeDtypeStruct((m, v_w), BF16)],
        compiler_params=_cparams("parallel"), name="mla_proj")(
            x_bf, w_in, gq.reshape(1, q_rank).astype(F32), gkv.reshape(1, kv_rank).astype(F32), tab, w_q, w_kv)


GATHER_ISSUE_UNROLL = 8


def _row_copy(src_hbm, buf, sem, src_row, dst_row):
    return pltpu.make_async_copy(src_hbm.at[pl.ds(src_row, 1), :], buf.at[pl.ds(dst_row, 1), :], sem)


def _wait_all_rows(src_hbm, buf, sem):
    pltpu.make_async_copy(src_hbm.at[pl.ds(0, buf.shape[0]), :], buf, sem).wait()


def _gather_rows_kernel(tok_ref, x_hbm, *refs, n_weights, n_slabs):
    w_refs, o_ref, wo_refs = refs[:n_weights], refs[n_weights], refs[n_weights + 1:2 * n_weights + 1]
    buf, sem = refs[2 * n_weights + 1:]
    rows = buf.shape[1]
    i, n = pl.program_id(0), pl.num_programs(0)

    @pl.when(i < n_slabs)
    def _():
        for w_ref, wo_ref in zip(w_refs, wo_refs):
            wo_ref[...] = w_ref[...].astype(wo_ref.dtype)

    def issue(step, slot):
        def start(r, _):
            _row_copy(x_hbm, buf.at[slot], sem.at[slot], tok_ref[step * rows + r], r).start()
            return 0
        lax.fori_loop(0, rows, start, 0, unroll=GATHER_ISSUE_UNROLL)

    @pl.when(i == 0)
    def _():
        issue(0, 0)

    @pl.when(i + 1 < n)
    def _():
        issue(i + 1, (i + 1) % 2)

    slot = i % 2
    _wait_all_rows(x_hbm, buf.at[slot], sem.at[slot])
    o_ref[...] = buf[slot].astype(o_ref.dtype)


BF16_SUBLANES = 16


def _slab_rows(total_rows, max_slabs):
    rows = BF16_SUBLANES
    while total_rows % rows or total_rows // rows > max_slabs:
        rows += BF16_SUBLANES
    return rows


def _gather_rows_cast_weights(h, row_token, rows_per_step, weights):
    n_rows = row_token.shape[0]
    d = h.shape[1]
    n_steps = n_rows // rows_per_step
    flat = [w.reshape(-1, w.shape[-1]) for w in weights]
    slab = [_slab_rows(w.shape[0], n_steps) for w in flat]
    n_slabs = flat[0].shape[0] // slab[0]
    assert all(w.shape[0] // s == n_slabs for w, s in zip(flat, slab))
    slab_spec = lambda w, s: pl.BlockSpec((s, w.shape[1]), lambda i, tok: (jnp.minimum(i, n_slabs - 1), 0))
    outs = pl.pallas_call(
        functools.partial(_gather_rows_kernel, n_weights=len(flat), n_slabs=n_slabs),
        grid_spec=pltpu.PrefetchScalarGridSpec(
            num_scalar_prefetch=1, grid=(n_steps,),
            in_specs=[pl.BlockSpec(memory_space=pl.ANY)] + [slab_spec(w, s) for w, s in zip(flat, slab)],
            out_specs=[pl.BlockSpec((rows_per_step, d), lambda i, tok: (i, 0))]
            + [slab_spec(w, s) for w, s in zip(flat, slab)],
            scratch_shapes=[pltpu.VMEM((2, rows_per_step, d), F32), pltpu.SemaphoreType.DMA((2,))]),
        out_shape=[jax.ShapeDtypeStruct((n_rows, d), BF16)]
        + [jax.ShapeDtypeStruct(w.shape, BF16) for w in flat],
        compiler_params=_cparams("arbitrary"), name="moe_gather")(row_token, h, *flat)
    return outs[0], [o.reshape(w.shape) for o, w in zip(outs[1:], weights)]


def _moe_ffn_kernel(te_ref, tv_ref, x_ref, wg_ref, wu_ref, wd_ref, o_ref):
    i, j = pl.program_id(0), pl.program_id(1)

    @pl.when(j == 0)
    def _():
        o_ref[...] = jnp.zeros_like(o_ref)

    @pl.when(tv_ref[i] == 1)
    def _():
        o_ref[...] += _swiglu(x_ref[...], wg_ref[0], wu_ref[0], wd_ref[0].astype(BF16))


def _moe_ffn(xg, wg, wu, wd, tile_expert, tile_valid, tm):
    n_rows, d = xg.shape
    f = wg.shape[2]
    tf = _tile(f, 1024)
    nj = f // tf
    col = lambda i, j, te, tv: jnp.where(tv[i] == 1, j, nj - 1)
    return pl.pallas_call(
        _moe_ffn_kernel,
        grid_spec=pltpu.PrefetchScalarGridSpec(
            num_scalar_prefetch=2, grid=(n_rows // tm, nj),
            in_specs=[pl.BlockSpec((tm, d), lambda i, j, te, tv: (i, 0)),
                      pl.BlockSpec((1, d, tf), lambda i, j, te, tv: (te[i], 0, col(i, j, te, tv))),
                      pl.BlockSpec((1, d, tf), lambda i, j, te, tv: (te[i], 0, col(i, j, te, tv))),
                      pl.BlockSpec((1, tf, d), lambda i, j, te, tv: (te[i], col(i, j, te, tv), 0))],
            out_specs=pl.BlockSpec((tm, d), lambda i, j, te, tv: (i, 0))),
        out_shape=jax.ShapeDtypeStruct((n_rows, d), F32),
        compiler_params=_cparams("arbitrary", "arbitrary"), name="moe_ffn")(
            tile_expert, tile_valid, xg, wg, wu, wd)


def _combine_ln_kernel(pos_ref, h_ref, gate_ref, g_ref, b_ref, y_hbm, o_ref, buf, sem):
    rows = buf.shape[2]
    i, n = pl.program_id(0), pl.num_programs(0)

    def issue(step, slot):
        def start(r, _):
            for k in range(TOP_K):
                src = pos_ref[TOP_K * (step * rows + r) + k]
                _row_copy(y_hbm, buf.at[slot, k], sem.at[slot, k], src, r).start()
            return 0
        lax.fori_loop(0, rows, start, 0, unroll=GATHER_ISSUE_UNROLL // TOP_K)

    @pl.when(i == 0)
    def _():
        issue(0, 0)

    @pl.when(i + 1 < n)
    def _():
        issue(i + 1, (i + 1) % 2)

    slot = i % 2
    gate = gate_ref[...]
    y = None
    for k in range(TOP_K):
        _wait_all_rows(y_hbm, buf.at[slot, k], sem.at[slot, k])
        term = buf[slot, k] * gate[:, k:k + 1]
        y = term if y is None else y + term
    o_ref[...] = _layer_norm(DEEPNORM_ALPHA * h_ref[...] + y, g_ref[...], b_ref[...])


def _moe_combine_ln(y_rows, pos, h, gates, g, b):
    m, d = h.shape
    tm = _tile(m, 256)
    row = lambda i, p: (i, 0)
    fixed = lambda i, p: (0, 0)
    return pl.pallas_call(
        _combine_ln_kernel,
        grid_spec=pltpu.PrefetchScalarGridSpec(
            num_scalar_prefetch=1, grid=(m // tm,),
            in_specs=[pl.BlockSpec((tm, d), row), pl.BlockSpec((tm, LANE), row),
                      pl.BlockSpec((1, d), fixed), pl.BlockSpec((1, d), fixed),
                      pl.BlockSpec(memory_space=pl.ANY)],
            out_specs=pl.BlockSpec((tm, d), row),
            scratch_shapes=[pltpu.VMEM((2, TOP_K, tm, d), F32), pltpu.SemaphoreType.DMA((2, TOP_K))]),
        out_shape=jax.ShapeDtypeStruct((m, d), F32),
        compiler_params=_cparams("arbitrary"), name="moe_combine_ln")(
            pos, h, gates, g.reshape(1, d), b.reshape(1, d), y_rows)


def _moe_routing(top_idx, n_experts, tm):
    n_assign = top_idx.size
    expert_of = top_idx.reshape(n_assign)
    onehot = (expert_of[:, None] == jnp.arange(n_experts, dtype=jnp.int32)[None, :]).astype(jnp.int32)
    csum = jnp.cumsum(onehot, axis=0)
    rank = jnp.sum(onehot * (csum - 1), axis=1)
    counts = csum[-1]
    padded = ((counts + tm - 1) // tm) * tm
    padded_end = jnp.cumsum(padded)
    padded_start = padded_end - padded
    pos = (padded_start[expert_of] + rank).astype(jnp.int32)
    n_tiles = -(-n_assign // tm) + n_experts
    row_token = jnp.zeros((n_tiles * tm,), jnp.int32).at[pos].set(
        jnp.arange(n_assign, dtype=jnp.int32) // TOP_K)
    tile_start = jnp.arange(n_tiles, dtype=jnp.int32) * tm
    tile_valid = (tile_start < padded_end[-1]).astype(jnp.int32)
    tile_expert = jnp.minimum(jnp.searchsorted(padded_end, tile_start, side='right'), n_experts - 1)
    last_expert = jnp.max(jnp.where(tile_valid == 1, tile_expert, 0))
    tile_expert = jnp.where(tile_valid == 1, tile_expert, last_expert).astype(jnp.int32)
    return pos, row_token, tile_expert, tile_valid


def _rotate_half_cols(w):
    half = w.shape[-1] // 2
    return jnp.concatenate([-w[..., half:], w[..., :half]], axis=-1)


def kernel(x, positions, w_in0, b_forget, sinks_b, w_out0, ln0_mix_g, ln0_mix_b, w_ffn_gate, w_ffn_up, w_ffn_down, ln0_ffn_g, ln0_ffn_b, w_in1, q_norm_g, w_uq, kv_norm_g, w_ukv, w_out1, ln1_mix_g, ln1_mix_b, w_router, w_moe_gate, w_moe_up, w_moe_down, ln1_ffn_g, ln1_ffn_b):
    batch, seq, d = x.shape
    n = batch * seq
    fox_heads = b_forget.shape[0]
    fox_w = fox_heads * HEAD_DIM
    swa_q = sinks_b.shape[0]
    swa_qw = swa_q * HEAD_DIM
    swa_kvw = (w_in0.shape[1] - 3 * fox_w - fox_heads - swa_qw) // 2
    swa_kv = swa_kvw // HEAD_DIM
    q_rank, kv_rank = q_norm_g.shape[0], kv_norm_g.shape[0]
    mla_heads = w_uq.shape[1] // (MLA_NOPE_DIM + MLA_ROPE_DIM)
    n_experts = w_router.shape[1]

    xf = x.reshape(n, d)
    x_bf = xf.astype(BF16)
    cos_a, sin_a, tab_m = _rope_tables(positions)

    c0 = 3 * fox_w
    c1 = c0 + fox_heads
    c2 = c1 + swa_qw + swa_kvw
    qkv_a = _matmul(x_bf, w_in0[:, :c0].astype(BF16), 2048, _tile(fox_w, 512), "proj_fox",
                    scaled_cols=fox_w, col_scale=HEAD_DIM ** -0.5 * LOG2_E)
    qk_b = _matmul_rope(x_bf, w_in0[:, c1:c2].astype(BF16), cos_a, sin_a, 2048, 256, "proj_swa_qk")
    v_b = _matmul(x_bf, w_in0[:, c2:].astype(BF16), 2048, 512, "proj_swa_v")
    cum_log_f = _fox_cum_log_forget(x_bf, w_in0[:, c0:c1], b_forget, batch, seq)
    o_a = _causal_attention(qkv_a, qkv_a, qkv_a, 0, fox_heads, 2 * fox_heads, HEAD_DIM, HEAD_DIM,
                            fox_heads, batch, seq, ck=cum_log_f, name="fox_attn")
    o_b = _swa_attention(qk_b, v_b, sinks_b, swa_q, swa_kv, batch, seq, HEAD_DIM ** -0.5)
    w_out0_bf = w_out0.astype(BF16)
    (h1,) = _matmul_residual_ln([(o_a, w_out0_bf[:fox_w]), (o_b, w_out0_bf[fox_w:])], xf,
                                ln0_mix_g, ln0_mix_b, "out0_ln")
    h2 = _swiglu_residual_ln(h1, w_ffn_gate.astype(BF16), w_ffn_up.astype(BF16), w_ffn_down.astype(BF16),
                             ln0_ffn_g, ln0_ffn_b)

    r0 = q_rank + kv_rank
    w_in1_x = jnp.concatenate([w_in1, _rotate_half_cols(w_in1[:, r0:])], axis=1).astype(BF16)
    w_uq_h = w_uq.reshape(q_rank, mla_heads, MLA_NOPE_DIM + MLA_ROPE_DIM)
    w_uq_x = jnp.concatenate([w_uq_h, _rotate_half_cols(w_uq_h[..., MLA_NOPE_DIM:])], axis=-1)
    q_m, k_m, v_m = _mla_projections(
        h2, w_in1_x, q_norm_g, kv_norm_g, tab_m,
        w_uq_x.reshape(q_rank, mla_heads * MLA_QK_PAD).astype(BF16), w_ukv.astype(BF16),
        q_rank, kv_rank, mla_heads, (MLA_NOPE_DIM + MLA_ROPE_DIM) ** -0.5 * LOG2_E)
    o_m = _causal_attention(q_m, k_m, v_m, 0, 0, 0, MLA_QK_PAD, MLA_V_DIM, mla_heads, batch, seq,
                            name="mla_attn")
    h3, idx, gates = _matmul_residual_ln([(o_m, w_out1.astype(BF16))], h2, ln1_mix_g, ln1_mix_b,
                                         "out1_ln_route", w_router=w_router)

    moe_tm = _tile(n * TOP_K, 512)
    pos, row_token, tile_expert, tile_valid = _moe_routing(idx[:, :TOP_K], n_experts, moe_tm)
    xg, (wg_bf, wu_bf) = _gather_rows_cast_weights(h3, row_token, _tile(moe_tm, 256), [w_moe_gate, w_moe_up])
    y_rows = _moe_ffn(xg, wg_bf, wu_bf, w_moe_down, tile_expert, tile_valid, moe_tm)
    out = _moe_combine_ln(y_rows, pos, h3, gates, ln1_ffn_g, ln1_ffn_b)
    return out.reshape(batch, seq, d)
```

```python
import functools
import math

import jax
import jax.numpy as jnp
from jax import lax
from jax.experimental import pallas as pl
from jax.experimental.pallas import tpu as pltpu

F32 = jnp.float32
BF16 = jnp.bfloat16

LANE = 128
HEAD_DIM = 128
SWA_WINDOW = 128
MLA_NOPE_DIM = 128
MLA_ROPE_DIM = 64
MLA_V_DIM = 128
MLA_QK_PAD = 256
ROPE_THETA = 10000.0
TOP_K = 2
LN_EPS = 1e-5
RMS_EPS = 1e-6
DEPTH = 2
DEEPNORM_ALPHA = (2 * DEPTH) ** 0.25
LOG2_E = math.log2(math.e)

NT_DIMS = (((1,), (1,)), ((), ()))


def _tile(n, pref):
    if n <= pref:
        return n
    t = (pref // LANE) * LANE
    while n % t:
        t -= LANE
    return t


def _cparams(*sem):
    return pltpu.CompilerParams(dimension_semantics=sem)


def _dot(a, b):
    return jnp.dot(a, b, preferred_element_type=F32)


def _layer_norm(y, g, b):
    mu = jnp.mean(y, axis=-1, keepdims=True)
    d = y - mu
    var = jnp.mean(d * d, axis=-1, keepdims=True)
    return d * lax.rsqrt(var + LN_EPS) * g + b


def _silu(g):
    return g / (1.0 + jnp.exp(-g))


def _rope_table_kernel(pos_ref, freq_ref, cos_a_ref, sin_a_ref, tab_m_ref):
    pos = pos_ref[...]
    ang_a = pos * freq_ref[0:1, :]
    lane = lax.broadcasted_iota(jnp.int32, ang_a.shape, 1)
    cos_a_ref[...] = jnp.cos(ang_a)
    sin_a_ref[...] = jnp.where(lane < HEAD_DIM // 2, -jnp.sin(ang_a), jnp.sin(ang_a))
    ang_m = pos * freq_ref[1:2, :]
    tab_m_ref[...] = jnp.where(lane < MLA_ROPE_DIM, jnp.cos(ang_m), jnp.sin(ang_m))


def _rope_tables(positions):
    n = positions.size
    pos = positions.astype(F32).reshape(n, 1)
    half_a = HEAD_DIM // 2
    half_m = MLA_ROPE_DIM // 2
    inv_a = ROPE_THETA ** (-2.0 * jnp.arange(half_a, dtype=F32) / HEAD_DIM)
    inv_m = ROPE_THETA ** (-2.0 * jnp.arange(half_m, dtype=F32) / MLA_ROPE_DIM)
    freq = jnp.stack([jnp.tile(inv_a, 2), jnp.tile(inv_m, 4)])
    tm = _tile(n, 1024)
    out = jax.ShapeDtypeStruct((n, LANE), F32)
    row = pl.BlockSpec((tm, LANE), lambda i: (i, 0))
    return pl.pallas_call(
        _rope_table_kernel, grid=(n // tm,),
        in_specs=[pl.BlockSpec((tm, 1), lambda i: (i, 0)), pl.BlockSpec((2, LANE), lambda i: (0, 0))],
        out_specs=[row, row, row], out_shape=[out, out, out],
        compiler_params=_cparams("parallel"), name="rope_tables")(pos, freq)


PLAIN, SCALED, ROTARY = "plain", "scaled", "rotary"


def _proj_kernel(x_ref, w_ref, cos_ref, sin_ref, o_ref, xb_ref, *, tile_kinds, scale):
    j = pl.program_id(1)

    @pl.when(j == 0)
    def _():
        xb_ref[...] = x_ref[...].astype(BF16)

    def tile(kinds):
        acc = _dot(xb_ref[...], w_ref[...])
        for c, kind in enumerate(kinds):
            cols = slice(c * HEAD_DIM, (c + 1) * HEAD_DIM)
            a = acc[:, cols]
            if kind == SCALED:
                a = a * scale
            elif kind == ROTARY:
                a = a * cos_ref[...] + pltpu.roll(a, HEAD_DIM // 2, 1) * sin_ref[...]
            o_ref[:, cols] = a.astype(o_ref.dtype)

    for kinds in sorted(set(tile_kinds)):
        first = tile_kinds.index(kinds)
        count = tile_kinds.count(kinds)
        assert tile_kinds[first:first + count] == (kinds,) * count
        pl.when((j >= first) & (j < first + count))(functools.partial(tile, kinds))


def _project(x, w, group_kinds, scale, cos, sin, tm, tn, name):
    m, k = x.shape
    n = w.shape[1]
    tm, tn = _tile(m, tm), _tile(n, tn)
    per_tile = tn // HEAD_DIM
    tile_kinds = tuple(tuple(group_kinds[t * per_tile:(t + 1) * per_tile]) for t in range(n // tn))
    return pl.pallas_call(
        functools.partial(_proj_kernel, tile_kinds=tile_kinds, scale=scale), grid=(m // tm, n // tn),
        in_specs=[pl.BlockSpec((tm, k), lambda i, j: (i, 0)), pl.BlockSpec((k, tn), lambda i, j: (0, j)),
                  pl.BlockSpec((tm, LANE), lambda i, j: (i, 0)), pl.BlockSpec((tm, LANE), lambda i, j: (i, 0))],
        out_specs=pl.BlockSpec((tm, tn), lambda i, j: (i, j)),
        out_shape=jax.ShapeDtypeStruct((m, n), BF16),
        scratch_shapes=[pltpu.VMEM((tm, k), BF16)],
        compiler_params=_cparams("parallel", "arbitrary"), name=name)(x, w, cos, sin)


def _split3(x):
    hi = x.astype(BF16)
    r = x - hi.astype(F32)
    mid = r.astype(BF16)
    lo = (r - mid.astype(F32)).astype(BF16)
    return hi, mid, lo


def _fox_gate_kernel(x_ref, wf_ref, bf_ref, o_ref, carry_ref, *, n_heads):
    @pl.when(pl.program_id(1) == 0)
    def _():
        carry_ref[...] = jnp.zeros_like(carry_ref)

    f = lax.dot_general(wf_ref[...], x_ref[...].astype(BF16), NT_DIMS, preferred_element_type=F32)
    z = f + bf_ref[...]
    log_f = jnp.minimum(z, 0.0) - jnp.log1p(jnp.exp(-jnp.abs(z)))
    tc = z.shape[1]
    src = lax.broadcasted_iota(jnp.int32, (tc, tc), 0)
    dst = lax.broadcasted_iota(jnp.int32, (tc, tc), 1)
    tri = jnp.where(src <= dst, 1.0, 0.0).astype(BF16)
    hi, mid, lo = _split3(log_f)
    cum = _dot(hi, tri) + _dot(mid, tri) + _dot(lo, tri) + carry_ref[:, 0:1]
    o_ref[0] = cum[:n_heads] * LOG2_E
    carry_ref[...] = jnp.broadcast_to(cum[:, tc - 1:tc], carry_ref.shape)


def _fox_cum_log_forget(x, w_f, b_forget, batch, seq):
    n, d = x.shape
    n_heads = w_f.shape[1]
    rows = 16
    wf_t = jnp.zeros((rows, d), BF16).at[:n_heads].set(w_f.T.astype(BF16))
    bf = jnp.zeros((rows, 1), F32).at[:n_heads, 0].set(b_forget.astype(F32))
    tc = _tile(seq, 512)
    nc = seq // tc
    return pl.pallas_call(
        functools.partial(_fox_gate_kernel, n_heads=n_heads), grid=(batch, nc),
        in_specs=[pl.BlockSpec((tc, d), lambda b, s: (b * nc + s, 0)),
                  pl.BlockSpec((rows, d), lambda b, s: (0, 0)),
                  pl.BlockSpec((rows, 1), lambda b, s: (0, 0))],
        out_specs=pl.BlockSpec((1, n_heads, tc), lambda b, s: (b, 0, s)),
        out_shape=jax.ShapeDtypeStruct((batch, n_heads, seq), F32),
        scratch_shapes=[pltpu.VMEM((rows, LANE), F32)],
        compiler_params=_cparams("parallel", "arbitrary"), name="fox_gate")(x, wf_t, bf)


def _flash_kernel(*refs, tk, dk, dv, heads, has_bias):
    if has_bias:
        q_ref, k_ref, v_ref, ck_ref, o_ref = refs
    else:
        q_ref, k_ref, v_ref, o_ref = refs
    qi = pl.program_id(2)
    tq = q_ref.shape[0]
    qs = [q_ref[:, g * dk:(g + 1) * dk] for g in range(heads)]

    def step(g, state, j, masked):
        m, l, acc = state
        kv_rows = pl.ds(pl.multiple_of(j * tk, tk), tk)
        s = lax.dot_general(qs[g], k_ref[kv_rows, g * dk:(g + 1) * dk], NT_DIMS, preferred_element_type=F32)
        if has_bias:
            s = s - ck_ref[0, g, j]
        if masked:
            row = lax.broadcasted_iota(jnp.int32, s.shape, 0)
            col = lax.broadcasted_iota(jnp.int32, s.shape, 1)
            s = jnp.where(row >= col, s, -jnp.inf)
        m_new = jnp.maximum(m, jnp.max(s, axis=-1, keepdims=True))
        a = jnp.exp2(m - m_new)
        p = jnp.exp2(s - m_new)
        l = a * l + jnp.sum(p, axis=-1, keepdims=True)
        acc = a * acc + _dot(p.astype(BF16), v_ref[kv_rows, g * dv:(g + 1) * dv])
        return m_new, l, acc

    def body(j, carry):
        return tuple(step(g, carry[g], j, False) for g in range(heads))

    init = (jnp.full((tq, 1), -jnp.inf, F32), jnp.zeros((tq, 1), F32), jnp.zeros((tq, dv), F32))
    carry = lax.fori_loop(0, qi, body, (init,) * heads)
    for g in range(heads):
        _, l, acc = step(g, carry[g], qi, True)
        o_ref[:, g * dv:(g + 1) * dv] = (acc / l).astype(o_ref.dtype)


def _causal_attention(q_arr, k_arr, v_arr, q_col, k_col, v_col, dk, dv, n_heads, batch, seq,
                      ck=None, name="attn"):
    heads = 4
    assert n_heads % heads == 0 and q_col % heads == 0 and k_col % heads == 0 and v_col % heads == 0
    t = _tile(seq, 512)
    nq = seq // t
    in_specs = [pl.BlockSpec((t, heads * dk), lambda b, h, i: (b * nq + i, q_col // heads + h)),
                pl.BlockSpec((seq, heads * dk), lambda b, h, i: (b, k_col // heads + h)),
                pl.BlockSpec((seq, heads * dv), lambda b, h, i: (b, v_col // heads + h))]
    args = [q_arr, k_arr, v_arr]
    if ck is not None:
        args.append(ck.reshape(batch, n_heads, nq, 1, t))
        in_specs.append(pl.BlockSpec((1, heads, nq, 1, t), lambda b, h, i: (b, h, 0, 0, 0)))
    return pl.pallas_call(
        functools.partial(_flash_kernel, tk=t, dk=dk, dv=dv, heads=heads, has_bias=ck is not None),
        grid=(batch, n_heads // heads, nq), in_specs=in_specs,
        out_specs=pl.BlockSpec((t, heads * dv), lambda b, h, i: (b * nq + i, h)),
        out_shape=jax.ShapeDtypeStruct((batch * seq, n_heads * dv), BF16),
        compiler_params=_cparams("parallel", "parallel", "arbitrary"), name=name)(*args)


def _swa_kernel(sink_ref, q_ref, kp_ref, kc_ref, vp_ref, vc_ref, o_ref, *, scale, n_q, n_kv):
    w = SWA_WINDOW
    qi = lax.broadcasted_iota(jnp.int32, (w, 2 * w), 0)
    ji = lax.broadcasted_iota(jnp.int32, (w, 2 * w), 1)
    has_prev = pl.program_id(1) > 0
    mask = (ji > qi) & (ji <= qi + w) & ((ji >= w) | has_prev)
    group = n_q // n_kv
    for kvh in range(n_kv):
        cols = slice(kvh * HEAD_DIM, (kvh + 1) * HEAD_DIM)
        k = jnp.concatenate([kp_ref[:, cols], kc_ref[:, cols]], axis=0)
        v = jnp.concatenate([vp_ref[:, cols], vc_ref[:, cols]], axis=0)
        for g in range(group):
            h = kvh * group + g
            hc = slice(h * HEAD_DIM, (h + 1) * HEAD_DIM)
            s = lax.dot_general(q_ref[:, hc], k, NT_DIMS, preferred_element_type=F32) * scale
            s = jnp.where(mask, s, -jnp.inf)
            sink = sink_ref[h]
            m = jnp.maximum(jnp.max(s, axis=-1, keepdims=True), sink)
            p = jnp.exp(s - m)
            denom = jnp.sum(p, axis=-1, keepdims=True) + jnp.exp(sink - m)
            o_ref[:, hc] = (_dot(p.astype(BF16), v) / denom).astype(o_ref.dtype)


def _swa_attention(qkv, q_off, k_off, v_off, sinks, n_q, n_kv, batch, seq, scale):
    w = SWA_WINDOW
    nb = seq // w
    qw, kw = n_q * HEAD_DIM, n_kv * HEAD_DIM
    assert q_off % qw == 0 and k_off % kw == 0 and v_off % kw == 0
    q_col, k_col, v_col = q_off // qw, k_off // kw, v_off // kw
    cur = lambda b, n: b * nb + n
    prev = lambda b, n: b * nb + jnp.maximum(n - 1, 0)
    return pl.pallas_call(
        functools.partial(_swa_kernel, scale=scale, n_q=n_q, n_kv=n_kv), grid=(batch, nb),
        in_specs=[pl.BlockSpec(memory_space=pltpu.SMEM),
                  pl.BlockSpec((w, qw), lambda b, n: (cur(b, n), q_col)),
                  pl.BlockSpec((w, kw), lambda b, n: (prev(b, n), k_col)),
                  pl.BlockSpec((w, kw), lambda b, n: (cur(b, n), k_col)),
                  pl.BlockSpec((w, kw), lambda b, n: (prev(b, n), v_col)),
                  pl.BlockSpec((w, kw), lambda b, n: (cur(b, n), v_col))],
        out_specs=pl.BlockSpec((w, qw), lambda b, n: (cur(b, n), 0)),
        out_shape=jax.ShapeDtypeStruct((batch * seq, qw), BF16),
        compiler_params=_cparams("parallel", "arbitrary"), name="swa_attn")(
            sinks.astype(F32), qkv, qkv, qkv, qkv, qkv)


def _top2_route(x, w, n_experts):
    xh = x.astype(BF16)
    xl = (x - xh.astype(F32)).astype(BF16)
    wh = w.astype(BF16)
    wl = (w - wh.astype(F32)).astype(BF16)
    logits = _dot(xh, wh) + _dot(xh, wl) + _dot(xl, wh)
    lane = lax.broadcasted_iota(jnp.int32, logits.shape, 1)
    lane_f = lane.astype(F32)
    l1 = jnp.where(lane < n_experts, logits, -jnp.inf)
    v1 = jnp.max(l1, axis=-1, keepdims=True)
    i1 = jnp.min(jnp.where(l1 == v1, lane_f, float(LANE)), axis=-1, keepdims=True)
    l2 = jnp.where(lane_f == i1, -jnp.inf, l1)
    v2 = jnp.max(l2, axis=-1, keepdims=True)
    i2 = jnp.min(jnp.where(l2 == v2, lane_f, float(LANE)), axis=-1, keepdims=True)
    e2 = jnp.exp(v2 - v1)
    idx = jnp.where(lane == 0, i1, jnp.where(lane == 1, i2, 0.0)).astype(jnp.int32)
    gate = jnp.where(lane == 0, 1.0 / (1.0 + e2), jnp.where(lane == 1, e2 / (1.0 + e2), 0.0))
    return idx, gate


def _mm_res_ln_kernel(*refs, n_pairs, n_experts):
    a_refs, w_refs = refs[:n_pairs], refs[n_pairs:2 * n_pairs]
    if n_experts:
        res_ref, g_ref, b_ref, wr_ref, o_ref, idx_ref, gate_ref = refs[2 * n_pairs:]
    else:
        res_ref, g_ref, b_ref, o_ref = refs[2 * n_pairs:]
    tm = o_ref.shape[0]
    sub = _tile(tm, 256)
    for r in range(tm // sub):
        rows = slice(r * sub, (r + 1) * sub)
        acc = _dot(a_refs[0][rows, :], w_refs[0][...])
        for a_ref, w_ref in zip(a_refs[1:], w_refs[1:]):
            acc = acc + _dot(a_ref[rows, :], w_ref[...])
        y = _layer_norm(DEEPNORM_ALPHA * res_ref[rows, :] + acc, g_ref[...], b_ref[...])
        o_ref[rows, :] = y
        if n_experts:
            idx_ref[rows, :], gate_ref[rows, :] = _top2_route(y, wr_ref[...], n_experts)


def _matmul_residual_ln(pairs, res, g, b, name, w_router=None):
    m, d = res.shape
    tm = _tile(m, 512)
    row = lambda i: (i, 0)
    fixed = lambda i: (0, 0)
    resident = lambda shape: pl.BlockSpec(shape, fixed, pipeline_mode=pl.Buffered(1))
    in_specs = ([pl.BlockSpec((tm, a.shape[1]), row) for a, _ in pairs] + [resident(w.shape) for _, w in pairs]
                + [pl.BlockSpec((tm, d), row), pl.BlockSpec((1, d), fixed), pl.BlockSpec((1, d), fixed)])
    args = [a for a, _ in pairs] + [w for _, w in pairs] + [res, g.reshape(1, d), b.reshape(1, d)]
    out_specs = [pl.BlockSpec((tm, d), row)]
    out_shape = [jax.ShapeDtypeStruct((m, d), F32)]
    n_experts = 0
    if w_router is not None:
        n_experts = w_router.shape[1]
        in_specs.append(resident((d, LANE)))
        args.append(jnp.zeros((d, LANE), F32).at[:, :n_experts].set(w_router.astype(F32)))
        out_specs += [pl.BlockSpec((tm, LANE), row), pl.BlockSpec((tm, LANE), row)]
        out_shape += [jax.ShapeDtypeStruct((m, LANE), jnp.int32), jax.ShapeDtypeStruct((m, LANE), F32)]
    return pl.pallas_call(
        functools.partial(_mm_res_ln_kernel, n_pairs=len(pairs), n_experts=n_experts),
        grid=(m // tm,), in_specs=in_specs, out_specs=out_specs, out_shape=out_shape,
        compiler_params=_cparams("parallel"), name=name)(*args)


def _swiglu(x, wg, wu, wd):
    a = _silu(_dot(x, wg)) * _dot(x, wu)
    return _dot(a.astype(BF16), wd)


def _ffn_ln_kernel(h_ref, wg_ref, wu_ref, wd_ref, g_ref, b_ref, o_ref, x_ref):
    j = pl.program_id(1)

    @pl.when(j == 0)
    def _():
        x_ref[...] = h_ref[...].astype(BF16)
        o_ref[...] = jnp.zeros_like(o_ref)

    o_ref[...] += _swiglu(x_ref[...], wg_ref[...], wu_ref[...], wd_ref[...])

    @pl.when(j == pl.num_programs(1) - 1)
    def _():
        o_ref[...] = _layer_norm(DEEPNORM_ALPHA * h_ref[...] + o_ref[...], g_ref[...], b_ref[...])


def _swiglu_residual_ln(h, wg, wu, wd, g, b):
    m, d = h.shape
    f = wg.shape[1]
    tm, tf = _tile(m, 512), _tile(f, 512)
    row = lambda i, j: (i, 0)
    fixed = lambda i, j: (0, 0)
    return pl.pallas_call(
        _ffn_ln_kernel, grid=(m // tm, f // tf),
        in_specs=[pl.BlockSpec((tm, d), row),
                  pl.BlockSpec((d, tf), lambda i, j: (0, j)), pl.BlockSpec((d, tf), lambda i, j: (0, j)),
                  pl.BlockSpec((tf, d), lambda i, j: (j, 0)),
                  pl.BlockSpec((1, d), fixed), pl.BlockSpec((1, d), fixed)],
        out_specs=pl.BlockSpec((tm, d), row),
        out_shape=jax.ShapeDtypeStruct((m, d), F32),
        scratch_shapes=[pltpu.VMEM((tm, d), BF16)],
        compiler_params=_cparams("parallel", "arbitrary"), name="ffn_ln")(
            h, wg, wu, wd, g.reshape(1, d), b.reshape(1, d))


def _rms_norm(c, g):
    ms = jnp.mean(c * c, axis=-1, keepdims=True)
    return c * lax.rsqrt(ms + RMS_EPS) * g


def _rope64(pair, tab):
    w = pair * tab
    lane = lax.broadcasted_iota(jnp.int32, w.shape, 1)
    return jnp.where(lane < MLA_ROPE_DIM, w + pltpu.roll(w, MLA_ROPE_DIM, 1), 0.0)


def _mla_proj_kernel(x_ref, win_ref, gq_ref, gkv_ref, tab_ref, wq_ref, wkv_ref, q_ref, k_ref, v_ref, *,
                     q_rank, kv_rank, n_heads, q_scale):
    acc = _dot(x_ref[...].astype(BF16), win_ref[...])
    cq = _rms_norm(acc[:, :q_rank], gq_ref[...]).astype(BF16)
    ckv = _rms_norm(acc[:, q_rank:q_rank + kv_rank], gkv_ref[...]).astype(BF16)
    tab = tab_ref[...]
    k_pe = _rope64(acc[:, q_rank + kv_rank:], tab).astype(BF16)
    wkv_w = MLA_NOPE_DIM + MLA_V_DIM
    for h in range(n_heads):
        q0 = h * MLA_QK_PAD
        qh = _dot(cq, wq_ref[:, q0:q0 + MLA_QK_PAD]) * q_scale
        q_ref[:, q0:q0 + MLA_NOPE_DIM] = qh[:, :MLA_NOPE_DIM].astype(BF16)
        q_ref[:, q0 + MLA_NOPE_DIM:q0 + MLA_QK_PAD] = _rope64(qh[:, MLA_NOPE_DIM:], tab).astype(BF16)
        kvh = _dot(ckv, wkv_ref[:, h * wkv_w:(h + 1) * wkv_w])
        k_ref[:, q0:q0 + MLA_NOPE_DIM] = kvh[:, :MLA_NOPE_DIM].astype(BF16)
        k_ref[:, q0 + MLA_NOPE_DIM:q0 + MLA_QK_PAD] = k_pe
        v_ref[:, h * MLA_V_DIM:(h + 1) * MLA_V_DIM] = kvh[:, MLA_NOPE_DIM:].astype(BF16)


def _mla_projections(x, w_in, gq, gkv, tab, w_q, w_kv, q_rank, kv_rank, n_heads, q_scale):
    m, d = x.shape
    tm = _tile(m, 512)
    row = lambda i: (i, 0)
    fixed = lambda i: (0, 0)
    resident = lambda shape: pl.BlockSpec(shape, fixed, pipeline_mode=pl.Buffered(1))
    qk_w, v_w = n_heads * MLA_QK_PAD, n_heads * MLA_V_DIM
    return pl.pallas_call(
        functools.partial(_mla_proj_kernel, q_rank=q_rank, kv_rank=kv_rank, n_heads=n_heads, q_scale=q_scale),
        grid=(m // tm,),
        in_specs=[pl.BlockSpec((tm, d), row), resident(w_in.shape),
                  pl.BlockSpec((1, q_rank), fixed), pl.BlockSpec((1, kv_rank), fixed),
                  pl.BlockSpec((tm, LANE), row), resident(w_q.shape), resident(w_kv.shape)],
        out_specs=[pl.BlockSpec((tm, qk_w), row), pl.BlockSpec((tm, qk_w), row), pl.BlockSpec((tm, v_w), row)],
        out_shape=[jax.ShapeDtypeStruct((m, qk_w), BF16), jax.ShapeDtypeStruct((m, qk_w), BF16),
                   jax.ShapeDtypeStruct((m, v_w), BF16)],
        compiler_params=_cparams("parallel"), name="mla_proj")(
            x, w_in, gq.reshape(1, q_rank).astype(F32), gkv.reshape(1, kv_rank).astype(F32), tab, w_q, w_kv)


GATHER_ISSUE_UNROLL = 8


def _row_copy(src_hbm, buf, sem, src_row, dst_row):
    return pltpu.make_async_copy(src_hbm.at[pl.ds(src_row, 1), :], buf.at[pl.ds(dst_row, 1), :], sem)


def _wait_all_rows(src_hbm, buf, sem):
    pltpu.make_async_copy(src_hbm.at[pl.ds(0, buf.shape[0]), :], buf, sem).wait()


def _gather_rows_kernel(tok_ref, x_hbm, *refs, n_weights, n_slabs):
    w_refs, o_ref, wo_refs = refs[:n_weights], refs[n_weights], refs[n_weights + 1:2 * n_weights + 1]
    buf, sem = refs[2 * n_weights + 1:]
    rows = buf.shape[1]
    i, n = pl.program_id(0), pl.num_programs(0)

    @pl.when(i < n_slabs)
    def _():
        for w_ref, wo_ref in zip(w_refs, wo_refs):
            wo_ref[...] = w_ref[...].astype(wo_ref.dtype)

    def issue(step, slot):
        def start(r, _):
            _row_copy(x_hbm, buf.at[slot], sem.at[slot], tok_ref[step * rows + r], r).start()
            return 0
        lax.fori_loop(0, rows, start, 0, unroll=GATHER_ISSUE_UNROLL)

    @pl.when(i == 0)
    def _():
        issue(0, 0)

    @pl.when(i + 1 < n)
    def _():
        issue(i + 1, (i + 1) % 2)

    slot = i % 2
    _wait_all_rows(x_hbm, buf.at[slot], sem.at[slot])
    o_ref[...] = buf[slot].astype(o_ref.dtype)


BF16_SUBLANES = 16


def _slab_rows(total_rows, max_slabs):
    rows = BF16_SUBLANES
    while total_rows % rows or total_rows // rows > max_slabs:
        rows += BF16_SUBLANES
    return rows


def _gather_rows_cast_weights(h, row_token, rows_per_step, weights):
    n_rows = row_token.shape[0]
    d = h.shape[1]
    n_steps = n_rows // rows_per_step
    flat = [w.reshape(-1, w.shape[-1]) for w in weights]
    slab = [_slab_rows(w.shape[0], n_steps) for w in flat]
    n_slabs = flat[0].shape[0] // slab[0]
    assert all(w.shape[0] // s == n_slabs for w, s in zip(flat, slab))
    slab_spec = lambda w, s: pl.BlockSpec((s, w.shape[1]), lambda i, tok: (jnp.minimum(i, n_slabs - 1), 0))
    outs = pl.pallas_call(
        functools.partial(_gather_rows_kernel, n_weights=len(flat), n_slabs=n_slabs),
        grid_spec=pltpu.PrefetchScalarGridSpec(
            num_scalar_prefetch=1, grid=(n_steps,),
            in_specs=[pl.BlockSpec(memory_space=pl.ANY)] + [slab_spec(w, s) for w, s in zip(flat, slab)],
            out_specs=[pl.BlockSpec((rows_per_step, d), lambda i, tok: (i, 0))]
            + [slab_spec(w, s) for w, s in zip(flat, slab)],
            scratch_shapes=[pltpu.VMEM((2, rows_per_step, d), F32), pltpu.SemaphoreType.DMA((2,))]),
        out_shape=[jax.ShapeDtypeStruct((n_rows, d), BF16)]
        + [jax.ShapeDtypeStruct(w.shape, BF16) for w in flat],
        compiler_params=_cparams("arbitrary"), name="moe_gather")(row_token, h, *flat)
    return outs[0], [o.reshape(w.shape) for o, w in zip(outs[1:], weights)]


def _moe_ffn_kernel(te_ref, tv_ref, x_ref, wg_ref, wu_ref, wd_ref, o_ref):
    i, j = pl.program_id(0), pl.program_id(1)

    @pl.when(j == 0)
    def _():
        o_ref[...] = jnp.zeros_like(o_ref)

    @pl.when(tv_ref[i] == 1)
    def _():
        o_ref[...] += _swiglu(x_ref[...], wg_ref[0], wu_ref[0], wd_ref[0].astype(BF16))


def _moe_ffn(xg, wg, wu, wd, tile_expert, tile_valid, tm):
    n_rows, d = xg.shape
    f = wg.shape[2]
    tf = _tile(f, 1024)
    nj = f // tf
    col = lambda i, j, te, tv: jnp.where(tv[i] == 1, j, nj - 1)
    return pl.pallas_call(
        _moe_ffn_kernel,
        grid_spec=pltpu.PrefetchScalarGridSpec(
            num_scalar_prefetch=2, grid=(n_rows // tm, nj),
            in_specs=[pl.BlockSpec((tm, d), lambda i, j, te, tv: (i, 0)),
                      pl.BlockSpec((1, d, tf), lambda i, j, te, tv: (te[i], 0, col(i, j, te, tv))),
                      pl.BlockSpec((1, d, tf), lambda i, j, te, tv: (te[i], 0, col(i, j, te, tv))),
                      pl.BlockSpec((1, tf, d), lambda i, j, te, tv: (te[i], col(i, j, te, tv), 0))],
            out_specs=pl.BlockSpec((tm, d), lambda i, j, te, tv: (i, 0))),
        out_shape=jax.ShapeDtypeStruct((n_rows, d), F32),
        compiler_params=_cparams("arbitrary", "arbitrary"), name="moe_ffn")(
            tile_expert, tile_valid, xg, wg, wu, wd)


def _combine_ln_kernel(pos_ref, h_ref, gate_ref, g_ref, b_ref, y_hbm, o_ref, buf, sem):
    rows = buf.shape[2]
    i, n = pl.program_id(0), pl.num_programs(0)

    def issue(step, slot):
        def start(r, _):
            for k in range(TOP_K):
                src = pos_ref[TOP_K * (step * rows + r) + k]
                _row_copy(y_hbm, buf.at[slot, k], sem.at[slot, k], src, r).start()
            return 0
        lax.fori_loop(0, rows, start, 0, unroll=GATHER_ISSUE_UNROLL // TOP_K)

    @pl.when(i == 0)
    def _():
        issue(0, 0)

    @pl.when(i + 1 < n)
    def _():
        issue(i + 1, (i + 1) % 2)

    slot = i % 2
    gate = gate_ref[...]
    y = None
    for k in range(TOP_K):
        _wait_all_rows(y_hbm, buf.at[slot, k], sem.at[slot, k])
        term = buf[slot, k] * gate[:, k:k + 1]
        y = term if y is None else y + term
    o_ref[...] = _layer_norm(DEEPNORM_ALPHA * h_ref[...] + y, g_ref[...], b_ref[...])


def _moe_combine_ln(y_rows, pos, h, gates, g, b):
    m, d = h.shape
    tm = _tile(m, 256)
    row = lambda i, p: (i, 0)
    fixed = lambda i, p: (0, 0)
    return pl.pallas_call(
        _combine_ln_kernel,
        grid_spec=pltpu.PrefetchScalarGridSpec(
            num_scalar_prefetch=1, grid=(m // tm,),
            in_specs=[pl.BlockSpec((tm, d), row), pl.BlockSpec((tm, LANE), row),
                      pl.BlockSpec((1, d), fixed), pl.BlockSpec((1, d), fixed),
                      pl.BlockSpec(memory_space=pl.ANY)],
            out_specs=pl.BlockSpec((tm, d), row),
            scratch_shapes=[pltpu.VMEM((2, TOP_K, tm, d), F32), pltpu.SemaphoreType.DMA((2, TOP_K))]),
        out_shape=jax.ShapeDtypeStruct((m, d), F32),
        compiler_params=_cparams("arbitrary"), name="moe_combine_ln")(
            pos, h, gates, g.reshape(1, d), b.reshape(1, d), y_rows)


def _moe_routing(top_idx, n_experts, tm):
    n_assign = top_idx.size
    expert_of = top_idx.reshape(n_assign)
    onehot = (expert_of[:, None] == jnp.arange(n_experts, dtype=jnp.int32)[None, :]).astype(jnp.int32)
    csum = jnp.cumsum(onehot, axis=0)
    rank = jnp.sum(onehot * (csum - 1), axis=1)
    counts = csum[-1]
    padded = ((counts + tm - 1) // tm) * tm
    padded_end = jnp.cumsum(padded)
    padded_start = padded_end - padded
    pos = (padded_start[expert_of] + rank).astype(jnp.int32)
    n_tiles = -(-n_assign // tm) + n_experts
    row_token = jnp.zeros((n_tiles * tm,), jnp.int32).at[pos].set(
        jnp.arange(n_assign, dtype=jnp.int32) // TOP_K, unique_indices=True, mode="promise_in_bounds")
    tile_start = jnp.arange(n_tiles, dtype=jnp.int32) * tm
    tile_valid = (tile_start < padded_end[-1]).astype(jnp.int32)
    tile_expert = jnp.minimum(jnp.searchsorted(padded_end, tile_start, side='right'), n_experts - 1)
    last_expert = jnp.max(jnp.where(tile_valid == 1, tile_expert, 0))
    tile_expert = jnp.where(tile_valid == 1, tile_expert, last_expert).astype(jnp.int32)
    return pos, row_token, tile_expert, tile_valid


def _rotate_half_cols(w):
    half = w.shape[-1] // 2
    return jnp.concatenate([-w[..., half:], w[..., :half]], axis=-1)


def kernel(x, positions, w_in0, b_forget, sinks_b, w_out0, ln0_mix_g, ln0_mix_b, w_ffn_gate, w_ffn_up, w_ffn_down, ln0_ffn_g, ln0_ffn_b, w_in1, q_norm_g, w_uq, kv_norm_g, w_ukv, w_out1, ln1_mix_g, ln1_mix_b, w_router, w_moe_gate, w_moe_up, w_moe_down, ln1_ffn_g, ln1_ffn_b):
    batch, seq, d = x.shape
    n = batch * seq
    fox_heads = b_forget.shape[0]
    fox_w = fox_heads * HEAD_DIM
    swa_q = sinks_b.shape[0]
    swa_qw = swa_q * HEAD_DIM
    swa_kvw = (w_in0.shape[1] - 3 * fox_w - fox_heads - swa_qw) // 2
    swa_kv = swa_kvw // HEAD_DIM
    q_rank, kv_rank = q_norm_g.shape[0], kv_norm_g.shape[0]
    mla_heads = w_uq.shape[1] // (MLA_NOPE_DIM + MLA_ROPE_DIM)
    n_experts = w_router.shape[1]

    xf = x.reshape(n, d)
    cos_a, sin_a, tab_m = _rope_tables(positions)

    c0 = 3 * fox_w
    c1 = c0 + fox_heads
    w_qkv = jnp.concatenate([w_in0[:, :c0], w_in0[:, c1:]], axis=1).astype(BF16)
    group_kinds = ([SCALED] * fox_heads + [PLAIN] * (2 * fox_heads)
                   + [ROTARY] * (swa_q + swa_kv) + [PLAIN] * swa_kv)
    qkv = _project(xf, w_qkv, group_kinds, HEAD_DIM ** -0.5 * LOG2_E, cos_a, sin_a, 1024, 512, "proj0")
    cum_log_f = _fox_cum_log_forget(xf, w_in0[:, c0:c1], b_forget, batch, seq)
    o_a = _causal_attention(qkv, qkv, qkv, 0, fox_heads, 2 * fox_heads, HEAD_DIM, HEAD_DIM,
                            fox_heads, batch, seq, ck=cum_log_f, name="fox_attn")
    o_b = _swa_attention(qkv, c0, c0 + swa_qw, c0 + swa_qw + swa_kvw, sinks_b, swa_q, swa_kv, batch, seq,
                         HEAD_DIM ** -0.5)
    w_out0_bf = w_out0.astype(BF16)
    (h1,) = _matmul_residual_ln([(o_a, w_out0_bf[:fox_w]), (o_b, w_out0_bf[fox_w:])], xf,
                                ln0_mix_g, ln0_mix_b, "out0_ln")
    h2 = _swiglu_residual_ln(h1, w_ffn_gate.astype(BF16), w_ffn_up.astype(BF16), w_ffn_down.astype(BF16),
                             ln0_ffn_g, ln0_ffn_b)

    r0 = q_rank + kv_rank
    w_in1_x = jnp.concatenate([w_in1, _rotate_half_cols(w_in1[:, r0:])], axis=1).astype(BF16)
    w_uq_h = w_uq.reshape(q_rank, mla_heads, MLA_NOPE_DIM + MLA_ROPE_DIM)
    w_uq_x = jnp.concatenate([w_uq_h, _rotate_half_cols(w_uq_h[..., MLA_NOPE_DIM:])], axis=-1)
    q_m, k_m, v_m = _mla_projections(
        h2, w_in1_x, q_norm_g, kv_norm_g, tab_m,
        w_uq_x.reshape(q_rank, mla_heads * MLA_QK_PAD).astype(BF16), w_ukv.astype(BF16),
        q_rank, kv_rank, mla_heads, (MLA_NOPE_DIM + MLA_ROPE_DIM) ** -0.5 * LOG2_E)
    o_m = _causal_attention(q_m, k_m, v_m, 0, 0, 0, MLA_QK_PAD, MLA_V_DIM, mla_heads, batch, seq,
                            name="mla_attn")
    h3, idx, gates = _matmul_residual_ln([(o_m, w_out1.astype(BF16))], h2, ln1_mix_g, ln1_mix_b,
                                         "out1_ln_route", w_router=w_router)

    moe_tm = _tile(n * TOP_K, 512)
    pos, row_token, tile_expert, tile_valid = _moe_routing(idx[:, :TOP_K], n_experts, moe_tm)
    xg, (wg_bf, wu_bf) = _gather_rows_cast_weights(h3, row_token, _tile(moe_tm, 256), [w_moe_gate, w_moe_up])
    y_rows = _moe_ffn(xg, wg_bf, wu_bf, w_moe_down, tile_expert, tile_valid, moe_tm)
    out = _moe_combine_ln(y_rows, pos, h3, gates, ln1_ffn_g, ln1_ffn_b)
    return out.reshape(batch, seq, d)
```

```python
import functools
import math

import jax
import jax.numpy as jnp
from jax import lax
from jax.experimental import pallas as pl
from jax.experimental.pallas import tpu as pltpu

F32 = jnp.float32
BF16 = jnp.bfloat16

LANE = 128
HEAD_DIM = 128
SWA_WINDOW = 128
MLA_NOPE_DIM = 128
MLA_ROPE_DIM = 64
MLA_V_DIM = 128
MLA_QK_PAD = 256
ROPE_THETA = 10000.0
TOP_K = 2
LN_EPS = 1e-5
RMS_EPS = 1e-6
DEPTH = 2
DEEPNORM_ALPHA = (2 * DEPTH) ** 0.25
LOG2_E = math.log2(math.e)

NT_DIMS = (((1,), (1,)), ((), ()))


def _tile(n, pref):
    if n <= pref:
        return n
    t = (pref // LANE) * LANE
    while n % t:
        t -= LANE
    return t


def _cparams(*sem):
    return pltpu.CompilerParams(dimension_semantics=sem)


def _dot(a, b):
    return jnp.dot(a, b, preferred_element_type=F32)


def _layer_norm(y, g, b):
    mu = jnp.mean(y, axis=-1, keepdims=True)
    d = y - mu
    var = jnp.mean(d * d, axis=-1, keepdims=True)
    return d * lax.rsqrt(var + LN_EPS) * g + b


def _silu(g):
    return g / (1.0 + jnp.exp(-g))


def _rope_table_kernel(pos_ref, freq_ref, cos_a_ref, sin_a_ref, tab_m_ref):
    pos = pos_ref[...]
    ang_a = pos * freq_ref[0:1, :]
    lane = lax.broadcasted_iota(jnp.int32, ang_a.shape, 1)
    cos_a_ref[...] = jnp.cos(ang_a)
    sin_a_ref[...] = jnp.where(lane < HEAD_DIM // 2, -jnp.sin(ang_a), jnp.sin(ang_a))
    ang_m = pos * freq_ref[1:2, :]
    tab_m_ref[...] = jnp.where(lane < MLA_ROPE_DIM, jnp.cos(ang_m), jnp.sin(ang_m))


def _rope_tables(positions):
    n = positions.size
    pos = positions.astype(F32).reshape(n, 1)
    half_a = HEAD_DIM // 2
    half_m = MLA_ROPE_DIM // 2
    inv_a = ROPE_THETA ** (-2.0 * jnp.arange(half_a, dtype=F32) / HEAD_DIM)
    inv_m = ROPE_THETA ** (-2.0 * jnp.arange(half_m, dtype=F32) / MLA_ROPE_DIM)
    freq = jnp.stack([jnp.tile(inv_a, 2), jnp.tile(inv_m, 4)])
    tm = _tile(n, 1024)
    out = jax.ShapeDtypeStruct((n, LANE), F32)
    row = pl.BlockSpec((tm, LANE), lambda i: (i, 0))
    return pl.pallas_call(
        _rope_table_kernel, grid=(n // tm,),
        in_specs=[pl.BlockSpec((tm, 1), lambda i: (i, 0)), pl.BlockSpec((2, LANE), lambda i: (0, 0))],
        out_specs=[row, row, row], out_shape=[out, out, out],
        compiler_params=_cparams("parallel"), name="rope_tables")(pos, freq)


PLAIN, SCALED, ROTARY = "plain", "scaled", "rotary"


def _proj_kernel(x_ref, w_ref, cos_ref, sin_ref, o_ref, xb_ref, *, tile_kinds, scale):
    j = pl.program_id(1)

    @pl.when(j == 0)
    def _():
        xb_ref[...] = x_ref[...].astype(BF16)

    def tile(kinds):
        acc = _dot(xb_ref[...], w_ref[...])
        for c, kind in enumerate(kinds):
            cols = slice(c * HEAD_DIM, (c + 1) * HEAD_DIM)
            a = acc[:, cols]
            if kind == SCALED:
                a = a * scale
            elif kind == ROTARY:
                a = a * cos_ref[...] + pltpu.roll(a, HEAD_DIM // 2, 1) * sin_ref[...]
            o_ref[:, cols] = a.astype(o_ref.dtype)

    for kinds in sorted(set(tile_kinds)):
        first = tile_kinds.index(kinds)
        count = tile_kinds.count(kinds)
        assert tile_kinds[first:first + count] == (kinds,) * count
        pl.when((j >= first) & (j < first + count))(functools.partial(tile, kinds))


def _project(x, w, group_kinds, scale, cos, sin, tm, tn, name):
    m, k = x.shape
    n = w.shape[1]
    tm, tn = _tile(m, tm), _tile(n, tn)
    per_tile = tn // HEAD_DIM
    tile_kinds = tuple(tuple(group_kinds[t * per_tile:(t + 1) * per_tile]) for t in range(n // tn))
    return pl.pallas_call(
        functools.partial(_proj_kernel, tile_kinds=tile_kinds, scale=scale), grid=(m // tm, n // tn),
        in_specs=[pl.BlockSpec((tm, k), lambda i, j: (i, 0)), pl.BlockSpec((k, tn), lambda i, j: (0, j)),
                  pl.BlockSpec((tm, LANE), lambda i, j: (i, 0)), pl.BlockSpec((tm, LANE), lambda i, j: (i, 0))],
        out_specs=pl.BlockSpec((tm, tn), lambda i, j: (i, j)),
        out_shape=jax.ShapeDtypeStruct((m, n), BF16),
        scratch_shapes=[pltpu.VMEM((tm, k), BF16)],
        compiler_params=_cparams("parallel", "arbitrary"), name=name)(x, w, cos, sin)


def _split3(x):
    hi = x.astype(BF16)
    r = x - hi.astype(F32)
    mid = r.astype(BF16)
    lo = (r - mid.astype(F32)).astype(BF16)
    return hi, mid, lo


def _fox_gate_kernel(x_ref, wf_ref, bf_ref, o_ref, carry_ref, *, n_heads):
    @pl.when(pl.program_id(1) == 0)
    def _():
        carry_ref[...] = jnp.zeros_like(carry_ref)

    f = lax.dot_general(wf_ref[...], x_ref[...].astype(BF16), NT_DIMS, preferred_element_type=F32)
    z = f + bf_ref[...]
    log_f = jnp.minimum(z, 0.0) - jnp.log1p(jnp.exp(-jnp.abs(z)))
    tc = z.shape[1]
    src = lax.broadcasted_iota(jnp.int32, (tc, tc), 0)
    dst = lax.broadcasted_iota(jnp.int32, (tc, tc), 1)
    tri = jnp.where(src <= dst, 1.0, 0.0).astype(BF16)
    hi, mid, lo = _split3(log_f)
    cum = _dot(hi, tri) + _dot(mid, tri) + _dot(lo, tri) + carry_ref[:, 0:1]
    o_ref[0] = cum[:n_heads] * LOG2_E
    carry_ref[...] = jnp.broadcast_to(cum[:, tc - 1:tc], carry_ref.shape)


def _fox_cum_log_forget(x, w_f, b_forget, batch, seq):
    n, d = x.shape
    n_heads = w_f.shape[1]
    rows = 16
    wf_t = jnp.zeros((rows, d), BF16).at[:n_heads].set(w_f.T.astype(BF16))
    bf = jnp.zeros((rows, 1), F32).at[:n_heads, 0].set(b_forget.astype(F32))
    tc = _tile(seq, 512)
    nc = seq // tc
    return pl.pallas_call(
        functools.partial(_fox_gate_kernel, n_heads=n_heads), grid=(batch, nc),
        in_specs=[pl.BlockSpec((tc, d), lambda b, s: (b * nc + s, 0)),
                  pl.BlockSpec((rows, d), lambda b, s: (0, 0)),
                  pl.BlockSpec((rows, 1), lambda b, s: (0, 0))],
        out_specs=pl.BlockSpec((1, n_heads, tc), lambda b, s: (b, 0, s)),
        out_shape=jax.ShapeDtypeStruct((batch, n_heads, seq), F32),
        scratch_shapes=[pltpu.VMEM((rows, LANE), F32)],
        compiler_params=_cparams("parallel", "arbitrary"), name="fox_gate")(x, wf_t, bf)


def _flash_kernel(*refs, tk, dk, dv, heads, has_bias, n_cast_slabs):
    refs = list(refs)
    q_ref, k_ref, v_ref = refs[:3]
    del refs[:3]
    ck_ref = refs.pop(0) if has_bias else None
    w_ref = refs.pop(0) if n_cast_slabs else None
    o_ref = refs.pop(0)
    qi = pl.program_id(2)
    tq = q_ref.shape[0]
    qs = [q_ref[:, g * dk:(g + 1) * dk] for g in range(heads)]

    if n_cast_slabs:
        step = (pl.program_id(0) * pl.num_programs(1) + pl.program_id(1)) * pl.num_programs(2) + qi

        @pl.when(step < n_cast_slabs)
        def _():
            refs[0][...] = w_ref[...].astype(BF16)

    def step(g, state, j, masked):
        m, l, acc = state
        kv_rows = pl.ds(pl.multiple_of(j * tk, tk), tk)
        s = lax.dot_general(qs[g], k_ref[kv_rows, g * dk:(g + 1) * dk], NT_DIMS, preferred_element_type=F32)
        if has_bias:
            s = s - ck_ref[0, g, j]
        if masked:
            row = lax.broadcasted_iota(jnp.int32, s.shape, 0)
            col = lax.broadcasted_iota(jnp.int32, s.shape, 1)
            s = jnp.where(row >= col, s, -jnp.inf)
        m_new = jnp.maximum(m, jnp.max(s, axis=-1, keepdims=True))
        a = jnp.exp2(m - m_new)
        p = jnp.exp2(s - m_new)
        l = a * l + jnp.sum(p, axis=-1, keepdims=True)
        acc = a * acc + _dot(p.astype(BF16), v_ref[kv_rows, g * dv:(g + 1) * dv])
        return m_new, l, acc

    def body(j, carry):
        return tuple(step(g, carry[g], j, False) for g in range(heads))

    init = (jnp.full((tq, 1), -jnp.inf, F32), jnp.zeros((tq, 1), F32), jnp.zeros((tq, dv), F32))
    carry = lax.fori_loop(0, qi, body, (init,) * heads)
    for g in range(heads):
        _, l, acc = step(g, carry[g], qi, True)
        o_ref[:, g * dv:(g + 1) * dv] = (acc / l).astype(o_ref.dtype)


BF16_SUBLANES = 16


def _slab_rows(total_rows, max_slabs):
    rows = BF16_SUBLANES
    while total_rows % rows or total_rows // rows > max_slabs:
        rows += BF16_SUBLANES
    return rows


def _causal_attention(q_arr, k_arr, v_arr, q_col, k_col, v_col, dk, dv, n_heads, batch, seq,
                      ck=None, cast_rider=None, name="attn"):
    heads = 4
    assert n_heads % heads == 0 and q_col % heads == 0 and k_col % heads == 0 and v_col % heads == 0
    t = _tile(seq, 512)
    nq = seq // t
    grid = (batch, n_heads // heads, nq)
    in_specs = [pl.BlockSpec((t, heads * dk), lambda b, h, i: (b * nq + i, q_col // heads + h)),
                pl.BlockSpec((seq, heads * dk), lambda b, h, i: (b, k_col // heads + h)),
                pl.BlockSpec((seq, heads * dv), lambda b, h, i: (b, v_col // heads + h))]
    args = [q_arr, k_arr, v_arr]
    out_specs = [pl.BlockSpec((t, heads * dv), lambda b, h, i: (b * nq + i, h))]
    out_shape = [jax.ShapeDtypeStruct((batch * seq, n_heads * dv), BF16)]
    if ck is not None:
        args.append(ck.reshape(batch, n_heads, nq, 1, t))
        in_specs.append(pl.BlockSpec((1, heads, nq, 1, t), lambda b, h, i: (b, h, 0, 0, 0)))
    n_slabs = 0
    if cast_rider is not None:
        flat = cast_rider.reshape(-1, cast_rider.shape[-1])
        rows = _slab_rows(flat.shape[0], grid[0] * grid[1] * grid[2])
        n_slabs = flat.shape[0] // rows
        slab = pl.BlockSpec((rows, flat.shape[1]),
                            lambda b, h, i: (jnp.minimum((b * grid[1] + h) * nq + i, n_slabs - 1), 0))
        args.append(flat)
        in_specs.append(slab)
        out_specs.append(slab)
        out_shape.append(jax.ShapeDtypeStruct(flat.shape, BF16))
    outs = pl.pallas_call(
        functools.partial(_flash_kernel, tk=t, dk=dk, dv=dv, heads=heads, has_bias=ck is not None,
                          n_cast_slabs=n_slabs),
        grid=grid, in_specs=in_specs, out_specs=out_specs, out_shape=out_shape,
        compiler_params=_cparams("parallel", "parallel", "arbitrary"), name=name)(*args)
    if cast_rider is None:
        return outs[0]
    return outs[0], outs[1].reshape(cast_rider.shape)


def _swa_kernel(sink_ref, q_ref, kp_ref, kc_ref, vp_ref, vc_ref, o_ref, *, scale, n_q, n_kv):
    w = SWA_WINDOW
    qi = lax.broadcasted_iota(jnp.int32, (w, 2 * w), 0)
    ji = lax.broadcasted_iota(jnp.int32, (w, 2 * w), 1)
    has_prev = pl.program_id(1) > 0
    mask = (ji > qi) & (ji <= qi + w) & ((ji >= w) | has_prev)
    group = n_q // n_kv
    for kvh in range(n_kv):
        cols = slice(kvh * HEAD_DIM, (kvh + 1) * HEAD_DIM)
        k = jnp.concatenate([kp_ref[:, cols], kc_ref[:, cols]], axis=0)
        v = jnp.concatenate([vp_ref[:, cols], vc_ref[:, cols]], axis=0)
        for g in range(group):
            h = kvh * group + g
            hc = slice(h * HEAD_DIM, (h + 1) * HEAD_DIM)
            s = lax.dot_general(q_ref[:, hc], k, NT_DIMS, preferred_element_type=F32) * scale
            s = jnp.where(mask, s, -jnp.inf)
            sink = sink_ref[h]
            m = jnp.maximum(jnp.max(s, axis=-1, keepdims=True), sink)
            p = jnp.exp(s - m)
            denom = jnp.sum(p, axis=-1, keepdims=True) + jnp.exp(sink - m)
            o_ref[:, hc] = (_dot(p.astype(BF16), v) / denom).astype(o_ref.dtype)


def _swa_attention(qkv, q_off, k_off, v_off, sinks, n_q, n_kv, batch, seq, scale):
    w = SWA_WINDOW
    nb = seq // w
    qw, kw = n_q * HEAD_DIM, n_kv * HEAD_DIM
    assert q_off % qw == 0 and k_off % kw == 0 and v_off % kw == 0
    q_col, k_col, v_col = q_off // qw, k_off // kw, v_off // kw
    cur = lambda b, n: b * nb + n
    prev = lambda b, n: b * nb + jnp.maximum(n - 1, 0)
    return pl.pallas_call(
        functools.partial(_swa_kernel, scale=scale, n_q=n_q, n_kv=n_kv), grid=(batch, nb),
        in_specs=[pl.BlockSpec(memory_space=pltpu.SMEM),
                  pl.BlockSpec((w, qw), lambda b, n: (cur(b, n), q_col)),
                  pl.BlockSpec((w, kw), lambda b, n: (prev(b, n), k_col)),
                  pl.BlockSpec((w, kw), lambda b, n: (cur(b, n), k_col)),
                  pl.BlockSpec((w, kw), lambda b, n: (prev(b, n), v_col)),
                  pl.BlockSpec((w, kw), lambda b, n: (cur(b, n), v_col))],
        out_specs=pl.BlockSpec((w, qw), lambda b, n: (cur(b, n), 0)),
        out_shape=jax.ShapeDtypeStruct((batch * seq, qw), BF16),
        compiler_params=_cparams("parallel", "arbitrary"), name="swa_attn")(
            sinks.astype(F32), qkv, qkv, qkv, qkv, qkv)


def _top2_route(x, w, n_experts):
    xh = x.astype(BF16)
    xl = (x - xh.astype(F32)).astype(BF16)
    wh = w.astype(BF16)
    wl = (w - wh.astype(F32)).astype(BF16)
    logits = _dot(xh, wh) + _dot(xh, wl) + _dot(xl, wh)
    lane = lax.broadcasted_iota(jnp.int32, logits.shape, 1)
    lane_f = lane.astype(F32)
    l1 = jnp.where(lane < n_experts, logits, -jnp.inf)
    v1 = jnp.max(l1, axis=-1, keepdims=True)
    i1 = jnp.min(jnp.where(l1 == v1, lane_f, float(LANE)), axis=-1, keepdims=True)
    l2 = jnp.where(lane_f == i1, -jnp.inf, l1)
    v2 = jnp.max(l2, axis=-1, keepdims=True)
    i2 = jnp.min(jnp.where(l2 == v2, lane_f, float(LANE)), axis=-1, keepdims=True)
    e2 = jnp.exp(v2 - v1)
    idx = jnp.where(lane == 0, i1, jnp.where(lane == 1, i2, 0.0)).astype(jnp.int32)
    gate = jnp.where(lane == 0, 1.0 / (1.0 + e2), jnp.where(lane == 1, e2 / (1.0 + e2), 0.0))
    return idx, gate


def _mm_res_ln_kernel(*refs, n_pairs, n_experts):
    a_refs, w_refs = refs[:n_pairs], refs[n_pairs:2 * n_pairs]
    if n_experts:
        res_ref, g_ref, b_ref, wr_ref, o_ref, idx_ref, gate_ref = refs[2 * n_pairs:]
    else:
        res_ref, g_ref, b_ref, o_ref = refs[2 * n_pairs:]
    tm = o_ref.shape[0]
    sub = _tile(tm, 256)
    for r in range(tm // sub):
        rows = slice(r * sub, (r + 1) * sub)
        acc = _dot(a_refs[0][rows, :], w_refs[0][...])
        for a_ref, w_ref in zip(a_refs[1:], w_refs[1:]):
            acc = acc + _dot(a_ref[rows, :], w_ref[...])
        y = _layer_norm(DEEPNORM_ALPHA * res_ref[rows, :] + acc, g_ref[...], b_ref[...])
        o_ref[rows, :] = y
        if n_experts:
            idx_ref[rows, :], gate_ref[rows, :] = _top2_route(y, wr_ref[...], n_experts)


def _matmul_residual_ln(pairs, res, g, b, name, w_router=None):
    m, d = res.shape
    tm = _tile(m, 512)
    row = lambda i: (i, 0)
    fixed = lambda i: (0, 0)
    resident = lambda shape: pl.BlockSpec(shape, fixed, pipeline_mode=pl.Buffered(1))
    in_specs = ([pl.BlockSpec((tm, a.shape[1]), row) for a, _ in pairs] + [resident(w.shape) for _, w in pairs]
                + [pl.BlockSpec((tm, d), row), pl.BlockSpec((1, d), fixed), pl.BlockSpec((1, d), fixed)])
    args = [a for a, _ in pairs] + [w for _, w in pairs] + [res, g.reshape(1, d), b.reshape(1, d)]
    out_specs = [pl.BlockSpec((tm, d), row)]
    out_shape = [jax.ShapeDtypeStruct((m, d), F32)]
    n_experts = 0
    if w_router is not None:
        n_experts = w_router.shape[1]
        in_specs.append(resident((d, LANE)))
        args.append(jnp.zeros((d, LANE), F32).at[:, :n_experts].set(w_router.astype(F32)))
        out_specs += [pl.BlockSpec((tm, LANE), row), pl.BlockSpec((tm, LANE), row)]
        out_shape += [jax.ShapeDtypeStruct((m, LANE), jnp.int32), jax.ShapeDtypeStruct((m, LANE), F32)]
    return pl.pallas_call(
        functools.partial(_mm_res_ln_kernel, n_pairs=len(pairs), n_experts=n_experts),
        grid=(m // tm,), in_specs=in_specs, out_specs=out_specs, out_shape=out_shape,
        compiler_params=_cparams("parallel"), name=name)(*args)


def _swiglu(x, wg, wu, wd):
    a = _silu(_dot(x, wg)) * _dot(x, wu)
    return _dot(a.astype(BF16), wd)


def _ffn_ln_kernel(h_ref, wg_ref, wu_ref, wd_ref, g_ref, b_ref, o_ref, x_ref):
    j = pl.program_id(1)

    @pl.when(j == 0)
    def _():
        x_ref[...] = h_ref[...].astype(BF16)
        o_ref[...] = jnp.zeros_like(o_ref)

    o_ref[...] += _swiglu(x_ref[...], wg_ref[...], wu_ref[...], wd_ref[...])

    @pl.when(j == pl.num_programs(1) - 1)
    def _():
        o_ref[...] = _layer_norm(DEEPNORM_ALPHA * h_ref[...] + o_ref[...], g_ref[...], b_ref[...])


def _swiglu_residual_ln(h, wg, wu, wd, g, b):
    m, d = h.shape
    f = wg.shape[1]
    tm, tf = _tile(m, 512), _tile(f, 512)
    row = lambda i, j: (i, 0)
    fixed = lambda i, j: (0, 0)
    return pl.pallas_call(
        _ffn_ln_kernel, grid=(m // tm, f // tf),
        in_specs=[pl.BlockSpec((tm, d), row),
                  pl.BlockSpec((d, tf), lambda i, j: (0, j)), pl.BlockSpec((d, tf), lambda i, j: (0, j)),
                  pl.BlockSpec((tf, d), lambda i, j: (j, 0)),
                  pl.BlockSpec((1, d), fixed), pl.BlockSpec((1, d), fixed)],
        out_specs=pl.BlockSpec((tm, d), row),
        out_shape=jax.ShapeDtypeStruct((m, d), F32),
        scratch_shapes=[pltpu.VMEM((tm, d), BF16)],
        compiler_params=_cparams("parallel", "arbitrary"), name="ffn_ln")(
            h, wg, wu, wd, g.reshape(1, d), b.reshape(1, d))


def _rms_norm(c, g):
    ms = jnp.mean(c * c, axis=-1, keepdims=True)
    return c * lax.rsqrt(ms + RMS_EPS) * g


def _rope64(pair, tab):
    w = pair * tab
    lane = lax.broadcasted_iota(jnp.int32, w.shape, 1)
    return jnp.where(lane < MLA_ROPE_DIM, w + pltpu.roll(w, MLA_ROPE_DIM, 1), 0.0)


def _mla_proj_kernel(x_ref, win_ref, gq_ref, gkv_ref, tab_ref, wq_ref, wkv_ref, q_ref, k_ref, v_ref, *,
                     q_rank, kv_rank, n_heads, q_scale):
    acc = _dot(x_ref[...].astype(BF16), win_ref[...])
    cq = _rms_norm(acc[:, :q_rank], gq_ref[...]).astype(BF16)
    ckv = _rms_norm(acc[:, q_rank:q_rank + kv_rank], gkv_ref[...]).astype(BF16)
    tab = tab_ref[...]
    k_pe = _rope64(acc[:, q_rank + kv_rank:], tab).astype(BF16)
    wkv_w = MLA_NOPE_DIM + MLA_V_DIM
    for h in range(n_heads):
        q0 = h * MLA_QK_PAD
        qh = _dot(cq, wq_ref[:, q0:q0 + MLA_QK_PAD]) * q_scale
        q_ref[:, q0:q0 + MLA_NOPE_DIM] = qh[:, :MLA_NOPE_DIM].astype(BF16)
        q_ref[:, q0 + MLA_NOPE_DIM:q0 + MLA_QK_PAD] = _rope64(qh[:, MLA_NOPE_DIM:], tab).astype(BF16)
        kvh = _dot(ckv, wkv_ref[:, h * wkv_w:(h + 1) * wkv_w])
        k_ref[:, q0:q0 + MLA_NOPE_DIM] = kvh[:, :MLA_NOPE_DIM].astype(BF16)
        k_ref[:, q0 + MLA_NOPE_DIM:q0 + MLA_QK_PAD] = k_pe
        v_ref[:, h * MLA_V_DIM:(h + 1) * MLA_V_DIM] = kvh[:, MLA_NOPE_DIM:].astype(BF16)


def _mla_projections(x, w_in, gq, gkv, tab, w_q, w_kv, q_rank, kv_rank, n_heads, q_scale):
    m, d = x.shape
    tm = _tile(m, 512)
    row = lambda i: (i, 0)
    fixed = lambda i: (0, 0)
    resident = lambda shape: pl.BlockSpec(shape, fixed, pipeline_mode=pl.Buffered(1))
    qk_w, v_w = n_heads * MLA_QK_PAD, n_heads * MLA_V_DIM
    return pl.pallas_call(
        functools.partial(_mla_proj_kernel, q_rank=q_rank, kv_rank=kv_rank, n_heads=n_heads, q_scale=q_scale),
        grid=(m // tm,),
        in_specs=[pl.BlockSpec((tm, d), row), resident(w_in.shape),
                  pl.BlockSpec((1, q_rank), fixed), pl.BlockSpec((1, kv_rank), fixed),
                  pl.BlockSpec((tm, LANE), row), resident(w_q.shape), resident(w_kv.shape)],
        out_specs=[pl.BlockSpec((tm, qk_w), row), pl.BlockSpec((tm, qk_w), row), pl.BlockSpec((tm, v_w), row)],
        out_shape=[jax.ShapeDtypeStruct((m, qk_w), BF16), jax.ShapeDtypeStruct((m, qk_w), BF16),
                   jax.ShapeDtypeStruct((m, v_w), BF16)],
        compiler_params=_cparams("parallel"), name="mla_proj")(
            x, w_in, gq.reshape(1, q_rank).astype(F32), gkv.reshape(1, kv_rank).astype(F32), tab, w_q, w_kv)


GATHER_ISSUE_UNROLL = 8


def _row_copy(src_hbm, buf, sem, src_row, dst_row):
    return pltpu.make_async_copy(src_hbm.at[pl.ds(src_row, 1), :], buf.at[pl.ds(dst_row, 1), :], sem)


def _wait_all_rows(src_hbm, buf, sem):
    pltpu.make_async_copy(src_hbm.at[pl.ds(0, buf.shape[0]), :], buf, sem).wait()


def _gather_rows_kernel(tok_ref, x_hbm, o_ref, buf, sem):
    rows = buf.shape[1]
    i, n = pl.program_id(0), pl.num_programs(0)

    def issue(step, slot):
        def start(r, _):
            _row_copy(x_hbm, buf.at[slot], sem.at[slot], tok_ref[step * rows + r], r).start()
            return 0
        lax.fori_loop(0, rows, start, 0, unroll=GATHER_ISSUE_UNROLL)

    @pl.when(i == 0)
    def _():
        issue(0, 0)

    @pl.when(i + 1 < n)
    def _():
        issue(i + 1, (i + 1) % 2)

    slot = i % 2
    _wait_all_rows(x_hbm, buf.at[slot], sem.at[slot])
    o_ref[...] = buf[slot].astype(o_ref.dtype)


def _gather_rows(h, row_token, rows_per_step):
    n_rows = row_token.shape[0]
    d = h.shape[1]
    return pl.pallas_call(
        _gather_rows_kernel,
        grid_spec=pltpu.PrefetchScalarGridSpec(
            num_scalar_prefetch=1, grid=(n_rows // rows_per_step,),
            in_specs=[pl.BlockSpec(memory_space=pl.ANY)],
            out_specs=pl.BlockSpec((rows_per_step, d), lambda i, tok: (i, 0)),
            scratch_shapes=[pltpu.VMEM((2, rows_per_step, d), F32), pltpu.SemaphoreType.DMA((2,))]),
        out_shape=jax.ShapeDtypeStruct((n_rows, d), BF16),
        compiler_params=_cparams("arbitrary"), name="moe_gather")(row_token, h)


def _moe_ffn_kernel(te_ref, tv_ref, x_ref, wg_ref, wu_ref, wd_ref, o_ref):
    i, j = pl.program_id(0), pl.program_id(1)

    @pl.when(j == 0)
    def _():
        o_ref[...] = jnp.zeros_like(o_ref)

    @pl.when(tv_ref[i] == 1)
    def _():
        o_ref[...] += _swiglu(x_ref[...], wg_ref[0], wu_ref[0], wd_ref[0].astype(BF16))


def _moe_ffn(xg, wg, wu, wd, tile_expert, tile_valid, tm):
    n_rows, d = xg.shape
    f = wg.shape[2]
    tf = _tile(f, 1024)
    nj = f // tf
    col = lambda i, j, te, tv: jnp.where(tv[i] == 1, j, nj - 1)
    return pl.pallas_call(
        _moe_ffn_kernel,
        grid_spec=pltpu.PrefetchScalarGridSpec(
            num_scalar_prefetch=2, grid=(n_rows // tm, nj),
            in_specs=[pl.BlockSpec((tm, d), lambda i, j, te, tv: (i, 0)),
                      pl.BlockSpec((1, d, tf), lambda i, j, te, tv: (te[i], 0, col(i, j, te, tv))),
                      pl.BlockSpec((1, d, tf), lambda i, j, te, tv: (te[i], 0, col(i, j, te, tv))),
                      pl.BlockSpec((1, tf, d), lambda i, j, te, tv: (te[i], col(i, j, te, tv), 0))],
            out_specs=pl.BlockSpec((tm, d), lambda i, j, te, tv: (i, 0))),
        out_shape=jax.ShapeDtypeStruct((n_rows, d), F32),
        compiler_params=_cparams("arbitrary", "arbitrary"), name="moe_ffn")(
            tile_expert, tile_valid, xg, wg, wu, wd)


def _combine_ln_kernel(pos_ref, h_ref, gate_ref, g_ref, b_ref, y_hbm, o_ref, buf, sem):
    rows = buf.shape[2]
    i, n = pl.program_id(0), pl.num_programs(0)

    def issue(step, slot):
        def start(r, _):
            for k in range(TOP_K):
                src = pos_ref[TOP_K * (step * rows + r) + k]
                _row_copy(y_hbm, buf.at[slot, k], sem.at[slot, k], src, r).start()
            return 0
        lax.fori_loop(0, rows, start, 0, unroll=GATHER_ISSUE_UNROLL // TOP_K)

    @pl.when(i == 0)
    def _():
        issue(0, 0)

    @pl.when(i + 1 < n)
    def _():
        issue(i + 1, (i + 1) % 2)

    slot = i % 2
    gate = gate_ref[...]
    y = None
    for k in range(TOP_K):
        _wait_all_rows(y_hbm, buf.at[slot, k], sem.at[slot, k])
        term = buf[slot, k] * gate[:, k:k + 1]
        y = term if y is None else y + term
    o_ref[...] = _layer_norm(DEEPNORM_ALPHA * h_ref[...] + y, g_ref[...], b_ref[...])


def _moe_combine_ln(y_rows, pos, h, gates, g, b):
    m, d = h.shape
    tm = _tile(m, 256)
    row = lambda i, p: (i, 0)
    fixed = lambda i, p: (0, 0)
    return pl.pallas_call(
        _combine_ln_kernel,
        grid_spec=pltpu.PrefetchScalarGridSpec(
            num_scalar_prefetch=1, grid=(m // tm,),
            in_specs=[pl.BlockSpec((tm, d), row), pl.BlockSpec((tm, LANE), row),
                      pl.BlockSpec((1, d), fixed), pl.BlockSpec((1, d), fixed),
                      pl.BlockSpec(memory_space=pl.ANY)],
            out_specs=pl.BlockSpec((tm, d), row),
            scratch_shapes=[pltpu.VMEM((2, TOP_K, tm, d), F32), pltpu.SemaphoreType.DMA((2, TOP_K))]),
        out_shape=jax.ShapeDtypeStruct((m, d), F32),
        compiler_params=_cparams("arbitrary"), name="moe_combine_ln")(
            pos, h, gates, g.reshape(1, d), b.reshape(1, d), y_rows)


def _moe_routing(top_idx, n_experts, tm):
    n_assign = top_idx.size
    expert_of = top_idx.reshape(n_assign)
    onehot = (expert_of[:, None] == jnp.arange(n_experts, dtype=jnp.int32)[None, :]).astype(jnp.int32)
    csum = jnp.cumsum(onehot, axis=0)
    rank = jnp.sum(onehot * (csum - 1), axis=1)
    counts = csum[-1]
    padded = ((counts + tm - 1) // tm) * tm
    padded_end = jnp.cumsum(padded)
    padded_start = padded_end - padded
    pos = (padded_start[expert_of] + rank).astype(jnp.int32)
    n_tiles = -(-n_assign // tm) + n_experts
    row_token = jnp.zeros((n_tiles * tm,), jnp.int32).at[pos].set(
        jnp.arange(n_assign, dtype=jnp.int32) // TOP_K, unique_indices=True, mode="promise_in_bounds")
    tile_start = jnp.arange(n_tiles, dtype=jnp.int32) * tm
    tile_valid = (tile_start < padded_end[-1]).astype(jnp.int32)
    tile_expert = jnp.minimum(jnp.searchsorted(padded_end, tile_start, side='right'), n_experts - 1)
    last_expert = jnp.max(jnp.where(tile_valid == 1, tile_expert, 0))
    tile_expert = jnp.where(tile_valid == 1, tile_expert, last_expert).astype(jnp.int32)
    return pos, row_token, tile_expert, tile_valid


def _rotate_half_cols(w):
    half = w.shape[-1] // 2
    return jnp.concatenate([-w[..., half:], w[..., :half]], axis=-1)


def kernel(x, positions, w_in0, b_forget, sinks_b, w_out0, ln0_mix_g, ln0_mix_b, w_ffn_gate, w_ffn_up, w_ffn_down, ln0_ffn_g, ln0_ffn_b, w_in1, q_norm_g, w_uq, kv_norm_g, w_ukv, w_out1, ln1_mix_g, ln1_mix_b, w_router, w_moe_gate, w_moe_up, w_moe_down, ln1_ffn_g, ln1_ffn_b):
    batch, seq, d = x.shape
    n = batch * seq
    fox_heads = b_forget.shape[0]
    fox_w = fox_heads * HEAD_DIM
    swa_q = sinks_b.shape[0]
    swa_qw = swa_q * HEAD_DIM
    swa_kvw = (w_in0.shape[1] - 3 * fox_w - fox_heads - swa_qw) // 2
    swa_kv = swa_kvw // HEAD_DIM
    q_rank, kv_rank = q_norm_g.shape[0], kv_norm_g.shape[0]
    mla_heads = w_uq.shape[1] // (MLA_NOPE_DIM + MLA_ROPE_DIM)
    n_experts = w_router.shape[1]

    xf = x.reshape(n, d)
    cos_a, sin_a, tab_m = _rope_tables(positions)

    c0 = 3 * fox_w
    c1 = c0 + fox_heads
    w_qkv = jnp.concatenate([w_in0[:, :c0], w_in0[:, c1:]], axis=1).astype(BF16)
    group_kinds = ([SCALED] * fox_heads + [PLAIN] * (2 * fox_heads)
                   + [ROTARY] * (swa_q + swa_kv) + [PLAIN] * swa_kv)
    qkv = _project(xf, w_qkv, group_kinds, HEAD_DIM ** -0.5 * LOG2_E, cos_a, sin_a, 1024, 512, "proj0")
    cum_log_f = _fox_cum_log_forget(xf, w_in0[:, c0:c1], b_forget, batch, seq)
    o_a, w_moe_up_bf = _causal_attention(qkv, qkv, qkv, 0, fox_heads, 2 * fox_heads, HEAD_DIM, HEAD_DIM,
                                         fox_heads, batch, seq, ck=cum_log_f, cast_rider=w_moe_up,
                                         name="fox_attn")
    o_b = _swa_attention(qkv, c0, c0 + swa_qw, c0 + swa_qw + swa_kvw, sinks_b, swa_q, swa_kv, batch, seq,
                         HEAD_DIM ** -0.5)
    w_out0_bf = w_out0.astype(BF16)
    (h1,) = _matmul_residual_ln([(o_a, w_out0_bf[:fox_w]), (o_b, w_out0_bf[fox_w:])], xf,
                                ln0_mix_g, ln0_mix_b, "out0_ln")
    h2 = _swiglu_residual_ln(h1, w_ffn_gate.astype(BF16), w_ffn_up.astype(BF16), w_ffn_down.astype(BF16),
                             ln0_ffn_g, ln0_ffn_b)

    r0 = q_rank + kv_rank
    w_in1_x = jnp.concatenate([w_in1, _rotate_half_cols(w_in1[:, r0:])], axis=1).astype(BF16)
    w_uq_h = w_uq.reshape(q_rank, mla_heads, MLA_NOPE_DIM + MLA_ROPE_DIM)
    w_uq_x = jnp.concatenate([w_uq_h, _rotate_half_cols(w_uq_h[..., MLA_NOPE_DIM:])], axis=-1)
    q_m, k_m, v_m = _mla_projections(
        h2, w_in1_x, q_norm_g, kv_norm_g, tab_m,
        w_uq_x.reshape(q_rank, mla_heads * MLA_QK_PAD).astype(BF16), w_ukv.astype(BF16),
        q_rank, kv_rank, mla_heads, (MLA_NOPE_DIM + MLA_ROPE_DIM) ** -0.5 * LOG2_E)
    o_m, w_moe_gate_bf = _causal_attention(q_m, k_m, v_m, 0, 0, 0, MLA_QK_PAD, MLA_V_DIM, mla_heads, batch,
                                           seq, cast_rider=w_moe_gate, name="mla_attn")
    h3, idx, gates = _matmul_residual_ln([(o_m, w_out1.astype(BF16))], h2, ln1_mix_g, ln1_mix_b,
                                         "out1_ln_route", w_router=w_router)

    moe_tm = _tile(n * TOP_K, 512)
    pos, row_token, tile_expert, tile_valid = _moe_routing(idx[:, :TOP_K], n_experts, moe_tm)
    xg = _gather_rows(h3, row_token, _tile(moe_tm, 256))
    y_rows = _moe_ffn(xg, w_moe_gate_bf, w_moe_up_bf, w_moe_down, tile_expert, tile_valid, moe_tm)
    out = _moe_combine_ln(y_rows, pos, h3, gates, ln1_ffn_g, ln1_ffn_b)
    return out.reshape(batch, seq, d)
```

```python
import functools
import math

import jax
import jax.numpy as jnp
from jax import lax
from jax.experimental import pallas as pl
from jax.experimental.pallas import tpu as pltpu

F32 = jnp.float32
BF16 = jnp.bfloat16

LANE = 128
HEAD_DIM = 128
SWA_WINDOW = 128
MLA_NOPE_DIM = 128
MLA_ROPE_DIM = 64
MLA_V_DIM = 128
MLA_QK_PAD = 256
ROPE_THETA = 10000.0
TOP_K = 2
LN_EPS = 1e-5
RMS_EPS = 1e-6
DEPTH = 2
DEEPNORM_ALPHA = (2 * DEPTH) ** 0.25
LOG2_E = math.log2(math.e)

NT_DIMS = (((1,), (1,)), ((), ()))


def _tile(n, pref):
    if n <= pref:
        return n
    t = (pref // LANE) * LANE
    while n % t:
        t -= LANE
    return t


def _cparams(*sem):
    return pltpu.CompilerParams(dimension_semantics=sem)


def _dot(a, b):
    return jnp.dot(a, b, preferred_element_type=F32)


def _layer_norm(y, g, b):
    mu = jnp.mean(y, axis=-1, keepdims=True)
    d = y - mu
    var = jnp.mean(d * d, axis=-1, keepdims=True)
    return d * lax.rsqrt(var + LN_EPS) * g + b


def _silu(g):
    return g / (1.0 + jnp.exp(-g))


def _rope_table_kernel(pos_ref, freq_ref, cos_a_ref, sin_a_ref, tab_m_ref):
    pos = pos_ref[...]
    ang_a = pos * freq_ref[0:1, :]
    lane = lax.broadcasted_iota(jnp.int32, ang_a.shape, 1)
    cos_a_ref[...] = jnp.cos(ang_a)
    sin_a_ref[...] = jnp.where(lane < HEAD_DIM // 2, -jnp.sin(ang_a), jnp.sin(ang_a))
    ang_m = pos * freq_ref[1:2, :]
    tab_m_ref[...] = jnp.where(lane < MLA_ROPE_DIM, jnp.cos(ang_m), jnp.sin(ang_m))


def _rope_tables(positions):
    n = positions.size
    pos = positions.astype(F32).reshape(n, 1)
    half_a = HEAD_DIM // 2
    half_m = MLA_ROPE_DIM // 2
    inv_a = ROPE_THETA ** (-2.0 * jnp.arange(half_a, dtype=F32) / HEAD_DIM)
    inv_m = ROPE_THETA ** (-2.0 * jnp.arange(half_m, dtype=F32) / MLA_ROPE_DIM)
    freq = jnp.stack([jnp.tile(inv_a, 2), jnp.tile(inv_m, 4)])
    tm = _tile(n, 1024)
    out = jax.ShapeDtypeStruct((n, LANE), F32)
    row = pl.BlockSpec((tm, LANE), lambda i: (i, 0))
    return pl.pallas_call(
        _rope_table_kernel, grid=(n // tm,),
        in_specs=[pl.BlockSpec((tm, 1), lambda i: (i, 0)), pl.BlockSpec((2, LANE), lambda i: (0, 0))],
        out_specs=[row, row, row], out_shape=[out, out, out],
        compiler_params=_cparams("parallel"), name="rope_tables")(pos, freq)


PLAIN, SCALED, ROTARY = "plain", "scaled", "rotary"


def _proj_kernel(x_ref, w_ref, cos_ref, sin_ref, o_ref, xb_ref, *, tile_kinds, scale):
    j = pl.program_id(1)

    @pl.when(j == 0)
    def _():
        xb_ref[...] = x_ref[...].astype(BF16)

    def tile(kinds):
        acc = _dot(xb_ref[...], w_ref[...])
        for c, kind in enumerate(kinds):
            cols = slice(c * HEAD_DIM, (c + 1) * HEAD_DIM)
            a = acc[:, cols]
            if kind == SCALED:
                a = a * scale
            elif kind == ROTARY:
                a = a * cos_ref[...] + pltpu.roll(a, HEAD_DIM // 2, 1) * sin_ref[...]
            o_ref[:, cols] = a.astype(o_ref.dtype)

    for kinds in sorted(set(tile_kinds)):
        first = tile_kinds.index(kinds)
        count = tile_kinds.count(kinds)
        assert tile_kinds[first:first + count] == (kinds,) * count
        pl.when((j >= first) & (j < first + count))(functools.partial(tile, kinds))


def _project(x, w, group_kinds, scale, cos, sin, tm, tn, name):
    m, k = x.shape
    n = w.shape[1]
    tm, tn = _tile(m, tm), _tile(n, tn)
    per_tile = tn // HEAD_DIM
    tile_kinds = tuple(tuple(group_kinds[t * per_tile:(t + 1) * per_tile]) for t in range(n // tn))
    return pl.pallas_call(
        functools.partial(_proj_kernel, tile_kinds=tile_kinds, scale=scale), grid=(m // tm, n // tn),
        in_specs=[pl.BlockSpec((tm, k), lambda i, j: (i, 0)), pl.BlockSpec((k, tn), lambda i, j: (0, j)),
                  pl.BlockSpec((tm, LANE), lambda i, j: (i, 0)), pl.BlockSpec((tm, LANE), lambda i, j: (i, 0))],
        out_specs=pl.BlockSpec((tm, tn), lambda i, j: (i, j)),
        out_shape=jax.ShapeDtypeStruct((m, n), BF16),
        scratch_shapes=[pltpu.VMEM((tm, k), BF16)],
        compiler_params=_cparams("parallel", "arbitrary"), name=name)(x, w, cos, sin)


def _split3(x):
    hi = x.astype(BF16)
    r = x - hi.astype(F32)
    mid = r.astype(BF16)
    lo = (r - mid.astype(F32)).astype(BF16)
    return hi, mid, lo


def _fox_gate_kernel(x_ref, wf_ref, bf_ref, o_ref, carry_ref, *, n_heads):
    @pl.when(pl.program_id(1) == 0)
    def _():
        carry_ref[...] = jnp.zeros_like(carry_ref)

    f = lax.dot_general(wf_ref[...], x_ref[...].astype(BF16), NT_DIMS, preferred_element_type=F32)
    z = f + bf_ref[...]
    log_f = jnp.minimum(z, 0.0) - jnp.log1p(jnp.exp(-jnp.abs(z)))
    tc = z.shape[1]
    src = lax.broadcasted_iota(jnp.int32, (tc, tc), 0)
    dst = lax.broadcasted_iota(jnp.int32, (tc, tc), 1)
    tri = jnp.where(src <= dst, 1.0, 0.0).astype(BF16)
    hi, mid, lo = _split3(log_f)
    cum = _dot(hi, tri) + _dot(mid, tri) + _dot(lo, tri) + carry_ref[:, 0:1]
    o_ref[0] = cum[:n_heads] * LOG2_E
    carry_ref[...] = jnp.broadcast_to(cum[:, tc - 1:tc], carry_ref.shape)


def _fox_cum_log_forget(x, w_f, b_forget, batch, seq):
    n, d = x.shape
    n_heads = w_f.shape[1]
    rows = 16
    wf_t = jnp.zeros((rows, d), BF16).at[:n_heads].set(w_f.T.astype(BF16))
    bf = jnp.zeros((rows, 1), F32).at[:n_heads, 0].set(b_forget.astype(F32))
    tc = _tile(seq, 512)
    nc = seq // tc
    return pl.pallas_call(
        functools.partial(_fox_gate_kernel, n_heads=n_heads), grid=(batch, nc),
        in_specs=[pl.BlockSpec((tc, d), lambda b, s: (b * nc + s, 0)),
                  pl.BlockSpec((rows, d), lambda b, s: (0, 0)),
                  pl.BlockSpec((rows, 1), lambda b, s: (0, 0))],
        out_specs=pl.BlockSpec((1, n_heads, tc), lambda b, s: (b, 0, s)),
        out_shape=jax.ShapeDtypeStruct((batch, n_heads, seq), F32),
        scratch_shapes=[pltpu.VMEM((rows, LANE), F32)],
        compiler_params=_cparams("parallel", "arbitrary"), name="fox_gate")(x, wf_t, bf)


def _flash_kernel(*refs, tk, dk, dv, heads, has_bias, n_cast_slabs):
    refs = list(refs)
    q_ref, k_ref, v_ref = refs[:3]
    del refs[:3]
    ck_ref = refs.pop(0) if has_bias else None
    w_ref = refs.pop(0) if n_cast_slabs else None
    o_ref = refs.pop(0)
    qi = pl.program_id(2)
    tq = q_ref.shape[0]
    qs = [q_ref[:, g * dk:(g + 1) * dk] for g in range(heads)]

    if n_cast_slabs:
        step = (pl.program_id(0) * pl.num_programs(1) + pl.program_id(1)) * pl.num_programs(2) + qi
        _cast_rider_step(step, n_cast_slabs, w_ref, refs[0])

    def step(g, state, j, masked):
        m, l, acc = state
        kv_rows = pl.ds(pl.multiple_of(j * tk, tk), tk)
        s = lax.dot_general(qs[g], k_ref[kv_rows, g * dk:(g + 1) * dk], NT_DIMS, preferred_element_type=F32)
        if has_bias:
            s = s - ck_ref[0, g, j]
        if masked:
            row = lax.broadcasted_iota(jnp.int32, s.shape, 0)
            col = lax.broadcasted_iota(jnp.int32, s.shape, 1)
            s = jnp.where(row >= col, s, -jnp.inf)
        m_new = jnp.maximum(m, jnp.max(s, axis=-1, keepdims=True))
        a = jnp.exp2(m - m_new)
        p = jnp.exp2(s - m_new)
        l = a * l + jnp.sum(p, axis=-1, keepdims=True)
        acc = a * acc + _dot(p.astype(BF16), v_ref[kv_rows, g * dv:(g + 1) * dv])
        return m_new, l, acc

    def body(j, carry):
        return tuple(step(g, carry[g], j, False) for g in range(heads))

    init = (jnp.full((tq, 1), -jnp.inf, F32), jnp.zeros((tq, 1), F32), jnp.zeros((tq, dv), F32))
    carry = lax.fori_loop(0, qi, body, (init,) * heads)
    for g in range(heads):
        _, l, acc = step(g, carry[g], qi, True)
        o_ref[:, g * dv:(g + 1) * dv] = (acc / l).astype(o_ref.dtype)


BF16_SUBLANES = 16


def _slab_rows(total_rows, max_slabs):
    rows = BF16_SUBLANES
    while total_rows % rows or total_rows // rows > max_slabs:
        rows += BF16_SUBLANES
    return rows


def _cast_rider(w, grid, step_of):
    flat = w.reshape(-1, w.shape[-1])
    rows = _slab_rows(flat.shape[0], math.prod(grid))
    n_slabs = flat.shape[0] // rows
    spec = pl.BlockSpec((rows, flat.shape[1]), lambda *g: (jnp.minimum(step_of(*g), n_slabs - 1), 0))
    return flat, spec, jax.ShapeDtypeStruct(flat.shape, BF16), n_slabs


def _cast_rider_step(step, n_slabs, w_ref, o_ref):
    @pl.when(step < n_slabs)
    def _():
        o_ref[...] = w_ref[...].astype(o_ref.dtype)


def _causal_attention(q_arr, k_arr, v_arr, q_col, k_col, v_col, dk, dv, n_heads, batch, seq,
                      ck=None, cast_rider=None, name="attn"):
    heads = 4
    assert n_heads % heads == 0 and q_col % heads == 0 and k_col % heads == 0 and v_col % heads == 0
    t = _tile(seq, 512)
    nq = seq // t
    grid = (batch, n_heads // heads, nq)
    in_specs = [pl.BlockSpec((t, heads * dk), lambda b, h, i: (b * nq + i, q_col // heads + h)),
                pl.BlockSpec((seq, heads * dk), lambda b, h, i: (b, k_col // heads + h)),
                pl.BlockSpec((seq, heads * dv), lambda b, h, i: (b, v_col // heads + h))]
    args = [q_arr, k_arr, v_arr]
    out_specs = [pl.BlockSpec((t, heads * dv), lambda b, h, i: (b * nq + i, h))]
    out_shape = [jax.ShapeDtypeStruct((batch * seq, n_heads * dv), BF16)]
    if ck is not None:
        args.append(ck.reshape(batch, n_heads, nq, 1, t))
        in_specs.append(pl.BlockSpec((1, heads, nq, 1, t), lambda b, h, i: (b, h, 0, 0, 0)))
    n_slabs = 0
    if cast_rider is not None:
        flat, slab, flat_bf, n_slabs = _cast_rider(cast_rider, grid, lambda b, h, i: (b * grid[1] + h) * nq + i)
        args.append(flat)
        in_specs.append(slab)
        out_specs.append(slab)
        out_shape.append(flat_bf)
    outs = pl.pallas_call(
        functools.partial(_flash_kernel, tk=t, dk=dk, dv=dv, heads=heads, has_bias=ck is not None,
                          n_cast_slabs=n_slabs),
        grid=grid, in_specs=in_specs, out_specs=out_specs, out_shape=out_shape,
        compiler_params=_cparams("parallel", "parallel", "arbitrary"), name=name)(*args)
    if cast_rider is None:
        return outs[0]
    return outs[0], outs[1].reshape(cast_rider.shape)


def _swa_kernel(sink_ref, q_ref, kp_ref, kc_ref, vp_ref, vc_ref, o_ref, *, scale, n_q, n_kv):
    w = SWA_WINDOW
    qi = lax.broadcasted_iota(jnp.int32, (w, 2 * w), 0)
    ji = lax.broadcasted_iota(jnp.int32, (w, 2 * w), 1)
    has_prev = pl.program_id(1) > 0
    mask = (ji > qi) & (ji <= qi + w) & ((ji >= w) | has_prev)
    group = n_q // n_kv
    for kvh in range(n_kv):
        cols = slice(kvh * HEAD_DIM, (kvh + 1) * HEAD_DIM)
        k = jnp.concatenate([kp_ref[:, cols], kc_ref[:, cols]], axis=0)
        v = jnp.concatenate([vp_ref[:, cols], vc_ref[:, cols]], axis=0)
        for g in range(group):
            h = kvh * group + g
            hc = slice(h * HEAD_DIM, (h + 1) * HEAD_DIM)
            s = lax.dot_general(q_ref[:, hc], k, NT_DIMS, preferred_element_type=F32) * scale
            s = jnp.where(mask, s, -jnp.inf)
            sink = sink_ref[h]
            m = jnp.maximum(jnp.max(s, axis=-1, keepdims=True), sink)
            p = jnp.exp(s - m)
            denom = jnp.sum(p, axis=-1, keepdims=True) + jnp.exp(sink - m)
            o_ref[:, hc] = (_dot(p.astype(BF16), v) / denom).astype(o_ref.dtype)


def _swa_attention(qkv, q_off, k_off, v_off, sinks, n_q, n_kv, batch, seq, scale):
    w = SWA_WINDOW
    nb = seq // w
    qw, kw = n_q * HEAD_DIM, n_kv * HEAD_DIM
    assert q_off % qw == 0 and k_off % kw == 0 and v_off % kw == 0
    q_col, k_col, v_col = q_off // qw, k_off // kw, v_off // kw
    cur = lambda b, n: b * nb + n
    prev = lambda b, n: b * nb + jnp.maximum(n - 1, 0)
    return pl.pallas_call(
        functools.partial(_swa_kernel, scale=scale, n_q=n_q, n_kv=n_kv), grid=(batch, nb),
        in_specs=[pl.BlockSpec(memory_space=pltpu.SMEM),
                  pl.BlockSpec((w, qw), lambda b, n: (cur(b, n), q_col)),
                  pl.BlockSpec((w, kw), lambda b, n: (prev(b, n), k_col)),
                  pl.BlockSpec((w, kw), lambda b, n: (cur(b, n), k_col)),
                  pl.BlockSpec((w, kw), lambda b, n: (prev(b, n), v_col)),
                  pl.BlockSpec((w, kw), lambda b, n: (cur(b, n), v_col))],
        out_specs=pl.BlockSpec((w, qw), lambda b, n: (cur(b, n), 0)),
        out_shape=jax.ShapeDtypeStruct((batch * seq, qw), BF16),
        compiler_params=_cparams("parallel", "arbitrary"), name="swa_attn")(
            sinks.astype(F32), qkv, qkv, qkv, qkv, qkv)


def _top2_route(x, w, n_experts):
    xh = x.astype(BF16)
    xl = (x - xh.astype(F32)).astype(BF16)
    wh = w.astype(BF16)
    wl = (w - wh.astype(F32)).astype(BF16)
    logits = _dot(xh, wh) + _dot(xh, wl) + _dot(xl, wh)
    lane = lax.broadcasted_iota(jnp.int32, logits.shape, 1)
    lane_f = lane.astype(F32)
    l1 = jnp.where(lane < n_experts, logits, -jnp.inf)
    v1 = jnp.max(l1, axis=-1, keepdims=True)
    i1 = jnp.min(jnp.where(l1 == v1, lane_f, float(LANE)), axis=-1, keepdims=True)
    l2 = jnp.where(lane_f == i1, -jnp.inf, l1)
    v2 = jnp.max(l2, axis=-1, keepdims=True)
    i2 = jnp.min(jnp.where(l2 == v2, lane_f, float(LANE)), axis=-1, keepdims=True)
    e2 = jnp.exp(v2 - v1)
    idx = jnp.where(lane == 0, i1, jnp.where(lane == 1, i2, 0.0)).astype(jnp.int32)
    gate = jnp.where(lane == 0, 1.0 / (1.0 + e2), jnp.where(lane == 1, e2 / (1.0 + e2), 0.0))
    return idx, gate


def _mm_res_ln_kernel(*refs, n_pairs, n_experts):
    a_refs, w_refs = refs[:n_pairs], refs[n_pairs:2 * n_pairs]
    if n_experts:
        res_ref, g_ref, b_ref, wr_ref, o_ref, idx_ref, gate_ref = refs[2 * n_pairs:]
    else:
        res_ref, g_ref, b_ref, o_ref = refs[2 * n_pairs:]
    tm = o_ref.shape[0]
    sub = _tile(tm, 256)
    for r in range(tm // sub):
        rows = slice(r * sub, (r + 1) * sub)
        acc = _dot(a_refs[0][rows, :], w_refs[0][...])
        for a_ref, w_ref in zip(a_refs[1:], w_refs[1:]):
            acc = acc + _dot(a_ref[rows, :], w_ref[...])
        y = _layer_norm(DEEPNORM_ALPHA * res_ref[rows, :] + acc, g_ref[...], b_ref[...])
        o_ref[rows, :] = y
        if n_experts:
            idx_ref[rows, :], gate_ref[rows, :] = _top2_route(y, wr_ref[...], n_experts)


def _matmul_residual_ln(pairs, res, g, b, name, w_router=None):
    m, d = res.shape
    tm = _tile(m, 512)
    row = lambda i: (i, 0)
    fixed = lambda i: (0, 0)
    resident = lambda shape: pl.BlockSpec(shape, fixed, pipeline_mode=pl.Buffered(1))
    in_specs = ([pl.BlockSpec((tm, a.shape[1]), row) for a, _ in pairs] + [resident(w.shape) for _, w in pairs]
                + [pl.BlockSpec((tm, d), row), pl.BlockSpec((1, d), fixed), pl.BlockSpec((1, d), fixed)])
    args = [a for a, _ in pairs] + [w for _, w in pairs] + [res, g.reshape(1, d), b.reshape(1, d)]
    out_specs = [pl.BlockSpec((tm, d), row)]
    out_shape = [jax.ShapeDtypeStruct((m, d), F32)]
    n_experts = 0
    if w_router is not None:
        n_experts = w_router.shape[1]
        in_specs.append(resident((d, LANE)))
        args.append(jnp.zeros((d, LANE), F32).at[:, :n_experts].set(w_router.astype(F32)))
        out_specs += [pl.BlockSpec((tm, LANE), row), pl.BlockSpec((tm, LANE), row)]
        out_shape += [jax.ShapeDtypeStruct((m, LANE), jnp.int32), jax.ShapeDtypeStruct((m, LANE), F32)]
    return pl.pallas_call(
        functools.partial(_mm_res_ln_kernel, n_pairs=len(pairs), n_experts=n_experts),
        grid=(m // tm,), in_specs=in_specs, out_specs=out_specs, out_shape=out_shape,
        compiler_params=_cparams("parallel"), name=name)(*args)


def _swiglu(x, wg, wu, wd):
    a = _silu(_dot(x, wg)) * _dot(x, wu)
    return _dot(a.astype(BF16), wd)


def _ffn_ln_kernel(h_ref, wg_ref, wu_ref, wd_ref, g_ref, b_ref, rider_ref, o_ref, rider_o_ref, x_ref, *,
                   n_cast_slabs):
    j = pl.program_id(1)
    _cast_rider_step(pl.program_id(0) * pl.num_programs(1) + j, n_cast_slabs, rider_ref, rider_o_ref)

    @pl.when(j == 0)
    def _():
        x_ref[...] = h_ref[...].astype(BF16)
        o_ref[...] = jnp.zeros_like(o_ref)

    o_ref[...] += _swiglu(x_ref[...], wg_ref[...], wu_ref[...], wd_ref[...])

    @pl.when(j == pl.num_programs(1) - 1)
    def _():
        o_ref[...] = _layer_norm(DEEPNORM_ALPHA * h_ref[...] + o_ref[...], g_ref[...], b_ref[...])


def _swiglu_residual_ln(h, wg, wu, wd, g, b, cast_rider):
    m, d = h.shape
    f = wg.shape[1]
    tm, tf = _tile(m, 512), _tile(f, 512)
    grid = (m // tm, f // tf)
    row = lambda i, j: (i, 0)
    fixed = lambda i, j: (0, 0)
    flat, slab, flat_bf, n_slabs = _cast_rider(cast_rider, grid, lambda i, j: i * grid[1] + j)
    out, cast = pl.pallas_call(
        functools.partial(_ffn_ln_kernel, n_cast_slabs=n_slabs), grid=grid,
        in_specs=[pl.BlockSpec((tm, d), row),
                  pl.BlockSpec((d, tf), lambda i, j: (0, j)), pl.BlockSpec((d, tf), lambda i, j: (0, j)),
                  pl.BlockSpec((tf, d), lambda i, j: (j, 0)),
                  pl.BlockSpec((1, d), fixed), pl.BlockSpec((1, d), fixed), slab],
        out_specs=[pl.BlockSpec((tm, d), row), slab],
        out_shape=[jax.ShapeDtypeStruct((m, d), F32), flat_bf],
        scratch_shapes=[pltpu.VMEM((tm, d), BF16)],
        compiler_params=_cparams("arbitrary", "arbitrary"), name="ffn_ln")(
            h, wg, wu, wd, g.reshape(1, d), b.reshape(1, d), flat)
    return out, cast.reshape(cast_rider.shape)


def _rms_norm(c, g):
    ms = jnp.mean(c * c, axis=-1, keepdims=True)
    return c * lax.rsqrt(ms + RMS_EPS) * g


def _rope64(pair, tab):
    w = pair * tab
    lane = lax.broadcasted_iota(jnp.int32, w.shape, 1)
    return jnp.where(lane < MLA_ROPE_DIM, w + pltpu.roll(w, MLA_ROPE_DIM, 1), 0.0)


def _mla_proj_kernel(x_ref, win_ref, gq_ref, gkv_ref, tab_ref, wq_ref, wkv_ref, q_ref, k_ref, v_ref, *,
                     q_rank, kv_rank, n_heads, q_scale):
    acc = _dot(x_ref[...].astype(BF16), win_ref[...])
    cq = _rms_norm(acc[:, :q_rank], gq_ref[...]).astype(BF16)
    ckv = _rms_norm(acc[:, q_rank:q_rank + kv_rank], gkv_ref[...]).astype(BF16)
    tab = tab_ref[...]
    k_pe = _rope64(acc[:, q_rank + kv_rank:], tab).astype(BF16)
    wkv_w = MLA_NOPE_DIM + MLA_V_DIM
    for h in range(n_heads):
        q0 = h * MLA_QK_PAD
        qh = _dot(cq, wq_ref[:, q0:q0 + MLA_QK_PAD]) * q_scale
        q_ref[:, q0:q0 + MLA_NOPE_DIM] = qh[:, :MLA_NOPE_DIM].astype(BF16)
        q_ref[:, q0 + MLA_NOPE_DIM:q0 + MLA_QK_PAD] = _rope64(qh[:, MLA_NOPE_DIM:], tab).astype(BF16)
        kvh = _dot(ckv, wkv_ref[:, h * wkv_w:(h + 1) * wkv_w])
        k_ref[:, q0:q0 + MLA_NOPE_DIM] = kvh[:, :MLA_NOPE_DIM].astype(BF16)
        k_ref[:, q0 + MLA_NOPE_DIM:q0 + MLA_QK_PAD] = k_pe
        v_ref[:, h * MLA_V_DIM:(h + 1) * MLA_V_DIM] = kvh[:, MLA_NOPE_DIM:].astype(BF16)


def _mla_projections(x, w_in, gq, gkv, tab, w_q, w_kv, q_rank, kv_rank, n_heads, q_scale):
    m, d = x.shape
    tm = _tile(m, 512)
    row = lambda i: (i, 0)
    fixed = lambda i: (0, 0)
    resident = lambda shape: pl.BlockSpec(shape, fixed, pipeline_mode=pl.Buffered(1))
    qk_w, v_w = n_heads * MLA_QK_PAD, n_heads * MLA_V_DIM
    return pl.pallas_call(
        functools.partial(_mla_proj_kernel, q_rank=q_rank, kv_rank=kv_rank, n_heads=n_heads, q_scale=q_scale),
        grid=(m // tm,),
        in_specs=[pl.BlockSpec((tm, d), row), resident(w_in.shape),
                  pl.BlockSpec((1, q_rank), fixed), pl.BlockSpec((1, kv_rank), fixed),
                  pl.BlockSpec((tm, LANE), row), resident(w_q.shape), resident(w_kv.shape)],
        out_specs=[pl.BlockSpec((tm, qk_w), row), pl.BlockSpec((tm, qk_w), row), pl.BlockSpec((tm, v_w), row)],
        out_shape=[jax.ShapeDtypeStruct((m, qk_w), BF16), jax.ShapeDtypeStruct((m, qk_w), BF16),
                   jax.ShapeDtypeStruct((m, v_w), BF16)],
        compiler_params=_cparams("parallel"), name="mla_proj")(
            x, w_in, gq.reshape(1, q_rank).astype(F32), gkv.reshape(1, kv_rank).astype(F32), tab, w_q, w_kv)


GATHER_ISSUE_UNROLL = 8


def _row_copy(src_hbm, buf, sem, src_row, dst_row):
    return pltpu.make_async_copy(src_hbm.at[pl.ds(src_row, 1), :], buf.at[pl.ds(dst_row, 1), :], sem)


def _wait_all_rows(src_hbm, buf, sem):
    pltpu.make_async_copy(src_hbm.at[pl.ds(0, buf.shape[0]), :], buf, sem).wait()


def _start_row_gather(src_hbm, buf, sem, index_of):
    def start(r, _):
        _row_copy(src_hbm, buf, sem, index_of(r), r).start()
        return 0
    lax.fori_loop(0, buf.shape[0], start, 0, unroll=GATHER_ISSUE_UNROLL)


def _moe_ffn_kernel(tok_ref, te_ref, tv_ref, h_hbm, wg_ref, wu_ref, wd_ref, o_ref, rows_buf, x_ref, sem):
    i, j = pl.program_id(0), pl.program_id(1)
    tm = x_ref.shape[0]

    def gather(tile, slot):
        _start_row_gather(h_hbm, rows_buf.at[slot], sem.at[slot], lambda r: tok_ref[tile * tm + r])

    @pl.when(j == 0)
    def _():
        @pl.when(i == 0)
        def _():
            gather(0, 0)

        @pl.when(i + 1 < pl.num_programs(0))
        def _():
            gather(i + 1, (i + 1) % 2)

        slot = i % 2
        _wait_all_rows(h_hbm, rows_buf.at[slot], sem.at[slot])
        x_ref[...] = rows_buf[slot].astype(BF16)
        o_ref[...] = jnp.zeros_like(o_ref)

    @pl.when(tv_ref[i] == 1)
    def _():
        o_ref[...] += _swiglu(x_ref[...], wg_ref[0], wu_ref[0], wd_ref[0])


def _moe_ffn(h, row_token, wg, wu, wd, tile_expert, tile_valid, tm):
    n_rows = row_token.shape[0]
    d = h.shape[1]
    f = wg.shape[2]
    tf = _tile(f, 1024)
    nj = f // tf
    col = lambda i, j, tok, te, tv: jnp.where(tv[i] == 1, j, nj - 1)
    return pl.pallas_call(
        _moe_ffn_kernel,
        grid_spec=pltpu.PrefetchScalarGridSpec(
            num_scalar_prefetch=3, grid=(n_rows // tm, nj),
            in_specs=[pl.BlockSpec(memory_space=pl.ANY),
                      pl.BlockSpec((1, d, tf), lambda i, j, tok, te, tv: (te[i], 0, col(i, j, tok, te, tv))),
                      pl.BlockSpec((1, d, tf), lambda i, j, tok, te, tv: (te[i], 0, col(i, j, tok, te, tv))),
                      pl.BlockSpec((1, tf, d), lambda i, j, tok, te, tv: (te[i], col(i, j, tok, te, tv), 0))],
            out_specs=pl.BlockSpec((tm, d), lambda i, j, tok, te, tv: (i, 0)),
            scratch_shapes=[pltpu.VMEM((2, tm, d), F32), pltpu.VMEM((tm, d), BF16),
                            pltpu.SemaphoreType.DMA((2,))]),
        out_shape=jax.ShapeDtypeStruct((n_rows, d), F32),
        compiler_params=_cparams("arbitrary", "arbitrary"), name="moe_ffn")(
            row_token, tile_expert, tile_valid, h, wg, wu, wd)


def _combine_ln_kernel(pos_ref, h_ref, gate_ref, g_ref, b_ref, y_hbm, o_ref, buf, sem):
    rows = buf.shape[2]
    i, n = pl.program_id(0), pl.num_programs(0)

    def issue(step, slot):
        for k in range(TOP_K):
            _start_row_gather(y_hbm, buf.at[slot, k], sem.at[slot, k],
                              lambda r: pos_ref[TOP_K * (step * rows + r) + k])

    @pl.when(i == 0)
    def _():
        issue(0, 0)

    @pl.when(i + 1 < n)
    def _():
        issue(i + 1, (i + 1) % 2)

    slot = i % 2
    gate = gate_ref[...]
    y = None
    for k in range(TOP_K):
        _wait_all_rows(y_hbm, buf.at[slot, k], sem.at[slot, k])
        term = buf[slot, k] * gate[:, k:k + 1]
        y = term if y is None else y + term
    o_ref[...] = _layer_norm(DEEPNORM_ALPHA * h_ref[...] + y, g_ref[...], b_ref[...])


def _moe_combine_ln(y_rows, pos, h, gates, g, b):
    m, d = h.shape
    tm = _tile(m, 256)
    row = lambda i, p: (i, 0)
    fixed = lambda i, p: (0, 0)
    return pl.pallas_call(
        _combine_ln_kernel,
        grid_spec=pltpu.PrefetchScalarGridSpec(
            num_scalar_prefetch=1, grid=(m // tm,),
            in_specs=[pl.BlockSpec((tm, d), row), pl.BlockSpec((tm, LANE), row),
                      pl.BlockSpec((1, d), fixed), pl.BlockSpec((1, d), fixed),
                      pl.BlockSpec(memory_space=pl.ANY)],
            out_specs=pl.BlockSpec((tm, d), row),
            scratch_shapes=[pltpu.VMEM((2, TOP_K, tm, d), F32), pltpu.SemaphoreType.DMA((2, TOP_K))]),
        out_shape=jax.ShapeDtypeStruct((m, d), F32),
        compiler_params=_cparams("arbitrary"), name="moe_combine_ln")(
            pos, h, gates, g.reshape(1, d), b.reshape(1, d), y_rows)


def _moe_routing(top_idx, n_experts, tm):
    n_assign = top_idx.size
    expert_of = top_idx.reshape(n_assign)
    onehot = (expert_of[:, None] == jnp.arange(n_experts, dtype=jnp.int32)[None, :]).astype(jnp.int32)
    csum = jnp.cumsum(onehot, axis=0)
    rank = jnp.sum(onehot * (csum - 1), axis=1)
    counts = csum[-1]
    padded = ((counts + tm - 1) // tm) * tm
    padded_end = jnp.cumsum(padded)
    padded_start = padded_end - padded
    pos = (padded_start[expert_of] + rank).astype(jnp.int32)
    n_tiles = -(-n_assign // tm) + n_experts
    row_token = jnp.zeros((n_tiles * tm,), jnp.int32).at[pos].set(
        jnp.arange(n_assign, dtype=jnp.int32) // TOP_K, unique_indices=True, mode="promise_in_bounds")
    tile_start = jnp.arange(n_tiles, dtype=jnp.int32) * tm
    tile_valid = (tile_start < padded_end[-1]).astype(jnp.int32)
    tile_expert = jnp.minimum(jnp.searchsorted(padded_end, tile_start, side='right'), n_experts - 1)
    last_expert = jnp.max(jnp.where(tile_valid == 1, tile_expert, 0))
    tile_expert = jnp.where(tile_valid == 1, tile_expert, last_expert).astype(jnp.int32)
    return pos, row_token, tile_expert, tile_valid


def _rotate_half_cols(w):
    half = w.shape[-1] // 2
    return jnp.concatenate([-w[..., half:], w[..., :half]], axis=-1)


def kernel(x, positions, w_in0, b_forget, sinks_b, w_out0, ln0_mix_g, ln0_mix_b, w_ffn_gate, w_ffn_up, w_ffn_down, ln0_ffn_g, ln0_ffn_b, w_in1, q_norm_g, w_uq, kv_norm_g, w_ukv, w_out1, ln1_mix_g, ln1_mix_b, w_router, w_moe_gate, w_moe_up, w_moe_down, ln1_ffn_g, ln1_ffn_b):
    batch, seq, d = x.shape
    n = batch * seq
    fox_heads = b_forget.shape[0]
    fox_w = fox_heads * HEAD_DIM
    swa_q = sinks_b.shape[0]
    swa_qw = swa_q * HEAD_DIM
    swa_kvw = (w_in0.shape[1] - 3 * fox_w - fox_heads - swa_qw) // 2
    swa_kv = swa_kvw // HEAD_DIM
    q_rank, kv_rank = q_norm_g.shape[0], kv_norm_g.shape[0]
    mla_heads = w_uq.shape[1] // (MLA_NOPE_DIM + MLA_ROPE_DIM)
    n_experts = w_router.shape[1]

    xf = x.reshape(n, d)
    cos_a, sin_a, tab_m = _rope_tables(positions)

    c0 = 3 * fox_w
    c1 = c0 + fox_heads
    w_qkv = jnp.concatenate([w_in0[:, :c0], w_in0[:, c1:]], axis=1).astype(BF16)
    group_kinds = ([SCALED] * fox_heads + [PLAIN] * (2 * fox_heads)
                   + [ROTARY] * (swa_q + swa_kv) + [PLAIN] * swa_kv)
    qkv = _project(xf, w_qkv, group_kinds, HEAD_DIM ** -0.5 * LOG2_E, cos_a, sin_a, 1024, 512, "proj0")
    cum_log_f = _fox_cum_log_forget(xf, w_in0[:, c0:c1], b_forget, batch, seq)
    o_a, w_moe_up_bf = _causal_attention(qkv, qkv, qkv, 0, fox_heads, 2 * fox_heads, HEAD_DIM, HEAD_DIM,
                                         fox_heads, batch, seq, ck=cum_log_f, cast_rider=w_moe_up,
                                         name="fox_attn")
    o_b = _swa_attention(qkv, c0, c0 + swa_qw, c0 + swa_qw + swa_kvw, sinks_b, swa_q, swa_kv, batch, seq,
                         HEAD_DIM ** -0.5)
    w_out0_bf = w_out0.astype(BF16)
    (h1,) = _matmul_residual_ln([(o_a, w_out0_bf[:fox_w]), (o_b, w_out0_bf[fox_w:])], xf,
                                ln0_mix_g, ln0_mix_b, "out0_ln")
    h2, w_moe_down_bf = _swiglu_residual_ln(h1, w_ffn_gate.astype(BF16), w_ffn_up.astype(BF16),
                                            w_ffn_down.astype(BF16), ln0_ffn_g, ln0_ffn_b,
                                            cast_rider=w_moe_down)

    r0 = q_rank + kv_rank
    w_in1_x = jnp.concatenate([w_in1, _rotate_half_cols(w_in1[:, r0:])], axis=1).astype(BF16)
    w_uq_h = w_uq.reshape(q_rank, mla_heads, MLA_NOPE_DIM + MLA_ROPE_DIM)
    w_uq_x = jnp.concatenate([w_uq_h, _rotate_half_cols(w_uq_h[..., MLA_NOPE_DIM:])], axis=-1)
    q_m, k_m, v_m = _mla_projections(
        h2, w_in1_x, q_norm_g, kv_norm_g, tab_m,
        w_uq_x.reshape(q_rank, mla_heads * MLA_QK_PAD).astype(BF16), w_ukv.astype(BF16),
        q_rank, kv_rank, mla_heads, (MLA_NOPE_DIM + MLA_ROPE_DIM) ** -0.5 * LOG2_E)
    o_m, w_moe_gate_bf = _causal_attention(q_m, k_m, v_m, 0, 0, 0, MLA_QK_PAD, MLA_V_DIM, mla_heads, batch,
                                           seq, cast_rider=w_moe_gate, name="mla_attn")
    h3, idx, gates = _matmul_residual_ln([(o_m, w_out1.astype(BF16))], h2, ln1_mix_g, ln1_mix_b,
                                         "out1_ln_route", w_router=w_router)

    moe_tm = _tile(n * TOP_K, 512)
    pos, row_token, tile_expert, tile_valid = _moe_routing(idx[:, :TOP_K], n_experts, moe_tm)
    y_rows = _moe_ffn(h3, row_token, w_moe_gate_bf, w_moe_up_bf, w_moe_down_bf, tile_expert, tile_valid,
                      moe_tm)
    out = _moe_combine_ln(y_rows, pos, h3, gates, ln1_ffn_g, ln1_ffn_b)
    return out.reshape(batch, seq, d)
```

```python
import functools
import math

import jax
import jax.numpy as jnp
from jax import lax
from jax.experimental import pallas as pl
from jax.experimental.pallas import tpu as pltpu

F32 = jnp.float32
BF16 = jnp.bfloat16

LANE = 128
HEAD_DIM = 128
SWA_WINDOW = 128
MLA_NOPE_DIM = 128
MLA_ROPE_DIM = 64
MLA_V_DIM = 128
MLA_QK_PAD = 256
ROPE_THETA = 10000.0
TOP_K = 2
LN_EPS = 1e-5
RMS_EPS = 1e-6
DEPTH = 2
DEEPNORM_ALPHA = (2 * DEPTH) ** 0.25
LOG2_E = math.log2(math.e)

NT_DIMS = (((1,), (1,)), ((), ()))


def _tile(n, pref):
    if n <= pref:
        return n
    t = (pref // LANE) * LANE
    while n % t:
        t -= LANE
    return t


def _cparams(*sem):
    return pltpu.CompilerParams(dimension_semantics=sem)


def _dot(a, b):
    return jnp.dot(a, b, preferred_element_type=F32)


def _layer_norm(y, g, b):
    mu = jnp.mean(y, axis=-1, keepdims=True)
    d = y - mu
    var = jnp.mean(d * d, axis=-1, keepdims=True)
    return d * lax.rsqrt(var + LN_EPS) * g + b


def _silu(g):
    return g / (1.0 + jnp.exp(-g))


def _rope_table_kernel(pos_ref, freq_ref, cos_a_ref, sin_a_ref, tab_m_ref):
    pos = pos_ref[...]
    ang_a = pos * freq_ref[0:1, :]
    lane = lax.broadcasted_iota(jnp.int32, ang_a.shape, 1)
    cos_a_ref[...] = jnp.cos(ang_a)
    sin_a_ref[...] = jnp.where(lane < HEAD_DIM // 2, -jnp.sin(ang_a), jnp.sin(ang_a))
    ang_m = pos * freq_ref[1:2, :]
    tab_m_ref[...] = jnp.where(lane < MLA_ROPE_DIM, jnp.cos(ang_m), jnp.sin(ang_m))


def _rope_tables(positions):
    n = positions.size
    pos = positions.astype(F32).reshape(n, 1)
    half_a = HEAD_DIM // 2
    half_m = MLA_ROPE_DIM // 2
    inv_a = ROPE_THETA ** (-2.0 * jnp.arange(half_a, dtype=F32) / HEAD_DIM)
    inv_m = ROPE_THETA ** (-2.0 * jnp.arange(half_m, dtype=F32) / MLA_ROPE_DIM)
    freq = jnp.stack([jnp.tile(inv_a, 2), jnp.tile(inv_m, 4)])
    tm = _tile(n, 1024)
    out = jax.ShapeDtypeStruct((n, LANE), F32)
    row = pl.BlockSpec((tm, LANE), lambda i: (i, 0))
    return pl.pallas_call(
        _rope_table_kernel, grid=(n // tm,),
        in_specs=[pl.BlockSpec((tm, 1), lambda i: (i, 0)), pl.BlockSpec((2, LANE), lambda i: (0, 0))],
        out_specs=[row, row, row], out_shape=[out, out, out],
        compiler_params=_cparams("parallel"), name="rope_tables")(pos, freq)


PLAIN, SCALED, ROTARY = "plain", "scaled", "rotary"


def _proj_kernel(x_ref, w_ref, cos_ref, sin_ref, o_ref, xb_ref, *, tile_kinds, scale):
    j = pl.program_id(1)

    @pl.when(j == 0)
    def _():
        xb_ref[...] = x_ref[...].astype(BF16)

    def tile(kinds):
        acc = _dot(xb_ref[...], w_ref[...])
        for c, kind in enumerate(kinds):
            cols = slice(c * HEAD_DIM, (c + 1) * HEAD_DIM)
            a = acc[:, cols]
            if kind == SCALED:
                a = a * scale
            elif kind == ROTARY:
                a = a * cos_ref[...] + pltpu.roll(a, HEAD_DIM // 2, 1) * sin_ref[...]
            o_ref[:, cols] = a.astype(o_ref.dtype)

    for kinds in sorted(set(tile_kinds)):
        first = tile_kinds.index(kinds)
        count = tile_kinds.count(kinds)
        assert tile_kinds[first:first + count] == (kinds,) * count
        pl.when((j >= first) & (j < first + count))(functools.partial(tile, kinds))


def _project(x, w, group_kinds, scale, cos, sin, tm, tn, name):
    m, k = x.shape
    n = w.shape[1]
    tm, tn = _tile(m, tm), _tile(n, tn)
    per_tile = tn // HEAD_DIM
    tile_kinds = tuple(tuple(group_kinds[t * per_tile:(t + 1) * per_tile]) for t in range(n // tn))
    return pl.pallas_call(
        functools.partial(_proj_kernel, tile_kinds=tile_kinds, scale=scale), grid=(m // tm, n // tn),
        in_specs=[pl.BlockSpec((tm, k), lambda i, j: (i, 0)), pl.BlockSpec((k, tn), lambda i, j: (0, j)),
                  pl.BlockSpec((tm, LANE), lambda i, j: (i, 0)), pl.BlockSpec((tm, LANE), lambda i, j: (i, 0))],
        out_specs=pl.BlockSpec((tm, tn), lambda i, j: (i, j)),
        out_shape=jax.ShapeDtypeStruct((m, n), BF16),
        scratch_shapes=[pltpu.VMEM((tm, k), BF16)],
        compiler_params=_cparams("parallel", "arbitrary"), name=name)(x, w, cos, sin)


def _split3(x):
    hi = x.astype(BF16)
    r = x - hi.astype(F32)
    mid = r.astype(BF16)
    lo = (r - mid.astype(F32)).astype(BF16)
    return hi, mid, lo


def _fox_gate_kernel(x_ref, wf_ref, bf_ref, o_ref, carry_ref, *, n_heads):
    @pl.when(pl.program_id(1) == 0)
    def _():
        carry_ref[...] = jnp.zeros_like(carry_ref)

    f = lax.dot_general(wf_ref[...], x_ref[...].astype(BF16), NT_DIMS, preferred_element_type=F32)
    z = f + bf_ref[...]
    log_f = jnp.minimum(z, 0.0) - jnp.log1p(jnp.exp(-jnp.abs(z)))
    tc = z.shape[1]
    src = lax.broadcasted_iota(jnp.int32, (tc, tc), 0)
    dst = lax.broadcasted_iota(jnp.int32, (tc, tc), 1)
    tri = jnp.where(src <= dst, 1.0, 0.0).astype(BF16)
    hi, mid, lo = _split3(log_f)
    cum = _dot(hi, tri) + _dot(mid, tri) + _dot(lo, tri) + carry_ref[:, 0:1]
    o_ref[0] = cum[:n_heads] * LOG2_E
    carry_ref[...] = jnp.broadcast_to(cum[:, tc - 1:tc], carry_ref.shape)


def _fox_cum_log_forget(x, w_f, b_forget, batch, seq):
    n, d = x.shape
    n_heads = w_f.shape[1]
    rows = 16
    wf_t = jnp.zeros((rows, d), BF16).at[:n_heads].set(w_f.T.astype(BF16))
    bf = jnp.zeros((rows, 1), F32).at[:n_heads, 0].set(b_forget.astype(F32))
    tc = _tile(seq, 512)
    nc = seq // tc
    return pl.pallas_call(
        functools.partial(_fox_gate_kernel, n_heads=n_heads), grid=(batch, nc),
        in_specs=[pl.BlockSpec((tc, d), lambda b, s: (b * nc + s, 0)),
                  pl.BlockSpec((rows, d), lambda b, s: (0, 0)),
                  pl.BlockSpec((rows, 1), lambda b, s: (0, 0))],
        out_specs=pl.BlockSpec((1, n_heads, tc), lambda b, s: (b, 0, s)),
        out_shape=jax.ShapeDtypeStruct((batch, n_heads, seq), F32),
        scratch_shapes=[pltpu.VMEM((rows, LANE), F32)],
        compiler_params=_cparams("parallel", "arbitrary"), name="fox_gate")(x, wf_t, bf)


def _flash_kernel(*refs, tk, dk, dv, heads, has_bias, n_cast_slabs):
    refs = list(refs)
    q_ref, k_ref, v_ref = refs[:3]
    del refs[:3]
    ck_ref = refs.pop(0) if has_bias else None
    w_ref = refs.pop(0) if n_cast_slabs else None
    o_ref = refs.pop(0)
    qi = pl.program_id(2)
    tq = q_ref.shape[0]
    qs = [q_ref[:, g * dk:(g + 1) * dk] for g in range(heads)]

    if n_cast_slabs:
        step = (pl.program_id(0) * pl.num_programs(1) + pl.program_id(1)) * pl.num_programs(2) + qi
        _cast_rider_step(step, n_cast_slabs, w_ref, refs[0])

    def step(g, state, j, masked):
        m, l, acc = state
        kv_rows = pl.ds(pl.multiple_of(j * tk, tk), tk)
        s = lax.dot_general(qs[g], k_ref[kv_rows, g * dk:(g + 1) * dk], NT_DIMS, preferred_element_type=F32)
        if has_bias:
            s = s - ck_ref[0, g, j]
        if masked:
            row = lax.broadcasted_iota(jnp.int32, s.shape, 0)
            col = lax.broadcasted_iota(jnp.int32, s.shape, 1)
            s = jnp.where(row >= col, s, -jnp.inf)
        m_new = jnp.maximum(m, jnp.max(s, axis=-1, keepdims=True))
        a = jnp.exp2(m - m_new)
        p = jnp.exp2(s - m_new)
        l = a * l + jnp.sum(p, axis=-1, keepdims=True)
        acc = a * acc + _dot(p.astype(BF16), v_ref[kv_rows, g * dv:(g + 1) * dv])
        return m_new, l, acc

    def body(j, carry):
        return tuple(step(g, carry[g], j, False) for g in range(heads))

    init = (jnp.full((tq, 1), -jnp.inf, F32), jnp.zeros((tq, 1), F32), jnp.zeros((tq, dv), F32))
    carry = lax.fori_loop(0, qi, body, (init,) * heads)
    for g in range(heads):
        _, l, acc = step(g, carry[g], qi, True)
        o_ref[:, g * dv:(g + 1) * dv] = (acc / l).astype(o_ref.dtype)


BF16_SUBLANES = 16


def _slab_rows(total_rows, max_slabs):
    rows = BF16_SUBLANES
    while total_rows % rows or total_rows // rows > max_slabs:
        rows += BF16_SUBLANES
    return rows


def _cast_rider(w, grid, step_of):
    flat = w.reshape(-1, w.shape[-1])
    rows = _slab_rows(flat.shape[0], math.prod(grid))
    n_slabs = flat.shape[0] // rows
    spec = pl.BlockSpec((rows, flat.shape[1]), lambda *g: (jnp.minimum(step_of(*g), n_slabs - 1), 0))
    return flat, spec, jax.ShapeDtypeStruct(flat.shape, BF16), n_slabs


def _cast_rider_step(step, n_slabs, w_ref, o_ref):
    @pl.when(step < n_slabs)
    def _():
        o_ref[...] = w_ref[...].astype(o_ref.dtype)


def _causal_attention(q_arr, k_arr, v_arr, q_col, k_col, v_col, dk, dv, n_heads, batch, seq,
                      ck=None, cast_rider=None, name="attn"):
    heads = 4
    assert n_heads % heads == 0 and q_col % heads == 0 and k_col % heads == 0 and v_col % heads == 0
    t = _tile(seq, 512)
    nq = seq // t
    grid = (batch, n_heads // heads, nq)
    in_specs = [pl.BlockSpec((t, heads * dk), lambda b, h, i: (b * nq + i, q_col // heads + h)),
                pl.BlockSpec((seq, heads * dk), lambda b, h, i: (b, k_col // heads + h)),
                pl.BlockSpec((seq, heads * dv), lambda b, h, i: (b, v_col // heads + h))]
    args = [q_arr, k_arr, v_arr]
    out_specs = [pl.BlockSpec((t, heads * dv), lambda b, h, i: (b * nq + i, h))]
    out_shape = [jax.ShapeDtypeStruct((batch * seq, n_heads * dv), BF16)]
    if ck is not None:
        args.append(ck.reshape(batch, n_heads, nq, 1, t))
        in_specs.append(pl.BlockSpec((1, heads, nq, 1, t), lambda b, h, i: (b, h, 0, 0, 0)))
    n_slabs = 0
    if cast_rider is not None:
        flat, slab, flat_bf, n_slabs = _cast_rider(cast_rider, grid, lambda b, h, i: (b * grid[1] + h) * nq + i)
        args.append(flat)
        in_specs.append(slab)
        out_specs.append(slab)
        out_shape.append(flat_bf)
    outs = pl.pallas_call(
        functools.partial(_flash_kernel, tk=t, dk=dk, dv=dv, heads=heads, has_bias=ck is not None,
                          n_cast_slabs=n_slabs),
        grid=grid, in_specs=in_specs, out_specs=out_specs, out_shape=out_shape,
        compiler_params=_cparams("parallel", "parallel", "arbitrary"), name=name)(*args)
    if cast_rider is None:
        return outs[0]
    return outs[0], outs[1].reshape(cast_rider.shape)


def _swa_kernel(sink_ref, q_ref, kp_ref, kc_ref, vp_ref, vc_ref, *rest, scale, n_q, n_kv, rider_slabs):
    n_riders = len(rider_slabs)
    o_ref = rest[n_riders]
    step = pl.program_id(0) * pl.num_programs(1) + pl.program_id(1)
    for n_slabs, w_ref, wo_ref in zip(rider_slabs, rest[:n_riders], rest[n_riders + 1:]):
        _cast_rider_step(step, n_slabs, w_ref, wo_ref)
    w = SWA_WINDOW
    qi = lax.broadcasted_iota(jnp.int32, (w, 2 * w), 0)
    ji = lax.broadcasted_iota(jnp.int32, (w, 2 * w), 1)
    has_prev = pl.program_id(1) > 0
    mask = (ji > qi) & (ji <= qi + w) & ((ji >= w) | has_prev)
    group = n_q // n_kv
    for kvh in range(n_kv):
        cols = slice(kvh * HEAD_DIM, (kvh + 1) * HEAD_DIM)
        k = jnp.concatenate([kp_ref[:, cols], kc_ref[:, cols]], axis=0)
        v = jnp.concatenate([vp_ref[:, cols], vc_ref[:, cols]], axis=0)
        for g in range(group):
            h = kvh * group + g
            hc = slice(h * HEAD_DIM, (h + 1) * HEAD_DIM)
            s = lax.dot_general(q_ref[:, hc], k, NT_DIMS, preferred_element_type=F32) * scale
            s = jnp.where(mask, s, -jnp.inf)
            sink = sink_ref[h]
            m = jnp.maximum(jnp.max(s, axis=-1, keepdims=True), sink)
            p = jnp.exp(s - m)
            denom = jnp.sum(p, axis=-1, keepdims=True) + jnp.exp(sink - m)
            o_ref[:, hc] = (_dot(p.astype(BF16), v) / denom).astype(o_ref.dtype)


def _swa_attention(qkv, q_off, k_off, v_off, sinks, n_q, n_kv, batch, seq, scale, cast_riders):
    w = SWA_WINDOW
    nb = seq // w
    grid = (batch, nb)
    qw, kw = n_q * HEAD_DIM, n_kv * HEAD_DIM
    assert q_off % qw == 0 and k_off % kw == 0 and v_off % kw == 0
    q_col, k_col, v_col = q_off // qw, k_off // kw, v_off // kw
    cur = lambda b, n: b * nb + n
    prev = lambda b, n: b * nb + jnp.maximum(n - 1, 0)
    riders = [_cast_rider(r, grid, cur) for r in cast_riders]
    outs = pl.pallas_call(
        functools.partial(_swa_kernel, scale=scale, n_q=n_q, n_kv=n_kv,
                          rider_slabs=tuple(r[3] for r in riders)), grid=grid,
        in_specs=[pl.BlockSpec(memory_space=pltpu.SMEM),
                  pl.BlockSpec((w, qw), lambda b, n: (cur(b, n), q_col)),
                  pl.BlockSpec((w, kw), lambda b, n: (prev(b, n), k_col)),
                  pl.BlockSpec((w, kw), lambda b, n: (cur(b, n), k_col)),
                  pl.BlockSpec((w, kw), lambda b, n: (prev(b, n), v_col)),
                  pl.BlockSpec((w, kw), lambda b, n: (cur(b, n), v_col))] + [r[1] for r in riders],
        out_specs=[pl.BlockSpec((w, qw), lambda b, n: (cur(b, n), 0))] + [r[1] for r in riders],
        out_shape=[jax.ShapeDtypeStruct((batch * seq, qw), BF16)] + [r[2] for r in riders],
        compiler_params=_cparams("arbitrary", "arbitrary"), name="swa_attn")(
            sinks.astype(F32), qkv, qkv, qkv, qkv, qkv, *[r[0] for r in riders])
    return outs[0], [o.reshape(r.shape) for o, r in zip(outs[1:], cast_riders)]


def _top2_route(x, w, n_experts):
    xh = x.astype(BF16)
    xl = (x - xh.astype(F32)).astype(BF16)
    wh = w.astype(BF16)
    wl = (w - wh.astype(F32)).astype(BF16)
    logits = _dot(xh, wh) + _dot(xh, wl) + _dot(xl, wh)
    lane = lax.broadcasted_iota(jnp.int32, logits.shape, 1)
    lane_f = lane.astype(F32)
    l1 = jnp.where(lane < n_experts, logits, -jnp.inf)
    v1 = jnp.max(l1, axis=-1, keepdims=True)
    i1 = jnp.min(jnp.where(l1 == v1, lane_f, float(LANE)), axis=-1, keepdims=True)
    l2 = jnp.where(lane_f == i1, -jnp.inf, l1)
    v2 = jnp.max(l2, axis=-1, keepdims=True)
    i2 = jnp.min(jnp.where(l2 == v2, lane_f, float(LANE)), axis=-1, keepdims=True)
    e2 = jnp.exp(v2 - v1)
    idx = jnp.where(lane == 0, i1, jnp.where(lane == 1, i2, 0.0)).astype(jnp.int32)
    gate = jnp.where(lane == 0, 1.0 / (1.0 + e2), jnp.where(lane == 1, e2 / (1.0 + e2), 0.0))
    return idx, gate


def _mm_res_ln_kernel(*refs, n_pairs, n_experts):
    a_refs, w_refs = refs[:n_pairs], refs[n_pairs:2 * n_pairs]
    if n_experts:
        res_ref, g_ref, b_ref, wr_ref, o_ref, idx_ref, gate_ref = refs[2 * n_pairs:]
    else:
        res_ref, g_ref, b_ref, o_ref = refs[2 * n_pairs:]
    tm = o_ref.shape[0]
    sub = _tile(tm, 256)
    for r in range(tm // sub):
        rows = slice(r * sub, (r + 1) * sub)
        acc = _dot(a_refs[0][rows, :], w_refs[0][...])
        for a_ref, w_ref in zip(a_refs[1:], w_refs[1:]):
            acc = acc + _dot(a_ref[rows, :], w_ref[...])
        y = _layer_norm(DEEPNORM_ALPHA * res_ref[rows, :] + acc, g_ref[...], b_ref[...])
        o_ref[rows, :] = y
        if n_experts:
            idx_ref[rows, :], gate_ref[rows, :] = _top2_route(y, wr_ref[...], n_experts)


def _matmul_residual_ln(pairs, res, g, b, name, w_router=None):
    m, d = res.shape
    tm = _tile(m, 512)
    row = lambda i: (i, 0)
    fixed = lambda i: (0, 0)
    resident = lambda shape: pl.BlockSpec(shape, fixed, pipeline_mode=pl.Buffered(1))
    in_specs = ([pl.BlockSpec((tm, a.shape[1]), row) for a, _ in pairs] + [resident(w.shape) for _, w in pairs]
                + [pl.BlockSpec((tm, d), row), pl.BlockSpec((1, d), fixed), pl.BlockSpec((1, d), fixed)])
    args = [a for a, _ in pairs] + [w for _, w in pairs] + [res, g.reshape(1, d), b.reshape(1, d)]
    out_specs = [pl.BlockSpec((tm, d), row)]
    out_shape = [jax.ShapeDtypeStruct((m, d), F32)]
    n_experts = 0
    if w_router is not None:
        n_experts = w_router.shape[1]
        in_specs.append(resident((d, LANE)))
        args.append(jnp.zeros((d, LANE), F32).at[:, :n_experts].set(w_router.astype(F32)))
        out_specs += [pl.BlockSpec((tm, LANE), row), pl.BlockSpec((tm, LANE), row)]
        out_shape += [jax.ShapeDtypeStruct((m, LANE), jnp.int32), jax.ShapeDtypeStruct((m, LANE), F32)]
    return pl.pallas_call(
        functools.partial(_mm_res_ln_kernel, n_pairs=len(pairs), n_experts=n_experts),
        grid=(m // tm,), in_specs=in_specs, out_specs=out_specs, out_shape=out_shape,
        compiler_params=_cparams("parallel"), name=name)(*args)


def _swiglu(x, wg, wu, wd):
    a = _silu(_dot(x, wg)) * _dot(x, wu)
    return _dot(a.astype(BF16), wd)


def _ffn_ln_kernel(h_ref, wg_ref, wu_ref, wd_ref, g_ref, b_ref, o_ref, x_ref):
    j = pl.program_id(1)

    @pl.when(j == 0)
    def _():
        x_ref[...] = h_ref[...].astype(BF16)
        o_ref[...] = jnp.zeros_like(o_ref)

    o_ref[...] += _swiglu(x_ref[...], wg_ref[...], wu_ref[...], wd_ref[...])

    @pl.when(j == pl.num_programs(1) - 1)
    def _():
        o_ref[...] = _layer_norm(DEEPNORM_ALPHA * h_ref[...] + o_ref[...], g_ref[...], b_ref[...])


def _swiglu_residual_ln(h, wg, wu, wd, g, b):
    m, d = h.shape
    f = wg.shape[1]
    tm, tf = _tile(m, 512), _tile(f, 512)
    row = lambda i, j: (i, 0)
    fixed = lambda i, j: (0, 0)
    return pl.pallas_call(
        _ffn_ln_kernel, grid=(m // tm, f // tf),
        in_specs=[pl.BlockSpec((tm, d), row),
                  pl.BlockSpec((d, tf), lambda i, j: (0, j)), pl.BlockSpec((d, tf), lambda i, j: (0, j)),
                  pl.BlockSpec((tf, d), lambda i, j: (j, 0)),
                  pl.BlockSpec((1, d), fixed), pl.BlockSpec((1, d), fixed)],
        out_specs=pl.BlockSpec((tm, d), row),
        out_shape=jax.ShapeDtypeStruct((m, d), F32),
        scratch_shapes=[pltpu.VMEM((tm, d), BF16)],
        compiler_params=_cparams("parallel", "arbitrary"), name="ffn_ln")(
            h, wg, wu, wd, g.reshape(1, d), b.reshape(1, d))


def _rms_norm(c, g):
    ms = jnp.mean(c * c, axis=-1, keepdims=True)
    return c * lax.rsqrt(ms + RMS_EPS) * g


def _rope64(pair, tab):
    w = pair * tab
    lane = lax.broadcasted_iota(jnp.int32, w.shape, 1)
    return jnp.where(lane < MLA_ROPE_DIM, w + pltpu.roll(w, MLA_ROPE_DIM, 1), 0.0)


def _mla_proj_kernel(x_ref, win_ref, gq_ref, gkv_ref, tab_ref, wq_ref, wkv_ref, q_ref, k_ref, v_ref, *,
                     q_rank, kv_rank, n_heads, q_scale):
    acc = _dot(x_ref[...].astype(BF16), win_ref[...])
    cq = _rms_norm(acc[:, :q_rank], gq_ref[...]).astype(BF16)
    ckv = _rms_norm(acc[:, q_rank:q_rank + kv_rank], gkv_ref[...]).astype(BF16)
    tab = tab_ref[...]
    k_pe = _rope64(acc[:, q_rank + kv_rank:], tab).astype(BF16)
    wkv_w = MLA_NOPE_DIM + MLA_V_DIM
    for h in range(n_heads):
        q0 = h * MLA_QK_PAD
        qh = _dot(cq, wq_ref[:, q0:q0 + MLA_QK_PAD]) * q_scale
        q_ref[:, q0:q0 + MLA_NOPE_DIM] = qh[:, :MLA_NOPE_DIM].astype(BF16)
        q_ref[:, q0 + MLA_NOPE_DIM:q0 + MLA_QK_PAD] = _rope64(qh[:, MLA_NOPE_DIM:], tab).astype(BF16)
        kvh = _dot(ckv, wkv_ref[:, h * wkv_w:(h + 1) * wkv_w])
        k_ref[:, q0:q0 + MLA_NOPE_DIM] = kvh[:, :MLA_NOPE_DIM].astype(BF16)
        k_ref[:, q0 + MLA_NOPE_DIM:q0 + MLA_QK_PAD] = k_pe
        v_ref[:, h * MLA_V_DIM:(h + 1) * MLA_V_DIM] = kvh[:, MLA_NOPE_DIM:].astype(BF16)


def _mla_projections(x, w_in, gq, gkv, tab, w_q, w_kv, q_rank, kv_rank, n_heads, q_scale):
    m, d = x.shape
    tm = _tile(m, 512)
    row = lambda i: (i, 0)
    fixed = lambda i: (0, 0)
    resident = lambda shape: pl.BlockSpec(shape, fixed, pipeline_mode=pl.Buffered(1))
    qk_w, v_w = n_heads * MLA_QK_PAD, n_heads * MLA_V_DIM
    return pl.pallas_call(
        functools.partial(_mla_proj_kernel, q_rank=q_rank, kv_rank=kv_rank, n_heads=n_heads, q_scale=q_scale),
        grid=(m // tm,),
        in_specs=[pl.BlockSpec((tm, d), row), resident(w_in.shape),
                  pl.BlockSpec((1, q_rank), fixed), pl.BlockSpec((1, kv_rank), fixed),
                  pl.BlockSpec((tm, LANE), row), resident(w_q.shape), resident(w_kv.shape)],
        out_specs=[pl.BlockSpec((tm, qk_w), row), pl.BlockSpec((tm, qk_w), row), pl.BlockSpec((tm, v_w), row)],
        out_shape=[jax.ShapeDtypeStruct((m, qk_w), BF16), jax.ShapeDtypeStruct((m, qk_w), BF16),
                   jax.ShapeDtypeStruct((m, v_w), BF16)],
        compiler_params=_cparams("parallel"), name="mla_proj")(
            x, w_in, gq.reshape(1, q_rank).astype(F32), gkv.reshape(1, kv_rank).astype(F32), tab, w_q, w_kv)


GATHER_ISSUE_UNROLL = 8


def _row_copy(src_hbm, buf, sem, src_row, dst_row):
    return pltpu.make_async_copy(src_hbm.at[pl.ds(src_row, 1), :], buf.at[pl.ds(dst_row, 1), :], sem)


def _wait_all_rows(src_hbm, buf, sem):
    pltpu.make_async_copy(src_hbm.at[pl.ds(0, buf.shape[0]), :], buf, sem).wait()


def _gather_rows_kernel(tok_ref, x_hbm, o_ref, buf, sem):
    rows = buf.shape[1]
    i, n = pl.program_id(0), pl.num_programs(0)

    def issue(step, slot):
        def start(r, _):
            _row_copy(x_hbm, buf.at[slot], sem.at[slot], tok_ref[step * rows + r], r).start()
            return 0
        lax.fori_loop(0, rows, start, 0, unroll=GATHER_ISSUE_UNROLL)

    @pl.when(i == 0)
    def _():
        issue(0, 0)

    @pl.when(i + 1 < n)
    def _():
        issue(i + 1, (i + 1) % 2)

    slot = i % 2
    _wait_all_rows(x_hbm, buf.at[slot], sem.at[slot])
    o_ref[...] = buf[slot].astype(o_ref.dtype)


def _gather_rows(h, row_token, rows_per_step):
    n_rows = row_token.shape[0]
    d = h.shape[1]
    return pl.pallas_call(
        _gather_rows_kernel,
        grid_spec=pltpu.PrefetchScalarGridSpec(
            num_scalar_prefetch=1, grid=(n_rows // rows_per_step,),
            in_specs=[pl.BlockSpec(memory_space=pl.ANY)],
            out_specs=pl.BlockSpec((rows_per_step, d), lambda i, tok: (i, 0)),
            scratch_shapes=[pltpu.VMEM((2, rows_per_step, d), F32), pltpu.SemaphoreType.DMA((2,))]),
        out_shape=jax.ShapeDtypeStruct((n_rows, d), BF16),
        compiler_params=_cparams("arbitrary"), name="moe_gather")(row_token, h)


def _moe_ffn_kernel(te_ref, tr_ref, x_ref, wg_ref, wu_ref, wd_ref, o_ref):
    i, j = pl.program_id(0), pl.program_id(1)
    half = x_ref.shape[0] // 2
    n_assigned = tr_ref[i]

    @pl.when(j == 0)
    def _():
        o_ref[...] = jnp.zeros_like(o_ref)

    @pl.when(n_assigned > half)
    def _():
        o_ref[...] += _swiglu(x_ref[...], wg_ref[0], wu_ref[0], wd_ref[0].astype(BF16))

    @pl.when((n_assigned > 0) & (n_assigned <= half))
    def _():
        o_ref[:half, :] += _swiglu(x_ref[:half, :], wg_ref[0], wu_ref[0], wd_ref[0].astype(BF16))


def _moe_ffn(xg, wg, wu, wd, tile_expert, tile_rows, tm):
    n_rows, d = xg.shape
    f = wg.shape[2]
    tf = _tile(f, 1024)
    nj = f // tf
    col = lambda i, j, te, tr: jnp.where(tr[i] > 0, j, nj - 1)
    return pl.pallas_call(
        _moe_ffn_kernel,
        grid_spec=pltpu.PrefetchScalarGridSpec(
            num_scalar_prefetch=2, grid=(n_rows // tm, nj),
            in_specs=[pl.BlockSpec((tm, d), lambda i, j, te, tr: (i, 0)),
                      pl.BlockSpec((1, d, tf), lambda i, j, te, tr: (te[i], 0, col(i, j, te, tr))),
                      pl.BlockSpec((1, d, tf), lambda i, j, te, tr: (te[i], 0, col(i, j, te, tr))),
                      pl.BlockSpec((1, tf, d), lambda i, j, te, tr: (te[i], col(i, j, te, tr), 0))],
            out_specs=pl.BlockSpec((tm, d), lambda i, j, te, tr: (i, 0))),
        out_shape=jax.ShapeDtypeStruct((n_rows, d), F32),
        compiler_params=_cparams("arbitrary", "arbitrary"), name="moe_ffn")(
            tile_expert, tile_rows, xg, wg, wu, wd)


def _combine_ln_kernel(pos_ref, h_ref, gate_ref, g_ref, b_ref, y_hbm, o_ref, buf, sem):
    rows = buf.shape[2]
    i, n = pl.program_id(0), pl.num_programs(0)

    def issue(step, slot):
        def start(r, _):
            for k in range(TOP_K):
                src = pos_ref[TOP_K * (step * rows + r) + k]
                _row_copy(y_hbm, buf.at[slot, k], sem.at[slot, k], src, r).start()
            return 0
        lax.fori_loop(0, rows, start, 0, unroll=GATHER_ISSUE_UNROLL // TOP_K)

    @pl.when(i == 0)
    def _():
        issue(0, 0)

    @pl.when(i + 1 < n)
    def _():
        issue(i + 1, (i + 1) % 2)

    slot = i % 2
    gate = gate_ref[...]
    y = None
    for k in range(TOP_K):
        _wait_all_rows(y_hbm, buf.at[slot, k], sem.at[slot, k])
        term = buf[slot, k] * gate[:, k:k + 1]
        y = term if y is None else y + term
    o_ref[...] = _layer_norm(DEEPNORM_ALPHA * h_ref[...] + y, g_ref[...], b_ref[...])


def _moe_combine_ln(y_rows, pos, h, gates, g, b):
    m, d = h.shape
    tm = _tile(m, 256)
    row = lambda i, p: (i, 0)
    fixed = lambda i, p: (0, 0)
    return pl.pallas_call(
        _combine_ln_kernel,
        grid_spec=pltpu.PrefetchScalarGridSpec(
            num_scalar_prefetch=1, grid=(m // tm,),
            in_specs=[pl.BlockSpec((tm, d), row), pl.BlockSpec((tm, LANE), row),
                      pl.BlockSpec((1, d), fixed), pl.BlockSpec((1, d), fixed),
                      pl.BlockSpec(memory_space=pl.ANY)],
            out_specs=pl.BlockSpec((tm, d), row),
            scratch_shapes=[pltpu.VMEM((2, TOP_K, tm, d), F32), pltpu.SemaphoreType.DMA((2, TOP_K))]),
        out_shape=jax.ShapeDtypeStruct((m, d), F32),
        compiler_params=_cparams("arbitrary"), name="moe_combine_ln")(
            pos, h, gates, g.reshape(1, d), b.reshape(1, d), y_rows)


def _moe_routing(top_idx, n_experts, tm):
    n_assign = top_idx.size
    expert_of = top_idx.reshape(n_assign)
    onehot = (expert_of[:, None] == jnp.arange(n_experts, dtype=jnp.int32)[None, :]).astype(jnp.int32)
    csum = jnp.cumsum(onehot, axis=0)
    rank = jnp.sum(onehot * (csum - 1), axis=1)
    counts = csum[-1]
    padded = ((counts + tm - 1) // tm) * tm
    padded_end = jnp.cumsum(padded)
    padded_start = padded_end - padded
    pos = (padded_start[expert_of] + rank).astype(jnp.int32)
    n_tiles = -(-n_assign // tm) + n_experts
    row_token = jnp.zeros((n_tiles * tm,), jnp.int32).at[pos].set(
        jnp.arange(n_assign, dtype=jnp.int32) // TOP_K, unique_indices=True, mode="promise_in_bounds")
    tile_start = jnp.arange(n_tiles, dtype=jnp.int32) * tm
    tile_used = tile_start < padded_end[-1]
    tile_expert = jnp.minimum(jnp.searchsorted(padded_end, tile_start, side='right'), n_experts - 1)
    tile_rows = jnp.clip(counts[tile_expert] - (tile_start - padded_start[tile_expert]), 0, tm)
    tile_rows = jnp.where(tile_used, tile_rows, 0).astype(jnp.int32)
    last_expert = jnp.max(jnp.where(tile_used, tile_expert, 0))
    tile_expert = jnp.where(tile_used, tile_expert, last_expert).astype(jnp.int32)
    return pos, row_token, tile_expert, tile_rows


def _rotate_half_cols(w):
    half = w.shape[-1] // 2
    return jnp.concatenate([-w[..., half:], w[..., :half]], axis=-1)


def kernel(x, positions, w_in0, b_forget, sinks_b, w_out0, ln0_mix_g, ln0_mix_b, w_ffn_gate, w_ffn_up, w_ffn_down, ln0_ffn_g, ln0_ffn_b, w_in1, q_norm_g, w_uq, kv_norm_g, w_ukv, w_out1, ln1_mix_g, ln1_mix_b, w_router, w_moe_gate, w_moe_up, w_moe_down, ln1_ffn_g, ln1_ffn_b):
    batch, seq, d = x.shape
    n = batch * seq
    fox_heads = b_forget.shape[0]
    fox_w = fox_heads * HEAD_DIM
    swa_q = sinks_b.shape[0]
    swa_qw = swa_q * HEAD_DIM
    swa_kvw = (w_in0.shape[1] - 3 * fox_w - fox_heads - swa_qw) // 2
    swa_kv = swa_kvw // HEAD_DIM
    q_rank, kv_rank = q_norm_g.shape[0], kv_norm_g.shape[0]
    mla_heads = w_uq.shape[1] // (MLA_NOPE_DIM + MLA_ROPE_DIM)
    n_experts = w_router.shape[1]

    xf = x.reshape(n, d)
    cos_a, sin_a, tab_m = _rope_tables(positions)

    c0 = 3 * fox_w
    c1 = c0 + fox_heads
    w_qkv = jnp.concatenate([w_in0[:, :c0], w_in0[:, c1:]], axis=1).astype(BF16)
    group_kinds = ([SCALED] * fox_heads + [PLAIN] * (2 * fox_heads)
                   + [ROTARY] * (swa_q + swa_kv) + [PLAIN] * swa_kv)
    qkv = _project(xf, w_qkv, group_kinds, HEAD_DIM ** -0.5 * LOG2_E, cos_a, sin_a, 1024, 768, "proj0")
    cum_log_f = _fox_cum_log_forget(xf, w_in0[:, c0:c1], b_forget, batch, seq)
    o_a, w_moe_up_bf = _causal_attention(qkv, qkv, qkv, 0, fox_heads, 2 * fox_heads, HEAD_DIM, HEAD_DIM,
                                         fox_heads, batch, seq, ck=cum_log_f, cast_rider=w_moe_up,
                                         name="fox_attn")
    o_b, ffn_w_bf = _swa_attention(qkv, c0, c0 + swa_qw, c0 + swa_qw + swa_kvw, sinks_b, swa_q, swa_kv,
                                   batch, seq, HEAD_DIM ** -0.5,
                                   cast_riders=[w_ffn_gate, w_ffn_up, w_ffn_down])
    w_out0_bf = w_out0.astype(BF16)
    (h1,) = _matmul_residual_ln([(o_a, w_out0_bf[:fox_w]), (o_b, w_out0_bf[fox_w:])], xf,
                                ln0_mix_g, ln0_mix_b, "out0_ln")
    h2 = _swiglu_residual_ln(h1, *ffn_w_bf, ln0_ffn_g, ln0_ffn_b)

    r0 = q_rank + kv_rank
    w_in1_x = jnp.concatenate([w_in1, _rotate_half_cols(w_in1[:, r0:])], axis=1).astype(BF16)
    w_uq_h = w_uq.reshape(q_rank, mla_heads, MLA_NOPE_DIM + MLA_ROPE_DIM)
    w_uq_x = jnp.concatenate([w_uq_h, _rotate_half_cols(w_uq_h[..., MLA_NOPE_DIM:])], axis=-1)
    q_m, k_m, v_m = _mla_projections(
        h2, w_in1_x, q_norm_g, kv_norm_g, tab_m,
        w_uq_x.reshape(q_rank, mla_heads * MLA_QK_PAD).astype(BF16), w_ukv.astype(BF16),
        q_rank, kv_rank, mla_heads, (MLA_NOPE_DIM + MLA_ROPE_DIM) ** -0.5 * LOG2_E)
    o_m, w_moe_gate_bf = _causal_attention(q_m, k_m, v_m, 0, 0, 0, MLA_QK_PAD, MLA_V_DIM, mla_heads, batch,
                                           seq, cast_rider=w_moe_gate, name="mla_attn")
    h3, idx, gates = _matmul_residual_ln([(o_m, w_out1.astype(BF16))], h2, ln1_mix_g, ln1_mix_b,
                                         "out1_ln_route", w_router=w_router)

    moe_tm = _tile(n * TOP_K, 512)
    pos, row_token, tile_expert, tile_rows = _moe_routing(idx[:, :TOP_K], n_experts, moe_tm)
    xg = _gather_rows(h3, row_token, _tile(moe_tm, 256))
    y_rows = _moe_ffn(xg, w_moe_gate_bf, w_moe_up_bf, w_moe_down, tile_expert, tile_rows, moe_tm)
    out = _moe_combine_ln(y_rows, pos, h3, gates, ln1_ffn_g, ln1_ffn_b)
    return out.reshape(batch, seq, d)
```

```python
import functools
import math

import jax
import jax.numpy as jnp
from jax import lax
from jax.experimental import pallas as pl
from jax.experimental.pallas import tpu as pltpu

F32 = jnp.float32
BF16 = jnp.bfloat16

LANE = 128
HEAD_DIM = 128
SWA_WINDOW = 128
MLA_NOPE_DIM = 128
MLA_ROPE_DIM = 64
MLA_V_DIM = 128
MLA_QK_PAD = 256
ROPE_THETA = 10000.0
TOP_K = 2
LN_EPS = 1e-5
RMS_EPS = 1e-6
DEPTH = 2
DEEPNORM_ALPHA = (2 * DEPTH) ** 0.25
LOG2_E = math.log2(math.e)

NT_DIMS = (((1,), (1,)), ((), ()))


def _tile(n, pref):
    if n <= pref:
        return n
    t = (pref // LANE) * LANE
    while n % t:
        t -= LANE
    return t


def _cparams(*sem):
    return pltpu.CompilerParams(dimension_semantics=sem)


def _dot(a, b):
    return jnp.dot(a, b, preferred_element_type=F32)


def _layer_norm(y, g, b):
    mu = jnp.mean(y, axis=-1, keepdims=True)
    d = y - mu
    var = jnp.mean(d * d, axis=-1, keepdims=True)
    return d * lax.rsqrt(var + LN_EPS) * g + b


def _silu(g):
    return g / (1.0 + jnp.exp(-g))


def _rope_table_kernel(pos_ref, freq_ref, cos_a_ref, sin_a_ref, tab_m_ref):
    pos = pos_ref[...]
    ang_a = pos * freq_ref[0:1, :]
    lane = lax.broadcasted_iota(jnp.int32, ang_a.shape, 1)
    cos_a_ref[...] = jnp.cos(ang_a)
    sin_a_ref[...] = jnp.where(lane < HEAD_DIM // 2, -jnp.sin(ang_a), jnp.sin(ang_a))
    ang_m = pos * freq_ref[1:2, :]
    tab_m_ref[...] = jnp.where(lane < MLA_ROPE_DIM, jnp.cos(ang_m), jnp.sin(ang_m))


def _rope_tables(positions):
    n = positions.size
    pos = positions.astype(F32).reshape(n, 1)
    half_a = HEAD_DIM // 2
    half_m = MLA_ROPE_DIM // 2
    inv_a = ROPE_THETA ** (-2.0 * jnp.arange(half_a, dtype=F32) / HEAD_DIM)
    inv_m = ROPE_THETA ** (-2.0 * jnp.arange(half_m, dtype=F32) / MLA_ROPE_DIM)
    freq = jnp.stack([jnp.tile(inv_a, 2), jnp.tile(inv_m, 4)])
    tm = _tile(n, 1024)
    out = jax.ShapeDtypeStruct((n, LANE), F32)
    row = pl.BlockSpec((tm, LANE), lambda i: (i, 0))
    return pl.pallas_call(
        _rope_table_kernel, grid=(n // tm,),
        in_specs=[pl.BlockSpec((tm, 1), lambda i: (i, 0)), pl.BlockSpec((2, LANE), lambda i: (0, 0))],
        out_specs=[row, row, row], out_shape=[out, out, out],
        compiler_params=_cparams("parallel"), name="rope_tables")(pos, freq)


PLAIN, SCALED, ROTARY = "plain", "scaled", "rotary"


def _proj_kernel(x_ref, w_ref, cos_ref, sin_ref, o_ref, xb_ref, *, tile_kinds, scale):
    j = pl.program_id(1)

    @pl.when(j == 0)
    def _():
        xb_ref[...] = x_ref[...].astype(BF16)

    def tile(kinds):
        acc = _dot(xb_ref[...], w_ref[...])
        for c, kind in enumerate(kinds):
            cols = slice(c * HEAD_DIM, (c + 1) * HEAD_DIM)
            a = acc[:, cols]
            if kind == SCALED:
                a = a * scale
            elif kind == ROTARY:
                a = a * cos_ref[...] + pltpu.roll(a, HEAD_DIM // 2, 1) * sin_ref[...]
            o_ref[:, cols] = a.astype(o_ref.dtype)

    for kinds in sorted(set(tile_kinds)):
        first = tile_kinds.index(kinds)
        count = tile_kinds.count(kinds)
        assert tile_kinds[first:first + count] == (kinds,) * count
        pl.when((j >= first) & (j < first + count))(functools.partial(tile, kinds))


def _project(x, w, group_kinds, scale, cos, sin, tm, tn, name):
    m, k = x.shape
    n = w.shape[1]
    tm, tn = _tile(m, tm), _tile(n, tn)
    per_tile = tn // HEAD_DIM
    tile_kinds = tuple(tuple(group_kinds[t * per_tile:(t + 1) * per_tile]) for t in range(n // tn))
    return pl.pallas_call(
        functools.partial(_proj_kernel, tile_kinds=tile_kinds, scale=scale), grid=(m // tm, n // tn),
        in_specs=[pl.BlockSpec((tm, k), lambda i, j: (i, 0)), pl.BlockSpec((k, tn), lambda i, j: (0, j)),
                  pl.BlockSpec((tm, LANE), lambda i, j: (i, 0)), pl.BlockSpec((tm, LANE), lambda i, j: (i, 0))],
        out_specs=pl.BlockSpec((tm, tn), lambda i, j: (i, j)),
        out_shape=jax.ShapeDtypeStruct((m, n), BF16),
        scratch_shapes=[pltpu.VMEM((tm, k), BF16)],
        compiler_params=_cparams("parallel", "arbitrary"), name=name)(x, w, cos, sin)


def _split3(x):
    hi = x.astype(BF16)
    r = x - hi.astype(F32)
    mid = r.astype(BF16)
    lo = (r - mid.astype(F32)).astype(BF16)
    return hi, mid, lo


def _fox_gate_kernel(x_ref, wf_ref, bf_ref, o_ref, carry_ref, *, n_heads):
    @pl.when(pl.program_id(1) == 0)
    def _():
        carry_ref[...] = jnp.zeros_like(carry_ref)

    f = lax.dot_general(wf_ref[...], x_ref[...].astype(BF16), NT_DIMS, preferred_element_type=F32)
    z = f + bf_ref[...]
    log_f = jnp.minimum(z, 0.0) - jnp.log1p(jnp.exp(-jnp.abs(z)))
    tc = z.shape[1]
    src = lax.broadcasted_iota(jnp.int32, (tc, tc), 0)
    dst = lax.broadcasted_iota(jnp.int32, (tc, tc), 1)
    tri = jnp.where(src <= dst, 1.0, 0.0).astype(BF16)
    hi, mid, lo = _split3(log_f)
    cum = _dot(hi, tri) + _dot(mid, tri) + _dot(lo, tri) + carry_ref[:, 0:1]
    o_ref[0] = cum[:n_heads] * LOG2_E
    carry_ref[...] = jnp.broadcast_to(cum[:, tc - 1:tc], carry_ref.shape)


def _fox_cum_log_forget(x, w_f, b_forget, batch, seq):
    n, d = x.shape
    n_heads = w_f.shape[1]
    rows = 16
    wf_t = jnp.zeros((rows, d), BF16).at[:n_heads].set(w_f.T.astype(BF16))
    bf = jnp.zeros((rows, 1), F32).at[:n_heads, 0].set(b_forget.astype(F32))
    tc = _tile(seq, 512)
    nc = seq // tc
    return pl.pallas_call(
        functools.partial(_fox_gate_kernel, n_heads=n_heads), grid=(batch, nc),
        in_specs=[pl.BlockSpec((tc, d), lambda b, s: (b * nc + s, 0)),
                  pl.BlockSpec((rows, d), lambda b, s: (0, 0)),
                  pl.BlockSpec((rows, 1), lambda b, s: (0, 0))],
        out_specs=pl.BlockSpec((1, n_heads, tc), lambda b, s: (b, 0, s)),
        out_shape=jax.ShapeDtypeStruct((batch, n_heads, seq), F32),
        scratch_shapes=[pltpu.VMEM((rows, LANE), F32)],
        compiler_params=_cparams("parallel", "arbitrary"), name="fox_gate")(x, wf_t, bf)


def _flash_kernel(*refs, tk, dk, dv, heads, has_bias, n_cast_slabs):
    refs = list(refs)
    q_ref, k_ref, v_ref = refs[:3]
    del refs[:3]
    ck_ref = refs.pop(0) if has_bias else None
    w_ref = refs.pop(0) if n_cast_slabs else None
    o_ref = refs.pop(0)
    qi = pl.program_id(2)
    tq = q_ref.shape[0]
    qs = [q_ref[:, g * dk:(g + 1) * dk] for g in range(heads)]

    if n_cast_slabs:
        step = (pl.program_id(0) * pl.num_programs(1) + pl.program_id(1)) * pl.num_programs(2) + qi
        _cast_rider_step(step, n_cast_slabs, w_ref, refs[0])

    def step(g, state, j, masked):
        m, l, acc = state
        kv_rows = pl.ds(pl.multiple_of(j * tk, tk), tk)
        s = lax.dot_general(qs[g], k_ref[kv_rows, g * dk:(g + 1) * dk], NT_DIMS, preferred_element_type=F32)
        if has_bias:
            s = s - ck_ref[0, g, j]
        if masked:
            row = lax.broadcasted_iota(jnp.int32, s.shape, 0)
            col = lax.broadcasted_iota(jnp.int32, s.shape, 1)
            s = jnp.where(row >= col, s, -jnp.inf)
        m_new = jnp.maximum(m, jnp.max(s, axis=-1, keepdims=True))
        a = jnp.exp2(m - m_new)
        p = jnp.exp2(s - m_new)
        l = a * l + jnp.sum(p, axis=-1, keepdims=True)
        acc = a * acc + _dot(p.astype(BF16), v_ref[kv_rows, g * dv:(g + 1) * dv])
        return m_new, l, acc

    def body(j, carry):
        return tuple(step(g, carry[g], j, False) for g in range(heads))

    init = (jnp.full((tq, 1), -jnp.inf, F32), jnp.zeros((tq, 1), F32), jnp.zeros((tq, dv), F32))
    carry = lax.fori_loop(0, qi, body, (init,) * heads)
    for g in range(heads):
        _, l, acc = step(g, carry[g], qi, True)
        o_ref[:, g * dv:(g + 1) * dv] = (acc / l).astype(o_ref.dtype)


BF16_SUBLANES = 16


def _slab_rows(total_rows, max_slabs):
    rows = BF16_SUBLANES
    while total_rows % rows or total_rows // rows > max_slabs:
        rows += BF16_SUBLANES
    return rows


def _cast_rider(w, grid, step_of):
    flat = w.reshape(-1, w.shape[-1])
    rows = _slab_rows(flat.shape[0], math.prod(grid))
    n_slabs = flat.shape[0] // rows
    spec = pl.BlockSpec((rows, flat.shape[1]), lambda *g: (jnp.minimum(step_of(*g), n_slabs - 1), 0))
    return flat, spec, jax.ShapeDtypeStruct(flat.shape, BF16), n_slabs


def _cast_rider_step(step, n_slabs, w_ref, o_ref):
    @pl.when(step < n_slabs)
    def _():
        o_ref[...] = w_ref[...].astype(o_ref.dtype)


def _causal_attention(q_arr, k_arr, v_arr, q_col, k_col, v_col, dk, dv, n_heads, batch, seq,
                      ck=None, cast_rider=None, name="attn"):
    heads = 4
    assert n_heads % heads == 0 and q_col % heads == 0 and k_col % heads == 0 and v_col % heads == 0
    t = _tile(seq, 512)
    nq = seq // t
    grid = (batch, n_heads // heads, nq)
    in_specs = [pl.BlockSpec((t, heads * dk), lambda b, h, i: (b * nq + i, q_col // heads + h)),
                pl.BlockSpec((seq, heads * dk), lambda b, h, i: (b, k_col // heads + h)),
                pl.BlockSpec((seq, heads * dv), lambda b, h, i: (b, v_col // heads + h))]
    args = [q_arr, k_arr, v_arr]
    out_specs = [pl.BlockSpec((t, heads * dv), lambda b, h, i: (b * nq + i, h))]
    out_shape = [jax.ShapeDtypeStruct((batch * seq, n_heads * dv), BF16)]
    if ck is not None:
        args.append(ck.reshape(batch, n_heads, nq, 1, t))
        in_specs.append(pl.BlockSpec((1, heads, nq, 1, t), lambda b, h, i: (b, h, 0, 0, 0)))
    n_slabs = 0
    if cast_rider is not None:
        flat, slab, flat_bf, n_slabs = _cast_rider(cast_rider, grid, lambda b, h, i: (b * grid[1] + h) * nq + i)
        args.append(flat)
        in_specs.append(slab)
        out_specs.append(slab)
        out_shape.append(flat_bf)
    outs = pl.pallas_call(
        functools.partial(_flash_kernel, tk=t, dk=dk, dv=dv, heads=heads, has_bias=ck is not None,
                          n_cast_slabs=n_slabs),
        grid=grid, in_specs=in_specs, out_specs=out_specs, out_shape=out_shape,
        compiler_params=_cparams("parallel", "parallel", "arbitrary"), name=name)(*args)
    if cast_rider is None:
        return outs[0]
    return outs[0], outs[1].reshape(cast_rider.shape)


def _swa_kernel(sink_ref, q_ref, kp_ref, kc_ref, vp_ref, vc_ref, *rest, scale, n_q, n_kv, rider_slabs):
    n_riders = len(rider_slabs)
    o_ref = rest[n_riders]
    step = pl.program_id(0) * pl.num_programs(1) + pl.program_id(1)
    for n_slabs, w_ref, wo_ref in zip(rider_slabs, rest[:n_riders], rest[n_riders + 1:]):
        _cast_rider_step(step, n_slabs, w_ref, wo_ref)
    w = SWA_WINDOW
    qi = lax.broadcasted_iota(jnp.int32, (w, 2 * w), 0)
    ji = lax.broadcasted_iota(jnp.int32, (w, 2 * w), 1)
    has_prev = pl.program_id(1) > 0
    mask = (ji > qi) & (ji <= qi + w) & ((ji >= w) | has_prev)
    group = n_q // n_kv
    for kvh in range(n_kv):
        cols = slice(kvh * HEAD_DIM, (kvh + 1) * HEAD_DIM)
        k = jnp.concatenate([kp_ref[:, cols], kc_ref[:, cols]], axis=0)
        v = jnp.concatenate([vp_ref[:, cols], vc_ref[:, cols]], axis=0)
        for g in range(group):
            h = kvh * group + g
            hc = slice(h * HEAD_DIM, (h + 1) * HEAD_DIM)
            s = lax.dot_general(q_ref[:, hc], k, NT_DIMS, preferred_element_type=F32) * scale
            s = jnp.where(mask, s, -jnp.inf)
            sink = sink_ref[h]
            m = jnp.maximum(jnp.max(s, axis=-1, keepdims=True), sink)
            p = jnp.exp(s - m)
            denom = jnp.sum(p, axis=-1, keepdims=True) + jnp.exp(sink - m)
            o_ref[:, hc] = (_dot(p.astype(BF16), v) / denom).astype(o_ref.dtype)


def _swa_attention(qkv, q_off, k_off, v_off, sinks, n_q, n_kv, batch, seq, scale, cast_riders):
    w = SWA_WINDOW
    nb = seq // w
    grid = (batch, nb)
    qw, kw = n_q * HEAD_DIM, n_kv * HEAD_DIM
    assert q_off % qw == 0 and k_off % kw == 0 and v_off % kw == 0
    q_col, k_col, v_col = q_off // qw, k_off // kw, v_off // kw
    cur = lambda b, n: b * nb + n
    prev = lambda b, n: b * nb + jnp.maximum(n - 1, 0)
    riders = [_cast_rider(r, grid, cur) for r in cast_riders]
    outs = pl.pallas_call(
        functools.partial(_swa_kernel, scale=scale, n_q=n_q, n_kv=n_kv,
                          rider_slabs=tuple(r[3] for r in riders)), grid=grid,
        in_specs=[pl.BlockSpec(memory_space=pltpu.SMEM),
                  pl.BlockSpec((w, qw), lambda b, n: (cur(b, n), q_col)),
                  pl.BlockSpec((w, kw), lambda b, n: (prev(b, n), k_col)),
                  pl.BlockSpec((w, kw), lambda b, n: (cur(b, n), k_col)),
                  pl.BlockSpec((w, kw), lambda b, n: (prev(b, n), v_col)),
                  pl.BlockSpec((w, kw), lambda b, n: (cur(b, n), v_col))] + [r[1] for r in riders],
        out_specs=[pl.BlockSpec((w, qw), lambda b, n: (cur(b, n), 0))] + [r[1] for r in riders],
        out_shape=[jax.ShapeDtypeStruct((batch * seq, qw), BF16)] + [r[2] for r in riders],
        compiler_params=_cparams("arbitrary", "arbitrary"), name="swa_attn")(
            sinks.astype(F32), qkv, qkv, qkv, qkv, qkv, *[r[0] for r in riders])
    return outs[0], [o.reshape(r.shape) for o, r in zip(outs[1:], cast_riders)]


def _top2_route(x, w, n_experts):
    xh = x.astype(BF16)
    xl = (x - xh.astype(F32)).astype(BF16)
    wh = w.astype(BF16)
    wl = (w - wh.astype(F32)).astype(BF16)
    logits = _dot(xh, wh) + _dot(xh, wl) + _dot(xl, wh)
    lane = lax.broadcasted_iota(jnp.int32, logits.shape, 1)
    lane_f = lane.astype(F32)
    l1 = jnp.where(lane < n_experts, logits, -jnp.inf)
    v1 = jnp.max(l1, axis=-1, keepdims=True)
    i1 = jnp.min(jnp.where(l1 == v1, lane_f, float(LANE)), axis=-1, keepdims=True)
    l2 = jnp.where(lane_f == i1, -jnp.inf, l1)
    v2 = jnp.max(l2, axis=-1, keepdims=True)
    i2 = jnp.min(jnp.where(l2 == v2, lane_f, float(LANE)), axis=-1, keepdims=True)
    e2 = jnp.exp(v2 - v1)
    idx = jnp.where(lane == 0, i1, jnp.where(lane == 1, i2, 0.0)).astype(jnp.int32)
    gate = jnp.where(lane == 0, 1.0 / (1.0 + e2), jnp.where(lane == 1, e2 / (1.0 + e2), 0.0))
    return idx, gate


def _mm_res_ln_kernel(*refs, n_pairs, n_experts):
    a_refs, w_refs = refs[:n_pairs], refs[n_pairs:2 * n_pairs]
    if n_experts:
        res_ref, g_ref, b_ref, wr_ref, o_ref, idx_ref, gate_ref = refs[2 * n_pairs:]
    else:
        res_ref, g_ref, b_ref, o_ref = refs[2 * n_pairs:]
    tm = o_ref.shape[0]
    sub = _tile(tm, 256)
    for r in range(tm // sub):
        rows = slice(r * sub, (r + 1) * sub)
        acc = _dot(a_refs[0][rows, :], w_refs[0][...])
        for a_ref, w_ref in zip(a_refs[1:], w_refs[1:]):
            acc = acc + _dot(a_ref[rows, :], w_ref[...])
        y = _layer_norm(DEEPNORM_ALPHA * res_ref[rows, :] + acc, g_ref[...], b_ref[...])
        o_ref[rows, :] = y
        if n_experts:
            idx_ref[rows, :], gate_ref[rows, :] = _top2_route(y, wr_ref[...], n_experts)


def _matmul_residual_ln(pairs, res, g, b, name, w_router=None):
    m, d = res.shape
    tm = _tile(m, 512)
    row = lambda i: (i, 0)
    fixed = lambda i: (0, 0)
    resident = lambda shape: pl.BlockSpec(shape, fixed, pipeline_mode=pl.Buffered(1))
    in_specs = ([pl.BlockSpec((tm, a.shape[1]), row) for a, _ in pairs] + [resident(w.shape) for _, w in pairs]
                + [pl.BlockSpec((tm, d), row), pl.BlockSpec((1, d), fixed), pl.BlockSpec((1, d), fixed)])
    args = [a for a, _ in pairs] + [w for _, w in pairs] + [res, g.reshape(1, d), b.reshape(1, d)]
    out_specs = [pl.BlockSpec((tm, d), row)]
    out_shape = [jax.ShapeDtypeStruct((m, d), F32)]
    n_experts = 0
    if w_router is not None:
        n_experts = w_router.shape[1]
        in_specs.append(resident((d, LANE)))
        args.append(jnp.zeros((d, LANE), F32).at[:, :n_experts].set(w_router.astype(F32)))
        out_specs += [pl.BlockSpec((tm, LANE), row), pl.BlockSpec((tm, LANE), row)]
        out_shape += [jax.ShapeDtypeStruct((m, LANE), jnp.int32), jax.ShapeDtypeStruct((m, LANE), F32)]
    return pl.pallas_call(
        functools.partial(_mm_res_ln_kernel, n_pairs=len(pairs), n_experts=n_experts),
        grid=(m // tm,), in_specs=in_specs, out_specs=out_specs, out_shape=out_shape,
        compiler_params=_cparams("parallel"), name=name)(*args)


def _swiglu(x, wg, wu, wd):
    a = _silu(_dot(x, wg)) * _dot(x, wu)
    return _dot(a.astype(BF16), wd)


def _ffn_ln_kernel(h_ref, wg_ref, wu_ref, wd_ref, g_ref, b_ref, o_ref, x_ref):
    j = pl.program_id(1)

    @pl.when(j == 0)
    def _():
        x_ref[...] = h_ref[...].astype(BF16)
        o_ref[...] = jnp.zeros_like(o_ref)

    o_ref[...] += _swiglu(x_ref[...], wg_ref[...], wu_ref[...], wd_ref[...])

    @pl.when(j == pl.num_programs(1) - 1)
    def _():
        o_ref[...] = _layer_norm(DEEPNORM_ALPHA * h_ref[...] + o_ref[...], g_ref[...], b_ref[...])


def _swiglu_residual_ln(h, wg, wu, wd, g, b):
    m, d = h.shape
    f = wg.shape[1]
    tm, tf = _tile(m, 512), _tile(f, 512)
    row = lambda i, j: (i, 0)
    fixed = lambda i, j: (0, 0)
    return pl.pallas_call(
        _ffn_ln_kernel, grid=(m // tm, f // tf),
        in_specs=[pl.BlockSpec((tm, d), row),
                  pl.BlockSpec((d, tf), lambda i, j: (0, j)), pl.BlockSpec((d, tf), lambda i, j: (0, j)),
                  pl.BlockSpec((tf, d), lambda i, j: (j, 0)),
                  pl.BlockSpec((1, d), fixed), pl.BlockSpec((1, d), fixed)],
        out_specs=pl.BlockSpec((tm, d), row),
        out_shape=jax.ShapeDtypeStruct((m, d), F32),
        scratch_shapes=[pltpu.VMEM((tm, d), BF16)],
        compiler_params=_cparams("parallel", "arbitrary"), name="ffn_ln")(
            h, wg, wu, wd, g.reshape(1, d), b.reshape(1, d))


def _rms_norm(c, g):
    ms = jnp.mean(c * c, axis=-1, keepdims=True)
    return c * lax.rsqrt(ms + RMS_EPS) * g


def _rope64(pair, tab):
    w = pair * tab
    lane = lax.broadcasted_iota(jnp.int32, w.shape, 1)
    return jnp.where(lane < MLA_ROPE_DIM, w + pltpu.roll(w, MLA_ROPE_DIM, 1), 0.0)


def _mla_proj_kernel(x_ref, win_ref, gq_ref, gkv_ref, tab_ref, wq_ref, wkv_ref, q_ref, k_ref, v_ref, *,
                     q_rank, kv_rank, n_heads, q_scale):
    acc = _dot(x_ref[...].astype(BF16), win_ref[...])
    cq = _rms_norm(acc[:, :q_rank], gq_ref[...]).astype(BF16)
    ckv = _rms_norm(acc[:, q_rank:q_rank + kv_rank], gkv_ref[...]).astype(BF16)
    tab = tab_ref[...]
    k_pe = _rope64(acc[:, q_rank + kv_rank:], tab).astype(BF16)
    wkv_w = MLA_NOPE_DIM + MLA_V_DIM
    for h in range(n_heads):
        q0 = h * MLA_QK_PAD
        qh = _dot(cq, wq_ref[:, q0:q0 + MLA_QK_PAD]) * q_scale
        q_ref[:, q0:q0 + MLA_NOPE_DIM] = qh[:, :MLA_NOPE_DIM].astype(BF16)
        q_ref[:, q0 + MLA_NOPE_DIM:q0 + MLA_QK_PAD] = _rope64(qh[:, MLA_NOPE_DIM:], tab).astype(BF16)
        kvh = _dot(ckv, wkv_ref[:, h * wkv_w:(h + 1) * wkv_w])
        k_ref[:, q0:q0 + MLA_NOPE_DIM] = kvh[:, :MLA_NOPE_DIM].astype(BF16)
        k_ref[:, q0 + MLA_NOPE_DIM:q0 + MLA_QK_PAD] = k_pe
        v_ref[:, h * MLA_V_DIM:(h + 1) * MLA_V_DIM] = kvh[:, MLA_NOPE_DIM:].astype(BF16)


def _mla_projections(x, w_in, gq, gkv, tab, w_q, w_kv, q_rank, kv_rank, n_heads, q_scale):
    m, d = x.shape
    tm = _tile(m, 512)
    row = lambda i: (i, 0)
    fixed = lambda i: (0, 0)
    resident = lambda shape: pl.BlockSpec(shape, fixed, pipeline_mode=pl.Buffered(1))
    qk_w, v_w = n_heads * MLA_QK_PAD, n_heads * MLA_V_DIM
    return pl.pallas_call(
        functools.partial(_mla_proj_kernel, q_rank=q_rank, kv_rank=kv_rank, n_heads=n_heads, q_scale=q_scale),
        grid=(m // tm,),
        in_specs=[pl.BlockSpec((tm, d), row), resident(w_in.shape),
                  pl.BlockSpec((1, q_rank), fixed), pl.BlockSpec((1, kv_rank), fixed),
                  pl.BlockSpec((tm, LANE), row), resident(w_q.shape), resident(w_kv.shape)],
        out_specs=[pl.BlockSpec((tm, qk_w), row), pl.BlockSpec((tm, qk_w), row), pl.BlockSpec((tm, v_w), row)],
        out_shape=[jax.ShapeDtypeStruct((m, qk_w), BF16), jax.ShapeDtypeStruct((m, qk_w), BF16),
                   jax.ShapeDtypeStruct((m, v_w), BF16)],
        compiler_params=_cparams("parallel"), name="mla_proj")(
            x, w_in, gq.reshape(1, q_rank).astype(F32), gkv.reshape(1, kv_rank).astype(F32), tab, w_q, w_kv)


F32_SUBLANES = 8
MOE_TILE_PARTS = 4


def _row_copy(src_hbm, src_row, group_buf, sublane, sem):
    return pltpu.make_async_copy(src_hbm.at[pl.ds(src_row, 1), :], group_buf.at[pl.ds(sublane, 1), :], sem)


def _wait_group(src_hbm, group_buf, sem):
    pltpu.make_async_copy(src_hbm.at[pl.ds(0, group_buf.shape[0]), :], group_buf, sem).wait()


def _gather_rows_kernel(tok_ref, x_hbm, o_ref, buf, sem):
    i, n = pl.program_id(0), pl.num_programs(0)
    groups, sub, d = buf.shape[1:]
    rows = groups * sub

    def issue(step, slot):
        def start(g, _):
            for s in range(sub):
                _row_copy(x_hbm, tok_ref[step * rows + g * sub + s], buf.at[slot, g], s, sem.at[slot]).start()
            return 0
        lax.fori_loop(0, groups, start, 0)

    @pl.when(i == 0)
    def _():
        issue(0, 0)

    @pl.when(i + 1 < n)
    def _():
        issue(i + 1, (i + 1) % 2)

    slot = i % 2

    def wait(g, _):
        _wait_group(x_hbm, buf.at[slot, g], sem.at[slot])
        return 0
    lax.fori_loop(0, groups, wait, 0)
    o_ref[...] = buf[slot].reshape(rows, d).astype(o_ref.dtype)


def _gather_rows(h, row_token, rows_per_step):
    n_rows = row_token.shape[0]
    d = h.shape[1]
    return pl.pallas_call(
        _gather_rows_kernel,
        grid_spec=pltpu.PrefetchScalarGridSpec(
            num_scalar_prefetch=1, grid=(n_rows // rows_per_step,),
            in_specs=[pl.BlockSpec(memory_space=pl.ANY)],
            out_specs=pl.BlockSpec((rows_per_step, d), lambda i, tok: (i, 0)),
            scratch_shapes=[pltpu.VMEM((2, rows_per_step // F32_SUBLANES, F32_SUBLANES, d), F32),
                            pltpu.SemaphoreType.DMA((2,))]),
        out_shape=jax.ShapeDtypeStruct((n_rows, d), BF16),
        compiler_params=_cparams("arbitrary"), name="moe_gather")(row_token, h)


def _moe_ffn_kernel(te_ref, tr_ref, x_ref, wg_ref, wu_ref, wd_ref, o_ref):
    i, j = pl.program_id(0), pl.program_id(1)
    part = x_ref.shape[0] // MOE_TILE_PARTS
    parts_used = (tr_ref[i] + part - 1) // part

    @pl.when(j == 0)
    def _():
        o_ref[...] = jnp.zeros_like(o_ref)

    for used in range(1, MOE_TILE_PARTS + 1):
        @pl.when(parts_used == used)
        def _():
            rows = slice(0, used * part)
            o_ref[rows, :] += _swiglu(x_ref[rows, :], wg_ref[0], wu_ref[0], wd_ref[0].astype(BF16))


def _moe_ffn(xg, wg, wu, wd, tile_expert, tile_rows, tm):
    n_rows, d = xg.shape
    f = wg.shape[2]
    tf = _tile(f, 1024)
    nj = f // tf
    col = lambda i, j, te, tr: jnp.where(tr[i] > 0, j, nj - 1)
    return pl.pallas_call(
        _moe_ffn_kernel,
        grid_spec=pltpu.PrefetchScalarGridSpec(
            num_scalar_prefetch=2, grid=(n_rows // tm, nj),
            in_specs=[pl.BlockSpec((tm, d), lambda i, j, te, tr: (i, 0)),
                      pl.BlockSpec((1, d, tf), lambda i, j, te, tr: (te[i], 0, col(i, j, te, tr))),
                      pl.BlockSpec((1, d, tf), lambda i, j, te, tr: (te[i], 0, col(i, j, te, tr))),
                      pl.BlockSpec((1, tf, d), lambda i, j, te, tr: (te[i], col(i, j, te, tr), 0))],
            out_specs=pl.BlockSpec((tm, d), lambda i, j, te, tr: (i, 0))),
        out_shape=jax.ShapeDtypeStruct((n_rows, d), F32),
        compiler_params=_cparams("arbitrary", "arbitrary"), name="moe_ffn")(
            tile_expert, tile_rows, xg, wg, wu, wd)


def _combine_ln_kernel(pos_ref, h_ref, gate_ref, g_ref, b_ref, y_hbm, o_ref, buf, sem):
    groups, sub, d = buf.shape[2:]
    rows = groups * sub
    i, n = pl.program_id(0), pl.num_programs(0)

    def issue(step, slot):
        def start(g, _):
            for s in range(sub):
                for k in range(TOP_K):
                    src = pos_ref[TOP_K * (step * rows + g * sub + s) + k]
                    _row_copy(y_hbm, src, buf.at[slot, k, g], s, sem.at[slot, k]).start()
            return 0
        lax.fori_loop(0, groups, start, 0)

    @pl.when(i == 0)
    def _():
        issue(0, 0)

    @pl.when(i + 1 < n)
    def _():
        issue(i + 1, (i + 1) % 2)

    slot = i % 2

    def wait(g, _):
        for k in range(TOP_K):
            _wait_group(y_hbm, buf.at[slot, k, g], sem.at[slot, k])
        return 0
    lax.fori_loop(0, groups, wait, 0)
    gate = gate_ref[...]
    y = None
    for k in range(TOP_K):
        term = buf[slot, k].reshape(rows, d) * gate[:, k:k + 1]
        y = term if y is None else y + term
    o_ref[...] = _layer_norm(DEEPNORM_ALPHA * h_ref[...] + y, g_ref[...], b_ref[...])


def _moe_combine_ln(y_rows, pos, h, gates, g, b):
    m, d = h.shape
    tm = _tile(m, 256)
    row = lambda i, p: (i, 0)
    fixed = lambda i, p: (0, 0)
    return pl.pallas_call(
        _combine_ln_kernel,
        grid_spec=pltpu.PrefetchScalarGridSpec(
            num_scalar_prefetch=1, grid=(m // tm,),
            in_specs=[pl.BlockSpec((tm, d), row), pl.BlockSpec((tm, LANE), row),
                      pl.BlockSpec((1, d), fixed), pl.BlockSpec((1, d), fixed),
                      pl.BlockSpec(memory_space=pl.ANY)],
            out_specs=pl.BlockSpec((tm, d), row),
            scratch_shapes=[pltpu.VMEM((2, TOP_K, tm // F32_SUBLANES, F32_SUBLANES, d), F32),
                            pltpu.SemaphoreType.DMA((2, TOP_K))]),
        out_shape=jax.ShapeDtypeStruct((m, d), F32),
        compiler_params=_cparams("arbitrary"), name="moe_combine_ln")(
            pos, h, gates, g.reshape(1, d), b.reshape(1, d), y_rows)


def _moe_routing(top_idx, n_experts, tm):
    n_assign = top_idx.size
    expert_of = top_idx.reshape(n_assign)
    onehot = (expert_of[:, None] == jnp.arange(n_experts, dtype=jnp.int32)[None, :]).astype(jnp.int32)
    csum = jnp.cumsum(onehot, axis=0)
    rank = jnp.sum(onehot * (csum - 1), axis=1)
    counts = csum[-1]
    padded = ((counts + tm - 1) // tm) * tm
    padded_end = jnp.cumsum(padded)
    padded_start = padded_end - padded
    pos = (padded_start[expert_of] + rank).astype(jnp.int32)
    n_tiles = -(-n_assign // tm) + n_experts
    row_token = jnp.zeros((n_tiles * tm,), jnp.int32).at[pos].set(
        jnp.arange(n_assign, dtype=jnp.int32) // TOP_K, unique_indices=True, mode="promise_in_bounds")
    tile_start = jnp.arange(n_tiles, dtype=jnp.int32) * tm
    tile_used = tile_start < padded_end[-1]
    tile_expert = jnp.minimum(jnp.searchsorted(padded_end, tile_start, side='right'), n_experts - 1)
    tile_rows = jnp.clip(counts[tile_expert] - (tile_start - padded_start[tile_expert]), 0, tm)
    tile_rows = jnp.where(tile_used, tile_rows, 0).astype(jnp.int32)
    last_expert = jnp.max(jnp.where(tile_used, tile_expert, 0))
    tile_expert = jnp.where(tile_used, tile_expert, last_expert).astype(jnp.int32)
    return pos, row_token, tile_expert, tile_rows


def _rotate_half_cols(w):
    half = w.shape[-1] // 2
    return jnp.concatenate([-w[..., half:], w[..., :half]], axis=-1)


def kernel(x, positions, w_in0, b_forget, sinks_b, w_out0, ln0_mix_g, ln0_mix_b, w_ffn_gate, w_ffn_up, w_ffn_down, ln0_ffn_g, ln0_ffn_b, w_in1, q_norm_g, w_uq, kv_norm_g, w_ukv, w_out1, ln1_mix_g, ln1_mix_b, w_router, w_moe_gate, w_moe_up, w_moe_down, ln1_ffn_g, ln1_ffn_b):
    batch, seq, d = x.shape
    n = batch * seq
    fox_heads = b_forget.shape[0]
    fox_w = fox_heads * HEAD_DIM
    swa_q = sinks_b.shape[0]
    swa_qw = swa_q * HEAD_DIM
    swa_kvw = (w_in0.shape[1] - 3 * fox_w - fox_heads - swa_qw) // 2
    swa_kv = swa_kvw // HEAD_DIM
    q_rank, kv_rank = q_norm_g.shape[0], kv_norm_g.shape[0]
    mla_heads = w_uq.shape[1] // (MLA_NOPE_DIM + MLA_ROPE_DIM)
    n_experts = w_router.shape[1]

    xf = x.reshape(n, d)
    cos_a, sin_a, tab_m = _rope_tables(positions)

    c0 = 3 * fox_w
    c1 = c0 + fox_heads
    w_qkv = jnp.concatenate([w_in0[:, :c0], w_in0[:, c1:]], axis=1).astype(BF16)
    group_kinds = ([SCALED] * fox_heads + [PLAIN] * (2 * fox_heads)
                   + [ROTARY] * (swa_q + swa_kv) + [PLAIN] * swa_kv)
    qkv = _project(xf, w_qkv, group_kinds, HEAD_DIM ** -0.5 * LOG2_E, cos_a, sin_a, 1024, 768, "proj0")
    cum_log_f = _fox_cum_log_forget(xf, w_in0[:, c0:c1], b_forget, batch, seq)
    o_a, w_moe_up_bf = _causal_attention(qkv, qkv, qkv, 0, fox_heads, 2 * fox_heads, HEAD_DIM, HEAD_DIM,
                                         fox_heads, batch, seq, ck=cum_log_f, cast_rider=w_moe_up,
                                         name="fox_attn")
    o_b, ffn_w_bf = _swa_attention(qkv, c0, c0 + swa_qw, c0 + swa_qw + swa_kvw, sinks_b, swa_q, swa_kv,
                                   batch, seq, HEAD_DIM ** -0.5,
                                   cast_riders=[w_ffn_gate, w_ffn_up, w_ffn_down])
    w_out0_bf = w_out0.astype(BF16)
    (h1,) = _matmul_residual_ln([(o_a, w_out0_bf[:fox_w]), (o_b, w_out0_bf[fox_w:])], xf,
                                ln0_mix_g, ln0_mix_b, "out0_ln")
    h2 = _swiglu_residual_ln(h1, *ffn_w_bf, ln0_ffn_g, ln0_ffn_b)

    r0 = q_rank + kv_rank
    w_in1_x = jnp.concatenate([w_in1, _rotate_half_cols(w_in1[:, r0:])], axis=1).astype(BF16)
    w_uq_h = w_uq.reshape(q_rank, mla_heads, MLA_NOPE_DIM + MLA_ROPE_DIM)
    w_uq_x = jnp.concatenate([w_uq_h, _rotate_half_cols(w_uq_h[..., MLA_NOPE_DIM:])], axis=-1)
    q_m, k_m, v_m = _mla_projections(
        h2, w_in1_x, q_norm_g, kv_norm_g, tab_m,
        w_uq_x.reshape(q_rank, mla_heads * MLA_QK_PAD).astype(BF16), w_ukv.astype(BF16),
        q_rank, kv_rank, mla_heads, (MLA_NOPE_DIM + MLA_ROPE_DIM) ** -0.5 * LOG2_E)
    o_m, w_moe_gate_bf = _causal_attention(q_m, k_m, v_m, 0, 0, 0, MLA_QK_PAD, MLA_V_DIM, mla_heads, batch,
                                           seq, cast_rider=w_moe_gate, name="mla_attn")
    h3, idx, gates = _matmul_residual_ln([(o_m, w_out1.astype(BF16))], h2, ln1_mix_g, ln1_mix_b,
                                         "out1_ln_route", w_router=w_router)

    moe_tm = _tile(n * TOP_K, 512)
    pos, row_token, tile_expert, tile_rows = _moe_routing(idx[:, :TOP_K], n_experts, moe_tm)
    xg = _gather_rows(h3, row_token, _tile(moe_tm, 256))
    y_rows = _moe_ffn(xg, w_moe_gate_bf, w_moe_up_bf, w_moe_down, tile_expert, tile_rows, moe_tm)
    out = _moe_combine_ln(y_rows, pos, h3, gates, ln1_ffn_g, ln1_ffn_b)
    return out.reshape(batch, seq, d)
```

```python
import functools
import math

import jax
import jax.numpy as jnp
from jax import lax
from jax.experimental import pallas as pl
from jax.experimental.pallas import tpu as pltpu

F32 = jnp.float32
BF16 = jnp.bfloat16

LANE = 128
HEAD_DIM = 128
SWA_WINDOW = 128
MLA_NOPE_DIM = 128
MLA_ROPE_DIM = 64
MLA_V_DIM = 128
MLA_QK_PAD = 256
ROPE_THETA = 10000.0
TOP_K = 2
LN_EPS = 1e-5
RMS_EPS = 1e-6
DEPTH = 2
DEEPNORM_ALPHA = (2 * DEPTH) ** 0.25
LOG2_E = math.log2(math.e)

NT_DIMS = (((1,), (1,)), ((), ()))


def _tile(n, pref):
    if n <= pref:
        return n
    t = (pref // LANE) * LANE
    while n % t:
        t -= LANE
    return t


def _cparams(*sem):
    return pltpu.CompilerParams(dimension_semantics=sem)


def _dot(a, b):
    return jnp.dot(a, b, preferred_element_type=F32)


def _layer_norm(y, g, b):
    mu = jnp.mean(y, axis=-1, keepdims=True)
    d = y - mu
    var = jnp.mean(d * d, axis=-1, keepdims=True)
    return d * lax.rsqrt(var + LN_EPS) * g + b


def _silu(g):
    return g / (1.0 + jnp.exp(-g))


def _rope_table_kernel(pos_ref, freq_ref, cos_a_ref, sin_a_ref, tab_m_ref):
    pos = pos_ref[...]
    ang_a = pos * freq_ref[0:1, :]
    lane = lax.broadcasted_iota(jnp.int32, ang_a.shape, 1)
    cos_a_ref[...] = jnp.cos(ang_a)
    sin_a_ref[...] = jnp.where(lane < HEAD_DIM // 2, -jnp.sin(ang_a), jnp.sin(ang_a))
    ang_m = pos * freq_ref[1:2, :]
    tab_m_ref[...] = jnp.where(lane < MLA_ROPE_DIM, jnp.cos(ang_m), jnp.sin(ang_m))


def _rope_tables(positions):
    n = positions.size
    pos = positions.astype(F32).reshape(n, 1)
    half_a = HEAD_DIM // 2
    half_m = MLA_ROPE_DIM // 2
    inv_a = ROPE_THETA ** (-2.0 * jnp.arange(half_a, dtype=F32) / HEAD_DIM)
    inv_m = ROPE_THETA ** (-2.0 * jnp.arange(half_m, dtype=F32) / MLA_ROPE_DIM)
    freq = jnp.stack([jnp.tile(inv_a, 2), jnp.tile(inv_m, 4)])
    tm = _tile(n, 1024)
    out = jax.ShapeDtypeStruct((n, LANE), F32)
    row = pl.BlockSpec((tm, LANE), lambda i: (i, 0))
    return pl.pallas_call(
        _rope_table_kernel, grid=(n // tm,),
        in_specs=[pl.BlockSpec((tm, 1), lambda i: (i, 0)), pl.BlockSpec((2, LANE), lambda i: (0, 0))],
        out_specs=[row, row, row], out_shape=[out, out, out],
        compiler_params=_cparams("parallel"), name="rope_tables")(pos, freq)


PLAIN, SCALED, ROTARY = "plain", "scaled", "rotary"


def _proj_kernel(x_ref, w_ref, cos_ref, sin_ref, o_ref, xb_ref, *, tile_kinds, scale):
    j = pl.program_id(1)

    @pl.when(j == 0)
    def _():
        xb_ref[...] = x_ref[...].astype(BF16)

    def tile(kinds):
        acc = _dot(xb_ref[...], w_ref[...])
        for c, kind in enumerate(kinds):
            cols = slice(c * HEAD_DIM, (c + 1) * HEAD_DIM)
            a = acc[:, cols]
            if kind == SCALED:
                a = a * scale
            elif kind == ROTARY:
                a = a * cos_ref[...] + pltpu.roll(a, HEAD_DIM // 2, 1) * sin_ref[...]
            o_ref[:, cols] = a.astype(o_ref.dtype)

    for kinds in sorted(set(tile_kinds)):
        first = tile_kinds.index(kinds)
        count = tile_kinds.count(kinds)
        assert tile_kinds[first:first + count] == (kinds,) * count
        pl.when((j >= first) & (j < first + count))(functools.partial(tile, kinds))


def _project(x, w, group_kinds, scale, cos, sin, tm, tn, name):
    m, k = x.shape
    n = w.shape[1]
    tm, tn = _tile(m, tm), _tile(n, tn)
    per_tile = tn // HEAD_DIM
    tile_kinds = tuple(tuple(group_kinds[t * per_tile:(t + 1) * per_tile]) for t in range(n // tn))
    return pl.pallas_call(
        functools.partial(_proj_kernel, tile_kinds=tile_kinds, scale=scale), grid=(m // tm, n // tn),
        in_specs=[pl.BlockSpec((tm, k), lambda i, j: (i, 0)), pl.BlockSpec((k, tn), lambda i, j: (0, j)),
                  pl.BlockSpec((tm, LANE), lambda i, j: (i, 0)), pl.BlockSpec((tm, LANE), lambda i, j: (i, 0))],
        out_specs=pl.BlockSpec((tm, tn), lambda i, j: (i, j)),
        out_shape=jax.ShapeDtypeStruct((m, n), BF16),
        scratch_shapes=[pltpu.VMEM((tm, k), BF16)],
        compiler_params=_cparams("parallel", "arbitrary"), name=name)(x, w, cos, sin)


def _split3(x):
    hi = x.astype(BF16)
    r = x - hi.astype(F32)
    mid = r.astype(BF16)
    lo = (r - mid.astype(F32)).astype(BF16)
    return hi, mid, lo


def _fox_gate_kernel(x_ref, wf_ref, bf_ref, o_ref, carry_ref, *, n_heads):
    @pl.when(pl.program_id(1) == 0)
    def _():
        carry_ref[...] = jnp.zeros_like(carry_ref)

    f = lax.dot_general(wf_ref[...], x_ref[...].astype(BF16), NT_DIMS, preferred_element_type=F32)
    z = f + bf_ref[...]
    log_f = jnp.minimum(z, 0.0) - jnp.log1p(jnp.exp(-jnp.abs(z)))
    tc = z.shape[1]
    src = lax.broadcasted_iota(jnp.int32, (tc, tc), 0)
    dst = lax.broadcasted_iota(jnp.int32, (tc, tc), 1)
    tri = jnp.where(src <= dst, 1.0, 0.0).astype(BF16)
    hi, mid, lo = _split3(log_f)
    cum = _dot(hi, tri) + _dot(mid, tri) + _dot(lo, tri) + carry_ref[:, 0:1]
    o_ref[0] = cum[:n_heads] * LOG2_E
    carry_ref[...] = jnp.broadcast_to(cum[:, tc - 1:tc], carry_ref.shape)


def _fox_cum_log_forget(x, w_f, b_forget, batch, seq):
    n, d = x.shape
    n_heads = w_f.shape[1]
    rows = 16
    wf_t = jnp.zeros((rows, d), BF16).at[:n_heads].set(w_f.T.astype(BF16))
    bf = jnp.zeros((rows, 1), F32).at[:n_heads, 0].set(b_forget.astype(F32))
    tc = _tile(seq, 512)
    nc = seq // tc
    return pl.pallas_call(
        functools.partial(_fox_gate_kernel, n_heads=n_heads), grid=(batch, nc),
        in_specs=[pl.BlockSpec((tc, d), lambda b, s: (b * nc + s, 0)),
                  pl.BlockSpec((rows, d), lambda b, s: (0, 0)),
                  pl.BlockSpec((rows, 1), lambda b, s: (0, 0))],
        out_specs=pl.BlockSpec((1, n_heads, tc), lambda b, s: (b, 0, s)),
        out_shape=jax.ShapeDtypeStruct((batch, n_heads, seq), F32),
        scratch_shapes=[pltpu.VMEM((rows, LANE), F32)],
        compiler_params=_cparams("parallel", "arbitrary"), name="fox_gate")(x, wf_t, bf)


def _flash_kernel(*refs, tk, dk, dv, heads, has_bias, n_cast_slabs):
    refs = list(refs)
    q_ref, k_ref, v_ref = refs[:3]
    del refs[:3]
    ck_ref = refs.pop(0) if has_bias else None
    w_ref = refs.pop(0) if n_cast_slabs else None
    o_ref = refs.pop(0)
    qi = pl.program_id(2)
    tq = q_ref.shape[0]
    qs = [q_ref[:, g * dk:(g + 1) * dk] for g in range(heads)]

    if n_cast_slabs:
        step = (pl.program_id(0) * pl.num_programs(1) + pl.program_id(1)) * pl.num_programs(2) + qi
        _cast_rider_step(step, n_cast_slabs, w_ref, refs[0])

    def step(g, state, j, masked):
        m, l, acc = state
        kv_rows = pl.ds(pl.multiple_of(j * tk, tk), tk)
        s = lax.dot_general(qs[g], k_ref[kv_rows, g * dk:(g + 1) * dk], NT_DIMS, preferred_element_type=F32)
        if has_bias:
            s = s - ck_ref[0, g, j]
        if masked:
            row = lax.broadcasted_iota(jnp.int32, s.shape, 0)
            col = lax.broadcasted_iota(jnp.int32, s.shape, 1)
            s = jnp.where(row >= col, s, -jnp.inf)
        m_new = jnp.maximum(m, jnp.max(s, axis=-1, keepdims=True))
        a = jnp.exp2(m - m_new)
        p = jnp.exp2(s - m_new)
        l = a * l + jnp.sum(p, axis=-1, keepdims=True)
        acc = a * acc + _dot(p.astype(BF16), v_ref[kv_rows, g * dv:(g + 1) * dv])
        return m_new, l, acc

    def body(j, carry):
        return tuple(step(g, carry[g], j, False) for g in range(heads))

    init = (jnp.full((tq, 1), -jnp.inf, F32), jnp.zeros((tq, 1), F32), jnp.zeros((tq, dv), F32))
    carry = lax.fori_loop(0, qi, body, (init,) * heads)
    for g in range(heads):
        _, l, acc = step(g, carry[g], qi, True)
        o_ref[:, g * dv:(g + 1) * dv] = (acc / l).astype(o_ref.dtype)


BF16_SUBLANES = 16


def _slab_rows(total_rows, max_slabs):
    rows = BF16_SUBLANES
    while total_rows % rows or total_rows // rows > max_slabs:
        rows += BF16_SUBLANES
    return rows


def _cast_rider(w, grid, step_of):
    flat = w.reshape(-1, w.shape[-1])
    rows = _slab_rows(flat.shape[0], math.prod(grid))
    n_slabs = flat.shape[0] // rows
    spec = pl.BlockSpec((rows, flat.shape[1]), lambda *g: (jnp.minimum(step_of(*g), n_slabs - 1), 0))
    return flat, spec, jax.ShapeDtypeStruct(flat.shape, BF16), n_slabs


def _cast_rider_step(step, n_slabs, w_ref, o_ref):
    @pl.when(step < n_slabs)
    def _():
        o_ref[...] = w_ref[...].astype(o_ref.dtype)


def _causal_attention(q_arr, k_arr, v_arr, q_col, k_col, v_col, dk, dv, n_heads, batch, seq,
                      ck=None, cast_rider=None, name="attn"):
    heads = 4
    assert n_heads % heads == 0 and q_col % heads == 0 and k_col % heads == 0 and v_col % heads == 0
    t = _tile(seq, 512)
    nq = seq // t
    grid = (batch, n_heads // heads, nq)
    in_specs = [pl.BlockSpec((t, heads * dk), lambda b, h, i: (b * nq + i, q_col // heads + h)),
                pl.BlockSpec((seq, heads * dk), lambda b, h, i: (b, k_col // heads + h)),
                pl.BlockSpec((seq, heads * dv), lambda b, h, i: (b, v_col // heads + h))]
    args = [q_arr, k_arr, v_arr]
    out_specs = [pl.BlockSpec((t, heads * dv), lambda b, h, i: (b * nq + i, h))]
    out_shape = [jax.ShapeDtypeStruct((batch * seq, n_heads * dv), BF16)]
    if ck is not None:
        args.append(ck.reshape(batch, n_heads, nq, 1, t))
        in_specs.append(pl.BlockSpec((1, heads, nq, 1, t), lambda b, h, i: (b, h, 0, 0, 0)))
    n_slabs = 0
    if cast_rider is not None:
        flat, slab, flat_bf, n_slabs = _cast_rider(cast_rider, grid, lambda b, h, i: (b * grid[1] + h) * nq + i)
        args.append(flat)
        in_specs.append(slab)
        out_specs.append(slab)
        out_shape.append(flat_bf)
    outs = pl.pallas_call(
        functools.partial(_flash_kernel, tk=t, dk=dk, dv=dv, heads=heads, has_bias=ck is not None,
                          n_cast_slabs=n_slabs),
        grid=grid, in_specs=in_specs, out_specs=out_specs, out_shape=out_shape,
        compiler_params=_cparams("parallel", "parallel", "arbitrary"), name=name)(*args)
    if cast_rider is None:
        return outs[0]
    return outs[0], outs[1].reshape(cast_rider.shape)


def _swa_kernel(sink_ref, q_ref, kp_ref, kc_ref, vp_ref, vc_ref, *rest, scale, n_q, n_kv, rider_slabs):
    n_riders = len(rider_slabs)
    o_ref = rest[n_riders]
    step = pl.program_id(0) * pl.num_programs(1) + pl.program_id(1)
    for n_slabs, w_ref, wo_ref in zip(rider_slabs, rest[:n_riders], rest[n_riders + 1:]):
        _cast_rider_step(step, n_slabs, w_ref, wo_ref)
    w = SWA_WINDOW
    qi = lax.broadcasted_iota(jnp.int32, (w, 2 * w), 0)
    ji = lax.broadcasted_iota(jnp.int32, (w, 2 * w), 1)
    has_prev = pl.program_id(1) > 0
    mask = (ji > qi) & (ji <= qi + w) & ((ji >= w) | has_prev)
    group = n_q // n_kv
    for kvh in range(n_kv):
        cols = slice(kvh * HEAD_DIM, (kvh + 1) * HEAD_DIM)
        k = jnp.concatenate([kp_ref[:, cols], kc_ref[:, cols]], axis=0)
        v = jnp.concatenate([vp_ref[:, cols], vc_ref[:, cols]], axis=0)
        for g in range(group):
            h = kvh * group + g
            hc = slice(h * HEAD_DIM, (h + 1) * HEAD_DIM)
            s = lax.dot_general(q_ref[:, hc], k, NT_DIMS, preferred_element_type=F32) * scale
            s = jnp.where(mask, s, -jnp.inf)
            sink = sink_ref[h]
            m = jnp.maximum(jnp.max(s, axis=-1, keepdims=True), sink)
            p = jnp.exp(s - m)
            denom = jnp.sum(p, axis=-1, keepdims=True) + jnp.exp(sink - m)
            o_ref[:, hc] = (_dot(p.astype(BF16), v) / denom).astype(o_ref.dtype)


def _swa_attention(qkv, q_off, k_off, v_off, sinks, n_q, n_kv, batch, seq, scale, cast_riders):
    w = SWA_WINDOW
    nb = seq // w
    grid = (batch, nb)
    qw, kw = n_q * HEAD_DIM, n_kv * HEAD_DIM
    assert q_off % qw == 0 and k_off % kw == 0 and v_off % kw == 0
    q_col, k_col, v_col = q_off // qw, k_off // kw, v_off // kw
    cur = lambda b, n: b * nb + n
    prev = lambda b, n: b * nb + jnp.maximum(n - 1, 0)
    riders = [_cast_rider(r, grid, cur) for r in cast_riders]
    outs = pl.pallas_call(
        functools.partial(_swa_kernel, scale=scale, n_q=n_q, n_kv=n_kv,
                          rider_slabs=tuple(r[3] for r in riders)), grid=grid,
        in_specs=[pl.BlockSpec(memory_space=pltpu.SMEM),
                  pl.BlockSpec((w, qw), lambda b, n: (cur(b, n), q_col)),
                  pl.BlockSpec((w, kw), lambda b, n: (prev(b, n), k_col)),
                  pl.BlockSpec((w, kw), lambda b, n: (cur(b, n), k_col)),
                  pl.BlockSpec((w, kw), lambda b, n: (prev(b, n), v_col)),
                  pl.BlockSpec((w, kw), lambda b, n: (cur(b, n), v_col))] + [r[1] for r in riders],
        out_specs=[pl.BlockSpec((w, qw), lambda b, n: (cur(b, n), 0))] + [r[1] for r in riders],
        out_shape=[jax.ShapeDtypeStruct((batch * seq, qw), BF16)] + [r[2] for r in riders],
        compiler_params=_cparams("arbitrary", "arbitrary"), name="swa_attn")(
            sinks.astype(F32), qkv, qkv, qkv, qkv, qkv, *[r[0] for r in riders])
    return outs[0], [o.reshape(r.shape) for o, r in zip(outs[1:], cast_riders)]


def _top2_route(x, w, n_experts):
    xh = x.astype(BF16)
    xl = (x - xh.astype(F32)).astype(BF16)
    wh = w.astype(BF16)
    wl = (w - wh.astype(F32)).astype(BF16)
    hi = _dot(xh, jnp.concatenate([wh, wl], axis=1))
    logits = hi[:, :LANE] + hi[:, LANE:] + _dot(xl, wh)
    lane = lax.broadcasted_iota(jnp.int32, logits.shape, 1)
    lane_f = lane.astype(F32)
    l1 = jnp.where(lane < n_experts, logits, -jnp.inf)
    v1 = jnp.max(l1, axis=-1, keepdims=True)
    i1 = jnp.min(jnp.where(l1 == v1, lane_f, float(LANE)), axis=-1, keepdims=True)
    l2 = jnp.where(lane_f == i1, -jnp.inf, l1)
    v2 = jnp.max(l2, axis=-1, keepdims=True)
    i2 = jnp.min(jnp.where(l2 == v2, lane_f, float(LANE)), axis=-1, keepdims=True)
    e2 = jnp.exp(v2 - v1)
    idx = jnp.where(lane == 0, i1, jnp.where(lane == 1, i2, 0.0)).astype(jnp.int32)
    gate = jnp.where(lane == 0, 1.0 / (1.0 + e2), jnp.where(lane == 1, e2 / (1.0 + e2), 0.0))
    return idx, gate


def _mm_res_ln_kernel(*refs, n_pairs, n_experts):
    a_refs, w_refs = refs[:n_pairs], refs[n_pairs:2 * n_pairs]
    if n_experts:
        res_ref, g_ref, b_ref, wr_ref, o_ref, idx_ref, gate_ref = refs[2 * n_pairs:]
    else:
        res_ref, g_ref, b_ref, o_ref = refs[2 * n_pairs:]
    tm = o_ref.shape[0]
    sub = _tile(tm, 256)
    for r in range(tm // sub):
        rows = slice(r * sub, (r + 1) * sub)
        acc = _dot(a_refs[0][rows, :], w_refs[0][...])
        for a_ref, w_ref in zip(a_refs[1:], w_refs[1:]):
            acc = acc + _dot(a_ref[rows, :], w_ref[...])
        y = _layer_norm(DEEPNORM_ALPHA * res_ref[rows, :] + acc, g_ref[...], b_ref[...])
        o_ref[rows, :] = y
        if n_experts:
            idx_ref[rows, :], gate_ref[rows, :] = _top2_route(y, wr_ref[...], n_experts)


def _matmul_residual_ln(pairs, res, g, b, name, w_router=None):
    m, d = res.shape
    tm = _tile(m, 512)
    row = lambda i: (i, 0)
    fixed = lambda i: (0, 0)
    resident = lambda shape: pl.BlockSpec(shape, fixed, pipeline_mode=pl.Buffered(1))
    in_specs = ([pl.BlockSpec((tm, a.shape[1]), row) for a, _ in pairs] + [resident(w.shape) for _, w in pairs]
                + [pl.BlockSpec((tm, d), row), pl.BlockSpec((1, d), fixed), pl.BlockSpec((1, d), fixed)])
    args = [a for a, _ in pairs] + [w for _, w in pairs] + [res, g.reshape(1, d), b.reshape(1, d)]
    out_specs = [pl.BlockSpec((tm, d), row)]
    out_shape = [jax.ShapeDtypeStruct((m, d), F32)]
    n_experts = 0
    if w_router is not None:
        n_experts = w_router.shape[1]
        in_specs.append(resident((d, LANE)))
        args.append(jnp.zeros((d, LANE), F32).at[:, :n_experts].set(w_router.astype(F32)))
        out_specs += [pl.BlockSpec((tm, LANE), row), pl.BlockSpec((tm, LANE), row)]
        out_shape += [jax.ShapeDtypeStruct((m, LANE), jnp.int32), jax.ShapeDtypeStruct((m, LANE), F32)]
    return pl.pallas_call(
        functools.partial(_mm_res_ln_kernel, n_pairs=len(pairs), n_experts=n_experts),
        grid=(m // tm,), in_specs=in_specs, out_specs=out_specs, out_shape=out_shape,
        compiler_params=_cparams("parallel"), name=name)(*args)


def _swiglu(x, wg, wu, wd):
    a = _silu(_dot(x, wg)) * _dot(x, wu)
    return _dot(a.astype(BF16), wd)


def _ffn_ln_kernel(h_ref, wg_ref, wu_ref, wd_ref, g_ref, b_ref, o_ref, x_ref):
    j = pl.program_id(1)

    @pl.when(j == 0)
    def _():
        x_ref[...] = h_ref[...].astype(BF16)
        o_ref[...] = jnp.zeros_like(o_ref)

    o_ref[...] += _swiglu(x_ref[...], wg_ref[...], wu_ref[...], wd_ref[...])

    @pl.when(j == pl.num_programs(1) - 1)
    def _():
        o_ref[...] = _layer_norm(DEEPNORM_ALPHA * h_ref[...] + o_ref[...], g_ref[...], b_ref[...])


def _swiglu_residual_ln(h, wg, wu, wd, g, b):
    m, d = h.shape
    f = wg.shape[1]
    tm, tf = _tile(m, 512), _tile(f, 512)
    row = lambda i, j: (i, 0)
    fixed = lambda i, j: (0, 0)
    return pl.pallas_call(
        _ffn_ln_kernel, grid=(m // tm, f // tf),
        in_specs=[pl.BlockSpec((tm, d), row),
                  pl.BlockSpec((d, tf), lambda i, j: (0, j)), pl.BlockSpec((d, tf), lambda i, j: (0, j)),
                  pl.BlockSpec((tf, d), lambda i, j: (j, 0)),
                  pl.BlockSpec((1, d), fixed), pl.BlockSpec((1, d), fixed)],
        out_specs=pl.BlockSpec((tm, d), row),
        out_shape=jax.ShapeDtypeStruct((m, d), F32),
        scratch_shapes=[pltpu.VMEM((tm, d), BF16)],
        compiler_params=_cparams("parallel", "arbitrary"), name="ffn_ln")(
            h, wg, wu, wd, g.reshape(1, d), b.reshape(1, d))


def _rms_norm(c, g):
    ms = jnp.mean(c * c, axis=-1, keepdims=True)
    return c * lax.rsqrt(ms + RMS_EPS) * g


def _rope64(pair, tab):
    w = pair * tab
    lane = lax.broadcasted_iota(jnp.int32, w.shape, 1)
    return jnp.where(lane < MLA_ROPE_DIM, w + pltpu.roll(w, MLA_ROPE_DIM, 1), 0.0)


def _mla_proj_kernel(x_ref, win_ref, gq_ref, gkv_ref, tab_ref, wq_ref, wkv_ref, q_ref, k_ref, v_ref, *,
                     q_rank, kv_rank, n_heads, q_scale):
    acc = _dot(x_ref[...].astype(BF16), win_ref[...])
    cq = _rms_norm(acc[:, :q_rank], gq_ref[...]).astype(BF16)
    ckv = _rms_norm(acc[:, q_rank:q_rank + kv_rank], gkv_ref[...]).astype(BF16)
    tab = tab_ref[...]
    k_pe = _rope64(acc[:, q_rank + kv_rank:], tab).astype(BF16)
    wkv_w = MLA_NOPE_DIM + MLA_V_DIM
    for h in range(n_heads):
        q0 = h * MLA_QK_PAD
        qh = _dot(cq, wq_ref[:, q0:q0 + MLA_QK_PAD]) * q_scale
        q_ref[:, q0:q0 + MLA_NOPE_DIM] = qh[:, :MLA_NOPE_DIM].astype(BF16)
        q_ref[:, q0 + MLA_NOPE_DIM:q0 + MLA_QK_PAD] = _rope64(qh[:, MLA_NOPE_DIM:], tab).astype(BF16)
        kvh = _dot(ckv, wkv_ref[:, h * wkv_w:(h + 1) * wkv_w])
        k_ref[:, q0:q0 + MLA_NOPE_DIM] = kvh[:, :MLA_NOPE_DIM].astype(BF16)
        k_ref[:, q0 + MLA_NOPE_DIM:q0 + MLA_QK_PAD] = k_pe
        v_ref[:, h * MLA_V_DIM:(h + 1) * MLA_V_DIM] = kvh[:, MLA_NOPE_DIM:].astype(BF16)


def _mla_projections(x, w_in, gq, gkv, tab, w_q, w_kv, q_rank, kv_rank, n_heads, q_scale):
    m, d = x.shape
    tm = _tile(m, 512)
    row = lambda i: (i, 0)
    fixed = lambda i: (0, 0)
    resident = lambda shape: pl.BlockSpec(shape, fixed, pipeline_mode=pl.Buffered(1))
    qk_w, v_w = n_heads * MLA_QK_PAD, n_heads * MLA_V_DIM
    return pl.pallas_call(
        functools.partial(_mla_proj_kernel, q_rank=q_rank, kv_rank=kv_rank, n_heads=n_heads, q_scale=q_scale),
        grid=(m // tm,),
        in_specs=[pl.BlockSpec((tm, d), row), resident(w_in.shape),
                  pl.BlockSpec((1, q_rank), fixed), pl.BlockSpec((1, kv_rank), fixed),
                  pl.BlockSpec((tm, LANE), row), resident(w_q.shape), resident(w_kv.shape)],
        out_specs=[pl.BlockSpec((tm, qk_w), row), pl.BlockSpec((tm, qk_w), row), pl.BlockSpec((tm, v_w), row)],
        out_shape=[jax.ShapeDtypeStruct((m, qk_w), BF16), jax.ShapeDtypeStruct((m, qk_w), BF16),
                   jax.ShapeDtypeStruct((m, v_w), BF16)],
        compiler_params=_cparams("parallel"), name="mla_proj")(
            x, w_in, gq.reshape(1, q_rank).astype(F32), gkv.reshape(1, kv_rank).astype(F32), tab, w_q, w_kv)


F32_SUBLANES = 8
MOE_TILE_PARTS = 4


def _row_copy(src_hbm, src_row, group_buf, sublane, sem):
    return pltpu.make_async_copy(src_hbm.at[pl.ds(src_row, 1), :], group_buf.at[pl.ds(sublane, 1), :], sem)


def _wait_group(src_hbm, group_buf, sem):
    pltpu.make_async_copy(src_hbm.at[pl.ds(0, group_buf.shape[0]), :], group_buf, sem).wait()


def _gather_rows_kernel(tok_ref, x_hbm, o_ref, buf, sem):
    i, n = pl.program_id(0), pl.num_programs(0)
    groups, sub, d = buf.shape[1:]
    rows = groups * sub

    def issue(step, slot):
        def start(g, _):
            for s in range(sub):
                _row_copy(x_hbm, tok_ref[step * rows + g * sub + s], buf.at[slot, g], s, sem.at[slot]).start()
            return 0
        lax.fori_loop(0, groups, start, 0)

    @pl.when(i == 0)
    def _():
        issue(0, 0)

    @pl.when(i + 1 < n)
    def _():
        issue(i + 1, (i + 1) % 2)

    slot = i % 2

    def wait(g, _):
        _wait_group(x_hbm, buf.at[slot, g], sem.at[slot])
        return 0
    lax.fori_loop(0, groups, wait, 0)
    o_ref[...] = buf[slot].reshape(rows, d).astype(o_ref.dtype)


def _gather_rows(h, row_token, rows_per_step):
    n_rows = row_token.shape[0]
    d = h.shape[1]
    return pl.pallas_call(
        _gather_rows_kernel,
        grid_spec=pltpu.PrefetchScalarGridSpec(
            num_scalar_prefetch=1, grid=(n_rows // rows_per_step,),
            in_specs=[pl.BlockSpec(memory_space=pl.ANY)],
            out_specs=pl.BlockSpec((rows_per_step, d), lambda i, tok: (i, 0)),
            scratch_shapes=[pltpu.VMEM((2, rows_per_step // F32_SUBLANES, F32_SUBLANES, d), F32),
                            pltpu.SemaphoreType.DMA((2,))]),
        out_shape=jax.ShapeDtypeStruct((n_rows, d), BF16),
        compiler_params=_cparams("arbitrary"), name="moe_gather")(row_token, h)


def _moe_ffn_kernel(te_ref, tr_ref, x_ref, wg_ref, wu_ref, wd_ref, o_ref):
    i, j = pl.program_id(0), pl.program_id(1)
    part = x_ref.shape[0] // MOE_TILE_PARTS
    parts_used = (tr_ref[i] + part - 1) // part

    @pl.when(j == 0)
    def _():
        o_ref[...] = jnp.zeros_like(o_ref)

    for used in range(1, MOE_TILE_PARTS + 1):
        @pl.when(parts_used == used)
        def _():
            rows = slice(0, used * part)
            o_ref[rows, :] += _swiglu(x_ref[rows, :], wg_ref[0], wu_ref[0], wd_ref[0].astype(BF16))


def _moe_ffn(xg, wg, wu, wd, tile_expert, tile_rows, tm):
    n_rows, d = xg.shape
    f = wg.shape[2]
    tf = _tile(f, 1024)
    nj = f // tf
    col = lambda i, j, te, tr: jnp.where(tr[i] > 0, j, nj - 1)
    return pl.pallas_call(
        _moe_ffn_kernel,
        grid_spec=pltpu.PrefetchScalarGridSpec(
            num_scalar_prefetch=2, grid=(n_rows // tm, nj),
            in_specs=[pl.BlockSpec((tm, d), lambda i, j, te, tr: (i, 0)),
                      pl.BlockSpec((1, d, tf), lambda i, j, te, tr: (te[i], 0, col(i, j, te, tr))),
                      pl.BlockSpec((1, d, tf), lambda i, j, te, tr: (te[i], 0, col(i, j, te, tr))),
                      pl.BlockSpec((1, tf, d), lambda i, j, te, tr: (te[i], col(i, j, te, tr), 0))],
            out_specs=pl.BlockSpec((tm, d), lambda i, j, te, tr: (i, 0))),
        out_shape=jax.ShapeDtypeStruct((n_rows, d), F32),
        compiler_params=_cparams("arbitrary", "arbitrary"), name="moe_ffn")(
            tile_expert, tile_rows, xg, wg, wu, wd)


def _combine_ln_kernel(pos_ref, h_ref, gate_ref, g_ref, b_ref, y_hbm, o_ref, buf, sem):
    groups, sub, d = buf.shape[2:]
    rows = groups * sub
    i, n = pl.program_id(0), pl.num_programs(0)

    def issue(step, slot):
        def start(g, _):
            for s in range(sub):
                for k in range(TOP_K):
                    src = pos_ref[TOP_K * (step * rows + g * sub + s) + k]
                    _row_copy(y_hbm, src, buf.at[slot, k, g], s, sem.at[slot, k]).start()
            return 0
        lax.fori_loop(0, groups, start, 0)

    @pl.when(i == 0)
    def _():
        issue(0, 0)

    @pl.when(i + 1 < n)
    def _():
        issue(i + 1, (i + 1) % 2)

    slot = i % 2

    def wait(g, _):
        for k in range(TOP_K):
            _wait_group(y_hbm, buf.at[slot, k, g], sem.at[slot, k])
        return 0
    lax.fori_loop(0, groups, wait, 0)
    gate = gate_ref[...]
    y = None
    for k in range(TOP_K):
        term = buf[slot, k].reshape(rows, d) * gate[:, k:k + 1]
        y = term if y is None else y + term
    o_ref[...] = _layer_norm(DEEPNORM_ALPHA * h_ref[...] + y, g_ref[...], b_ref[...])


def _moe_combine_ln(y_rows, pos, h, gates, g, b):
    m, d = h.shape
    tm = _tile(m, 512)
    row = lambda i, p: (i, 0)
    fixed = lambda i, p: (0, 0)
    return pl.pallas_call(
        _combine_ln_kernel,
        grid_spec=pltpu.PrefetchScalarGridSpec(
            num_scalar_prefetch=1, grid=(m // tm,),
            in_specs=[pl.BlockSpec((tm, d), row), pl.BlockSpec((tm, LANE), row),
                      pl.BlockSpec((1, d), fixed), pl.BlockSpec((1, d), fixed),
                      pl.BlockSpec(memory_space=pl.ANY)],
            out_specs=pl.BlockSpec((tm, d), row),
            scratch_shapes=[pltpu.VMEM((2, TOP_K, tm // F32_SUBLANES, F32_SUBLANES, d), F32),
                            pltpu.SemaphoreType.DMA((2, TOP_K))]),
        out_shape=jax.ShapeDtypeStruct((m, d), F32),
        compiler_params=_cparams("arbitrary"), name="moe_combine_ln")(
            pos, h, gates, g.reshape(1, d), b.reshape(1, d), y_rows)


def _moe_routing(top_idx, n_experts, tm):
    n_assign = top_idx.size
    expert_of = top_idx.reshape(n_assign)
    onehot = (expert_of[:, None] == jnp.arange(n_experts, dtype=jnp.int32)[None, :]).astype(jnp.int32)
    csum = jnp.cumsum(onehot, axis=0)
    rank = jnp.sum(onehot * (csum - 1), axis=1)
    counts = csum[-1]
    padded = ((counts + tm - 1) // tm) * tm
    padded_end = jnp.cumsum(padded)
    padded_start = padded_end - padded
    pos = (padded_start[expert_of] + rank).astype(jnp.int32)
    n_tiles = -(-n_assign // tm) + n_experts
    row_token = jnp.zeros((n_tiles * tm,), jnp.int32).at[pos].set(
        jnp.arange(n_assign, dtype=jnp.int32) // TOP_K, unique_indices=True, mode="promise_in_bounds")
    tile_start = jnp.arange(n_tiles, dtype=jnp.int32) * tm
    tile_used = tile_start < padded_end[-1]
    tile_expert = jnp.minimum(jnp.searchsorted(padded_end, tile_start, side='right'), n_experts - 1)
    tile_rows = jnp.clip(counts[tile_expert] - (tile_start - padded_start[tile_expert]), 0, tm)
    tile_rows = jnp.where(tile_used, tile_rows, 0).astype(jnp.int32)
    last_expert = jnp.max(jnp.where(tile_used, tile_expert, 0))
    tile_expert = jnp.where(tile_used, tile_expert, last_expert).astype(jnp.int32)
    return pos, row_token, tile_expert, tile_rows


def _rotate_half_cols(w):
    half = w.shape[-1] // 2
    return jnp.concatenate([-w[..., half:], w[..., :half]], axis=-1)


def kernel(x, positions, w_in0, b_forget, sinks_b, w_out0, ln0_mix_g, ln0_mix_b, w_ffn_gate, w_ffn_up, w_ffn_down, ln0_ffn_g, ln0_ffn_b, w_in1, q_norm_g, w_uq, kv_norm_g, w_ukv, w_out1, ln1_mix_g, ln1_mix_b, w_router, w_moe_gate, w_moe_up, w_moe_down, ln1_ffn_g, ln1_ffn_b):
    batch, seq, d = x.shape
    n = batch * seq
    fox_heads = b_forget.shape[0]
    fox_w = fox_heads * HEAD_DIM
    swa_q = sinks_b.shape[0]
    swa_qw = swa_q * HEAD_DIM
    swa_kvw = (w_in0.shape[1] - 3 * fox_w - fox_heads - swa_qw) // 2
    swa_kv = swa_kvw // HEAD_DIM
    q_rank, kv_rank = q_norm_g.shape[0], kv_norm_g.shape[0]
    mla_heads = w_uq.shape[1] // (MLA_NOPE_DIM + MLA_ROPE_DIM)
    n_experts = w_router.shape[1]

    xf = x.reshape(n, d)
    cos_a, sin_a, tab_m = _rope_tables(positions)

    c0 = 3 * fox_w
    c1 = c0 + fox_heads
    w_qkv = jnp.concatenate([w_in0[:, :c0], w_in0[:, c1:]], axis=1).astype(BF16)
    group_kinds = ([SCALED] * fox_heads + [PLAIN] * (2 * fox_heads)
                   + [ROTARY] * (swa_q + swa_kv) + [PLAIN] * swa_kv)
    qkv = _project(xf, w_qkv, group_kinds, HEAD_DIM ** -0.5 * LOG2_E, cos_a, sin_a, 1024, 768, "proj0")
    cum_log_f = _fox_cum_log_forget(xf, w_in0[:, c0:c1], b_forget, batch, seq)
    o_a, w_moe_up_bf = _causal_attention(qkv, qkv, qkv, 0, fox_heads, 2 * fox_heads, HEAD_DIM, HEAD_DIM,
                                         fox_heads, batch, seq, ck=cum_log_f, cast_rider=w_moe_up,
                                         name="fox_attn")
    o_b, ffn_w_bf = _swa_attention(qkv, c0, c0 + swa_qw, c0 + swa_qw + swa_kvw, sinks_b, swa_q, swa_kv,
                                   batch, seq, HEAD_DIM ** -0.5,
                                   cast_riders=[w_ffn_gate, w_ffn_up, w_ffn_down])
    w_out0_bf = w_out0.astype(BF16)
    (h1,) = _matmul_residual_ln([(o_a, w_out0_bf[:fox_w]), (o_b, w_out0_bf[fox_w:])], xf,
                                ln0_mix_g, ln0_mix_b, "out0_ln")
    h2 = _swiglu_residual_ln(h1, *ffn_w_bf, ln0_ffn_g, ln0_ffn_b)

    r0 = q_rank + kv_rank
    w_in1_x = jnp.concatenate([w_in1, _rotate_half_cols(w_in1[:, r0:])], axis=1).astype(BF16)
    w_uq_h = w_uq.reshape(q_rank, mla_heads, MLA_NOPE_DIM + MLA_ROPE_DIM)
    w_uq_x = jnp.concatenate([w_uq_h, _rotate_half_cols(w_uq_h[..., MLA_NOPE_DIM:])], axis=-1)
    q_m, k_m, v_m = _mla_projections(
        h2, w_in1_x, q_norm_g, kv_norm_g, tab_m,
        w_uq_x.reshape(q_rank, mla_heads * MLA_QK_PAD).astype(BF16), w_ukv.astype(BF16),
        q_rank, kv_rank, mla_heads, (MLA_NOPE_DIM + MLA_ROPE_DIM) ** -0.5 * LOG2_E)
    o_m, w_moe_gate_bf = _causal_attention(q_m, k_m, v_m, 0, 0, 0, MLA_QK_PAD, MLA_V_DIM, mla_heads, batch,
                                           seq, cast_rider=w_moe_gate, name="mla_attn")
    h3, idx, gates = _matmul_residual_ln([(o_m, w_out1.astype(BF16))], h2, ln1_mix_g, ln1_mix_b,
                                         "out1_ln_route", w_router=w_router)

    moe_tm = _tile(n * TOP_K, 512)
    pos, row_token, tile_expert, tile_rows = _moe_routing(idx[:, :TOP_K], n_experts, moe_tm)
    xg = _gather_rows(h3, row_token, moe_tm)
    y_rows = _moe_ffn(xg, w_moe_gate_bf, w_moe_up_bf, w_moe_down, tile_expert, tile_rows, moe_tm)
    out = _moe_combine_ln(y_rows, pos, h3, gates, ln1_ffn_g, ln1_ffn_b)
    return out.reshape(batch, seq, d)
```

```python
import functools
import math

import jax
import jax.numpy as jnp
from jax import lax
from jax.experimental import pallas as pl
from jax.experimental.pallas import tpu as pltpu

F32 = jnp.float32
BF16 = jnp.bfloat16

LANE = 128
HEAD_DIM = 128
SWA_WINDOW = 128
MLA_NOPE_DIM = 128
MLA_ROPE_DIM = 64
MLA_V_DIM = 128
MLA_QK_PAD = 256
ROPE_THETA = 10000.0
TOP_K = 2
LN_EPS = 1e-5
RMS_EPS = 1e-6
DEPTH = 2
DEEPNORM_ALPHA = (2 * DEPTH) ** 0.25
LOG2_E = math.log2(math.e)

NT_DIMS = (((1,), (1,)), ((), ()))


def _tile(n, pref):
    if n <= pref:
        return n
    t = (pref // LANE) * LANE
    while n % t:
        t -= LANE
    return t


def _cparams(*sem):
    return pltpu.CompilerParams(dimension_semantics=sem)


def _dot(a, b):
    return jnp.dot(a, b, preferred_element_type=F32)


def _layer_norm(y, g, b):
    mu = jnp.mean(y, axis=-1, keepdims=True)
    d = y - mu
    var = jnp.mean(d * d, axis=-1, keepdims=True)
    return d * lax.rsqrt(var + LN_EPS) * g + b


def _silu(g):
    return g / (1.0 + jnp.exp(-g))


def _rope_table_kernel(pos_ref, freq_ref, cos_a_ref, sin_a_ref, tab_m_ref):
    pos = pos_ref[...]
    ang_a = pos * freq_ref[0:1, :]
    lane = lax.broadcasted_iota(jnp.int32, ang_a.shape, 1)
    cos_a_ref[...] = jnp.cos(ang_a)
    sin_a_ref[...] = jnp.where(lane < HEAD_DIM // 2, -jnp.sin(ang_a), jnp.sin(ang_a))
    ang_m = pos * freq_ref[1:2, :]
    tab_m_ref[...] = jnp.where(lane < MLA_ROPE_DIM, jnp.cos(ang_m), jnp.sin(ang_m))


def _rope_tables(positions):
    n = positions.size
    pos = positions.astype(F32).reshape(n, 1)
    half_a = HEAD_DIM // 2
    half_m = MLA_ROPE_DIM // 2
    inv_a = ROPE_THETA ** (-2.0 * jnp.arange(half_a, dtype=F32) / HEAD_DIM)
    inv_m = ROPE_THETA ** (-2.0 * jnp.arange(half_m, dtype=F32) / MLA_ROPE_DIM)
    freq = jnp.stack([jnp.tile(inv_a, 2), jnp.tile(inv_m, 4)])
    tm = _tile(n, 1024)
    out = jax.ShapeDtypeStruct((n, LANE), F32)
    row = pl.BlockSpec((tm, LANE), lambda i: (i, 0))
    return pl.pallas_call(
        _rope_table_kernel, grid=(n // tm,),
        in_specs=[pl.BlockSpec((tm, 1), lambda i: (i, 0)), pl.BlockSpec((2, LANE), lambda i: (0, 0))],
        out_specs=[row, row, row], out_shape=[out, out, out],
        compiler_params=_cparams("parallel"), name="rope_tables")(pos, freq)


PLAIN, SCALED, ROTARY = "plain", "scaled", "rotary"


def _proj_kernel(x_ref, w_ref, cos_ref, sin_ref, o_ref, xb_ref, *, tile_kinds, scale):
    j = pl.program_id(1)

    @pl.when(j == 0)
    def _():
        xb_ref[...] = x_ref[...].astype(BF16)

    def tile(kinds):
        acc = _dot(xb_ref[...], w_ref[...])
        for c, kind in enumerate(kinds):
            cols = slice(c * HEAD_DIM, (c + 1) * HEAD_DIM)
            a = acc[:, cols]
            if kind == SCALED:
                a = a * scale
            elif kind == ROTARY:
                a = a * cos_ref[...] + pltpu.roll(a, HEAD_DIM // 2, 1) * sin_ref[...]
            o_ref[:, cols] = a.astype(o_ref.dtype)

    for kinds in sorted(set(tile_kinds)):
        first = tile_kinds.index(kinds)
        count = tile_kinds.count(kinds)
        assert tile_kinds[first:first + count] == (kinds,) * count
        pl.when((j >= first) & (j < first + count))(functools.partial(tile, kinds))


def _project(x, w, group_kinds, scale, cos, sin, tm, tn, name):
    m, k = x.shape
    n = w.shape[1]
    tm, tn = _tile(m, tm), _tile(n, tn)
    per_tile = tn // HEAD_DIM
    tile_kinds = tuple(tuple(group_kinds[t * per_tile:(t + 1) * per_tile]) for t in range(n // tn))
    return pl.pallas_call(
        functools.partial(_proj_kernel, tile_kinds=tile_kinds, scale=scale), grid=(m // tm, n // tn),
        in_specs=[pl.BlockSpec((tm, k), lambda i, j: (i, 0)), pl.BlockSpec((k, tn), lambda i, j: (0, j)),
                  pl.BlockSpec((tm, LANE), lambda i, j: (i, 0)), pl.BlockSpec((tm, LANE), lambda i, j: (i, 0))],
        out_specs=pl.BlockSpec((tm, tn), lambda i, j: (i, j)),
        out_shape=jax.ShapeDtypeStruct((m, n), BF16),
        scratch_shapes=[pltpu.VMEM((tm, k), BF16)],
        compiler_params=_cparams("parallel", "arbitrary"), name=name)(x, w, cos, sin)


def _split3(x):
    hi = x.astype(BF16)
    r = x - hi.astype(F32)
    mid = r.astype(BF16)
    lo = (r - mid.astype(F32)).astype(BF16)
    return hi, mid, lo


def _fox_gate_kernel(x_ref, wf_ref, bf_ref, o_ref, carry_ref, *, n_heads):
    @pl.when(pl.program_id(1) == 0)
    def _():
        carry_ref[...] = jnp.zeros_like(carry_ref)

    f = lax.dot_general(wf_ref[...], x_ref[...].astype(BF16), NT_DIMS, preferred_element_type=F32)
    z = f + bf_ref[...]
    log_f = jnp.minimum(z, 0.0) - jnp.log1p(jnp.exp(-jnp.abs(z)))
    tc = z.shape[1]
    src = lax.broadcasted_iota(jnp.int32, (tc, tc), 0)
    dst = lax.broadcasted_iota(jnp.int32, (tc, tc), 1)
    tri = jnp.where(src <= dst, 1.0, 0.0).astype(BF16)
    hi, mid, lo = _split3(log_f)
    cum = _dot(hi, tri) + _dot(mid, tri) + _dot(lo, tri) + carry_ref[:, 0:1]
    o_ref[0] = cum[:n_heads] * LOG2_E
    carry_ref[...] = jnp.broadcast_to(cum[:, tc - 1:tc], carry_ref.shape)


def _fox_cum_log_forget(x, w_f, b_forget, batch, seq):
    n, d = x.shape
    n_heads = w_f.shape[1]
    rows = 16
    wf_t = jnp.zeros((rows, d), BF16).at[:n_heads].set(w_f.T.astype(BF16))
    bf = jnp.zeros((rows, 1), F32).at[:n_heads, 0].set(b_forget.astype(F32))
    tc = _tile(seq, 512)
    nc = seq // tc
    return pl.pallas_call(
        functools.partial(_fox_gate_kernel, n_heads=n_heads), grid=(batch, nc),
        in_specs=[pl.BlockSpec((tc, d), lambda b, s: (b * nc + s, 0)),
                  pl.BlockSpec((rows, d), lambda b, s: (0, 0)),
                  pl.BlockSpec((rows, 1), lambda b, s: (0, 0))],
        out_specs=pl.BlockSpec((1, n_heads, tc), lambda b, s: (b, 0, s)),
        out_shape=jax.ShapeDtypeStruct((batch, n_heads, seq), F32),
        scratch_shapes=[pltpu.VMEM((rows, LANE), F32)],
        compiler_params=_cparams("parallel", "arbitrary"), name="fox_gate")(x, wf_t, bf)


def _flash_kernel(*refs, tk, dk, dv, heads, has_bias, n_cast_slabs):
    refs = list(refs)
    q_ref, k_ref, v_ref = refs[:3]
    del refs[:3]
    ck_ref = refs.pop(0) if has_bias else None
    w_ref = refs.pop(0) if n_cast_slabs else None
    o_ref = refs.pop(0)
    qi = pl.program_id(2)
    tq = q_ref.shape[0]
    qs = [q_ref[:, g * dk:(g + 1) * dk] for g in range(heads)]

    if n_cast_slabs:
        step = (pl.program_id(0) * pl.num_programs(1) + pl.program_id(1)) * pl.num_programs(2) + qi
        _cast_rider_step(step, n_cast_slabs, w_ref, refs[0])

    def step(g, state, j, masked):
        m, l, acc = state
        kv_rows = pl.ds(pl.multiple_of(j * tk, tk), tk)
        s = lax.dot_general(qs[g], k_ref[kv_rows, g * dk:(g + 1) * dk], NT_DIMS, preferred_element_type=F32)
        if has_bias:
            s = s - ck_ref[0, g, j]
        if masked:
            row = lax.broadcasted_iota(jnp.int32, s.shape, 0)
            col = lax.broadcasted_iota(jnp.int32, s.shape, 1)
            s = jnp.where(row >= col, s, -jnp.inf)
        m_new = jnp.maximum(m, jnp.max(s, axis=-1, keepdims=True))
        a = jnp.exp2(m - m_new)
        p = jnp.exp2(s - m_new)
        l = a * l + jnp.sum(p, axis=-1, keepdims=True)
        acc = a * acc + _dot(p.astype(BF16), v_ref[kv_rows, g * dv:(g + 1) * dv])
        return m_new, l, acc

    def body(j, carry):
        return tuple(step(g, carry[g], j, False) for g in range(heads))

    init = (jnp.full((tq, 1), -jnp.inf, F32), jnp.zeros((tq, 1), F32), jnp.zeros((tq, dv), F32))
    carry = lax.fori_loop(0, qi, body, (init,) * heads)
    for g in range(heads):
        _, l, acc = step(g, carry[g], qi, True)
        o_ref[:, g * dv:(g + 1) * dv] = (acc / l).astype(o_ref.dtype)


BF16_SUBLANES = 16


def _slab_rows(total_rows, max_slabs):
    rows = BF16_SUBLANES
    while total_rows % rows or total_rows // rows > max_slabs:
        rows += BF16_SUBLANES
    return rows


def _cast_rider(w, grid, step_of):
    flat = w.reshape(-1, w.shape[-1])
    rows = _slab_rows(flat.shape[0], math.prod(grid))
    n_slabs = flat.shape[0] // rows
    spec = pl.BlockSpec((rows, flat.shape[1]), lambda *g: (jnp.minimum(step_of(*g), n_slabs - 1), 0))
    return flat, spec, jax.ShapeDtypeStruct(flat.shape, BF16), n_slabs


def _cast_rider_step(step, n_slabs, w_ref, o_ref):
    @pl.when(step < n_slabs)
    def _():
        o_ref[...] = w_ref[...].astype(o_ref.dtype)


def _causal_attention(q_arr, k_arr, v_arr, q_col, k_col, v_col, dk, dv, n_heads, batch, seq,
                      ck=None, cast_rider=None, name="attn"):
    heads = 4
    assert n_heads % heads == 0 and q_col % heads == 0 and k_col % heads == 0 and v_col % heads == 0
    t = _tile(seq, 512)
    nq = seq // t
    grid = (batch, n_heads // heads, nq)
    in_specs = [pl.BlockSpec((t, heads * dk), lambda b, h, i: (b * nq + i, q_col // heads + h)),
                pl.BlockSpec((seq, heads * dk), lambda b, h, i: (b, k_col // heads + h)),
                pl.BlockSpec((seq, heads * dv), lambda b, h, i: (b, v_col // heads + h))]
    args = [q_arr, k_arr, v_arr]
    out_specs = [pl.BlockSpec((t, heads * dv), lambda b, h, i: (b * nq + i, h))]
    out_shape = [jax.ShapeDtypeStruct((batch * seq, n_heads * dv), BF16)]
    if ck is not None:
        args.append(ck.reshape(batch, n_heads, nq, 1, t))
        in_specs.append(pl.BlockSpec((1, heads, nq, 1, t), lambda b, h, i: (b, h, 0, 0, 0)))
    n_slabs = 0
    if cast_rider is not None:
        flat, slab, flat_bf, n_slabs = _cast_rider(cast_rider, grid, lambda b, h, i: (b * grid[1] + h) * nq + i)
        args.append(flat)
        in_specs.append(slab)
        out_specs.append(slab)
        out_shape.append(flat_bf)
    outs = pl.pallas_call(
        functools.partial(_flash_kernel, tk=t, dk=dk, dv=dv, heads=heads, has_bias=ck is not None,
                          n_cast_slabs=n_slabs),
        grid=grid, in_specs=in_specs, out_specs=out_specs, out_shape=out_shape,
        compiler_params=_cparams("parallel", "parallel", "arbitrary"), name=name)(*args)
    if cast_rider is None:
        return outs[0]
    return outs[0], outs[1].reshape(cast_rider.shape)


def _swa_kernel(sink_ref, q_ref, kp_ref, kc_ref, vp_ref, vc_ref, *rest, scale, n_q, n_kv, rider_slabs):
    n_riders = len(rider_slabs)
    o_ref = rest[n_riders]
    step = pl.program_id(0) * pl.num_programs(1) + pl.program_id(1)
    for n_slabs, w_ref, wo_ref in zip(rider_slabs, rest[:n_riders], rest[n_riders + 1:]):
        _cast_rider_step(step, n_slabs, w_ref, wo_ref)
    w = SWA_WINDOW
    qi = lax.broadcasted_iota(jnp.int32, (w, 2 * w), 0)
    ji = lax.broadcasted_iota(jnp.int32, (w, 2 * w), 1)
    has_prev = pl.program_id(1) > 0
    mask = (ji > qi) & (ji <= qi + w) & ((ji >= w) | has_prev)
    group = n_q // n_kv
    for kvh in range(n_kv):
        cols = slice(kvh * HEAD_DIM, (kvh + 1) * HEAD_DIM)
        k = jnp.concatenate([kp_ref[:, cols], kc_ref[:, cols]], axis=0)
        v = jnp.concatenate([vp_ref[:, cols], vc_ref[:, cols]], axis=0)
        for g in range(group):
            h = kvh * group + g
            hc = slice(h * HEAD_DIM, (h + 1) * HEAD_DIM)
            s = lax.dot_general(q_ref[:, hc], k, NT_DIMS, preferred_element_type=F32) * scale
            s = jnp.where(mask, s, -jnp.inf)
            sink = sink_ref[h]
            m = jnp.maximum(jnp.max(s, axis=-1, keepdims=True), sink)
            p = jnp.exp(s - m)
            denom = jnp.sum(p, axis=-1, keepdims=True) + jnp.exp(sink - m)
            o_ref[:, hc] = (_dot(p.astype(BF16), v) / denom).astype(o_ref.dtype)


def _swa_attention(qkv, q_off, k_off, v_off, sinks, n_q, n_kv, batch, seq, scale, cast_riders):
    w = SWA_WINDOW
    nb = seq // w
    grid = (batch, nb)
    qw, kw = n_q * HEAD_DIM, n_kv * HEAD_DIM
    assert q_off % qw == 0 and k_off % kw == 0 and v_off % kw == 0
    q_col, k_col, v_col = q_off // qw, k_off // kw, v_off // kw
    cur = lambda b, n: b * nb + n
    prev = lambda b, n: b * nb + jnp.maximum(n - 1, 0)
    riders = [_cast_rider(r, grid, cur) for r in cast_riders]
    outs = pl.pallas_call(
        functools.partial(_swa_kernel, scale=scale, n_q=n_q, n_kv=n_kv,
                          rider_slabs=tuple(r[3] for r in riders)), grid=grid,
        in_specs=[pl.BlockSpec(memory_space=pltpu.SMEM),
                  pl.BlockSpec((w, qw), lambda b, n: (cur(b, n), q_col)),
                  pl.BlockSpec((w, kw), lambda b, n: (prev(b, n), k_col)),
                  pl.BlockSpec((w, kw), lambda b, n: (cur(b, n), k_col)),
                  pl.BlockSpec((w, kw), lambda b, n: (prev(b, n), v_col)),
                  pl.BlockSpec((w, kw), lambda b, n: (cur(b, n), v_col))] + [r[1] for r in riders],
        out_specs=[pl.BlockSpec((w, qw), lambda b, n: (cur(b, n), 0))] + [r[1] for r in riders],
        out_shape=[jax.ShapeDtypeStruct((batch * seq, qw), BF16)] + [r[2] for r in riders],
        compiler_params=_cparams("arbitrary", "arbitrary"), name="swa_attn")(
            sinks.astype(F32), qkv, qkv, qkv, qkv, qkv, *[r[0] for r in riders])
    return outs[0], [o.reshape(r.shape) for o, r in zip(outs[1:], cast_riders)]


def _top2_route(x, w, n_experts):
    xh = x.astype(BF16)
    xl = (x - xh.astype(F32)).astype(BF16)
    wh = w.astype(BF16)
    wl = (w - wh.astype(F32)).astype(BF16)
    hi = _dot(xh, jnp.concatenate([wh, wl], axis=1))
    logits = hi[:, :LANE] + hi[:, LANE:] + _dot(xl, wh)
    lane = lax.broadcasted_iota(jnp.int32, logits.shape, 1)
    lane_f = lane.astype(F32)
    l1 = jnp.where(lane < n_experts, logits, -jnp.inf)
    v1 = jnp.max(l1, axis=-1, keepdims=True)
    i1 = jnp.min(jnp.where(l1 == v1, lane_f, float(LANE)), axis=-1, keepdims=True)
    l2 = jnp.where(lane_f == i1, -jnp.inf, l1)
    v2 = jnp.max(l2, axis=-1, keepdims=True)
    i2 = jnp.min(jnp.where(l2 == v2, lane_f, float(LANE)), axis=-1, keepdims=True)
    e2 = jnp.exp(v2 - v1)
    idx = jnp.where(lane == 0, i1, jnp.where(lane == 1, i2, 0.0)).astype(jnp.int32)
    gate = jnp.where(lane == 0, 1.0 / (1.0 + e2), jnp.where(lane == 1, e2 / (1.0 + e2), 0.0))
    return idx, gate


def _mm_res_ln_kernel(*refs, n_pairs, n_experts):
    a_refs, w_refs = refs[:n_pairs], refs[n_pairs:2 * n_pairs]
    if n_experts:
        res_ref, g_ref, b_ref, wr_ref, o_ref, idx_ref, gate_ref = refs[2 * n_pairs:]
    else:
        res_ref, g_ref, b_ref, o_ref = refs[2 * n_pairs:]
    tm = o_ref.shape[0]
    sub = _tile(tm, 256)
    for r in range(tm // sub):
        rows = slice(r * sub, (r + 1) * sub)
        acc = _dot(a_refs[0][rows, :], w_refs[0][...])
        for a_ref, w_ref in zip(a_refs[1:], w_refs[1:]):
            acc = acc + _dot(a_ref[rows, :], w_ref[...])
        y = _layer_norm(DEEPNORM_ALPHA * res_ref[rows, :] + acc, g_ref[...], b_ref[...])
        o_ref[rows, :] = y
        if n_experts:
            idx_ref[rows, :], gate_ref[rows, :] = _top2_route(y, wr_ref[...], n_experts)


def _matmul_residual_ln(pairs, res, g, b, name, w_router=None):
    m, d = res.shape
    tm = _tile(m, 512)
    row = lambda i: (i, 0)
    fixed = lambda i: (0, 0)
    resident = lambda shape: pl.BlockSpec(shape, fixed, pipeline_mode=pl.Buffered(1))
    in_specs = ([pl.BlockSpec((tm, a.shape[1]), row) for a, _ in pairs] + [resident(w.shape) for _, w in pairs]
                + [pl.BlockSpec((tm, d), row), pl.BlockSpec((1, d), fixed), pl.BlockSpec((1, d), fixed)])
    args = [a for a, _ in pairs] + [w for _, w in pairs] + [res, g.reshape(1, d), b.reshape(1, d)]
    out_specs = [pl.BlockSpec((tm, d), row)]
    out_shape = [jax.ShapeDtypeStruct((m, d), F32)]
    n_experts = 0
    if w_router is not None:
        n_experts = w_router.shape[1]
        in_specs.append(resident((d, LANE)))
        args.append(jnp.zeros((d, LANE), F32).at[:, :n_experts].set(w_router.astype(F32)))
        out_specs += [pl.BlockSpec((tm, LANE), row), pl.BlockSpec((tm, LANE), row)]
        out_shape += [jax.ShapeDtypeStruct((m, LANE), jnp.int32), jax.ShapeDtypeStruct((m, LANE), F32)]
    return pl.pallas_call(
        functools.partial(_mm_res_ln_kernel, n_pairs=len(pairs), n_experts=n_experts),
        grid=(m // tm,), in_specs=in_specs, out_specs=out_specs, out_shape=out_shape,
        compiler_params=_cparams("parallel"), name=name)(*args)


def _swiglu(x, wg, wu, wd):
    a = _silu(_dot(x, wg)) * _dot(x, wu)
    return _dot(a.astype(BF16), wd)


def _ffn_ln_kernel(h_ref, wg_ref, wu_ref, wd_ref, g_ref, b_ref, o_ref, x_ref):
    j = pl.program_id(1)

    @pl.when(j == 0)
    def _():
        x_ref[...] = h_ref[...].astype(BF16)
        o_ref[...] = jnp.zeros_like(o_ref)

    o_ref[...] += _swiglu(x_ref[...], wg_ref[...], wu_ref[...], wd_ref[...])

    @pl.when(j == pl.num_programs(1) - 1)
    def _():
        o_ref[...] = _layer_norm(DEEPNORM_ALPHA * h_ref[...] + o_ref[...], g_ref[...], b_ref[...])


def _swiglu_residual_ln(h, wg, wu, wd, g, b):
    m, d = h.shape
    f = wg.shape[1]
    tm, tf = _tile(m, 512), _tile(f, 512)
    row = lambda i, j: (i, 0)
    fixed = lambda i, j: (0, 0)
    return pl.pallas_call(
        _ffn_ln_kernel, grid=(m // tm, f // tf),
        in_specs=[pl.BlockSpec((tm, d), row),
                  pl.BlockSpec((d, tf), lambda i, j: (0, j)), pl.BlockSpec((d, tf), lambda i, j: (0, j)),
                  pl.BlockSpec((tf, d), lambda i, j: (j, 0)),
                  pl.BlockSpec((1, d), fixed), pl.BlockSpec((1, d), fixed)],
        out_specs=pl.BlockSpec((tm, d), row),
        out_shape=jax.ShapeDtypeStruct((m, d), F32),
        scratch_shapes=[pltpu.VMEM((tm, d), BF16)],
        compiler_params=_cparams("parallel", "arbitrary"), name="ffn_ln")(
            h, wg, wu, wd, g.reshape(1, d), b.reshape(1, d))


def _rms_norm(c, g):
    ms = jnp.mean(c * c, axis=-1, keepdims=True)
    return c * lax.rsqrt(ms + RMS_EPS) * g


def _rope64(pair, tab):
    w = pair * tab
    lane = lax.broadcasted_iota(jnp.int32, w.shape, 1)
    return jnp.where(lane < MLA_ROPE_DIM, w + pltpu.roll(w, MLA_ROPE_DIM, 1), 0.0)


def _mla_proj_kernel(x_ref, win_ref, gq_ref, gkv_ref, tab_ref, wq_ref, wkv_ref, q_ref, k_ref, v_ref, *,
                     q_rank, kv_rank, n_heads, q_scale):
    acc = _dot(x_ref[...].astype(BF16), win_ref[...])
    cq = _rms_norm(acc[:, :q_rank], gq_ref[...]).astype(BF16)
    ckv = _rms_norm(acc[:, q_rank:q_rank + kv_rank], gkv_ref[...]).astype(BF16)
    tab = tab_ref[...]
    k_pe = _rope64(acc[:, q_rank + kv_rank:], tab).astype(BF16)
    wkv_w = MLA_NOPE_DIM + MLA_V_DIM
    for h in range(n_heads):
        q0 = h * MLA_QK_PAD
        qh = _dot(cq, wq_ref[:, q0:q0 + MLA_QK_PAD]) * q_scale
        q_ref[:, q0:q0 + MLA_NOPE_DIM] = qh[:, :MLA_NOPE_DIM].astype(BF16)
        q_ref[:, q0 + MLA_NOPE_DIM:q0 + MLA_QK_PAD] = _rope64(qh[:, MLA_NOPE_DIM:], tab).astype(BF16)
        kvh = _dot(ckv, wkv_ref[:, h * wkv_w:(h + 1) * wkv_w])
        k_ref[:, q0:q0 + MLA_NOPE_DIM] = kvh[:, :MLA_NOPE_DIM].astype(BF16)
        k_ref[:, q0 + MLA_NOPE_DIM:q0 + MLA_QK_PAD] = k_pe
        v_ref[:, h * MLA_V_DIM:(h + 1) * MLA_V_DIM] = kvh[:, MLA_NOPE_DIM:].astype(BF16)


def _mla_projections(x, w_in, gq, gkv, tab, w_q, w_kv, q_rank, kv_rank, n_heads, q_scale):
    m, d = x.shape
    tm = _tile(m, 512)
    row = lambda i: (i, 0)
    fixed = lambda i: (0, 0)
    resident = lambda shape: pl.BlockSpec(shape, fixed, pipeline_mode=pl.Buffered(1))
    qk_w, v_w = n_heads * MLA_QK_PAD, n_heads * MLA_V_DIM
    return pl.pallas_call(
        functools.partial(_mla_proj_kernel, q_rank=q_rank, kv_rank=kv_rank, n_heads=n_heads, q_scale=q_scale),
        grid=(m // tm,),
        in_specs=[pl.BlockSpec((tm, d), row), resident(w_in.shape),
                  pl.BlockSpec((1, q_rank), fixed), pl.BlockSpec((1, kv_rank), fixed),
                  pl.BlockSpec((tm, LANE), row), resident(w_q.shape), resident(w_kv.shape)],
        out_specs=[pl.BlockSpec((tm, qk_w), row), pl.BlockSpec((tm, qk_w), row), pl.BlockSpec((tm, v_w), row)],
        out_shape=[jax.ShapeDtypeStruct((m, qk_w), BF16), jax.ShapeDtypeStruct((m, qk_w), BF16),
                   jax.ShapeDtypeStruct((m, v_w), BF16)],
        compiler_params=_cparams("parallel"), name="mla_proj")(
            x, w_in, gq.reshape(1, q_rank).astype(F32), gkv.reshape(1, kv_rank).astype(F32), tab, w_q, w_kv)


F32_SUBLANES = 8
MOE_TILE_PARTS = 4


def _row_copy(src_hbm, src_row, group_buf, sublane, sem):
    return pltpu.make_async_copy(src_hbm.at[pl.ds(src_row, 1), :], group_buf.at[pl.ds(sublane, 1), :], sem)


def _wait_group(src_hbm, group_buf, sem):
    pltpu.make_async_copy(src_hbm.at[pl.ds(0, group_buf.shape[0]), :], group_buf, sem).wait()


def _gather_rows_kernel(tok_ref, x_hbm, o_ref, buf, sem):
    i, n = pl.program_id(0), pl.num_programs(0)
    groups, sub, d = buf.shape[1:]
    rows = groups * sub

    def issue(step, slot):
        def start(g, _):
            for s in range(sub):
                _row_copy(x_hbm, tok_ref[step * rows + g * sub + s], buf.at[slot, g], s, sem.at[slot]).start()
            return 0
        lax.fori_loop(0, groups, start, 0)

    @pl.when(i == 0)
    def _():
        issue(0, 0)

    @pl.when(i + 1 < n)
    def _():
        issue(i + 1, (i + 1) % 2)

    slot = i % 2

    def wait(g, _):
        _wait_group(x_hbm, buf.at[slot, g], sem.at[slot])
        return 0
    lax.fori_loop(0, groups, wait, 0)
    o_ref[...] = buf[slot].reshape(rows, d).astype(o_ref.dtype)


def _gather_rows(h, row_token, rows_per_step):
    n_rows = row_token.shape[0]
    d = h.shape[1]
    return pl.pallas_call(
        _gather_rows_kernel,
        grid_spec=pltpu.PrefetchScalarGridSpec(
            num_scalar_prefetch=1, grid=(n_rows // rows_per_step,),
            in_specs=[pl.BlockSpec(memory_space=pl.ANY)],
            out_specs=pl.BlockSpec((rows_per_step, d), lambda i, tok: (i, 0)),
            scratch_shapes=[pltpu.VMEM((2, rows_per_step // F32_SUBLANES, F32_SUBLANES, d), F32),
                            pltpu.SemaphoreType.DMA((2,))]),
        out_shape=jax.ShapeDtypeStruct((n_rows, d), BF16),
        compiler_params=_cparams("arbitrary"), name="moe_gather")(row_token, h)


def _moe_ffn_kernel(te_ref, tr_ref, x_ref, wg_ref, wu_ref, wd_ref, o_ref):
    i, j = pl.program_id(0), pl.program_id(1)
    part = x_ref.shape[0] // MOE_TILE_PARTS
    parts_used = (tr_ref[i] + part - 1) // part

    @pl.when(j == 0)
    def _():
        o_ref[...] = jnp.zeros_like(o_ref)

    for used in range(1, MOE_TILE_PARTS + 1):
        @pl.when(parts_used == used)
        def _():
            rows = slice(0, used * part)
            o_ref[rows, :] += _swiglu(x_ref[rows, :], wg_ref[0], wu_ref[0], wd_ref[0].astype(BF16))


def _moe_ffn(xg, wg, wu, wd, tile_expert, tile_rows, tm):
    n_rows, d = xg.shape
    f = wg.shape[2]
    tf = _tile(f, 1024)
    nj = f // tf
    col = lambda i, j, te, tr: jnp.where(tr[i] > 0, j, nj - 1)
    return pl.pallas_call(
        _moe_ffn_kernel,
        grid_spec=pltpu.PrefetchScalarGridSpec(
            num_scalar_prefetch=2, grid=(n_rows // tm, nj),
            in_specs=[pl.BlockSpec((tm, d), lambda i, j, te, tr: (i, 0)),
                      pl.BlockSpec((1, d, tf), lambda i, j, te, tr: (te[i], 0, col(i, j, te, tr))),
                      pl.BlockSpec((1, d, tf), lambda i, j, te, tr: (te[i], 0, col(i, j, te, tr))),
                      pl.BlockSpec((1, tf, d), lambda i, j, te, tr: (te[i], col(i, j, te, tr), 0))],
            out_specs=pl.BlockSpec((tm, d), lambda i, j, te, tr: (i, 0))),
        out_shape=jax.ShapeDtypeStruct((n_rows, d), F32),
        compiler_params=_cparams("arbitrary", "arbitrary"), name="moe_ffn")(
            tile_expert, tile_rows, xg, wg, wu, wd)


def _combine_ln_kernel(pos_ref, h_ref, gate_ref, g_ref, b_ref, y_hbm, o_ref, buf, sem):
    groups, sub, d = buf.shape[2:]
    rows = groups * sub
    i, n = pl.program_id(0), pl.num_programs(0)

    def issue(step, slot):
        def start(g, _):
            for s in range(sub):
                for k in range(TOP_K):
                    src = pos_ref[TOP_K * (step * rows + g * sub + s) + k]
                    _row_copy(y_hbm, src, buf.at[slot, k, g], s, sem.at[slot, k]).start()
            return 0
        lax.fori_loop(0, groups, start, 0)

    @pl.when(i == 0)
    def _():
        issue(0, 0)

    @pl.when(i + 1 < n)
    def _():
        issue(i + 1, (i + 1) % 2)

    slot = i % 2

    def wait(g, _):
        for k in range(TOP_K):
            _wait_group(y_hbm, buf.at[slot, k, g], sem.at[slot, k])
        return 0
    lax.fori_loop(0, groups, wait, 0)
    gate = gate_ref[...]
    y = None
    for k in range(TOP_K):
        term = buf[slot, k].reshape(rows, d) * gate[:, k:k + 1]
        y = term if y is None else y + term
    o_ref[...] = _layer_norm(DEEPNORM_ALPHA * h_ref[...] + y, g_ref[...], b_ref[...])


def _moe_combine_ln(y_rows, pos, h, gates, g, b):
    m, d = h.shape
    tm = _tile(m, 256)
    row = lambda i, p: (i, 0)
    fixed = lambda i, p: (0, 0)
    return pl.pallas_call(
        _combine_ln_kernel,
        grid_spec=pltpu.PrefetchScalarGridSpec(
            num_scalar_prefetch=1, grid=(m // tm,),
            in_specs=[pl.BlockSpec((tm, d), row), pl.BlockSpec((tm, LANE), row),
                      pl.BlockSpec((1, d), fixed), pl.BlockSpec((1, d), fixed),
                      pl.BlockSpec(memory_space=pl.ANY)],
            out_specs=pl.BlockSpec((tm, d), row),
            scratch_shapes=[pltpu.VMEM((2, TOP_K, tm // F32_SUBLANES, F32_SUBLANES, d), F32),
                            pltpu.SemaphoreType.DMA((2, TOP_K))]),
        out_shape=jax.ShapeDtypeStruct((m, d), F32),
        compiler_params=_cparams("arbitrary"), name="moe_combine_ln")(
            pos, h, gates, g.reshape(1, d), b.reshape(1, d), y_rows)


def _moe_routing(top_idx, n_experts, tm):
    n_assign = top_idx.size
    expert_of = top_idx.reshape(n_assign)
    onehot = (expert_of[:, None] == jnp.arange(n_experts, dtype=jnp.int32)[None, :]).astype(jnp.int32)
    csum = jnp.cumsum(onehot, axis=0)
    rank = jnp.sum(onehot * (csum - 1), axis=1)
    counts = csum[-1]
    padded = ((counts + tm - 1) // tm) * tm
    padded_end = jnp.cumsum(padded)
    padded_start = padded_end - padded
    pos = (padded_start[expert_of] + rank).astype(jnp.int32)
    n_tiles = -(-n_assign // tm) + n_experts
    row_token = jnp.zeros((n_tiles * tm,), jnp.int32).at[pos].set(
        jnp.arange(n_assign, dtype=jnp.int32) // TOP_K, unique_indices=True, mode="promise_in_bounds")
    tile_start = jnp.arange(n_tiles, dtype=jnp.int32) * tm
    tile_used = tile_start < padded_end[-1]
    tile_expert = jnp.minimum(jnp.searchsorted(padded_end, tile_start, side='right'), n_experts - 1)
    tile_rows = jnp.clip(counts[tile_expert] - (tile_start - padded_start[tile_expert]), 0, tm)
    tile_rows = jnp.where(tile_used, tile_rows, 0).astype(jnp.int32)
    last_expert = jnp.max(jnp.where(tile_used, tile_expert, 0))
    tile_expert = jnp.where(tile_used, tile_expert, last_expert).astype(jnp.int32)
    return pos, row_token, tile_expert, tile_rows


def _rotate_half_cols(w):
    half = w.shape[-1] // 2
    return jnp.concatenate([-w[..., half:], w[..., :half]], axis=-1)


def kernel(x, positions, w_in0, b_forget, sinks_b, w_out0, ln0_mix_g, ln0_mix_b, w_ffn_gate, w_ffn_up, w_ffn_down, ln0_ffn_g, ln0_ffn_b, w_in1, q_norm_g, w_uq, kv_norm_g, w_ukv, w_out1, ln1_mix_g, ln1_mix_b, w_router, w_moe_gate, w_moe_up, w_moe_down, ln1_ffn_g, ln1_ffn_b):
    batch, seq, d = x.shape
    n = batch * seq
    fox_heads = b_forget.shape[0]
    fox_w = fox_heads * HEAD_DIM
    swa_q = sinks_b.shape[0]
    swa_qw = swa_q * HEAD_DIM
    swa_kvw = (w_in0.shape[1] - 3 * fox_w - fox_heads - swa_qw) // 2
    swa_kv = swa_kvw // HEAD_DIM
    q_rank, kv_rank = q_norm_g.shape[0], kv_norm_g.shape[0]
    mla_heads = w_uq.shape[1] // (MLA_NOPE_DIM + MLA_ROPE_DIM)
    n_experts = w_router.shape[1]

    xf = x.reshape(n, d)
    cos_a, sin_a, tab_m = _rope_tables(positions)

    c0 = 3 * fox_w
    c1 = c0 + fox_heads
    w_qkv = jnp.concatenate([w_in0[:, :c0], w_in0[:, c1:]], axis=1).astype(BF16)
    group_kinds = ([SCALED] * fox_heads + [PLAIN] * (2 * fox_heads)
                   + [ROTARY] * (swa_q + swa_kv) + [PLAIN] * swa_kv)
    qkv = _project(xf, w_qkv, group_kinds, HEAD_DIM ** -0.5 * LOG2_E, cos_a, sin_a, 1024, 1152, "proj0")
    cum_log_f = _fox_cum_log_forget(xf, w_in0[:, c0:c1], b_forget, batch, seq)
    o_a, w_moe_up_bf = _causal_attention(qkv, qkv, qkv, 0, fox_heads, 2 * fox_heads, HEAD_DIM, HEAD_DIM,
                                         fox_heads, batch, seq, ck=cum_log_f, cast_rider=w_moe_up,
                                         name="fox_attn")
    o_b, ffn_w_bf = _swa_attention(qkv, c0, c0 + swa_qw, c0 + swa_qw + swa_kvw, sinks_b, swa_q, swa_kv,
                                   batch, seq, HEAD_DIM ** -0.5,
                                   cast_riders=[w_ffn_gate, w_ffn_up, w_ffn_down])
    w_out0_bf = w_out0.astype(BF16)
    (h1,) = _matmul_residual_ln([(o_a, w_out0_bf[:fox_w]), (o_b, w_out0_bf[fox_w:])], xf,
                                ln0_mix_g, ln0_mix_b, "out0_ln")
    h2 = _swiglu_residual_ln(h1, *ffn_w_bf, ln0_ffn_g, ln0_ffn_b)

    r0 = q_rank + kv_rank
    w_in1_x = jnp.concatenate([w_in1, _rotate_half_cols(w_in1[:, r0:])], axis=1).astype(BF16)
    w_uq_h = w_uq.reshape(q_rank, mla_heads, MLA_NOPE_DIM + MLA_ROPE_DIM)
    w_uq_x = jnp.concatenate([w_uq_h, _rotate_half_cols(w_uq_h[..., MLA_NOPE_DIM:])], axis=-1)
    q_m, k_m, v_m = _mla_projections(
        h2, w_in1_x, q_norm_g, kv_norm_g, tab_m,
        w_uq_x.reshape(q_rank, mla_heads * MLA_QK_PAD).astype(BF16), w_ukv.astype(BF16),
        q_rank, kv_rank, mla_heads, (MLA_NOPE_DIM + MLA_ROPE_DIM) ** -0.5 * LOG2_E)
    o_m, w_moe_gate_bf = _causal_attention(q_m, k_m, v_m, 0, 0, 0, MLA_QK_PAD, MLA_V_DIM, mla_heads, batch,
                                           seq, cast_rider=w_moe_gate, name="mla_attn")
    h3, idx, gates = _matmul_residual_ln([(o_m, w_out1.astype(BF16))], h2, ln1_mix_g, ln1_mix_b,
                                         "out1_ln_route", w_router=w_router)

    moe_tm = _tile(n * TOP_K, 512)
    pos, row_token, tile_expert, tile_rows = _moe_routing(idx[:, :TOP_K], n_experts, moe_tm)
    xg = _gather_rows(h3, row_token, moe_tm)
    y_rows = _moe_ffn(xg, w_moe_gate_bf, w_moe_up_bf, w_moe_down, tile_expert, tile_rows, moe_tm)
    out = _moe_combine_ln(y_rows, pos, h3, gates, ln1_ffn_g, ln1_ffn_b)
    return out.reshape(batch, seq, d)
```

```python
import functools
import math

import jax
import jax.numpy as jnp
from jax import lax
from jax.experimental import pallas as pl
from jax.experimental.pallas import tpu as pltpu

F32 = jnp.float32
BF16 = jnp.bfloat16

LANE = 128
F32_SUBLANES = 8
BF16_SUBLANES = 16
HEAD_DIM = 128
SWA_WINDOW = 128
MLA_NOPE_DIM = 128
MLA_ROPE_DIM = 64
MLA_V_DIM = 128
MLA_QK_PAD = 256
ROPE_THETA = 10000.0
TOP_K = 2
LN_EPS = 1e-5
RMS_EPS = 1e-6
DEPTH = 2
DEEPNORM_ALPHA = (2 * DEPTH) ** 0.25
LOG2_E = math.log2(math.e)

NT_DIMS = (((1,), (1,)), ((), ()))

ROPE_TABLE_ROWS = 1024
PROJ_ROWS, PROJ_COLS = 1024, 1152
GATE_CHUNK = 512
ATTN_TILE = 512
ATTN_HEADS_PER_STEP = 4
OUT_PROJ_ROWS, LN_ROW_GROUP = 512, 256
FFN_ROWS, FFN_COLS = 512, 512
MLA_PROJ_ROWS = 512
MOE_ROWS, MOE_COLS = 512, 1024
MOE_GATHER_ROWS, MOE_COMBINE_ROWS = 512, 256
MOE_TILE_PARTS = 8


def _tile(n, pref):
    if n <= pref:
        return n
    t = (pref // LANE) * LANE
    while n % t:
        t -= LANE
    return t


def _cparams(*sem):
    return pltpu.CompilerParams(dimension_semantics=sem)


def _dot(a, b):
    return jnp.dot(a, b, preferred_element_type=F32)


def _layer_norm(y, g, b):
    mu = jnp.mean(y, axis=-1, keepdims=True)
    d = y - mu
    var = jnp.mean(d * d, axis=-1, keepdims=True)
    return d * lax.rsqrt(var + LN_EPS) * g + b


def _silu(g):
    return g / (1.0 + jnp.exp(-g))


def _rope_table_kernel(pos_ref, freq_ref, cos_a_ref, sin_a_ref, tab_m_ref):
    pos = pos_ref[...]
    ang_a = pos * freq_ref[0:1, :]
    lane = lax.broadcasted_iota(jnp.int32, ang_a.shape, 1)
    cos_a_ref[...] = jnp.cos(ang_a)
    sin_a_ref[...] = jnp.where(lane < HEAD_DIM // 2, -jnp.sin(ang_a), jnp.sin(ang_a))
    ang_m = pos * freq_ref[1:2, :]
    tab_m_ref[...] = jnp.where(lane < MLA_ROPE_DIM, jnp.cos(ang_m), jnp.sin(ang_m))


def _rope_tables(positions):
    n = positions.size
    pos = positions.astype(F32).reshape(n, 1)
    half_a = HEAD_DIM // 2
    half_m = MLA_ROPE_DIM // 2
    inv_a = ROPE_THETA ** (-2.0 * jnp.arange(half_a, dtype=F32) / HEAD_DIM)
    inv_m = ROPE_THETA ** (-2.0 * jnp.arange(half_m, dtype=F32) / MLA_ROPE_DIM)
    freq = jnp.stack([jnp.tile(inv_a, 2), jnp.tile(inv_m, 4)])
    tm = _tile(n, ROPE_TABLE_ROWS)
    out = jax.ShapeDtypeStruct((n, LANE), F32)
    row = pl.BlockSpec((tm, LANE), lambda i: (i, 0))
    return pl.pallas_call(
        _rope_table_kernel, grid=(n // tm,),
        in_specs=[pl.BlockSpec((tm, 1), lambda i: (i, 0)), pl.BlockSpec((2, LANE), lambda i: (0, 0))],
        out_specs=[row, row, row], out_shape=[out, out, out],
        compiler_params=_cparams("parallel"), name="rope_tables")(pos, freq)


PLAIN, SCALED, ROTARY = "plain", "scaled", "rotary"


def _proj_kernel(x_ref, w_ref, cos_ref, sin_ref, o_ref, xb_ref, *, tile_kinds, scale):
    j = pl.program_id(1)

    @pl.when(j == 0)
    def _():
        xb_ref[...] = x_ref[...].astype(BF16)

    def tile(kinds):
        acc = _dot(xb_ref[...], w_ref[...])
        for c, kind in enumerate(kinds):
            cols = slice(c * HEAD_DIM, (c + 1) * HEAD_DIM)
            a = acc[:, cols]
            if kind == SCALED:
                a = a * scale
            elif kind == ROTARY:
                a = a * cos_ref[...] + pltpu.roll(a, HEAD_DIM // 2, 1) * sin_ref[...]
            o_ref[:, cols] = a.astype(o_ref.dtype)

    for kinds in sorted(set(tile_kinds)):
        first = tile_kinds.index(kinds)
        count = tile_kinds.count(kinds)
        assert tile_kinds[first:first + count] == (kinds,) * count
        pl.when((j >= first) & (j < first + count))(functools.partial(tile, kinds))


def _project(x, w, group_kinds, scale, cos, sin, tm, tn, name):
    m, k = x.shape
    n = w.shape[1]
    tm, tn = _tile(m, tm), _tile(n, tn)
    per_tile = tn // HEAD_DIM
    tile_kinds = tuple(tuple(group_kinds[t * per_tile:(t + 1) * per_tile]) for t in range(n // tn))
    return pl.pallas_call(
        functools.partial(_proj_kernel, tile_kinds=tile_kinds, scale=scale), grid=(m // tm, n // tn),
        in_specs=[pl.BlockSpec((tm, k), lambda i, j: (i, 0)), pl.BlockSpec((k, tn), lambda i, j: (0, j)),
                  pl.BlockSpec((tm, LANE), lambda i, j: (i, 0)), pl.BlockSpec((tm, LANE), lambda i, j: (i, 0))],
        out_specs=pl.BlockSpec((tm, tn), lambda i, j: (i, j)),
        out_shape=jax.ShapeDtypeStruct((m, n), BF16),
        scratch_shapes=[pltpu.VMEM((tm, k), BF16)],
        compiler_params=_cparams("parallel", "arbitrary"), name=name)(x, w, cos, sin)


def _split3(x):
    hi = x.astype(BF16)
    r = x - hi.astype(F32)
    mid = r.astype(BF16)
    lo = (r - mid.astype(F32)).astype(BF16)
    return hi, mid, lo


def _fox_gate_kernel(x_ref, wf_ref, bf_ref, o_ref, carry_ref, *, n_heads):
    @pl.when(pl.program_id(1) == 0)
    def _():
        carry_ref[...] = jnp.zeros_like(carry_ref)

    f = lax.dot_general(wf_ref[...], x_ref[...].astype(BF16), NT_DIMS, preferred_element_type=F32)
    z = f + bf_ref[...]
    log_f = jnp.minimum(z, 0.0) - jnp.log1p(jnp.exp(-jnp.abs(z)))
    tc = z.shape[1]
    src = lax.broadcasted_iota(jnp.int32, (tc, tc), 0)
    dst = lax.broadcasted_iota(jnp.int32, (tc, tc), 1)
    tri = jnp.where(src <= dst, 1.0, 0.0).astype(BF16)
    hi, mid, lo = _split3(log_f)
    cum = _dot(hi, tri) + _dot(mid, tri) + _dot(lo, tri) + carry_ref[:, 0:1]
    o_ref[0] = cum[:n_heads] * LOG2_E
    carry_ref[...] = jnp.broadcast_to(cum[:, tc - 1:tc], carry_ref.shape)


def _fox_cum_log_forget(x, w_f, b_forget, batch, seq):
    n, d = x.shape
    n_heads = w_f.shape[1]
    rows = BF16_SUBLANES
    wf_t = jnp.zeros((rows, d), BF16).at[:n_heads].set(w_f.T.astype(BF16))
    bf = jnp.zeros((rows, 1), F32).at[:n_heads, 0].set(b_forget.astype(F32))
    tc = _tile(seq, GATE_CHUNK)
    nc = seq // tc
    return pl.pallas_call(
        functools.partial(_fox_gate_kernel, n_heads=n_heads), grid=(batch, nc),
        in_specs=[pl.BlockSpec((tc, d), lambda b, s: (b * nc + s, 0)),
                  pl.BlockSpec((rows, d), lambda b, s: (0, 0)),
                  pl.BlockSpec((rows, 1), lambda b, s: (0, 0))],
        out_specs=pl.BlockSpec((1, n_heads, tc), lambda b, s: (b, 0, s)),
        out_shape=jax.ShapeDtypeStruct((batch, n_heads, seq), F32),
        scratch_shapes=[pltpu.VMEM((rows, LANE), F32)],
        compiler_params=_cparams("parallel", "arbitrary"), name="fox_gate")(x, wf_t, bf)


def _flash_kernel(*refs, tk, dk, dv, heads, has_bias, n_cast_slabs):
    refs = list(refs)
    q_ref, k_ref, v_ref = refs[:3]
    del refs[:3]
    ck_ref = refs.pop(0) if has_bias else None
    w_ref = refs.pop(0) if n_cast_slabs else None
    o_ref = refs.pop(0)
    qi = pl.program_id(2)
    tq = q_ref.shape[0]
    qs = [q_ref[:, g * dk:(g + 1) * dk] for g in range(heads)]

    if n_cast_slabs:
        step = (pl.program_id(0) * pl.num_programs(1) + pl.program_id(1)) * pl.num_programs(2) + qi
        _cast_rider_step(step, n_cast_slabs, w_ref, refs[0])

    def step(g, state, j, masked):
        m, l, acc = state
        kv_rows = pl.ds(pl.multiple_of(j * tk, tk), tk)
        s = lax.dot_general(qs[g], k_ref[kv_rows, g * dk:(g + 1) * dk], NT_DIMS, preferred_element_type=F32)
        if has_bias:
            s = s - ck_ref[0, g, j]
        if masked:
            row = lax.broadcasted_iota(jnp.int32, s.shape, 0)
            col = lax.broadcasted_iota(jnp.int32, s.shape, 1)
            s = jnp.where(row >= col, s, -jnp.inf)
        m_new = jnp.maximum(m, jnp.max(s, axis=-1, keepdims=True))
        a = jnp.exp2(m - m_new)
        p = jnp.exp2(s - m_new)
        l = a * l + jnp.sum(p, axis=-1, keepdims=True)
        acc = a * acc + _dot(p.astype(BF16), v_ref[kv_rows, g * dv:(g + 1) * dv])
        return m_new, l, acc

    def body(j, carry):
        return tuple(step(g, carry[g], j, False) for g in range(heads))

    init = (jnp.full((tq, 1), -jnp.inf, F32), jnp.zeros((tq, 1), F32), jnp.zeros((tq, dv), F32))
    carry = lax.fori_loop(0, qi, body, (init,) * heads)
    for g in range(heads):
        _, l, acc = step(g, carry[g], qi, True)
        o_ref[:, g * dv:(g + 1) * dv] = (acc / l).astype(o_ref.dtype)


def _slab_rows(total_rows, max_slabs):
    rows = BF16_SUBLANES
    while total_rows % rows or total_rows // rows > max_slabs:
        rows += BF16_SUBLANES
    return rows


def _cast_rider(w, grid, step_of):
    flat = w.reshape(-1, w.shape[-1])
    rows = _slab_rows(flat.shape[0], math.prod(grid))
    n_slabs = flat.shape[0] // rows
    spec = pl.BlockSpec((rows, flat.shape[1]), lambda *g: (jnp.minimum(step_of(*g), n_slabs - 1), 0))
    return flat, spec, jax.ShapeDtypeStruct(flat.shape, BF16), n_slabs


def _cast_rider_step(step, n_slabs, w_ref, o_ref):
    @pl.when(step < n_slabs)
    def _():
        o_ref[...] = w_ref[...].astype(o_ref.dtype)


def _causal_attention(q_arr, k_arr, v_arr, q_col, k_col, v_col, dk, dv, n_heads, batch, seq,
                      ck=None, cast_rider=None, name="attn"):
    heads = ATTN_HEADS_PER_STEP
    assert n_heads % heads == 0 and q_col % heads == 0 and k_col % heads == 0 and v_col % heads == 0
    t = _tile(seq, ATTN_TILE)
    nq = seq // t
    grid = (batch, n_heads // heads, nq)
    in_specs = [pl.BlockSpec((t, heads * dk), lambda b, h, i: (b * nq + i, q_col // heads + h)),
                pl.BlockSpec((seq, heads * dk), lambda b, h, i: (b, k_col // heads + h)),
                pl.BlockSpec((seq, heads * dv), lambda b, h, i: (b, v_col // heads + h))]
    args = [q_arr, k_arr, v_arr]
    out_specs = [pl.BlockSpec((t, heads * dv), lambda b, h, i: (b * nq + i, h))]
    out_shape = [jax.ShapeDtypeStruct((batch * seq, n_heads * dv), BF16)]
    if ck is not None:
        args.append(ck.reshape(batch, n_heads, nq, 1, t))
        in_specs.append(pl.BlockSpec((1, heads, nq, 1, t), lambda b, h, i: (b, h, 0, 0, 0)))
    n_slabs = 0
    if cast_rider is not None:
        flat, slab, flat_bf, n_slabs = _cast_rider(cast_rider, grid, lambda b, h, i: (b * grid[1] + h) * nq + i)
        args.append(flat)
        in_specs.append(slab)
        out_specs.append(slab)
        out_shape.append(flat_bf)
    outs = pl.pallas_call(
        functools.partial(_flash_kernel, tk=t, dk=dk, dv=dv, heads=heads, has_bias=ck is not None,
                          n_cast_slabs=n_slabs),
        grid=grid, in_specs=in_specs, out_specs=out_specs, out_shape=out_shape,
        compiler_params=_cparams("parallel", "parallel", "arbitrary"), name=name)(*args)
    if cast_rider is None:
        return outs[0]
    return outs[0], outs[1].reshape(cast_rider.shape)


def _swa_kernel(sink_ref, q_ref, kp_ref, kc_ref, vp_ref, vc_ref, *rest, scale, n_q, n_kv, rider_slabs):
    n_riders = len(rider_slabs)
    o_ref = rest[n_riders]
    step = pl.program_id(0) * pl.num_programs(1) + pl.program_id(1)
    for n_slabs, w_ref, wo_ref in zip(rider_slabs, rest[:n_riders], rest[n_riders + 1:]):
        _cast_rider_step(step, n_slabs, w_ref, wo_ref)
    w = SWA_WINDOW
    qi = lax.broadcasted_iota(jnp.int32, (w, 2 * w), 0)
    ji = lax.broadcasted_iota(jnp.int32, (w, 2 * w), 1)
    has_prev = pl.program_id(1) > 0
    mask = (ji > qi) & (ji <= qi + w) & ((ji >= w) | has_prev)
    group = n_q // n_kv
    for kvh in range(n_kv):
        cols = slice(kvh * HEAD_DIM, (kvh + 1) * HEAD_DIM)
        k = jnp.concatenate([kp_ref[:, cols], kc_ref[:, cols]], axis=0)
        v = jnp.concatenate([vp_ref[:, cols], vc_ref[:, cols]], axis=0)
        for g in range(group):
            h = kvh * group + g
            hc = slice(h * HEAD_DIM, (h + 1) * HEAD_DIM)
            s = lax.dot_general(q_ref[:, hc], k, NT_DIMS, preferred_element_type=F32) * scale
            s = jnp.where(mask, s, -jnp.inf)
            sink = sink_ref[h]
            m = jnp.maximum(jnp.max(s, axis=-1, keepdims=True), sink)
            p = jnp.exp(s - m)
            denom = jnp.sum(p, axis=-1, keepdims=True) + jnp.exp(sink - m)
            o_ref[:, hc] = (_dot(p.astype(BF16), v) / denom).astype(o_ref.dtype)


def _swa_attention(qkv, q_off, k_off, v_off, sinks, n_q, n_kv, batch, seq, scale, cast_riders):
    w = SWA_WINDOW
    nb = seq // w
    grid = (batch, nb)
    qw, kw = n_q * HEAD_DIM, n_kv * HEAD_DIM
    assert q_off % qw == 0 and k_off % kw == 0 and v_off % kw == 0
    q_col, k_col, v_col = q_off // qw, k_off // kw, v_off // kw
    cur = lambda b, n: b * nb + n
    prev = lambda b, n: b * nb + jnp.maximum(n - 1, 0)
    riders = [_cast_rider(r, grid, cur) for r in cast_riders]
    outs = pl.pallas_call(
        functools.partial(_swa_kernel, scale=scale, n_q=n_q, n_kv=n_kv,
                          rider_slabs=tuple(r[3] for r in riders)), grid=grid,
        in_specs=[pl.BlockSpec(memory_space=pltpu.SMEM),
                  pl.BlockSpec((w, qw), lambda b, n: (cur(b, n), q_col)),
                  pl.BlockSpec((w, kw), lambda b, n: (prev(b, n), k_col)),
                  pl.BlockSpec((w, kw), lambda b, n: (cur(b, n), k_col)),
                  pl.BlockSpec((w, kw), lambda b, n: (prev(b, n), v_col)),
                  pl.BlockSpec((w, kw), lambda b, n: (cur(b, n), v_col))] + [r[1] for r in riders],
        out_specs=[pl.BlockSpec((w, qw), lambda b, n: (cur(b, n), 0))] + [r[1] for r in riders],
        out_shape=[jax.ShapeDtypeStruct((batch * seq, qw), BF16)] + [r[2] for r in riders],
        compiler_params=_cparams("arbitrary", "arbitrary"), name="swa_attn")(
            sinks.astype(F32), qkv, qkv, qkv, qkv, qkv, *[r[0] for r in riders])
    return outs[0], [o.reshape(r.shape) for o, r in zip(outs[1:], cast_riders)]


def _top2_route(x, w, n_experts):
    xh = x.astype(BF16)
    xl = (x - xh.astype(F32)).astype(BF16)
    wh = w.astype(BF16)
    wl = (w - wh.astype(F32)).astype(BF16)
    hi = _dot(xh, jnp.concatenate([wh, wl], axis=1))
    logits = hi[:, :LANE] + hi[:, LANE:] + _dot(xl, wh)
    lane = lax.broadcasted_iota(jnp.int32, logits.shape, 1)
    lane_f = lane.astype(F32)
    l1 = jnp.where(lane < n_experts, logits, -jnp.inf)
    v1 = jnp.max(l1, axis=-1, keepdims=True)
    i1 = jnp.min(jnp.where(l1 == v1, lane_f, float(LANE)), axis=-1, keepdims=True)
    l2 = jnp.where(lane_f == i1, -jnp.inf, l1)
    v2 = jnp.max(l2, axis=-1, keepdims=True)
    i2 = jnp.min(jnp.where(l2 == v2, lane_f, float(LANE)), axis=-1, keepdims=True)
    e2 = jnp.exp(v2 - v1)
    idx = jnp.where(lane == 0, i1, jnp.where(lane == 1, i2, 0.0)).astype(jnp.int32)
    gate = jnp.where(lane == 0, 1.0 / (1.0 + e2), jnp.where(lane == 1, e2 / (1.0 + e2), 0.0))
    return idx, gate


def _mm_res_ln_kernel(*refs, n_pairs, n_experts):
    a_refs, w_refs = refs[:n_pairs], refs[n_pairs:2 * n_pairs]
    if n_experts:
        res_ref, g_ref, b_ref, wr_ref, o_ref, idx_ref, gate_ref = refs[2 * n_pairs:]
    else:
        res_ref, g_ref, b_ref, o_ref = refs[2 * n_pairs:]
    tm = o_ref.shape[0]
    sub = _tile(tm, LN_ROW_GROUP)
    for r in range(tm // sub):
        rows = slice(r * sub, (r + 1) * sub)
        acc = _dot(a_refs[0][rows, :], w_refs[0][...])
        for a_ref, w_ref in zip(a_refs[1:], w_refs[1:]):
            acc = acc + _dot(a_ref[rows, :], w_ref[...])
        y = _layer_norm(DEEPNORM_ALPHA * res_ref[rows, :] + acc, g_ref[...], b_ref[...])
        o_ref[rows, :] = y
        if n_experts:
            idx_ref[rows, :], gate_ref[rows, :] = _top2_route(y, wr_ref[...], n_experts)


def _matmul_residual_ln(pairs, res, g, b, name, w_router=None):
    m, d = res.shape
    tm = _tile(m, OUT_PROJ_ROWS)
    row = lambda i: (i, 0)
    fixed = lambda i: (0, 0)
    resident = lambda shape: pl.BlockSpec(shape, fixed, pipeline_mode=pl.Buffered(1))
    in_specs = ([pl.BlockSpec((tm, a.shape[1]), row) for a, _ in pairs] + [resident(w.shape) for _, w in pairs]
                + [pl.BlockSpec((tm, d), row), pl.BlockSpec((1, d), fixed), pl.BlockSpec((1, d), fixed)])
    args = [a for a, _ in pairs] + [w for _, w in pairs] + [res, g.reshape(1, d), b.reshape(1, d)]
    out_specs = [pl.BlockSpec((tm, d), row)]
    out_shape = [jax.ShapeDtypeStruct((m, d), F32)]
    n_experts = 0
    if w_router is not None:
        n_experts = w_router.shape[1]
        in_specs.append(resident((d, LANE)))
        args.append(jnp.zeros((d, LANE), F32).at[:, :n_experts].set(w_router.astype(F32)))
        out_specs += [pl.BlockSpec((tm, LANE), row), pl.BlockSpec((tm, LANE), row)]
        out_shape += [jax.ShapeDtypeStruct((m, LANE), jnp.int32), jax.ShapeDtypeStruct((m, LANE), F32)]
    return pl.pallas_call(
        functools.partial(_mm_res_ln_kernel, n_pairs=len(pairs), n_experts=n_experts),
        grid=(m // tm,), in_specs=in_specs, out_specs=out_specs, out_shape=out_shape,
        compiler_params=_cparams("parallel"), name=name)(*args)


def _swiglu(x, wg, wu, wd):
    a = _silu(_dot(x, wg)) * _dot(x, wu)
    return _dot(a.astype(BF16), wd)


def _ffn_ln_kernel(h_ref, wg_ref, wu_ref, wd_ref, g_ref, b_ref, o_ref, x_ref):
    j = pl.program_id(1)

    @pl.when(j == 0)
    def _():
        x_ref[...] = h_ref[...].astype(BF16)
        o_ref[...] = jnp.zeros_like(o_ref)

    o_ref[...] += _swiglu(x_ref[...], wg_ref[...], wu_ref[...], wd_ref[...])

    @pl.when(j == pl.num_programs(1) - 1)
    def _():
        o_ref[...] = _layer_norm(DEEPNORM_ALPHA * h_ref[...] + o_ref[...], g_ref[...], b_ref[...])


def _swiglu_residual_ln(h, wg, wu, wd, g, b):
    m, d = h.shape
    f = wg.shape[1]
    tm, tf = _tile(m, FFN_ROWS), _tile(f, FFN_COLS)
    row = lambda i, j: (i, 0)
    fixed = lambda i, j: (0, 0)
    return pl.pallas_call(
        _ffn_ln_kernel, grid=(m // tm, f // tf),
        in_specs=[pl.BlockSpec((tm, d), row),
                  pl.BlockSpec((d, tf), lambda i, j: (0, j)), pl.BlockSpec((d, tf), lambda i, j: (0, j)),
                  pl.BlockSpec((tf, d), lambda i, j: (j, 0)),
                  pl.BlockSpec((1, d), fixed), pl.BlockSpec((1, d), fixed)],
        out_specs=pl.BlockSpec((tm, d), row),
        out_shape=jax.ShapeDtypeStruct((m, d), F32),
        scratch_shapes=[pltpu.VMEM((tm, d), BF16)],
        compiler_params=_cparams("parallel", "arbitrary"), name="ffn_ln")(
            h, wg, wu, wd, g.reshape(1, d), b.reshape(1, d))


def _rms_norm(c, g):
    ms = jnp.mean(c * c, axis=-1, keepdims=True)
    return c * lax.rsqrt(ms + RMS_EPS) * g


def _rope64(pair, tab):
    w = pair * tab
    lane = lax.broadcasted_iota(jnp.int32, w.shape, 1)
    return jnp.where(lane < MLA_ROPE_DIM, w + pltpu.roll(w, MLA_ROPE_DIM, 1), 0.0)


def _mla_proj_kernel(x_ref, win_ref, gq_ref, gkv_ref, tab_ref, wq_ref, wkv_ref, q_ref, k_ref, v_ref, *,
                     q_rank, kv_rank, n_heads, q_scale):
    acc = _dot(x_ref[...].astype(BF16), win_ref[...])
    cq = _rms_norm(acc[:, :q_rank], gq_ref[...]).astype(BF16)
    ckv = _rms_norm(acc[:, q_rank:q_rank + kv_rank], gkv_ref[...]).astype(BF16)
    tab = tab_ref[...]
    k_pe = _rope64(acc[:, q_rank + kv_rank:], tab).astype(BF16)
    wkv_w = MLA_NOPE_DIM + MLA_V_DIM
    for h in range(n_heads):
        q0 = h * MLA_QK_PAD
        qh = _dot(cq, wq_ref[:, q0:q0 + MLA_QK_PAD]) * q_scale
        q_ref[:, q0:q0 + MLA_NOPE_DIM] = qh[:, :MLA_NOPE_DIM].astype(BF16)
        q_ref[:, q0 + MLA_NOPE_DIM:q0 + MLA_QK_PAD] = _rope64(qh[:, MLA_NOPE_DIM:], tab).astype(BF16)
        kvh = _dot(ckv, wkv_ref[:, h * wkv_w:(h + 1) * wkv_w])
        k_ref[:, q0:q0 + MLA_NOPE_DIM] = kvh[:, :MLA_NOPE_DIM].astype(BF16)
        k_ref[:, q0 + MLA_NOPE_DIM:q0 + MLA_QK_PAD] = k_pe
        v_ref[:, h * MLA_V_DIM:(h + 1) * MLA_V_DIM] = kvh[:, MLA_NOPE_DIM:].astype(BF16)


def _mla_projections(x, w_in, gq, gkv, tab, w_q, w_kv, q_rank, kv_rank, n_heads, q_scale):
    m, d = x.shape
    tm = _tile(m, MLA_PROJ_ROWS)
    row = lambda i: (i, 0)
    fixed = lambda i: (0, 0)
    resident = lambda shape: pl.BlockSpec(shape, fixed, pipeline_mode=pl.Buffered(1))
    qk_w, v_w = n_heads * MLA_QK_PAD, n_heads * MLA_V_DIM
    return pl.pallas_call(
        functools.partial(_mla_proj_kernel, q_rank=q_rank, kv_rank=kv_rank, n_heads=n_heads, q_scale=q_scale),
        grid=(m // tm,),
        in_specs=[pl.BlockSpec((tm, d), row), resident(w_in.shape),
                  pl.BlockSpec((1, q_rank), fixed), pl.BlockSpec((1, kv_rank), fixed),
                  pl.BlockSpec((tm, LANE), row), resident(w_q.shape), resident(w_kv.shape)],
        out_specs=[pl.BlockSpec((tm, qk_w), row), pl.BlockSpec((tm, qk_w), row), pl.BlockSpec((tm, v_w), row)],
        out_shape=[jax.ShapeDtypeStruct((m, qk_w), BF16), jax.ShapeDtypeStruct((m, qk_w), BF16),
                   jax.ShapeDtypeStruct((m, v_w), BF16)],
        compiler_params=_cparams("parallel"), name="mla_proj")(
            x, w_in, gq.reshape(1, q_rank).astype(F32), gkv.reshape(1, kv_rank).astype(F32), tab, w_q, w_kv)


def _row_copy(src_hbm, src_row, group_buf, sublane, sem):
    return pltpu.make_async_copy(src_hbm.at[pl.ds(src_row, 1), :], group_buf.at[pl.ds(sublane, 1), :], sem)


def _wait_group(src_hbm, group_buf, sem):
    pltpu.make_async_copy(src_hbm.at[pl.ds(0, group_buf.shape[0]), :], group_buf, sem).wait()


def _gather_rows_kernel(tok_ref, x_hbm, o_ref, buf, sem):
    i, n = pl.program_id(0), pl.num_programs(0)
    groups, sub, d = buf.shape[1:]
    rows = groups * sub

    def issue(step, slot):
        def start(g, _):
            for s in range(sub):
                _row_copy(x_hbm, tok_ref[step * rows + g * sub + s], buf.at[slot, g], s, sem.at[slot]).start()
            return 0
        lax.fori_loop(0, groups, start, 0)

    @pl.when(i == 0)
    def _():
        issue(0, 0)

    @pl.when(i + 1 < n)
    def _():
        issue(i + 1, (i + 1) % 2)

    slot = i % 2

    def wait(g, _):
        _wait_group(x_hbm, buf.at[slot, g], sem.at[slot])
        return 0
    lax.fori_loop(0, groups, wait, 0)
    o_ref[...] = buf[slot].reshape(rows, d).astype(o_ref.dtype)


def _gather_rows(h, row_token, rows_per_step):
    n_rows = row_token.shape[0]
    d = h.shape[1]
    return pl.pallas_call(
        _gather_rows_kernel,
        grid_spec=pltpu.PrefetchScalarGridSpec(
            num_scalar_prefetch=1, grid=(n_rows // rows_per_step,),
            in_specs=[pl.BlockSpec(memory_space=pl.ANY)],
            out_specs=pl.BlockSpec((rows_per_step, d), lambda i, tok: (i, 0)),
            scratch_shapes=[pltpu.VMEM((2, rows_per_step // F32_SUBLANES, F32_SUBLANES, d), F32),
                            pltpu.SemaphoreType.DMA((2,))]),
        out_shape=jax.ShapeDtypeStruct((n_rows, d), BF16),
        compiler_params=_cparams("arbitrary"), name="moe_gather")(row_token, h)


def _moe_ffn_kernel(te_ref, tr_ref, x_ref, wg_ref, wu_ref, wd_ref, o_ref):
    i, j = pl.program_id(0), pl.program_id(1)
    part = x_ref.shape[0] // MOE_TILE_PARTS
    parts_used = (tr_ref[i] + part - 1) // part

    @pl.when(j == 0)
    def _():
        o_ref[...] = jnp.zeros_like(o_ref)

    for used in range(1, MOE_TILE_PARTS + 1):
        @pl.when(parts_used == used)
        def _():
            rows = slice(0, used * part)
            o_ref[rows, :] += _swiglu(x_ref[rows, :], wg_ref[0], wu_ref[0], wd_ref[0].astype(BF16))


def _moe_ffn(xg, wg, wu, wd, tile_expert, tile_rows, tm):
    n_rows, d = xg.shape
    f = wg.shape[2]
    tf = _tile(f, MOE_COLS)
    nj = f // tf
    col = lambda i, j, te, tr: jnp.where(tr[i] > 0, j, nj - 1)
    return pl.pallas_call(
        _moe_ffn_kernel,
        grid_spec=pltpu.PrefetchScalarGridSpec(
            num_scalar_prefetch=2, grid=(n_rows // tm, nj),
            in_specs=[pl.BlockSpec((tm, d), lambda i, j, te, tr: (i, 0)),
                      pl.BlockSpec((1, d, tf), lambda i, j, te, tr: (te[i], 0, col(i, j, te, tr))),
                      pl.BlockSpec((1, d, tf), lambda i, j, te, tr: (te[i], 0, col(i, j, te, tr))),
                      pl.BlockSpec((1, tf, d), lambda i, j, te, tr: (te[i], col(i, j, te, tr), 0))],
            out_specs=pl.BlockSpec((tm, d), lambda i, j, te, tr: (i, 0))),
        out_shape=jax.ShapeDtypeStruct((n_rows, d), F32),
        compiler_params=_cparams("arbitrary", "arbitrary"), name="moe_ffn")(
            tile_expert, tile_rows, xg, wg, wu, wd)


def _combine_ln_kernel(pos_ref, h_ref, gate_ref, g_ref, b_ref, y_hbm, o_ref, buf, sem):
    groups, sub, d = buf.shape[2:]
    rows = groups * sub
    i, n = pl.program_id(0), pl.num_programs(0)

    def issue(step, slot):
        def start(g, _):
            for s in range(sub):
                for k in range(TOP_K):
                    src = pos_ref[TOP_K * (step * rows + g * sub + s) + k]
                    _row_copy(y_hbm, src, buf.at[slot, k, g], s, sem.at[slot, k]).start()
            return 0
        lax.fori_loop(0, groups, start, 0)

    @pl.when(i == 0)
    def _():
        issue(0, 0)

    @pl.when(i + 1 < n)
    def _():
        issue(i + 1, (i + 1) % 2)

    slot = i % 2

    def wait(g, _):
        for k in range(TOP_K):
            _wait_group(y_hbm, buf.at[slot, k, g], sem.at[slot, k])
        return 0
    lax.fori_loop(0, groups, wait, 0)
    gate = gate_ref[...]
    y = None
    for k in range(TOP_K):
        term = buf[slot, k].reshape(rows, d) * gate[:, k:k + 1]
        y = term if y is None else y + term
    o_ref[...] = _layer_norm(DEEPNORM_ALPHA * h_ref[...] + y, g_ref[...], b_ref[...])


def _moe_combine_ln(y_rows, pos, h, gates, g, b):
    m, d = h.shape
    tm = _tile(m, MOE_COMBINE_ROWS)
    row = lambda i, p: (i, 0)
    fixed = lambda i, p: (0, 0)
    return pl.pallas_call(
        _combine_ln_kernel,
        grid_spec=pltpu.PrefetchScalarGridSpec(
            num_scalar_prefetch=1, grid=(m // tm,),
            in_specs=[pl.BlockSpec((tm, d), row), pl.BlockSpec((tm, LANE), row),
                      pl.BlockSpec((1, d), fixed), pl.BlockSpec((1, d), fixed),
                      pl.BlockSpec(memory_space=pl.ANY)],
            out_specs=pl.BlockSpec((tm, d), row),
            scratch_shapes=[pltpu.VMEM((2, TOP_K, tm // F32_SUBLANES, F32_SUBLANES, d), F32),
                            pltpu.SemaphoreType.DMA((2, TOP_K))]),
        out_shape=jax.ShapeDtypeStruct((m, d), F32),
        compiler_params=_cparams("arbitrary"), name="moe_combine_ln")(
            pos, h, gates, g.reshape(1, d), b.reshape(1, d), y_rows)


def _moe_routing(top_idx, n_experts, tm):
    n_assign = top_idx.size
    expert_of = top_idx.reshape(n_assign)
    onehot = (expert_of[:, None] == jnp.arange(n_experts, dtype=jnp.int32)[None, :]).astype(jnp.int32)
    csum = jnp.cumsum(onehot, axis=0)
    rank = jnp.sum(onehot * (csum - 1), axis=1)
    counts = csum[-1]
    padded = ((counts + tm - 1) // tm) * tm
    padded_end = jnp.cumsum(padded)
    padded_start = padded_end - padded
    pos = (padded_start[expert_of] + rank).astype(jnp.int32)
    n_tiles = -(-n_assign // tm) + n_experts
    row_token = jnp.zeros((n_tiles * tm,), jnp.int32).at[pos].set(
        jnp.arange(n_assign, dtype=jnp.int32) // TOP_K, unique_indices=True, mode="promise_in_bounds")
    tile_start = jnp.arange(n_tiles, dtype=jnp.int32) * tm
    tile_used = tile_start < padded_end[-1]
    tile_expert = jnp.minimum(jnp.searchsorted(padded_end, tile_start, side='right'), n_experts - 1)
    tile_rows = jnp.clip(counts[tile_expert] - (tile_start - padded_start[tile_expert]), 0, tm)
    tile_rows = jnp.where(tile_used, tile_rows, 0).astype(jnp.int32)
    last_expert = jnp.max(jnp.where(tile_used, tile_expert, 0))
    tile_expert = jnp.where(tile_used, tile_expert, last_expert).astype(jnp.int32)
    return pos, row_token, tile_expert, tile_rows


def _rotate_half_cols(w):
    half = w.shape[-1] // 2
    return jnp.concatenate([-w[..., half:], w[..., :half]], axis=-1)


def kernel(x, positions, w_in0, b_forget, sinks_b, w_out0, ln0_mix_g, ln0_mix_b, w_ffn_gate, w_ffn_up, w_ffn_down, ln0_ffn_g, ln0_ffn_b, w_in1, q_norm_g, w_uq, kv_norm_g, w_ukv, w_out1, ln1_mix_g, ln1_mix_b, w_router, w_moe_gate, w_moe_up, w_moe_down, ln1_ffn_g, ln1_ffn_b):
    batch, seq, d = x.shape
    n = batch * seq
    fox_heads = b_forget.shape[0]
    fox_w = fox_heads * HEAD_DIM
    swa_q = sinks_b.shape[0]
    swa_qw = swa_q * HEAD_DIM
    swa_kvw = (w_in0.shape[1] - 3 * fox_w - fox_heads - swa_qw) // 2
    swa_kv = swa_kvw // HEAD_DIM
    q_rank, kv_rank = q_norm_g.shape[0], kv_norm_g.shape[0]
    mla_heads = w_uq.shape[1] // (MLA_NOPE_DIM + MLA_ROPE_DIM)
    n_experts = w_router.shape[1]

    xf = x.reshape(n, d)
    cos_a, sin_a, tab_m = _rope_tables(positions)

    c0 = 3 * fox_w
    c1 = c0 + fox_heads
    w_qkv = jnp.concatenate([w_in0[:, :c0], w_in0[:, c1:]], axis=1).astype(BF16)
    group_kinds = ([SCALED] * fox_heads + [PLAIN] * (2 * fox_heads)
                   + [ROTARY] * (swa_q + swa_kv) + [PLAIN] * swa_kv)
    qkv = _project(xf, w_qkv, group_kinds, HEAD_DIM ** -0.5 * LOG2_E, cos_a, sin_a, PROJ_ROWS, PROJ_COLS,
                   "proj0")
    cum_log_f = _fox_cum_log_forget(xf, w_in0[:, c0:c1], b_forget, batch, seq)
    o_a, w_moe_up_bf = _causal_attention(qkv, qkv, qkv, 0, fox_heads, 2 * fox_heads, HEAD_DIM, HEAD_DIM,
                                         fox_heads, batch, seq, ck=cum_log_f, cast_rider=w_moe_up,
                                         name="fox_attn")
    o_b, ffn_w_bf = _swa_attention(qkv, c0, c0 + swa_qw, c0 + swa_qw + swa_kvw, sinks_b, swa_q, swa_kv,
                                   batch, seq, HEAD_DIM ** -0.5,
                                   cast_riders=[w_ffn_gate, w_ffn_up, w_ffn_down])
    w_out0_bf = w_out0.astype(BF16)
    (h1,) = _matmul_residual_ln([(o_a, w_out0_bf[:fox_w]), (o_b, w_out0_bf[fox_w:])], xf,
                                ln0_mix_g, ln0_mix_b, "out0_ln")
    h2 = _swiglu_residual_ln(h1, *ffn_w_bf, ln0_ffn_g, ln0_ffn_b)

    r0 = q_rank + kv_rank
    w_in1_x = jnp.concatenate([w_in1, _rotate_half_cols(w_in1[:, r0:])], axis=1).astype(BF16)
    w_uq_h = w_uq.reshape(q_rank, mla_heads, MLA_NOPE_DIM + MLA_ROPE_DIM)
    w_uq_x = jnp.concatenate([w_uq_h, _rotate_half_cols(w_uq_h[..., MLA_NOPE_DIM:])], axis=-1)
    q_m, k_m, v_m = _mla_projections(
        h2, w_in1_x, q_norm_g, kv_norm_g, tab_m,
        w_uq_x.reshape(q_rank, mla_heads * MLA_QK_PAD).astype(BF16), w_ukv.astype(BF16),
        q_rank, kv_rank, mla_heads, (MLA_NOPE_DIM + MLA_ROPE_DIM) ** -0.5 * LOG2_E)
    o_m, w_moe_gate_bf = _causal_attention(q_m, k_m, v_m, 0, 0, 0, MLA_QK_PAD, MLA_V_DIM, mla_heads, batch,
                                           seq, cast_rider=w_moe_gate, name="mla_attn")
    h3, idx, gates = _matmul_residual_ln([(o_m, w_out1.astype(BF16))], h2, ln1_mix_g, ln1_mix_b,
                                         "out1_ln_route", w_router=w_router)

    moe_tm = _tile(n * TOP_K, MOE_ROWS)
    pos, row_token, tile_expert, tile_rows = _moe_routing(idx[:, :TOP_K], n_experts, moe_tm)
    xg = _gather_rows(h3, row_token, _tile(moe_tm, MOE_GATHER_ROWS))
    y_rows = _moe_ffn(xg, w_moe_gate_bf, w_moe_up_bf, w_moe_down, tile_expert, tile_rows, moe_tm)
    out = _moe_combine_ln(y_rows, pos, h3, gates, ln1_ffn_g, ln1_ffn_b)
    return out.reshape(batch, seq, d)
```

```python
import functools
import math

import jax
import jax.numpy as jnp
from jax import lax
from jax.experimental import pallas as pl
from jax.experimental.pallas import tpu as pltpu

F32 = jnp.float32
BF16 = jnp.bfloat16

LANE = 128
F32_SUBLANES = 8
BF16_SUBLANES = 16
HEAD_DIM = 128
SWA_WINDOW = 128
MLA_NOPE_DIM = 128
MLA_ROPE_DIM = 64
MLA_V_DIM = 128
MLA_QK_PAD = 256
ROPE_THETA = 10000.0
TOP_K = 2
LN_EPS = 1e-5
RMS_EPS = 1e-6
DEPTH = 2
DEEPNORM_ALPHA = (2 * DEPTH) ** 0.25
LOG2_E = math.log2(math.e)

NT_DIMS = (((1,), (1,)), ((), ()))

ROPE_TABLE_ROWS = 1024
PROJ_ROWS, PROJ_COLS = 1024, 1152
GATE_CHUNK = 512
ATTN_TILE = 512
ATTN_HEADS_PER_STEP = 4
OUT_PROJ_ROWS, LN_ROW_GROUP = 512, 256
FFN_ROWS, FFN_COLS = 512, 512
MLA_PROJ_ROWS = 512
MOE_ROWS, MOE_COLS = 512, 1024
MOE_GATHER_ROWS, MOE_COMBINE_ROWS = 512, 256
MOE_TILE_PARTS = 8
N_DMA_PRIORITIES = 2


def _tile(n, pref):
    if n <= pref:
        return n
    t = (pref // LANE) * LANE
    while n % t:
        t -= LANE
    return t


def _cparams(*sem):
    return pltpu.CompilerParams(dimension_semantics=sem)


def _dot(a, b):
    return jnp.dot(a, b, preferred_element_type=F32)


def _layer_norm(y, g, b):
    mu = jnp.mean(y, axis=-1, keepdims=True)
    d = y - mu
    var = jnp.mean(d * d, axis=-1, keepdims=True)
    return d * lax.rsqrt(var + LN_EPS) * g + b


def _silu(g):
    return g / (1.0 + jnp.exp(-g))


def _rope_table_kernel(pos_ref, freq_ref, cos_a_ref, sin_a_ref, tab_m_ref):
    pos = pos_ref[...]
    ang_a = pos * freq_ref[0:1, :]
    lane = lax.broadcasted_iota(jnp.int32, ang_a.shape, 1)
    cos_a_ref[...] = jnp.cos(ang_a)
    sin_a_ref[...] = jnp.where(lane < HEAD_DIM // 2, -jnp.sin(ang_a), jnp.sin(ang_a))
    ang_m = pos * freq_ref[1:2, :]
    tab_m_ref[...] = jnp.where(lane < MLA_ROPE_DIM, jnp.cos(ang_m), jnp.sin(ang_m))


def _rope_tables(positions):
    n = positions.size
    pos = positions.astype(F32).reshape(n, 1)
    half_a = HEAD_DIM // 2
    half_m = MLA_ROPE_DIM // 2
    inv_a = ROPE_THETA ** (-2.0 * jnp.arange(half_a, dtype=F32) / HEAD_DIM)
    inv_m = ROPE_THETA ** (-2.0 * jnp.arange(half_m, dtype=F32) / MLA_ROPE_DIM)
    freq = jnp.stack([jnp.tile(inv_a, 2), jnp.tile(inv_m, 4)])
    tm = _tile(n, ROPE_TABLE_ROWS)
    out = jax.ShapeDtypeStruct((n, LANE), F32)
    row = pl.BlockSpec((tm, LANE), lambda i: (i, 0))
    return pl.pallas_call(
        _rope_table_kernel, grid=(n // tm,),
        in_specs=[pl.BlockSpec((tm, 1), lambda i: (i, 0)), pl.BlockSpec((2, LANE), lambda i: (0, 0))],
        out_specs=[row, row, row], out_shape=[out, out, out],
        compiler_params=_cparams("parallel"), name="rope_tables")(pos, freq)


PLAIN, SCALED, ROTARY = "plain", "scaled", "rotary"


def _proj_kernel(x_ref, w_ref, cos_ref, sin_ref, o_ref, xb_ref, *, tile_kinds, scale):
    j = pl.program_id(1)

    @pl.when(j == 0)
    def _():
        xb_ref[...] = x_ref[...].astype(BF16)

    def tile(kinds):
        acc = _dot(xb_ref[...], w_ref[...])
        for c, kind in enumerate(kinds):
            cols = slice(c * HEAD_DIM, (c + 1) * HEAD_DIM)
            a = acc[:, cols]
            if kind == SCALED:
                a = a * scale
            elif kind == ROTARY:
                a = a * cos_ref[...] + pltpu.roll(a, HEAD_DIM // 2, 1) * sin_ref[...]
            o_ref[:, cols] = a.astype(o_ref.dtype)

    for kinds in sorted(set(tile_kinds)):
        first = tile_kinds.index(kinds)
        count = tile_kinds.count(kinds)
        assert tile_kinds[first:first + count] == (kinds,) * count
        pl.when((j >= first) & (j < first + count))(functools.partial(tile, kinds))


def _project(x, w, group_kinds, scale, cos, sin, tm, tn, name):
    m, k = x.shape
    n = w.shape[1]
    tm, tn = _tile(m, tm), _tile(n, tn)
    per_tile = tn // HEAD_DIM
    tile_kinds = tuple(tuple(group_kinds[t * per_tile:(t + 1) * per_tile]) for t in range(n // tn))
    return pl.pallas_call(
        functools.partial(_proj_kernel, tile_kinds=tile_kinds, scale=scale), grid=(m // tm, n // tn),
        in_specs=[pl.BlockSpec((tm, k), lambda i, j: (i, 0)), pl.BlockSpec((k, tn), lambda i, j: (0, j)),
                  pl.BlockSpec((tm, LANE), lambda i, j: (i, 0)), pl.BlockSpec((tm, LANE), lambda i, j: (i, 0))],
        out_specs=pl.BlockSpec((tm, tn), lambda i, j: (i, j)),
        out_shape=jax.ShapeDtypeStruct((m, n), BF16),
        scratch_shapes=[pltpu.VMEM((tm, k), BF16)],
        compiler_params=_cparams("parallel", "arbitrary"), name=name)(x, w, cos, sin)


def _split3(x):
    hi = x.astype(BF16)
    r = x - hi.astype(F32)
    mid = r.astype(BF16)
    lo = (r - mid.astype(F32)).astype(BF16)
    return hi, mid, lo


def _fox_gate_kernel(x_ref, wf_ref, bf_ref, o_ref, carry_ref, *, n_heads):
    @pl.when(pl.program_id(1) == 0)
    def _():
        carry_ref[...] = jnp.zeros_like(carry_ref)

    f = lax.dot_general(wf_ref[...], x_ref[...].astype(BF16), NT_DIMS, preferred_element_type=F32)
    z = f + bf_ref[...]
    log_f = jnp.minimum(z, 0.0) - jnp.log1p(jnp.exp(-jnp.abs(z)))
    tc = z.shape[1]
    src = lax.broadcasted_iota(jnp.int32, (tc, tc), 0)
    dst = lax.broadcasted_iota(jnp.int32, (tc, tc), 1)
    tri = jnp.where(src <= dst, 1.0, 0.0).astype(BF16)
    hi, mid, lo = _split3(log_f)
    cum = _dot(hi, tri) + _dot(mid, tri) + _dot(lo, tri) + carry_ref[:, 0:1]
    o_ref[0] = cum[:n_heads] * LOG2_E
    carry_ref[...] = jnp.broadcast_to(cum[:, tc - 1:tc], carry_ref.shape)


def _fox_cum_log_forget(x, w_f, b_forget, batch, seq):
    n, d = x.shape
    n_heads = w_f.shape[1]
    rows = BF16_SUBLANES
    wf_t = jnp.zeros((rows, d), BF16).at[:n_heads].set(w_f.T.astype(BF16))
    bf = jnp.zeros((rows, 1), F32).at[:n_heads, 0].set(b_forget.astype(F32))
    tc = _tile(seq, GATE_CHUNK)
    nc = seq // tc
    return pl.pallas_call(
        functools.partial(_fox_gate_kernel, n_heads=n_heads), grid=(batch, nc),
        in_specs=[pl.BlockSpec((tc, d), lambda b, s: (b * nc + s, 0)),
                  pl.BlockSpec((rows, d), lambda b, s: (0, 0)),
                  pl.BlockSpec((rows, 1), lambda b, s: (0, 0))],
        out_specs=pl.BlockSpec((1, n_heads, tc), lambda b, s: (b, 0, s)),
        out_shape=jax.ShapeDtypeStruct((batch, n_heads, seq), F32),
        scratch_shapes=[pltpu.VMEM((rows, LANE), F32)],
        compiler_params=_cparams("parallel", "arbitrary"), name="fox_gate")(x, wf_t, bf)


def _flash_kernel(*refs, tk, dk, dv, heads, has_bias, n_cast_slabs):
    refs = list(refs)
    q_ref, k_ref, v_ref = refs[:3]
    del refs[:3]
    ck_ref = refs.pop(0) if has_bias else None
    w_ref = refs.pop(0) if n_cast_slabs else None
    o_ref = refs.pop(0)
    qi = pl.program_id(2)
    tq = q_ref.shape[0]
    qs = [q_ref[:, g * dk:(g + 1) * dk] for g in range(heads)]

    if n_cast_slabs:
        step = (pl.program_id(0) * pl.num_programs(1) + pl.program_id(1)) * pl.num_programs(2) + qi
        _cast_rider_step(step, n_cast_slabs, w_ref, refs[0])

    def step(g, state, j, masked):
        m, l, acc = state
        kv_rows = pl.ds(pl.multiple_of(j * tk, tk), tk)
        s = lax.dot_general(qs[g], k_ref[kv_rows, g * dk:(g + 1) * dk], NT_DIMS, preferred_element_type=F32)
        if has_bias:
            s = s - ck_ref[0, g, j]
        if masked:
            row = lax.broadcasted_iota(jnp.int32, s.shape, 0)
            col = lax.broadcasted_iota(jnp.int32, s.shape, 1)
            s = jnp.where(row >= col, s, -jnp.inf)
        m_new = jnp.maximum(m, jnp.max(s, axis=-1, keepdims=True))
        a = jnp.exp2(m - m_new)
        p = jnp.exp2(s - m_new)
        l = a * l + jnp.sum(p, axis=-1, keepdims=True)
        acc = a * acc + _dot(p.astype(BF16), v_ref[kv_rows, g * dv:(g + 1) * dv])
        return m_new, l, acc

    def body(j, carry):
        return tuple(step(g, carry[g], j, False) for g in range(heads))

    init = (jnp.full((tq, 1), -jnp.inf, F32), jnp.zeros((tq, 1), F32), jnp.zeros((tq, dv), F32))
    carry = lax.fori_loop(0, qi, body, (init,) * heads)
    for g in range(heads):
        _, l, acc = step(g, carry[g], qi, True)
        o_ref[:, g * dv:(g + 1) * dv] = (acc / l).astype(o_ref.dtype)


def _slab_rows(total_rows, max_slabs):
    rows = BF16_SUBLANES
    while total_rows % rows or total_rows // rows > max_slabs:
        rows += BF16_SUBLANES
    return rows


def _cast_rider(w, grid, step_of):
    flat = w.reshape(-1, w.shape[-1])
    rows = _slab_rows(flat.shape[0], math.prod(grid))
    n_slabs = flat.shape[0] // rows
    spec = pl.BlockSpec((rows, flat.shape[1]), lambda *g: (jnp.minimum(step_of(*g), n_slabs - 1), 0))
    return flat, spec, jax.ShapeDtypeStruct(flat.shape, BF16), n_slabs


def _cast_rider_step(step, n_slabs, w_ref, o_ref):
    @pl.when(step < n_slabs)
    def _():
        o_ref[...] = w_ref[...].astype(o_ref.dtype)


def _causal_attention(q_arr, k_arr, v_arr, q_col, k_col, v_col, dk, dv, n_heads, batch, seq,
                      ck=None, cast_rider=None, name="attn"):
    heads = ATTN_HEADS_PER_STEP
    assert n_heads % heads == 0 and q_col % heads == 0 and k_col % heads == 0 and v_col % heads == 0
    t = _tile(seq, ATTN_TILE)
    nq = seq // t
    grid = (batch, n_heads // heads, nq)
    in_specs = [pl.BlockSpec((t, heads * dk), lambda b, h, i: (b * nq + i, q_col // heads + h)),
                pl.BlockSpec((seq, heads * dk), lambda b, h, i: (b, k_col // heads + h)),
                pl.BlockSpec((seq, heads * dv), lambda b, h, i: (b, v_col // heads + h))]
    args = [q_arr, k_arr, v_arr]
    out_specs = [pl.BlockSpec((t, heads * dv), lambda b, h, i: (b * nq + i, h))]
    out_shape = [jax.ShapeDtypeStruct((batch * seq, n_heads * dv), BF16)]
    if ck is not None:
        args.append(ck.reshape(batch, n_heads, nq, 1, t))
        in_specs.append(pl.BlockSpec((1, heads, nq, 1, t), lambda b, h, i: (b, h, 0, 0, 0)))
    n_slabs = 0
    if cast_rider is not None:
        flat, slab, flat_bf, n_slabs = _cast_rider(cast_rider, grid, lambda b, h, i: (b * grid[1] + h) * nq + i)
        args.append(flat)
        in_specs.append(slab)
        out_specs.append(slab)
        out_shape.append(flat_bf)
    outs = pl.pallas_call(
        functools.partial(_flash_kernel, tk=t, dk=dk, dv=dv, heads=heads, has_bias=ck is not None,
                          n_cast_slabs=n_slabs),
        grid=grid, in_specs=in_specs, out_specs=out_specs, out_shape=out_shape,
        compiler_params=_cparams("parallel", "parallel", "arbitrary"), name=name)(*args)
    if cast_rider is None:
        return outs[0]
    return outs[0], outs[1].reshape(cast_rider.shape)


def _swa_kernel(sink_ref, q_ref, kp_ref, kc_ref, vp_ref, vc_ref, *rest, scale, n_q, n_kv, rider_slabs):
    n_riders = len(rider_slabs)
    o_ref = rest[n_riders]
    step = pl.program_id(0) * pl.num_programs(1) + pl.program_id(1)
    for n_slabs, w_ref, wo_ref in zip(rider_slabs, rest[:n_riders], rest[n_riders + 1:]):
        _cast_rider_step(step, n_slabs, w_ref, wo_ref)
    w = SWA_WINDOW
    qi = lax.broadcasted_iota(jnp.int32, (w, 2 * w), 0)
    ji = lax.broadcasted_iota(jnp.int32, (w, 2 * w), 1)
    has_prev = pl.program_id(1) > 0
    mask = (ji > qi) & (ji <= qi + w) & ((ji >= w) | has_prev)
    group = n_q // n_kv
    for kvh in range(n_kv):
        cols = slice(kvh * HEAD_DIM, (kvh + 1) * HEAD_DIM)
        k = jnp.concatenate([kp_ref[:, cols], kc_ref[:, cols]], axis=0)
        v = jnp.concatenate([vp_ref[:, cols], vc_ref[:, cols]], axis=0)
        for g in range(group):
            h = kvh * group + g
            hc = slice(h * HEAD_DIM, (h + 1) * HEAD_DIM)
            s = lax.dot_general(q_ref[:, hc], k, NT_DIMS, preferred_element_type=F32) * scale
            s = jnp.where(mask, s, -jnp.inf)
            sink = sink_ref[h]
            m = jnp.maximum(jnp.max(s, axis=-1, keepdims=True), sink)
            p = jnp.exp(s - m)
            denom = jnp.sum(p, axis=-1, keepdims=True) + jnp.exp(sink - m)
            o_ref[:, hc] = (_dot(p.astype(BF16), v) / denom).astype(o_ref.dtype)


def _swa_attention(qkv, q_off, k_off, v_off, sinks, n_q, n_kv, batch, seq, scale, cast_riders):
    w = SWA_WINDOW
    nb = seq // w
    grid = (batch, nb)
    qw, kw = n_q * HEAD_DIM, n_kv * HEAD_DIM
    assert q_off % qw == 0 and k_off % kw == 0 and v_off % kw == 0
    q_col, k_col, v_col = q_off // qw, k_off // kw, v_off // kw
    cur = lambda b, n: b * nb + n
    prev = lambda b, n: b * nb + jnp.maximum(n - 1, 0)
    riders = [_cast_rider(r, grid, cur) for r in cast_riders]
    outs = pl.pallas_call(
        functools.partial(_swa_kernel, scale=scale, n_q=n_q, n_kv=n_kv,
                          rider_slabs=tuple(r[3] for r in riders)), grid=grid,
        in_specs=[pl.BlockSpec(memory_space=pltpu.SMEM),
                  pl.BlockSpec((w, qw), lambda b, n: (cur(b, n), q_col)),
                  pl.BlockSpec((w, kw), lambda b, n: (prev(b, n), k_col)),
                  pl.BlockSpec((w, kw), lambda b, n: (cur(b, n), k_col)),
                  pl.BlockSpec((w, kw), lambda b, n: (prev(b, n), v_col)),
                  pl.BlockSpec((w, kw), lambda b, n: (cur(b, n), v_col))] + [r[1] for r in riders],
        out_specs=[pl.BlockSpec((w, qw), lambda b, n: (cur(b, n), 0))] + [r[1] for r in riders],
        out_shape=[jax.ShapeDtypeStruct((batch * seq, qw), BF16)] + [r[2] for r in riders],
        compiler_params=_cparams("arbitrary", "arbitrary"), name="swa_attn")(
            sinks.astype(F32), qkv, qkv, qkv, qkv, qkv, *[r[0] for r in riders])
    return outs[0], [o.reshape(r.shape) for o, r in zip(outs[1:], cast_riders)]


def _top2_route(x, w, n_experts):
    xh = x.astype(BF16)
    xl = (x - xh.astype(F32)).astype(BF16)
    wh = w.astype(BF16)
    wl = (w - wh.astype(F32)).astype(BF16)
    hi = _dot(xh, jnp.concatenate([wh, wl], axis=1))
    logits = hi[:, :LANE] + hi[:, LANE:] + _dot(xl, wh)
    lane = lax.broadcasted_iota(jnp.int32, logits.shape, 1)
    lane_f = lane.astype(F32)
    l1 = jnp.where(lane < n_experts, logits, -jnp.inf)
    v1 = jnp.max(l1, axis=-1, keepdims=True)
    i1 = jnp.min(jnp.where(l1 == v1, lane_f, float(LANE)), axis=-1, keepdims=True)
    l2 = jnp.where(lane_f == i1, -jnp.inf, l1)
    v2 = jnp.max(l2, axis=-1, keepdims=True)
    i2 = jnp.min(jnp.where(l2 == v2, lane_f, float(LANE)), axis=-1, keepdims=True)
    e2 = jnp.exp(v2 - v1)
    idx = jnp.where(lane == 0, i1, jnp.where(lane == 1, i2, 0.0)).astype(jnp.int32)
    gate = jnp.where(lane == 0, 1.0 / (1.0 + e2), jnp.where(lane == 1, e2 / (1.0 + e2), 0.0))
    return idx, gate


def _mm_res_ln_kernel(*refs, n_pairs, n_experts):
    a_refs, w_refs = refs[:n_pairs], refs[n_pairs:2 * n_pairs]
    if n_experts:
        res_ref, g_ref, b_ref, wr_ref, o_ref, idx_ref, gate_ref = refs[2 * n_pairs:]
    else:
        res_ref, g_ref, b_ref, o_ref = refs[2 * n_pairs:]
    tm = o_ref.shape[0]
    sub = _tile(tm, LN_ROW_GROUP)
    for r in range(tm // sub):
        rows = slice(r * sub, (r + 1) * sub)
        acc = _dot(a_refs[0][rows, :], w_refs[0][...])
        for a_ref, w_ref in zip(a_refs[1:], w_refs[1:]):
            acc = acc + _dot(a_ref[rows, :], w_ref[...])
        y = _layer_norm(DEEPNORM_ALPHA * res_ref[rows, :] + acc, g_ref[...], b_ref[...])
        o_ref[rows, :] = y
        if n_experts:
            idx_ref[rows, :], gate_ref[rows, :] = _top2_route(y, wr_ref[...], n_experts)


def _matmul_residual_ln(pairs, res, g, b, name, w_router=None):
    m, d = res.shape
    tm = _tile(m, OUT_PROJ_ROWS)
    row = lambda i: (i, 0)
    fixed = lambda i: (0, 0)
    resident = lambda shape: pl.BlockSpec(shape, fixed, pipeline_mode=pl.Buffered(1))
    in_specs = ([pl.BlockSpec((tm, a.shape[1]), row) for a, _ in pairs] + [resident(w.shape) for _, w in pairs]
                + [pl.BlockSpec((tm, d), row), pl.BlockSpec((1, d), fixed), pl.BlockSpec((1, d), fixed)])
    args = [a for a, _ in pairs] + [w for _, w in pairs] + [res, g.reshape(1, d), b.reshape(1, d)]
    out_specs = [pl.BlockSpec((tm, d), row)]
    out_shape = [jax.ShapeDtypeStruct((m, d), F32)]
    n_experts = 0
    if w_router is not None:
        n_experts = w_router.shape[1]
        in_specs.append(resident((d, LANE)))
        args.append(jnp.zeros((d, LANE), F32).at[:, :n_experts].set(w_router.astype(F32)))
        out_specs += [pl.BlockSpec((tm, LANE), row), pl.BlockSpec((tm, LANE), row)]
        out_shape += [jax.ShapeDtypeStruct((m, LANE), jnp.int32), jax.ShapeDtypeStruct((m, LANE), F32)]
    return pl.pallas_call(
        functools.partial(_mm_res_ln_kernel, n_pairs=len(pairs), n_experts=n_experts),
        grid=(m // tm,), in_specs=in_specs, out_specs=out_specs, out_shape=out_shape,
        compiler_params=_cparams("parallel"), name=name)(*args)


def _swiglu(x, wg, wu, wd):
    a = _silu(_dot(x, wg)) * _dot(x, wu)
    return _dot(a.astype(BF16), wd)


FFN_WEIGHT_SLOTS = 3


def _ffn_ln_kernel(h_ref, g_ref, b_ref, wg_hbm, wu_hbm, wd_hbm, o_ref, x_ref, wg_buf, wu_buf, wd_buf, sem, *,
                   n_chunks):
    i, n = pl.program_id(0), pl.num_programs(0)
    tf = wg_buf.shape[2]

    def copies(c):
        slot, cols = c % FFN_WEIGHT_SLOTS, pl.ds(c * tf, tf)
        return (pltpu.make_async_copy(wg_hbm.at[:, cols], wg_buf.at[slot], sem.at[0, slot]),
                pltpu.make_async_copy(wu_hbm.at[:, cols], wu_buf.at[slot], sem.at[1, slot]),
                pltpu.make_async_copy(wd_hbm.at[cols, :], wd_buf.at[slot], sem.at[2, slot]))

    def start(c):
        for copy in copies(c):
            copy.start()

    @pl.when(i == 0)
    def _():
        start(0)

    x_ref[...] = h_ref[...].astype(BF16)
    for c in range(n_chunks):
        for copy in copies(c):
            copy.wait()
        if c + 1 < n_chunks:
            start(c + 1)
        else:
            pl.when(i + 1 < n)(functools.partial(start, 0))
        slot = c % FFN_WEIGHT_SLOTS
        y = _swiglu(x_ref[...], wg_buf[slot], wu_buf[slot], wd_buf[slot])
        if c == 0:
            o_ref[...] = y
        else:
            o_ref[...] += y
    o_ref[...] = _layer_norm(DEEPNORM_ALPHA * h_ref[...] + o_ref[...], g_ref[...], b_ref[...])


def _swiglu_residual_ln(h, wg, wu, wd, g, b):
    m, d = h.shape
    f = wg.shape[1]
    tm, tf = _tile(m, FFN_ROWS), _tile(f, FFN_COLS)
    while (f // tf) % FFN_WEIGHT_SLOTS == 1:
        tf //= 2
    assert tf % LANE == 0 and f % tf == 0
    row = lambda i: (i, 0)
    fixed = lambda i: (0, 0)
    hbm = pl.BlockSpec(memory_space=pl.ANY)
    return pl.pallas_call(
        functools.partial(_ffn_ln_kernel, n_chunks=f // tf), grid=(m // tm,),
        in_specs=[pl.BlockSpec((tm, d), row), pl.BlockSpec((1, d), fixed), pl.BlockSpec((1, d), fixed),
                  hbm, hbm, hbm],
        out_specs=pl.BlockSpec((tm, d), row),
        out_shape=jax.ShapeDtypeStruct((m, d), F32),
        scratch_shapes=[pltpu.VMEM((tm, d), BF16),
                        pltpu.VMEM((FFN_WEIGHT_SLOTS, d, tf), BF16), pltpu.VMEM((FFN_WEIGHT_SLOTS, d, tf), BF16),
                        pltpu.VMEM((FFN_WEIGHT_SLOTS, tf, d), BF16),
                        pltpu.SemaphoreType.DMA((3, FFN_WEIGHT_SLOTS))],
        compiler_params=_cparams("arbitrary"), name="ffn_ln")(
            h, g.reshape(1, d), b.reshape(1, d), wg, wu, wd)


def _rms_norm(c, g):
    ms = jnp.mean(c * c, axis=-1, keepdims=True)
    return c * lax.rsqrt(ms + RMS_EPS) * g


def _rope64(pair, tab):
    w = pair * tab
    lane = lax.broadcasted_iota(jnp.int32, w.shape, 1)
    return jnp.where(lane < MLA_ROPE_DIM, w + pltpu.roll(w, MLA_ROPE_DIM, 1), 0.0)


def _mla_proj_kernel(x_ref, win_ref, gq_ref, gkv_ref, tab_ref, wq_ref, wkv_ref, q_ref, k_ref, v_ref, *,
                     q_rank, kv_rank, n_heads, q_scale):
    acc = _dot(x_ref[...].astype(BF16), win_ref[...])
    cq = _rms_norm(acc[:, :q_rank], gq_ref[...]).astype(BF16)
    ckv = _rms_norm(acc[:, q_rank:q_rank + kv_rank], gkv_ref[...]).astype(BF16)
    tab = tab_ref[...]
    k_pe = _rope64(acc[:, q_rank + kv_rank:], tab).astype(BF16)
    wkv_w = MLA_NOPE_DIM + MLA_V_DIM
    for h in range(n_heads):
        q0 = h * MLA_QK_PAD
        qh = _dot(cq, wq_ref[:, q0:q0 + MLA_QK_PAD]) * q_scale
        q_ref[:, q0:q0 + MLA_NOPE_DIM] = qh[:, :MLA_NOPE_DIM].astype(BF16)
        q_ref[:, q0 + MLA_NOPE_DIM:q0 + MLA_QK_PAD] = _rope64(qh[:, MLA_NOPE_DIM:], tab).astype(BF16)
        kvh = _dot(ckv, wkv_ref[:, h * wkv_w:(h + 1) * wkv_w])
        k_ref[:, q0:q0 + MLA_NOPE_DIM] = kvh[:, :MLA_NOPE_DIM].astype(BF16)
        k_ref[:, q0 + MLA_NOPE_DIM:q0 + MLA_QK_PAD] = k_pe
        v_ref[:, h * MLA_V_DIM:(h + 1) * MLA_V_DIM] = kvh[:, MLA_NOPE_DIM:].astype(BF16)


def _mla_projections(x, w_in, gq, gkv, tab, w_q, w_kv, q_rank, kv_rank, n_heads, q_scale):
    m, d = x.shape
    tm = _tile(m, MLA_PROJ_ROWS)
    row = lambda i: (i, 0)
    fixed = lambda i: (0, 0)
    resident = lambda shape: pl.BlockSpec(shape, fixed, pipeline_mode=pl.Buffered(1))
    qk_w, v_w = n_heads * MLA_QK_PAD, n_heads * MLA_V_DIM
    return pl.pallas_call(
        functools.partial(_mla_proj_kernel, q_rank=q_rank, kv_rank=kv_rank, n_heads=n_heads, q_scale=q_scale),
        grid=(m // tm,),
        in_specs=[pl.BlockSpec((tm, d), row), resident(w_in.shape),
                  pl.BlockSpec((1, q_rank), fixed), pl.BlockSpec((1, kv_rank), fixed),
                  pl.BlockSpec((tm, LANE), row), resident(w_q.shape), resident(w_kv.shape)],
        out_specs=[pl.BlockSpec((tm, qk_w), row), pl.BlockSpec((tm, qk_w), row), pl.BlockSpec((tm, v_w), row)],
        out_shape=[jax.ShapeDtypeStruct((m, qk_w), BF16), jax.ShapeDtypeStruct((m, qk_w), BF16),
                   jax.ShapeDtypeStruct((m, v_w), BF16)],
        compiler_params=_cparams("parallel"), name="mla_proj")(
            x, w_in, gq.reshape(1, q_rank).astype(F32), gkv.reshape(1, kv_rank).astype(F32), tab, w_q, w_kv)


def _row_copy(src_hbm, src_row, group_buf, sublane, sem):
    return pltpu.make_async_copy(src_hbm.at[pl.ds(src_row, 1), :], group_buf.at[pl.ds(sublane, 1), :], sem)


def _wait_group(src_hbm, group_buf, sem):
    pltpu.make_async_copy(src_hbm.at[pl.ds(0, group_buf.shape[0]), :], group_buf, sem).wait()


def _gather_rows_kernel(tok_ref, x_hbm, o_ref, buf, sem):
    i, n = pl.program_id(0), pl.num_programs(0)
    groups, sub, d = buf.shape[1:]
    rows = groups * sub

    def issue(step, slot):
        def start(g, _):
            for s in range(sub):
                copy = _row_copy(x_hbm, tok_ref[step * rows + g * sub + s], buf.at[slot, g], s, sem.at[slot])
                copy.start(priority=s % N_DMA_PRIORITIES)
            return 0
        lax.fori_loop(0, groups, start, 0)

    @pl.when(i == 0)
    def _():
        issue(0, 0)

    @pl.when(i + 1 < n)
    def _():
        issue(i + 1, (i + 1) % 2)

    slot = i % 2

    def wait(g, _):
        _wait_group(x_hbm, buf.at[slot, g], sem.at[slot])
        return 0
    lax.fori_loop(0, groups, wait, 0)
    o_ref[...] = buf[slot].reshape(rows, d).astype(o_ref.dtype)


def _gather_rows(h, row_token, rows_per_step):
    n_rows = row_token.shape[0]
    d = h.shape[1]
    return pl.pallas_call(
        _gather_rows_kernel,
        grid_spec=pltpu.PrefetchScalarGridSpec(
            num_scalar_prefetch=1, grid=(n_rows // rows_per_step,),
            in_specs=[pl.BlockSpec(memory_space=pl.ANY)],
            out_specs=pl.BlockSpec((rows_per_step, d), lambda i, tok: (i, 0)),
            scratch_shapes=[pltpu.VMEM((2, rows_per_step // F32_SUBLANES, F32_SUBLANES, d), F32),
                            pltpu.SemaphoreType.DMA((2,))]),
        out_shape=jax.ShapeDtypeStruct((n_rows, d), BF16),
        compiler_params=_cparams("arbitrary"), name="moe_gather")(row_token, h)


def _moe_ffn_kernel(te_ref, tr_ref, x_ref, wg_ref, wu_ref, wd_ref, o_ref):
    i, j = pl.program_id(0), pl.program_id(1)
    part = x_ref.shape[0] // MOE_TILE_PARTS
    parts_used = (tr_ref[i] + part - 1) // part

    @pl.when(j == 0)
    def _():
        o_ref[...] = jnp.zeros_like(o_ref)

    for used in range(1, MOE_TILE_PARTS + 1):
        @pl.when(parts_used == used)
        def _():
            rows = slice(0, used * part)
            o_ref[rows, :] += _swiglu(x_ref[rows, :], wg_ref[0], wu_ref[0], wd_ref[0].astype(BF16))


def _moe_ffn(xg, wg, wu, wd, tile_expert, tile_rows, tm):
    n_rows, d = xg.shape
    f = wg.shape[2]
    tf = _tile(f, MOE_COLS)
    nj = f // tf
    col = lambda i, j, te, tr: jnp.where(tr[i] > 0, j, nj - 1)
    return pl.pallas_call(
        _moe_ffn_kernel,
        grid_spec=pltpu.PrefetchScalarGridSpec(
            num_scalar_prefetch=2, grid=(n_rows // tm, nj),
            in_specs=[pl.BlockSpec((tm, d), lambda i, j, te, tr: (i, 0)),
                      pl.BlockSpec((1, d, tf), lambda i, j, te, tr: (te[i], 0, col(i, j, te, tr))),
                      pl.BlockSpec((1, d, tf), lambda i, j, te, tr: (te[i], 0, col(i, j, te, tr))),
                      pl.BlockSpec((1, tf, d), lambda i, j, te, tr: (te[i], col(i, j, te, tr), 0))],
            out_specs=pl.BlockSpec((tm, d), lambda i, j, te, tr: (i, 0))),
        out_shape=jax.ShapeDtypeStruct((n_rows, d), F32),
        compiler_params=_cparams("arbitrary", "arbitrary"), name="moe_ffn")(
            tile_expert, tile_rows, xg, wg, wu, wd)


def _combine_ln_kernel(pos_ref, h_ref, gate_ref, g_ref, b_ref, y_hbm, o_ref, buf, sem):
    groups, sub, d = buf.shape[2:]
    rows = groups * sub
    i, n = pl.program_id(0), pl.num_programs(0)

    def issue(step, slot):
        def start(g, _):
            for s in range(sub):
                for k in range(TOP_K):
                    src = pos_ref[TOP_K * (step * rows + g * sub + s) + k]
                    _row_copy(y_hbm, src, buf.at[slot, k, g], s, sem.at[slot, k]).start(
                        priority=s % N_DMA_PRIORITIES)
            return 0
        lax.fori_loop(0, groups, start, 0)

    @pl.when(i == 0)
    def _():
        issue(0, 0)

    @pl.when(i + 1 < n)
    def _():
        issue(i + 1, (i + 1) % 2)

    slot = i % 2

    def wait(g, _):
        for k in range(TOP_K):
            _wait_group(y_hbm, buf.at[slot, k, g], sem.at[slot, k])
        return 0
    lax.fori_loop(0, groups, wait, 0)
    gate = gate_ref[...]
    y = None
    for k in range(TOP_K):
        term = buf[slot, k].reshape(rows, d) * gate[:, k:k + 1]
        y = term if y is None else y + term
    o_ref[...] = _layer_norm(DEEPNORM_ALPHA * h_ref[...] + y, g_ref[...], b_ref[...])


def _moe_combine_ln(y_rows, pos, h, gates, g, b):
    m, d = h.shape
    tm = _tile(m, MOE_COMBINE_ROWS)
    row = lambda i, p: (i, 0)
    fixed = lambda i, p: (0, 0)
    return pl.pallas_call(
        _combine_ln_kernel,
        grid_spec=pltpu.PrefetchScalarGridSpec(
            num_scalar_prefetch=1, grid=(m // tm,),
            in_specs=[pl.BlockSpec((tm, d), row), pl.BlockSpec((tm, LANE), row),
                      pl.BlockSpec((1, d), fixed), pl.BlockSpec((1, d), fixed),
                      pl.BlockSpec(memory_space=pl.ANY)],
            out_specs=pl.BlockSpec((tm, d), row),
            scratch_shapes=[pltpu.VMEM((2, TOP_K, tm // F32_SUBLANES, F32_SUBLANES, d), F32),
                            pltpu.SemaphoreType.DMA((2, TOP_K))]),
        out_shape=jax.ShapeDtypeStruct((m, d), F32),
        compiler_params=_cparams("arbitrary"), name="moe_combine_ln")(
            pos, h, gates, g.reshape(1, d), b.reshape(1, d), y_rows)


def _moe_routing(top_idx, n_experts, tm):
    n_assign = top_idx.size
    expert_of = top_idx.reshape(n_assign)
    onehot = (expert_of[:, None] == jnp.arange(n_experts, dtype=jnp.int32)[None, :]).astype(jnp.int32)
    csum = jnp.cumsum(onehot, axis=0)
    rank = jnp.sum(onehot * (csum - 1), axis=1)
    counts = csum[-1]
    padded = ((counts + tm - 1) // tm) * tm
    padded_end = jnp.cumsum(padded)
    padded_start = padded_end - padded
    pos = (padded_start[expert_of] + rank).astype(jnp.int32)
    n_tiles = -(-n_assign // tm) + n_experts
    row_token = jnp.zeros((n_tiles * tm,), jnp.int32).at[pos].set(
        jnp.arange(n_assign, dtype=jnp.int32) // TOP_K, unique_indices=True, mode="promise_in_bounds")
    tile_start = jnp.arange(n_tiles, dtype=jnp.int32) * tm
    tile_used = tile_start < padded_end[-1]
    tile_expert = jnp.minimum(jnp.searchsorted(padded_end, tile_start, side='right'), n_experts - 1)
    tile_rows = jnp.clip(counts[tile_expert] - (tile_start - padded_start[tile_expert]), 0, tm)
    tile_rows = jnp.where(tile_used, tile_rows, 0).astype(jnp.int32)
    last_expert = jnp.max(jnp.where(tile_used, tile_expert, 0))
    tile_expert = jnp.where(tile_used, tile_expert, last_expert).astype(jnp.int32)
    return pos, row_token, tile_expert, tile_rows


def _rotate_half_cols(w):
    half = w.shape[-1] // 2
    return jnp.concatenate([-w[..., half:], w[..., :half]], axis=-1)


def kernel(x, positions, w_in0, b_forget, sinks_b, w_out0, ln0_mix_g, ln0_mix_b, w_ffn_gate, w_ffn_up, w_ffn_down, ln0_ffn_g, ln0_ffn_b, w_in1, q_norm_g, w_uq, kv_norm_g, w_ukv, w_out1, ln1_mix_g, ln1_mix_b, w_router, w_moe_gate, w_moe_up, w_moe_down, ln1_ffn_g, ln1_ffn_b):
    batch, seq, d = x.shape
    n = batch * seq
    fox_heads = b_forget.shape[0]
    fox_w = fox_heads * HEAD_DIM
    swa_q = sinks_b.shape[0]
    swa_qw = swa_q * HEAD_DIM
    swa_kvw = (w_in0.shape[1] - 3 * fox_w - fox_heads - swa_qw) // 2
    swa_kv = swa_kvw // HEAD_DIM
    q_rank, kv_rank = q_norm_g.shape[0], kv_norm_g.shape[0]
    mla_heads = w_uq.shape[1] // (MLA_NOPE_DIM + MLA_ROPE_DIM)
    n_experts = w_router.shape[1]

    xf = x.reshape(n, d)
    cos_a, sin_a, tab_m = _rope_tables(positions)

    c0 = 3 * fox_w
    c1 = c0 + fox_heads
    w_qkv = jnp.concatenate([w_in0[:, :c0], w_in0[:, c1:]], axis=1).astype(BF16)
    group_kinds = ([SCALED] * fox_heads + [PLAIN] * (2 * fox_heads)
                   + [ROTARY] * (swa_q + swa_kv) + [PLAIN] * swa_kv)
    qkv = _project(xf, w_qkv, group_kinds, HEAD_DIM ** -0.5 * LOG2_E, cos_a, sin_a, PROJ_ROWS, PROJ_COLS,
                   "proj0")
    cum_log_f = _fox_cum_log_forget(xf, w_in0[:, c0:c1], b_forget, batch, seq)
    o_a, w_moe_up_bf = _causal_attention(qkv, qkv, qkv, 0, fox_heads, 2 * fox_heads, HEAD_DIM, HEAD_DIM,
                                         fox_heads, batch, seq, ck=cum_log_f, cast_rider=w_moe_up,
                                         name="fox_attn")
    o_b, ffn_w_bf = _swa_attention(qkv, c0, c0 + swa_qw, c0 + swa_qw + swa_kvw, sinks_b, swa_q, swa_kv,
                                   batch, seq, HEAD_DIM ** -0.5,
                                   cast_riders=[w_ffn_gate, w_ffn_up, w_ffn_down])
    w_out0_bf = w_out0.astype(BF16)
    (h1,) = _matmul_residual_ln([(o_a, w_out0_bf[:fox_w]), (o_b, w_out0_bf[fox_w:])], xf,
                                ln0_mix_g, ln0_mix_b, "out0_ln")
    h2 = _swiglu_residual_ln(h1, *ffn_w_bf, ln0_ffn_g, ln0_ffn_b)

    r0 = q_rank + kv_rank
    w_in1_x = jnp.concatenate([w_in1, _rotate_half_cols(w_in1[:, r0:])], axis=1).astype(BF16)
    w_uq_h = w_uq.reshape(q_rank, mla_heads, MLA_NOPE_DIM + MLA_ROPE_DIM)
    w_uq_x = jnp.concatenate([w_uq_h, _rotate_half_cols(w_uq_h[..., MLA_NOPE_DIM:])], axis=-1)
    q_m, k_m, v_m = _mla_projections(
        h2, w_in1_x, q_norm_g, kv_norm_g, tab_m,
        w_uq_x.reshape(q_rank, mla_heads * MLA_QK_PAD).astype(BF16), w_ukv.astype(BF16),
        q_rank, kv_rank, mla_heads, (MLA_NOPE_DIM + MLA_ROPE_DIM) ** -0.5 * LOG2_E)
    o_m, w_moe_gate_bf = _causal_attention(q_m, k_m, v_m, 0, 0, 0, MLA_QK_PAD, MLA_V_DIM, mla_heads, batch,
                                           seq, cast_rider=w_moe_gate, name="mla_attn")
    h3, idx, gates = _matmul_residual_ln([(o_m, w_out1.astype(BF16))], h2, ln1_mix_g, ln1_mix_b,
                                         "out1_ln_route", w_router=w_router)

    moe_tm = _tile(n * TOP_K, MOE_ROWS)
    pos, row_token, tile_expert, tile_rows = _moe_routing(idx[:, :TOP_K], n_experts, moe_tm)
    xg = _gather_rows(h3, row_token, _tile(moe_tm, MOE_GATHER_ROWS))
    y_rows = _moe_ffn(xg, w_moe_gate_bf, w_moe_up_bf, w_moe_down, tile_expert, tile_rows, moe_tm)
    out = _moe_combine_ln(y_rows, pos, h3, gates, ln1_ffn_g, ln1_ffn_b)
    return out.reshape(batch, seq, d)
```

```python
import functools
import math

import jax
import jax.numpy as jnp
from jax import lax
from jax.experimental import pallas as pl
from jax.experimental.pallas import tpu as pltpu

F32 = jnp.float32
BF16 = jnp.bfloat16

LANE = 128
F32_SUBLANES = 8
BF16_SUBLANES = 16
HEAD_DIM = 128
SWA_WINDOW = 128
MLA_NOPE_DIM = 128
MLA_ROPE_DIM = 64
MLA_V_DIM = 128
MLA_QK_PAD = 256
ROPE_THETA = 10000.0
TOP_K = 2
LN_EPS = 1e-5
RMS_EPS = 1e-6
DEPTH = 2
DEEPNORM_ALPHA = (2 * DEPTH) ** 0.25
LOG2_E = math.log2(math.e)

NT_DIMS = (((1,), (1,)), ((), ()))

ROPE_TABLE_ROWS = 1024
PROJ_ROWS, PROJ_COLS = 1024, 1152
GATE_CHUNK = 512
ATTN_TILE = 512
ATTN_HEADS_PER_STEP = 4
OUT_PROJ_ROWS, LN_ROW_GROUP = 512, 256
FFN_ROWS, FFN_COLS = 512, 512
MLA_PROJ_ROWS = 512
MOE_ROWS, MOE_COLS = 512, 1024
MOE_GATHER_ROWS, MOE_COMBINE_ROWS = 512, 256
MOE_TILE_PARTS = 8
N_DMA_PRIORITIES = 2


def _tile(n, pref):
    if n <= pref:
        return n
    t = (pref // LANE) * LANE
    while n % t:
        t -= LANE
    return t


def _cparams(*sem):
    return pltpu.CompilerParams(dimension_semantics=sem)


def _dot(a, b):
    return jnp.dot(a, b, preferred_element_type=F32)


def _layer_norm(y, g, b):
    mu = jnp.mean(y, axis=-1, keepdims=True)
    d = y - mu
    var = jnp.mean(d * d, axis=-1, keepdims=True)
    return d * lax.rsqrt(var + LN_EPS) * g + b


def _silu(g):
    return g / (1.0 + jnp.exp(-g))


def _rope_table_kernel(pos_ref, freq_ref, cos_a_ref, sin_a_ref, tab_m_ref):
    pos = pos_ref[...]
    ang_a = pos * freq_ref[0:1, :]
    lane = lax.broadcasted_iota(jnp.int32, ang_a.shape, 1)
    cos_a_ref[...] = jnp.cos(ang_a)
    sin_a_ref[...] = jnp.where(lane < HEAD_DIM // 2, -jnp.sin(ang_a), jnp.sin(ang_a))
    ang_m = pos * freq_ref[1:2, :]
    tab_m_ref[...] = jnp.where(lane < MLA_ROPE_DIM, jnp.cos(ang_m), jnp.sin(ang_m))


def _rope_tables(positions):
    n = positions.size
    pos = positions.astype(F32).reshape(n, 1)
    half_a = HEAD_DIM // 2
    half_m = MLA_ROPE_DIM // 2
    inv_a = ROPE_THETA ** (-2.0 * jnp.arange(half_a, dtype=F32) / HEAD_DIM)
    inv_m = ROPE_THETA ** (-2.0 * jnp.arange(half_m, dtype=F32) / MLA_ROPE_DIM)
    freq = jnp.stack([jnp.tile(inv_a, 2), jnp.tile(inv_m, 4)])
    tm = _tile(n, ROPE_TABLE_ROWS)
    out = jax.ShapeDtypeStruct((n, LANE), F32)
    row = pl.BlockSpec((tm, LANE), lambda i: (i, 0))
    return pl.pallas_call(
        _rope_table_kernel, grid=(n // tm,),
        in_specs=[pl.BlockSpec((tm, 1), lambda i: (i, 0)), pl.BlockSpec((2, LANE), lambda i: (0, 0))],
        out_specs=[row, row, row], out_shape=[out, out, out],
        compiler_params=_cparams("parallel"), name="rope_tables")(pos, freq)


PLAIN, SCALED, ROTARY = "plain", "scaled", "rotary"


def _proj_kernel(x_ref, w_ref, cos_ref, sin_ref, o_ref, xb_ref, *, tile_kinds, scale):
    j = pl.program_id(1)

    @pl.when(j == 0)
    def _():
        xb_ref[...] = x_ref[...].astype(BF16)

    def tile(kinds):
        acc = _dot(xb_ref[...], w_ref[...])
        for c, kind in enumerate(kinds):
            cols = slice(c * HEAD_DIM, (c + 1) * HEAD_DIM)
            a = acc[:, cols]
            if kind == SCALED:
                a = a * scale
            elif kind == ROTARY:
                a = a * cos_ref[...] + pltpu.roll(a, HEAD_DIM // 2, 1) * sin_ref[...]
            o_ref[:, cols] = a.astype(o_ref.dtype)

    for kinds in sorted(set(tile_kinds)):
        first = tile_kinds.index(kinds)
        count = tile_kinds.count(kinds)
        assert tile_kinds[first:first + count] == (kinds,) * count
        pl.when((j >= first) & (j < first + count))(functools.partial(tile, kinds))


def _project(x, w, group_kinds, scale, cos, sin, tm, tn, name):
    m, k = x.shape
    n = w.shape[1]
    tm, tn = _tile(m, tm), _tile(n, tn)
    per_tile = tn // HEAD_DIM
    tile_kinds = tuple(tuple(group_kinds[t * per_tile:(t + 1) * per_tile]) for t in range(n // tn))
    return pl.pallas_call(
        functools.partial(_proj_kernel, tile_kinds=tile_kinds, scale=scale), grid=(m // tm, n // tn),
        in_specs=[pl.BlockSpec((tm, k), lambda i, j: (i, 0)), pl.BlockSpec((k, tn), lambda i, j: (0, j)),
                  pl.BlockSpec((tm, LANE), lambda i, j: (i, 0)), pl.BlockSpec((tm, LANE), lambda i, j: (i, 0))],
        out_specs=pl.BlockSpec((tm, tn), lambda i, j: (i, j)),
        out_shape=jax.ShapeDtypeStruct((m, n), BF16),
        scratch_shapes=[pltpu.VMEM((tm, k), BF16)],
        compiler_params=_cparams("parallel", "arbitrary"), name=name)(x, w, cos, sin)


def _split3(x):
    hi = x.astype(BF16)
    r = x - hi.astype(F32)
    mid = r.astype(BF16)
    lo = (r - mid.astype(F32)).astype(BF16)
    return hi, mid, lo


def _fox_gate_kernel(x_ref, wf_ref, bf_ref, o_ref, carry_ref, *, n_heads):
    @pl.when(pl.program_id(1) == 0)
    def _():
        carry_ref[...] = jnp.zeros_like(carry_ref)

    f = lax.dot_general(wf_ref[...], x_ref[...].astype(BF16), NT_DIMS, preferred_element_type=F32)
    z = f + bf_ref[...]
    log_f = jnp.minimum(z, 0.0) - jnp.log1p(jnp.exp(-jnp.abs(z)))
    tc = z.shape[1]
    src = lax.broadcasted_iota(jnp.int32, (tc, tc), 0)
    dst = lax.broadcasted_iota(jnp.int32, (tc, tc), 1)
    tri = jnp.where(src <= dst, 1.0, 0.0).astype(BF16)
    hi, mid, lo = _split3(log_f)
    cum = _dot(hi, tri) + _dot(mid, tri) + _dot(lo, tri) + carry_ref[:, 0:1]
    o_ref[0] = cum[:n_heads] * LOG2_E
    carry_ref[...] = jnp.broadcast_to(cum[:, tc - 1:tc], carry_ref.shape)


def _fox_cum_log_forget(x, w_f, b_forget, batch, seq):
    n, d = x.shape
    n_heads = w_f.shape[1]
    rows = BF16_SUBLANES
    wf_t = jnp.zeros((rows, d), BF16).at[:n_heads].set(w_f.T.astype(BF16))
    bf = jnp.zeros((rows, 1), F32).at[:n_heads, 0].set(b_forget.astype(F32))
    tc = _tile(seq, GATE_CHUNK)
    nc = seq // tc
    return pl.pallas_call(
        functools.partial(_fox_gate_kernel, n_heads=n_heads), grid=(batch, nc),
        in_specs=[pl.BlockSpec((tc, d), lambda b, s: (b * nc + s, 0)),
                  pl.BlockSpec((rows, d), lambda b, s: (0, 0)),
                  pl.BlockSpec((rows, 1), lambda b, s: (0, 0))],
        out_specs=pl.BlockSpec((1, n_heads, tc), lambda b, s: (b, 0, s)),
        out_shape=jax.ShapeDtypeStruct((batch, n_heads, seq), F32),
        scratch_shapes=[pltpu.VMEM((rows, LANE), F32)],
        compiler_params=_cparams("parallel", "arbitrary"), name="fox_gate")(x, wf_t, bf)


def _flash_kernel(*refs, tk, dk, dv, heads, has_bias, n_cast_slabs):
    refs = list(refs)
    q_ref, k_ref, v_ref = refs[:3]
    del refs[:3]
    ck_ref = refs.pop(0) if has_bias else None
    w_ref = refs.pop(0) if n_cast_slabs else None
    o_ref = refs.pop(0)
    qi = pl.program_id(2)
    tq = q_ref.shape[0]
    qs = [q_ref[:, g * dk:(g + 1) * dk] for g in range(heads)]

    if n_cast_slabs:
        step = (pl.program_id(0) * pl.num_programs(1) + pl.program_id(1)) * pl.num_programs(2) + qi
        _cast_rider_step(step, n_cast_slabs, w_ref, refs[0])

    def step(g, state, j, masked):
        m, l, acc = state
        kv_rows = pl.ds(pl.multiple_of(j * tk, tk), tk)
        s = lax.dot_general(qs[g], k_ref[kv_rows, g * dk:(g + 1) * dk], NT_DIMS, preferred_element_type=F32)
        if has_bias:
            s = s - ck_ref[0, g, j]
        if masked:
            row = lax.broadcasted_iota(jnp.int32, s.shape, 0)
            col = lax.broadcasted_iota(jnp.int32, s.shape, 1)
            s = jnp.where(row >= col, s, -jnp.inf)
        m_new = jnp.maximum(m, jnp.max(s, axis=-1, keepdims=True))
        a = jnp.exp2(m - m_new)
        p = jnp.exp2(s - m_new)
        l = a * l + jnp.sum(p, axis=-1, keepdims=True)
        acc = a * acc + _dot(p.astype(BF16), v_ref[kv_rows, g * dv:(g + 1) * dv])
        return m_new, l, acc

    def body(j, carry):
        return tuple(step(g, carry[g], j, False) for g in range(heads))

    init = (jnp.full((tq, 1), -jnp.inf, F32), jnp.zeros((tq, 1), F32), jnp.zeros((tq, dv), F32))
    carry = lax.fori_loop(0, qi, body, (init,) * heads)
    for g in range(heads):
        _, l, acc = step(g, carry[g], qi, True)
        o_ref[:, g * dv:(g + 1) * dv] = (acc / l).astype(o_ref.dtype)


def _slab_rows(total_rows, max_slabs):
    rows = BF16_SUBLANES
    while total_rows % rows or total_rows // rows > max_slabs:
        rows += BF16_SUBLANES
    return rows


def _cast_rider(w, grid, step_of):
    flat = w.reshape(-1, w.shape[-1])
    rows = _slab_rows(flat.shape[0], math.prod(grid))
    n_slabs = flat.shape[0] // rows
    spec = pl.BlockSpec((rows, flat.shape[1]), lambda *g: (jnp.minimum(step_of(*g), n_slabs - 1), 0))
    return flat, spec, jax.ShapeDtypeStruct(flat.shape, BF16), n_slabs


def _cast_rider_step(step, n_slabs, w_ref, o_ref):
    @pl.when(step < n_slabs)
    def _():
        o_ref[...] = w_ref[...].astype(o_ref.dtype)


def _causal_attention(q_arr, k_arr, v_arr, q_col, k_col, v_col, dk, dv, n_heads, batch, seq,
                      ck=None, cast_rider=None, name="attn"):
    heads = ATTN_HEADS_PER_STEP
    assert n_heads % heads == 0 and q_col % heads == 0 and k_col % heads == 0 and v_col % heads == 0
    t = _tile(seq, ATTN_TILE)
    nq = seq // t
    grid = (batch, n_heads // heads, nq)
    in_specs = [pl.BlockSpec((t, heads * dk), lambda b, h, i: (b * nq + i, q_col // heads + h)),
                pl.BlockSpec((seq, heads * dk), lambda b, h, i: (b, k_col // heads + h)),
                pl.BlockSpec((seq, heads * dv), lambda b, h, i: (b, v_col // heads + h))]
    args = [q_arr, k_arr, v_arr]
    out_specs = [pl.BlockSpec((t, heads * dv), lambda b, h, i: (b * nq + i, h))]
    out_shape = [jax.ShapeDtypeStruct((batch * seq, n_heads * dv), BF16)]
    if ck is not None:
        args.append(ck.reshape(batch, n_heads, nq, 1, t))
        in_specs.append(pl.BlockSpec((1, heads, nq, 1, t), lambda b, h, i: (b, h, 0, 0, 0)))
    n_slabs = 0
    if cast_rider is not None:
        flat, slab, flat_bf, n_slabs = _cast_rider(cast_rider, grid, lambda b, h, i: (b * grid[1] + h) * nq + i)
        args.append(flat)
        in_specs.append(slab)
        out_specs.append(slab)
        out_shape.append(flat_bf)
    outs = pl.pallas_call(
        functools.partial(_flash_kernel, tk=t, dk=dk, dv=dv, heads=heads, has_bias=ck is not None,
                          n_cast_slabs=n_slabs),
        grid=grid, in_specs=in_specs, out_specs=out_specs, out_shape=out_shape,
        compiler_params=_cparams("parallel", "parallel", "arbitrary"), name=name)(*args)
    if cast_rider is None:
        return outs[0]
    return outs[0], outs[1].reshape(cast_rider.shape)


def _swa_kernel(sink_ref, q_ref, kp_ref, kc_ref, vp_ref, vc_ref, *rest, scale, n_q, n_kv, rider_slabs):
    n_riders = len(rider_slabs)
    o_ref = rest[n_riders]
    step = pl.program_id(0) * pl.num_programs(1) + pl.program_id(1)
    for n_slabs, w_ref, wo_ref in zip(rider_slabs, rest[:n_riders], rest[n_riders + 1:]):
        _cast_rider_step(step, n_slabs, w_ref, wo_ref)
    w = SWA_WINDOW
    qi = lax.broadcasted_iota(jnp.int32, (w, 2 * w), 0)
    ji = lax.broadcasted_iota(jnp.int32, (w, 2 * w), 1)
    has_prev = pl.program_id(1) > 0
    mask = (ji > qi) & (ji <= qi + w) & ((ji >= w) | has_prev)
    group = n_q // n_kv
    for kvh in range(n_kv):
        cols = slice(kvh * HEAD_DIM, (kvh + 1) * HEAD_DIM)
        k = jnp.concatenate([kp_ref[:, cols], kc_ref[:, cols]], axis=0)
        v = jnp.concatenate([vp_ref[:, cols], vc_ref[:, cols]], axis=0)
        for g in range(group):
            h = kvh * group + g
            hc = slice(h * HEAD_DIM, (h + 1) * HEAD_DIM)
            s = lax.dot_general(q_ref[:, hc], k, NT_DIMS, preferred_element_type=F32) * scale
            s = jnp.where(mask, s, -jnp.inf)
            sink = sink_ref[h]
            m = jnp.maximum(jnp.max(s, axis=-1, keepdims=True), sink)
            p = jnp.exp(s - m)
            denom = jnp.sum(p, axis=-1, keepdims=True) + jnp.exp(sink - m)
            o_ref[:, hc] = (_dot(p.astype(BF16), v) / denom).astype(o_ref.dtype)


def _swa_attention(qkv, q_off, k_off, v_off, sinks, n_q, n_kv, batch, seq, scale, cast_riders):
    w = SWA_WINDOW
    nb = seq // w
    grid = (batch, nb)
    qw, kw = n_q * HEAD_DIM, n_kv * HEAD_DIM
    assert q_off % qw == 0 and k_off % kw == 0 and v_off % kw == 0
    q_col, k_col, v_col = q_off // qw, k_off // kw, v_off // kw
    cur = lambda b, n: b * nb + n
    prev = lambda b, n: b * nb + jnp.maximum(n - 1, 0)
    riders = [_cast_rider(r, grid, cur) for r in cast_riders]
    outs = pl.pallas_call(
        functools.partial(_swa_kernel, scale=scale, n_q=n_q, n_kv=n_kv,
                          rider_slabs=tuple(r[3] for r in riders)), grid=grid,
        in_specs=[pl.BlockSpec(memory_space=pltpu.SMEM),
                  pl.BlockSpec((w, qw), lambda b, n: (cur(b, n), q_col)),
                  pl.BlockSpec((w, kw), lambda b, n: (prev(b, n), k_col)),
                  pl.BlockSpec((w, kw), lambda b, n: (cur(b, n), k_col)),
                  pl.BlockSpec((w, kw), lambda b, n: (prev(b, n), v_col)),
                  pl.BlockSpec((w, kw), lambda b, n: (cur(b, n), v_col))] + [r[1] for r in riders],
        out_specs=[pl.BlockSpec((w, qw), lambda b, n: (cur(b, n), 0))] + [r[1] for r in riders],
        out_shape=[jax.ShapeDtypeStruct((batch * seq, qw), BF16)] + [r[2] for r in riders],
        compiler_params=_cparams("arbitrary", "arbitrary"), name="swa_attn")(
            sinks.astype(F32), qkv, qkv, qkv, qkv, qkv, *[r[0] for r in riders])
    return outs[0], [o.reshape(r.shape) for o, r in zip(outs[1:], cast_riders)]


def _top2_route(x, w, n_experts):
    xh = x.astype(BF16)
    xl = (x - xh.astype(F32)).astype(BF16)
    wh = w.astype(BF16)
    wl = (w - wh.astype(F32)).astype(BF16)
    hi = _dot(xh, jnp.concatenate([wh, wl], axis=1))
    logits = hi[:, :LANE] + hi[:, LANE:] + _dot(xl, wh)
    lane = lax.broadcasted_iota(jnp.int32, logits.shape, 1)
    lane_f = lane.astype(F32)
    l1 = jnp.where(lane < n_experts, logits, -jnp.inf)
    v1 = jnp.max(l1, axis=-1, keepdims=True)
    i1 = jnp.min(jnp.where(l1 == v1, lane_f, float(LANE)), axis=-1, keepdims=True)
    l2 = jnp.where(lane_f == i1, -jnp.inf, l1)
    v2 = jnp.max(l2, axis=-1, keepdims=True)
    i2 = jnp.min(jnp.where(l2 == v2, lane_f, float(LANE)), axis=-1, keepdims=True)
    e2 = jnp.exp(v2 - v1)
    idx = jnp.where(lane == 0, i1, jnp.where(lane == 1, i2, 0.0)).astype(jnp.int32)
    gate = jnp.where(lane == 0, 1.0 / (1.0 + e2), jnp.where(lane == 1, e2 / (1.0 + e2), 0.0))
    return idx, gate


def _mm_res_ln_kernel(*refs, n_pairs, n_experts):
    a_refs, w_refs = refs[:n_pairs], refs[n_pairs:2 * n_pairs]
    if n_experts:
        res_ref, g_ref, b_ref, wr_ref, o_ref, idx_ref, gate_ref = refs[2 * n_pairs:]
    else:
        res_ref, g_ref, b_ref, o_ref = refs[2 * n_pairs:]
    tm = o_ref.shape[0]
    sub = _tile(tm, LN_ROW_GROUP)
    for r in range(tm // sub):
        rows = slice(r * sub, (r + 1) * sub)
        acc = _dot(a_refs[0][rows, :], w_refs[0][...])
        for a_ref, w_ref in zip(a_refs[1:], w_refs[1:]):
            acc = acc + _dot(a_ref[rows, :], w_ref[...])
        y = _layer_norm(DEEPNORM_ALPHA * res_ref[rows, :] + acc, g_ref[...], b_ref[...])
        o_ref[rows, :] = y
        if n_experts:
            idx_ref[rows, :], gate_ref[rows, :] = _top2_route(y, wr_ref[...], n_experts)


def _matmul_residual_ln(pairs, res, g, b, name, w_router=None):
    m, d = res.shape
    tm = _tile(m, OUT_PROJ_ROWS)
    row = lambda i: (i, 0)
    fixed = lambda i: (0, 0)
    resident = lambda shape: pl.BlockSpec(shape, fixed, pipeline_mode=pl.Buffered(1))
    in_specs = ([pl.BlockSpec((tm, a.shape[1]), row) for a, _ in pairs] + [resident(w.shape) for _, w in pairs]
                + [pl.BlockSpec((tm, d), row), pl.BlockSpec((1, d), fixed), pl.BlockSpec((1, d), fixed)])
    args = [a for a, _ in pairs] + [w for _, w in pairs] + [res, g.reshape(1, d), b.reshape(1, d)]
    out_specs = [pl.BlockSpec((tm, d), row)]
    out_shape = [jax.ShapeDtypeStruct((m, d), F32)]
    n_experts = 0
    if w_router is not None:
        n_experts = w_router.shape[1]
        in_specs.append(resident((d, LANE)))
        args.append(jnp.zeros((d, LANE), F32).at[:, :n_experts].set(w_router.astype(F32)))
        out_specs += [pl.BlockSpec((tm, LANE), row), pl.BlockSpec((tm, LANE), row)]
        out_shape += [jax.ShapeDtypeStruct((m, LANE), jnp.int32), jax.ShapeDtypeStruct((m, LANE), F32)]
    return pl.pallas_call(
        functools.partial(_mm_res_ln_kernel, n_pairs=len(pairs), n_experts=n_experts),
        grid=(m // tm,), in_specs=in_specs, out_specs=out_specs, out_shape=out_shape,
        compiler_params=_cparams("parallel"), name=name)(*args)


def _swiglu(x, wg, wu, wd):
    a = _silu(_dot(x, wg)) * _dot(x, wu)
    return _dot(a.astype(BF16), wd)


FFN_WEIGHT_SLOTS = 4
FFN_PREFETCH = 2


def _ffn_ring_ok(n_chunks):
    slot = lambda c: (c % n_chunks) % FFN_WEIGHT_SLOTS
    return n_chunks >= FFN_PREFETCH and all(
        slot(c + FFN_PREFETCH) not in {slot(c + d) for d in range(FFN_PREFETCH)} for c in range(n_chunks))


def _ffn_ln_kernel(h_ref, g_ref, b_ref, wg_hbm, wu_hbm, wd_hbm, o_ref, x_ref, wg_buf, wu_buf, wd_buf, sem, *,
                   n_chunks):
    i, n = pl.program_id(0), pl.num_programs(0)
    tf = wg_buf.shape[2]

    def copies(c):
        slot, cols = c % FFN_WEIGHT_SLOTS, pl.ds(c * tf, tf)
        return (pltpu.make_async_copy(wg_hbm.at[:, cols], wg_buf.at[slot], sem.at[0, slot]),
                pltpu.make_async_copy(wu_hbm.at[:, cols], wu_buf.at[slot], sem.at[1, slot]),
                pltpu.make_async_copy(wd_hbm.at[cols, :], wd_buf.at[slot], sem.at[2, slot]))

    def start(c):
        for copy in copies(c):
            copy.start()

    @pl.when(i == 0)
    def _():
        for c in range(FFN_PREFETCH):
            start(c)

    x_ref[...] = h_ref[...].astype(BF16)
    for c in range(n_chunks):
        for copy in copies(c):
            copy.wait()
        ahead = c + FFN_PREFETCH
        if ahead < n_chunks:
            start(ahead)
        else:
            pl.when(i + 1 < n)(functools.partial(start, ahead - n_chunks))
        slot = c % FFN_WEIGHT_SLOTS
        y = _swiglu(x_ref[...], wg_buf[slot], wu_buf[slot], wd_buf[slot])
        if c == 0:
            o_ref[...] = y
        else:
            o_ref[...] += y
    o_ref[...] = _layer_norm(DEEPNORM_ALPHA * h_ref[...] + o_ref[...], g_ref[...], b_ref[...])


def _swiglu_residual_ln(h, wg, wu, wd, g, b):
    m, d = h.shape
    f = wg.shape[1]
    tm, tf = _tile(m, FFN_ROWS), _tile(f, FFN_COLS)
    while not _ffn_ring_ok(f // tf):
        tf //= 2
    assert tf % LANE == 0 and f % tf == 0
    row = lambda i: (i, 0)
    fixed = lambda i: (0, 0)
    hbm = pl.BlockSpec(memory_space=pl.ANY)
    return pl.pallas_call(
        functools.partial(_ffn_ln_kernel, n_chunks=f // tf), grid=(m // tm,),
        in_specs=[pl.BlockSpec((tm, d), row), pl.BlockSpec((1, d), fixed), pl.BlockSpec((1, d), fixed),
                  hbm, hbm, hbm],
        out_specs=pl.BlockSpec((tm, d), row),
        out_shape=jax.ShapeDtypeStruct((m, d), F32),
        scratch_shapes=[pltpu.VMEM((tm, d), BF16),
                        pltpu.VMEM((FFN_WEIGHT_SLOTS, d, tf), BF16), pltpu.VMEM((FFN_WEIGHT_SLOTS, d, tf), BF16),
                        pltpu.VMEM((FFN_WEIGHT_SLOTS, tf, d), BF16),
                        pltpu.SemaphoreType.DMA((3, FFN_WEIGHT_SLOTS))],
        compiler_params=_cparams("arbitrary"), name="ffn_ln")(
            h, g.reshape(1, d), b.reshape(1, d), wg, wu, wd)


def _rms_norm(c, g):
    ms = jnp.mean(c * c, axis=-1, keepdims=True)
    return c * lax.rsqrt(ms + RMS_EPS) * g


def _rope64(pair, tab):
    w = pair * tab
    lane = lax.broadcasted_iota(jnp.int32, w.shape, 1)
    return jnp.where(lane < MLA_ROPE_DIM, w + pltpu.roll(w, MLA_ROPE_DIM, 1), 0.0)


def _mla_proj_kernel(x_ref, win_ref, gq_ref, gkv_ref, tab_ref, wq_ref, wkv_ref, q_ref, k_ref, v_ref, *,
                     q_rank, kv_rank, n_heads, q_scale):
    acc = _dot(x_ref[...].astype(BF16), win_ref[...])
    cq = _rms_norm(acc[:, :q_rank], gq_ref[...]).astype(BF16)
    ckv = _rms_norm(acc[:, q_rank:q_rank + kv_rank], gkv_ref[...]).astype(BF16)
    tab = tab_ref[...]
    k_pe = _rope64(acc[:, q_rank + kv_rank:], tab).astype(BF16)
    wkv_w = MLA_NOPE_DIM + MLA_V_DIM
    for h in range(n_heads):
        q0 = h * MLA_QK_PAD
        qh = _dot(cq, wq_ref[:, q0:q0 + MLA_QK_PAD]) * q_scale
        q_ref[:, q0:q0 + MLA_NOPE_DIM] = qh[:, :MLA_NOPE_DIM].astype(BF16)
        q_ref[:, q0 + MLA_NOPE_DIM:q0 + MLA_QK_PAD] = _rope64(qh[:, MLA_NOPE_DIM:], tab).astype(BF16)
        kvh = _dot(ckv, wkv_ref[:, h * wkv_w:(h + 1) * wkv_w])
        k_ref[:, q0:q0 + MLA_NOPE_DIM] = kvh[:, :MLA_NOPE_DIM].astype(BF16)
        k_ref[:, q0 + MLA_NOPE_DIM:q0 + MLA_QK_PAD] = k_pe
        v_ref[:, h * MLA_V_DIM:(h + 1) * MLA_V_DIM] = kvh[:, MLA_NOPE_DIM:].astype(BF16)


def _mla_projections(x, w_in, gq, gkv, tab, w_q, w_kv, q_rank, kv_rank, n_heads, q_scale):
    m, d = x.shape
    tm = _tile(m, MLA_PROJ_ROWS)
    row = lambda i: (i, 0)
    fixed = lambda i: (0, 0)
    resident = lambda shape: pl.BlockSpec(shape, fixed, pipeline_mode=pl.Buffered(1))
    qk_w, v_w = n_heads * MLA_QK_PAD, n_heads * MLA_V_DIM
    return pl.pallas_call(
        functools.partial(_mla_proj_kernel, q_rank=q_rank, kv_rank=kv_rank, n_heads=n_heads, q_scale=q_scale),
        grid=(m // tm,),
        in_specs=[pl.BlockSpec((tm, d), row), resident(w_in.shape),
                  pl.BlockSpec((1, q_rank), fixed), pl.BlockSpec((1, kv_rank), fixed),
                  pl.BlockSpec((tm, LANE), row), resident(w_q.shape), resident(w_kv.shape)],
        out_specs=[pl.BlockSpec((tm, qk_w), row), pl.BlockSpec((tm, qk_w), row), pl.BlockSpec((tm, v_w), row)],
        out_shape=[jax.ShapeDtypeStruct((m, qk_w), BF16), jax.ShapeDtypeStruct((m, qk_w), BF16),
                   jax.ShapeDtypeStruct((m, v_w), BF16)],
        compiler_params=_cparams("parallel"), name="mla_proj")(
            x, w_in, gq.reshape(1, q_rank).astype(F32), gkv.reshape(1, kv_rank).astype(F32), tab, w_q, w_kv)


def _row_copy(src_hbm, src_row, group_buf, sublane, sem):
    return pltpu.make_async_copy(src_hbm.at[pl.ds(src_row, 1), :], group_buf.at[pl.ds(sublane, 1), :], sem)


def _wait_group(src_hbm, group_buf, sem):
    pltpu.make_async_copy(src_hbm.at[pl.ds(0, group_buf.shape[0]), :], group_buf, sem).wait()


def _gather_rows_kernel(tok_ref, x_hbm, o_ref, buf, sem):
    i, n = pl.program_id(0), pl.num_programs(0)
    groups, sub, d = buf.shape[1:]
    rows = groups * sub

    def issue(step, slot):
        def start(g, _):
            for s in range(sub):
                copy = _row_copy(x_hbm, tok_ref[step * rows + g * sub + s], buf.at[slot, g], s, sem.at[slot])
                copy.start(priority=s % N_DMA_PRIORITIES)
            return 0
        lax.fori_loop(0, groups, start, 0)

    @pl.when(i == 0)
    def _():
        issue(0, 0)

    @pl.when(i + 1 < n)
    def _():
        issue(i + 1, (i + 1) % 2)

    slot = i % 2

    def wait(g, _):
        _wait_group(x_hbm, buf.at[slot, g], sem.at[slot])
        return 0
    lax.fori_loop(0, groups, wait, 0)
    o_ref[...] = buf[slot].reshape(rows, d).astype(o_ref.dtype)


def _gather_rows(h, row_token, rows_per_step):
    n_rows = row_token.shape[0]
    d = h.shape[1]
    return pl.pallas_call(
        _gather_rows_kernel,
        grid_spec=pltpu.PrefetchScalarGridSpec(
            num_scalar_prefetch=1, grid=(n_rows // rows_per_step,),
            in_specs=[pl.BlockSpec(memory_space=pl.ANY)],
            out_specs=pl.BlockSpec((rows_per_step, d), lambda i, tok: (i, 0)),
            scratch_shapes=[pltpu.VMEM((2, rows_per_step // F32_SUBLANES, F32_SUBLANES, d), F32),
                            pltpu.SemaphoreType.DMA((2,))]),
        out_shape=jax.ShapeDtypeStruct((n_rows, d), BF16),
        compiler_params=_cparams("arbitrary"), name="moe_gather")(row_token, h)


def _moe_ffn_kernel(te_ref, tr_ref, x_ref, wg_ref, wu_ref, wd_ref, o_ref):
    i, j = pl.program_id(0), pl.program_id(1)
    part = x_ref.shape[0] // MOE_TILE_PARTS
    parts_used = (tr_ref[i] + part - 1) // part

    @pl.when(j == 0)
    def _():
        o_ref[...] = jnp.zeros_like(o_ref)

    for used in range(1, MOE_TILE_PARTS + 1):
        @pl.when(parts_used == used)
        def _():
            rows = slice(0, used * part)
            o_ref[rows, :] += _swiglu(x_ref[rows, :], wg_ref[0], wu_ref[0], wd_ref[0].astype(BF16))


def _moe_ffn(xg, wg, wu, wd, tile_expert, tile_rows, tm):
    n_rows, d = xg.shape
    f = wg.shape[2]
    tf = _tile(f, MOE_COLS)
    nj = f // tf
    col = lambda i, j, te, tr: jnp.where(tr[i] > 0, j, nj - 1)
    return pl.pallas_call(
        _moe_ffn_kernel,
        grid_spec=pltpu.PrefetchScalarGridSpec(
            num_scalar_prefetch=2, grid=(n_rows // tm, nj),
            in_specs=[pl.BlockSpec((tm, d), lambda i, j, te, tr: (i, 0)),
                      pl.BlockSpec((1, d, tf), lambda i, j, te, tr: (te[i], 0, col(i, j, te, tr))),
                      pl.BlockSpec((1, d, tf), lambda i, j, te, tr: (te[i], 0, col(i, j, te, tr))),
                      pl.BlockSpec((1, tf, d), lambda i, j, te, tr: (te[i], col(i, j, te, tr), 0))],
            out_specs=pl.BlockSpec((tm, d), lambda i, j, te, tr: (i, 0))),
        out_shape=jax.ShapeDtypeStruct((n_rows, d), F32),
        compiler_params=_cparams("arbitrary", "arbitrary"), name="moe_ffn")(
            tile_expert, tile_rows, xg, wg, wu, wd)


def _combine_ln_kernel(pos_ref, h_ref, gate_ref, g_ref, b_ref, y_hbm, o_ref, buf, sem):
    groups, sub, d = buf.shape[2:]
    rows = groups * sub
    i, n = pl.program_id(0), pl.num_programs(0)

    def issue(step, slot):
        def start(g, _):
            for s in range(sub):
                for k in range(TOP_K):
                    src = pos_ref[TOP_K * (step * rows + g * sub + s) + k]
                    _row_copy(y_hbm, src, buf.at[slot, k, g], s, sem.at[slot, k]).start(
                        priority=s % N_DMA_PRIORITIES)
            return 0
        lax.fori_loop(0, groups, start, 0)

    @pl.when(i == 0)
    def _():
        issue(0, 0)

    @pl.when(i + 1 < n)
    def _():
        issue(i + 1, (i + 1) % 2)

    slot = i % 2

    def wait(g, _):
        for k in range(TOP_K):
            _wait_group(y_hbm, buf.at[slot, k, g], sem.at[slot, k])
        return 0
    lax.fori_loop(0, groups, wait, 0)
    gate = gate_ref[...]
    y = None
    for k in range(TOP_K):
        term = buf[slot, k].reshape(rows, d) * gate[:, k:k + 1]
        y = term if y is None else y + term
    o_ref[...] = _layer_norm(DEEPNORM_ALPHA * h_ref[...] + y, g_ref[...], b_ref[...])


def _moe_combine_ln(y_rows, pos, h, gates, g, b):
    m, d = h.shape
    tm = _tile(m, MOE_COMBINE_ROWS)
    row = lambda i, p: (i, 0)
    fixed = lambda i, p: (0, 0)
    return pl.pallas_call(
        _combine_ln_kernel,
        grid_spec=pltpu.PrefetchScalarGridSpec(
            num_scalar_prefetch=1, grid=(m // tm,),
            in_specs=[pl.BlockSpec((tm, d), row), pl.BlockSpec((tm, LANE), row),
                      pl.BlockSpec((1, d), fixed), pl.BlockSpec((1, d), fixed),
                      pl.BlockSpec(memory_space=pl.ANY)],
            out_specs=pl.BlockSpec((tm, d), row),
            scratch_shapes=[pltpu.VMEM((2, TOP_K, tm // F32_SUBLANES, F32_SUBLANES, d), F32),
                            pltpu.SemaphoreType.DMA((2, TOP_K))]),
        out_shape=jax.ShapeDtypeStruct((m, d), F32),
        compiler_params=_cparams("arbitrary"), name="moe_combine_ln")(
            pos, h, gates, g.reshape(1, d), b.reshape(1, d), y_rows)


def _moe_routing(top_idx, n_experts, tm):
    n_assign = top_idx.size
    expert_of = top_idx.reshape(n_assign)
    onehot = (expert_of[:, None] == jnp.arange(n_experts, dtype=jnp.int32)[None, :]).astype(jnp.int32)
    csum = jnp.cumsum(onehot, axis=0)
    rank = jnp.sum(onehot * (csum - 1), axis=1)
    counts = csum[-1]
    padded = ((counts + tm - 1) // tm) * tm
    padded_end = jnp.cumsum(padded)
    padded_start = padded_end - padded
    pos = (padded_start[expert_of] + rank).astype(jnp.int32)
    n_tiles = -(-n_assign // tm) + n_experts
    row_token = jnp.zeros((n_tiles * tm,), jnp.int32).at[pos].set(
        jnp.arange(n_assign, dtype=jnp.int32) // TOP_K, unique_indices=True, mode="promise_in_bounds")
    tile_start = jnp.arange(n_tiles, dtype=jnp.int32) * tm
    tile_used = tile_start < padded_end[-1]
    tile_expert = jnp.minimum(jnp.searchsorted(padded_end, tile_start, side='right'), n_experts - 1)
    tile_rows = jnp.clip(counts[tile_expert] - (tile_start - padded_start[tile_expert]), 0, tm)
    tile_rows = jnp.where(tile_used, tile_rows, 0).astype(jnp.int32)
    last_expert = jnp.max(jnp.where(tile_used, tile_expert, 0))
    tile_expert = jnp.where(tile_used, tile_expert, last_expert).astype(jnp.int32)
    return pos, row_token, tile_expert, tile_rows


def _rotate_half_cols(w):
    half = w.shape[-1] // 2
    return jnp.concatenate([-w[..., half:], w[..., :half]], axis=-1)


def kernel(x, positions, w_in0, b_forget, sinks_b, w_out0, ln0_mix_g, ln0_mix_b, w_ffn_gate, w_ffn_up, w_ffn_down, ln0_ffn_g, ln0_ffn_b, w_in1, q_norm_g, w_uq, kv_norm_g, w_ukv, w_out1, ln1_mix_g, ln1_mix_b, w_router, w_moe_gate, w_moe_up, w_moe_down, ln1_ffn_g, ln1_ffn_b):
    batch, seq, d = x.shape
    n = batch * seq
    fox_heads = b_forget.shape[0]
    fox_w = fox_heads * HEAD_DIM
    swa_q = sinks_b.shape[0]
    swa_qw = swa_q * HEAD_DIM
    swa_kvw = (w_in0.shape[1] - 3 * fox_w - fox_heads - swa_qw) // 2
    swa_kv = swa_kvw // HEAD_DIM
    q_rank, kv_rank = q_norm_g.shape[0], kv_norm_g.shape[0]
    mla_heads = w_uq.shape[1] // (MLA_NOPE_DIM + MLA_ROPE_DIM)
    n_experts = w_router.shape[1]

    xf = x.reshape(n, d)
    cos_a, sin_a, tab_m = _rope_tables(positions)

    c0 = 3 * fox_w
    c1 = c0 + fox_heads
    w_qkv = jnp.concatenate([w_in0[:, :c0], w_in0[:, c1:]], axis=1).astype(BF16)
    group_kinds = ([SCALED] * fox_heads + [PLAIN] * (2 * fox_heads)
                   + [ROTARY] * (swa_q + swa_kv) + [PLAIN] * swa_kv)
    qkv = _project(xf, w_qkv, group_kinds, HEAD_DIM ** -0.5 * LOG2_E, cos_a, sin_a, PROJ_ROWS, PROJ_COLS,
                   "proj0")
    cum_log_f = _fox_cum_log_forget(xf, w_in0[:, c0:c1], b_forget, batch, seq)
    o_a, w_moe_up_bf = _causal_attention(qkv, qkv, qkv, 0, fox_heads, 2 * fox_heads, HEAD_DIM, HEAD_DIM,
                                         fox_heads, batch, seq, ck=cum_log_f, cast_rider=w_moe_up,
                                         name="fox_attn")
    o_b, ffn_w_bf = _swa_attention(qkv, c0, c0 + swa_qw, c0 + swa_qw + swa_kvw, sinks_b, swa_q, swa_kv,
                                   batch, seq, HEAD_DIM ** -0.5,
                                   cast_riders=[w_ffn_gate, w_ffn_up, w_ffn_down])
    w_out0_bf = w_out0.astype(BF16)
    (h1,) = _matmul_residual_ln([(o_a, w_out0_bf[:fox_w]), (o_b, w_out0_bf[fox_w:])], xf,
                                ln0_mix_g, ln0_mix_b, "out0_ln")
    h2 = _swiglu_residual_ln(h1, *ffn_w_bf, ln0_ffn_g, ln0_ffn_b)

    r0 = q_rank + kv_rank
    w_in1_x = jnp.concatenate([w_in1, _rotate_half_cols(w_in1[:, r0:])], axis=1).astype(BF16)
    w_uq_h = w_uq.reshape(q_rank, mla_heads, MLA_NOPE_DIM + MLA_ROPE_DIM)
    w_uq_x = jnp.concatenate([w_uq_h, _rotate_half_cols(w_uq_h[..., MLA_NOPE_DIM:])], axis=-1)
    q_m, k_m, v_m = _mla_projections(
        h2, w_in1_x, q_norm_g, kv_norm_g, tab_m,
        w_uq_x.reshape(q_rank, mla_heads * MLA_QK_PAD).astype(BF16), w_ukv.astype(BF16),
        q_rank, kv_rank, mla_heads, (MLA_NOPE_DIM + MLA_ROPE_DIM) ** -0.5 * LOG2_E)
    o_m, w_moe_gate_bf = _causal_attention(q_m, k_m, v_m, 0, 0, 0, MLA_QK_PAD, MLA_V_DIM, mla_heads, batch,
                                           seq, cast_rider=w_moe_gate, name="mla_attn")
    h3, idx, gates = _matmul_residual_ln([(o_m, w_out1.astype(BF16))], h2, ln1_mix_g, ln1_mix_b,
                                         "out1_ln_route", w_router=w_router)

    moe_tm = _tile(n * TOP_K, MOE_ROWS)
    pos, row_token, tile_expert, tile_rows = _moe_routing(idx[:, :TOP_K], n_experts, moe_tm)
    xg = _gather_rows(h3, row_token, _tile(moe_tm, MOE_GATHER_ROWS))
    y_rows = _moe_ffn(xg, w_moe_gate_bf, w_moe_up_bf, w_moe_down, tile_expert, tile_rows, moe_tm)
    out = _moe_combine_ln(y_rows, pos, h3, gates, ln1_ffn_g, ln1_ffn_b)
    return out.reshape(batch, seq, d)
```

```python
import functools
import math

import jax
import jax.numpy as jnp
from jax import lax
from jax.experimental import pallas as pl
from jax.experimental.pallas import tpu as pltpu

F32 = jnp.float32
BF16 = jnp.bfloat16

LANE = 128
F32_SUBLANES = 8
BF16_SUBLANES = 16
HEAD_DIM = 128
SWA_WINDOW = 128
MLA_NOPE_DIM = 128
MLA_ROPE_DIM = 64
MLA_V_DIM = 128
MLA_QK_PAD = 256
ROPE_THETA = 10000.0
TOP_K = 2
LN_EPS = 1e-5
RMS_EPS = 1e-6
DEPTH = 2
DEEPNORM_ALPHA = (2 * DEPTH) ** 0.25
LOG2_E = math.log2(math.e)

NT_DIMS = (((1,), (1,)), ((), ()))

ROPE_TABLE_ROWS = 1024
PROJ_ROWS, PROJ_COLS = 1024, 1152
GATE_CHUNK = 512
ATTN_TILE = 512
ATTN_HEADS_PER_STEP = 4
OUT_PROJ_ROWS, LN_ROW_GROUP = 512, 256
FFN_ROWS, FFN_COLS = 512, 512
MLA_PROJ_ROWS = 512
MOE_ROWS, MOE_COLS = 512, 1024
MOE_GATHER_ROWS, MOE_COMBINE_ROWS = 512, 256
MOE_TILE_PARTS = 8
N_DMA_PRIORITIES = 2


def _tile(n, pref):
    if n <= pref:
        return n
    t = (pref // LANE) * LANE
    while n % t:
        t -= LANE
    return t


def _cparams(*sem):
    return pltpu.CompilerParams(dimension_semantics=sem)


def _dot(a, b):
    return jnp.dot(a, b, preferred_element_type=F32)


def _layer_norm(y, g, b):
    mu = jnp.mean(y, axis=-1, keepdims=True)
    d = y - mu
    var = jnp.mean(d * d, axis=-1, keepdims=True)
    return d * lax.rsqrt(var + LN_EPS) * g + b


def _silu(g):
    return g / (1.0 + jnp.exp(-g))


def _rope_table_kernel(pos_ref, freq_ref, cos_a_ref, sin_a_ref, tab_m_ref):
    pos = pos_ref[...]
    ang_a = pos * freq_ref[0:1, :]
    lane = lax.broadcasted_iota(jnp.int32, ang_a.shape, 1)
    cos_a_ref[...] = jnp.cos(ang_a)
    sin_a_ref[...] = jnp.where(lane < HEAD_DIM // 2, -jnp.sin(ang_a), jnp.sin(ang_a))
    ang_m = pos * freq_ref[1:2, :]
    tab_m_ref[...] = jnp.where(lane < MLA_ROPE_DIM, jnp.cos(ang_m), jnp.sin(ang_m))


def _rope_tables(positions):
    n = positions.size
    pos = positions.astype(F32).reshape(n, 1)
    half_a = HEAD_DIM // 2
    half_m = MLA_ROPE_DIM // 2
    inv_a = ROPE_THETA ** (-2.0 * jnp.arange(half_a, dtype=F32) / HEAD_DIM)
    inv_m = ROPE_THETA ** (-2.0 * jnp.arange(half_m, dtype=F32) / MLA_ROPE_DIM)
    freq = jnp.stack([jnp.tile(inv_a, 2), jnp.tile(inv_m, 4)])
    tm = _tile(n, ROPE_TABLE_ROWS)
    out = jax.ShapeDtypeStruct((n, LANE), F32)
    row = pl.BlockSpec((tm, LANE), lambda i: (i, 0))
    return pl.pallas_call(
        _rope_table_kernel, grid=(n // tm,),
        in_specs=[pl.BlockSpec((tm, 1), lambda i: (i, 0)), pl.BlockSpec((2, LANE), lambda i: (0, 0))],
        out_specs=[row, row, row], out_shape=[out, out, out],
        compiler_params=_cparams("parallel"), name="rope_tables")(pos, freq)


PLAIN, SCALED, ROTARY = "plain", "scaled", "rotary"


def _proj_kernel(x_ref, w_ref, cos_ref, sin_ref, o_ref, xb_ref, *, tile_kinds, scale):
    j = pl.program_id(1)

    @pl.when(j == 0)
    def _():
        xb_ref[...] = x_ref[...].astype(BF16)

    def tile(kinds):
        acc = _dot(xb_ref[...], w_ref[...])
        for c, kind in enumerate(kinds):
            cols = slice(c * HEAD_DIM, (c + 1) * HEAD_DIM)
            a = acc[:, cols]
            if kind == SCALED:
                a = a * scale
            elif kind == ROTARY:
                a = a * cos_ref[...] + pltpu.roll(a, HEAD_DIM // 2, 1) * sin_ref[...]
            o_ref[:, cols] = a.astype(o_ref.dtype)

    for kinds in sorted(set(tile_kinds)):
        first = tile_kinds.index(kinds)
        count = tile_kinds.count(kinds)
        assert tile_kinds[first:first + count] == (kinds,) * count
        pl.when((j >= first) & (j < first + count))(functools.partial(tile, kinds))


def _project(x, w, group_kinds, scale, cos, sin, tm, tn, name):
    m, k = x.shape
    n = w.shape[1]
    tm, tn = _tile(m, tm), _tile(n, tn)
    per_tile = tn // HEAD_DIM
    tile_kinds = tuple(tuple(group_kinds[t * per_tile:(t + 1) * per_tile]) for t in range(n // tn))
    return pl.pallas_call(
        functools.partial(_proj_kernel, tile_kinds=tile_kinds, scale=scale), grid=(m // tm, n // tn),
        in_specs=[pl.BlockSpec((tm, k), lambda i, j: (i, 0)), pl.BlockSpec((k, tn), lambda i, j: (0, j)),
                  pl.BlockSpec((tm, LANE), lambda i, j: (i, 0)), pl.BlockSpec((tm, LANE), lambda i, j: (i, 0))],
        out_specs=pl.BlockSpec((tm, tn), lambda i, j: (i, j)),
        out_shape=jax.ShapeDtypeStruct((m, n), BF16),
        scratch_shapes=[pltpu.VMEM((tm, k), BF16)],
        compiler_params=_cparams("parallel", "arbitrary"), name=name)(x, w, cos, sin)


def _split3(x):
    hi = x.astype(BF16)
    r = x - hi.astype(F32)
    mid = r.astype(BF16)
    lo = (r - mid.astype(F32)).astype(BF16)
    return hi, mid, lo


def _fox_gate_kernel(x_ref, wf_ref, bf_ref, o_ref, carry_ref, *, n_heads):
    @pl.when(pl.program_id(1) == 0)
    def _():
        carry_ref[...] = jnp.zeros_like(carry_ref)

    f = lax.dot_general(wf_ref[...], x_ref[...].astype(BF16), NT_DIMS, preferred_element_type=F32)
    z = f + bf_ref[...]
    log_f = jnp.minimum(z, 0.0) - jnp.log1p(jnp.exp(-jnp.abs(z)))
    tc = z.shape[1]
    src = lax.broadcasted_iota(jnp.int32, (tc, tc), 0)
    dst = lax.broadcasted_iota(jnp.int32, (tc, tc), 1)
    tri = jnp.where(src <= dst, 1.0, 0.0).astype(BF16)
    hi, mid, lo = _split3(log_f)
    cum = _dot(hi, tri) + _dot(mid, tri) + _dot(lo, tri) + carry_ref[:, 0:1]
    o_ref[0] = cum[:n_heads] * LOG2_E
    carry_ref[...] = jnp.broadcast_to(cum[:, tc - 1:tc], carry_ref.shape)


def _fox_cum_log_forget(x, w_f, b_forget, batch, seq):
    n, d = x.shape
    n_heads = w_f.shape[1]
    rows = BF16_SUBLANES
    wf_t = jnp.zeros((rows, d), BF16).at[:n_heads].set(w_f.T.astype(BF16))
    bf = jnp.zeros((rows, 1), F32).at[:n_heads, 0].set(b_forget.astype(F32))
    tc = _tile(seq, GATE_CHUNK)
    nc = seq // tc
    return pl.pallas_call(
        functools.partial(_fox_gate_kernel, n_heads=n_heads), grid=(batch, nc),
        in_specs=[pl.BlockSpec((tc, d), lambda b, s: (b * nc + s, 0)),
                  pl.BlockSpec((rows, d), lambda b, s: (0, 0)),
                  pl.BlockSpec((rows, 1), lambda b, s: (0, 0))],
        out_specs=pl.BlockSpec((1, n_heads, tc), lambda b, s: (b, 0, s)),
        out_shape=jax.ShapeDtypeStruct((batch, n_heads, seq), F32),
        scratch_shapes=[pltpu.VMEM((rows, LANE), F32)],
        compiler_params=_cparams("parallel", "arbitrary"), name="fox_gate")(x, wf_t, bf)


def _flash_kernel(*refs, tk, dk, dv, heads, has_bias, n_cast_slabs):
    refs = list(refs)
    q_ref, k_ref, v_ref = refs[:3]
    del refs[:3]
    ck_ref = refs.pop(0) if has_bias else None
    w_ref = refs.pop(0) if n_cast_slabs else None
    o_ref = refs.pop(0)
    qi = pl.program_id(2)
    tq = q_ref.shape[0]
    qs = [q_ref[:, g * dk:(g + 1) * dk] for g in range(heads)]

    if n_cast_slabs:
        step = (pl.program_id(0) * pl.num_programs(1) + pl.program_id(1)) * pl.num_programs(2) + qi
        _cast_rider_step(step, n_cast_slabs, w_ref, refs[0])

    def step(g, state, j, masked):
        m, l, acc = state
        kv_rows = pl.ds(pl.multiple_of(j * tk, tk), tk)
        s = lax.dot_general(qs[g], k_ref[kv_rows, g * dk:(g + 1) * dk], NT_DIMS, preferred_element_type=F32)
        if has_bias:
            s = s - ck_ref[0, g, j]
        if masked:
            row = lax.broadcasted_iota(jnp.int32, s.shape, 0)
            col = lax.broadcasted_iota(jnp.int32, s.shape, 1)
            s = jnp.where(row >= col, s, -jnp.inf)
        m_new = jnp.maximum(m, jnp.max(s, axis=-1, keepdims=True))
        a = jnp.exp2(m - m_new)
        p = jnp.exp2(s - m_new)
        l = a * l + jnp.sum(p, axis=-1, keepdims=True)
        acc = a * acc + _dot(p.astype(BF16), v_ref[kv_rows, g * dv:(g + 1) * dv])
        return m_new, l, acc

    def body(j, carry):
        return tuple(step(g, carry[g], j, False) for g in range(heads))

    init = (jnp.full((tq, 1), -jnp.inf, F32), jnp.zeros((tq, 1), F32), jnp.zeros((tq, dv), F32))
    carry = lax.fori_loop(0, qi, body, (init,) * heads)
    for g in range(heads):
        _, l, acc = step(g, carry[g], qi, True)
        o_ref[:, g * dv:(g + 1) * dv] = (acc / l).astype(o_ref.dtype)


def _slab_rows(total_rows, max_slabs):
    rows = BF16_SUBLANES
    while total_rows % rows or total_rows // rows > max_slabs:
        rows += BF16_SUBLANES
    return rows


def _cast_rider(w, grid, step_of):
    flat = w.reshape(-1, w.shape[-1])
    rows = _slab_rows(flat.shape[0], math.prod(grid))
    n_slabs = flat.shape[0] // rows
    spec = pl.BlockSpec((rows, flat.shape[1]), lambda *g: (jnp.minimum(step_of(*g), n_slabs - 1), 0))
    return flat, spec, jax.ShapeDtypeStruct(flat.shape, BF16), n_slabs


def _cast_rider_step(step, n_slabs, w_ref, o_ref):
    @pl.when(step < n_slabs)
    def _():
        o_ref[...] = w_ref[...].astype(o_ref.dtype)


def _causal_attention(q_arr, k_arr, v_arr, q_col, k_col, v_col, dk, dv, n_heads, batch, seq,
                      ck=None, cast_rider=None, name="attn"):
    heads = ATTN_HEADS_PER_STEP
    assert n_heads % heads == 0 and q_col % heads == 0 and k_col % heads == 0 and v_col % heads == 0
    t = _tile(seq, ATTN_TILE)
    nq = seq // t
    grid = (batch, n_heads // heads, nq)
    in_specs = [pl.BlockSpec((t, heads * dk), lambda b, h, i: (b * nq + i, q_col // heads + h)),
                pl.BlockSpec((seq, heads * dk), lambda b, h, i: (b, k_col // heads + h)),
                pl.BlockSpec((seq, heads * dv), lambda b, h, i: (b, v_col // heads + h))]
    args = [q_arr, k_arr, v_arr]
    out_specs = [pl.BlockSpec((t, heads * dv), lambda b, h, i: (b * nq + i, h))]
    out_shape = [jax.ShapeDtypeStruct((batch * seq, n_heads * dv), BF16)]
    if ck is not None:
        args.append(ck.reshape(batch, n_heads, nq, 1, t))
        in_specs.append(pl.BlockSpec((1, heads, nq, 1, t), lambda b, h, i: (b, h, 0, 0, 0)))
    n_slabs = 0
    if cast_rider is not None:
        flat, slab, flat_bf, n_slabs = _cast_rider(cast_rider, grid, lambda b, h, i: (b * grid[1] + h) * nq + i)
        args.append(flat)
        in_specs.append(slab)
        out_specs.append(slab)
        out_shape.append(flat_bf)
    outs = pl.pallas_call(
        functools.partial(_flash_kernel, tk=t, dk=dk, dv=dv, heads=heads, has_bias=ck is not None,
                          n_cast_slabs=n_slabs),
        grid=grid, in_specs=in_specs, out_specs=out_specs, out_shape=out_shape,
        compiler_params=_cparams("parallel", "parallel", "arbitrary"), name=name)(*args)
    if cast_rider is None:
        return outs[0]
    return outs[0], outs[1].reshape(cast_rider.shape)


def _swa_kernel(sink_ref, q_ref, kp_ref, kc_ref, vp_ref, vc_ref, *rest, scale, n_q, n_kv, rider_slabs):
    n_riders = len(rider_slabs)
    o_ref = rest[n_riders]
    step = pl.program_id(0) * pl.num_programs(1) + pl.program_id(1)
    for n_slabs, w_ref, wo_ref in zip(rider_slabs, rest[:n_riders], rest[n_riders + 1:]):
        _cast_rider_step(step, n_slabs, w_ref, wo_ref)
    w = SWA_WINDOW
    qi = lax.broadcasted_iota(jnp.int32, (w, 2 * w), 0)
    ji = lax.broadcasted_iota(jnp.int32, (w, 2 * w), 1)
    has_prev = pl.program_id(1) > 0
    mask = (ji > qi) & (ji <= qi + w) & ((ji >= w) | has_prev)
    group = n_q // n_kv
    for kvh in range(n_kv):
        cols = slice(kvh * HEAD_DIM, (kvh + 1) * HEAD_DIM)
        k = jnp.concatenate([kp_ref[:, cols], kc_ref[:, cols]], axis=0)
        v = jnp.concatenate([vp_ref[:, cols], vc_ref[:, cols]], axis=0)
        for g in range(group):
            h = kvh * group + g
            hc = slice(h * HEAD_DIM, (h + 1) * HEAD_DIM)
            s = lax.dot_general(q_ref[:, hc], k, NT_DIMS, preferred_element_type=F32) * scale
            s = jnp.where(mask, s, -jnp.inf)
            sink = sink_ref[h]
            m = jnp.maximum(jnp.max(s, axis=-1, keepdims=True), sink)
            p = jnp.exp(s - m)
            denom = jnp.sum(p, axis=-1, keepdims=True) + jnp.exp(sink - m)
            o_ref[:, hc] = (_dot(p.astype(BF16), v) / denom).astype(o_ref.dtype)


def _swa_attention(qkv, q_off, k_off, v_off, sinks, n_q, n_kv, batch, seq, scale, cast_riders):
    w = SWA_WINDOW
    nb = seq // w
    grid = (batch, nb)
    qw, kw = n_q * HEAD_DIM, n_kv * HEAD_DIM
    assert q_off % qw == 0 and k_off % kw == 0 and v_off % kw == 0
    q_col, k_col, v_col = q_off // qw, k_off // kw, v_off // kw
    cur = lambda b, n: b * nb + n
    prev = lambda b, n: b * nb + jnp.maximum(n - 1, 0)
    riders = [_cast_rider(r, grid, cur) for r in cast_riders]
    outs = pl.pallas_call(
        functools.partial(_swa_kernel, scale=scale, n_q=n_q, n_kv=n_kv,
                          rider_slabs=tuple(r[3] for r in riders)), grid=grid,
        in_specs=[pl.BlockSpec(memory_space=pltpu.SMEM),
                  pl.BlockSpec((w, qw), lambda b, n: (cur(b, n), q_col)),
                  pl.BlockSpec((w, kw), lambda b, n: (prev(b, n), k_col)),
                  pl.BlockSpec((w, kw), lambda b, n: (cur(b, n), k_col)),
                  pl.BlockSpec((w, kw), lambda b, n: (prev(b, n), v_col)),
                  pl.BlockSpec((w, kw), lambda b, n: (cur(b, n), v_col))] + [r[1] for r in riders],
        out_specs=[pl.BlockSpec((w, qw), lambda b, n: (cur(b, n), 0))] + [r[1] for r in riders],
        out_shape=[jax.ShapeDtypeStruct((batch * seq, qw), BF16)] + [r[2] for r in riders],
        compiler_params=_cparams("arbitrary", "arbitrary"), name="swa_attn")(
            sinks.astype(F32), qkv, qkv, qkv, qkv, qkv, *[r[0] for r in riders])
    return outs[0], [o.reshape(r.shape) for o, r in zip(outs[1:], cast_riders)]


def _top2_route(x, w, n_experts):
    xh = x.astype(BF16)
    xl = (x - xh.astype(F32)).astype(BF16)
    wh = w.astype(BF16)
    wl = (w - wh.astype(F32)).astype(BF16)
    hi = _dot(xh, jnp.concatenate([wh, wl], axis=1))
    logits = hi[:, :LANE] + hi[:, LANE:] + _dot(xl, wh)
    lane = lax.broadcasted_iota(jnp.int32, logits.shape, 1)
    lane_f = lane.astype(F32)
    l1 = jnp.where(lane < n_experts, logits, -jnp.inf)
    v1 = jnp.max(l1, axis=-1, keepdims=True)
    i1 = jnp.min(jnp.where(l1 == v1, lane_f, float(LANE)), axis=-1, keepdims=True)
    l2 = jnp.where(lane_f == i1, -jnp.inf, l1)
    v2 = jnp.max(l2, axis=-1, keepdims=True)
    i2 = jnp.min(jnp.where(l2 == v2, lane_f, float(LANE)), axis=-1, keepdims=True)
    e2 = jnp.exp(v2 - v1)
    idx = jnp.where(lane == 0, i1, jnp.where(lane == 1, i2, 0.0)).astype(jnp.int32)
    gate = jnp.where(lane == 0, 1.0 / (1.0 + e2), jnp.where(lane == 1, e2 / (1.0 + e2), 0.0))
    return idx, gate


def _mm_res_ln_kernel(*refs, n_pairs, n_experts):
    a_refs, w_refs = refs[:n_pairs], refs[n_pairs:2 * n_pairs]
    if n_experts:
        res_ref, g_ref, b_ref, wr_ref, o_ref, idx_ref, gate_ref = refs[2 * n_pairs:]
    else:
        res_ref, g_ref, b_ref, o_ref = refs[2 * n_pairs:]
    tm = o_ref.shape[0]
    sub = _tile(tm, LN_ROW_GROUP)
    for r in range(tm // sub):
        rows = slice(r * sub, (r + 1) * sub)
        acc = _dot(a_refs[0][rows, :], w_refs[0][...])
        for a_ref, w_ref in zip(a_refs[1:], w_refs[1:]):
            acc = acc + _dot(a_ref[rows, :], w_ref[...])
        y = _layer_norm(DEEPNORM_ALPHA * res_ref[rows, :] + acc, g_ref[...], b_ref[...])
        o_ref[rows, :] = y
        if n_experts:
            idx_ref[rows, :], gate_ref[rows, :] = _top2_route(y, wr_ref[...], n_experts)


def _matmul_residual_ln(pairs, res, g, b, name, w_router=None):
    m, d = res.shape
    tm = _tile(m, OUT_PROJ_ROWS)
    row = lambda i: (i, 0)
    fixed = lambda i: (0, 0)
    resident = lambda shape: pl.BlockSpec(shape, fixed, pipeline_mode=pl.Buffered(1))
    in_specs = ([pl.BlockSpec((tm, a.shape[1]), row) for a, _ in pairs] + [resident(w.shape) for _, w in pairs]
                + [pl.BlockSpec((tm, d), row), pl.BlockSpec((1, d), fixed), pl.BlockSpec((1, d), fixed)])
    args = [a for a, _ in pairs] + [w for _, w in pairs] + [res, g.reshape(1, d), b.reshape(1, d)]
    out_specs = [pl.BlockSpec((tm, d), row)]
    out_shape = [jax.ShapeDtypeStruct((m, d), F32)]
    n_experts = 0
    if w_router is not None:
        n_experts = w_router.shape[1]
        in_specs.append(resident((d, LANE)))
        args.append(jnp.zeros((d, LANE), F32).at[:, :n_experts].set(w_router.astype(F32)))
        out_specs += [pl.BlockSpec((tm, LANE), row), pl.BlockSpec((tm, LANE), row)]
        out_shape += [jax.ShapeDtypeStruct((m, LANE), jnp.int32), jax.ShapeDtypeStruct((m, LANE), F32)]
    return pl.pallas_call(
        functools.partial(_mm_res_ln_kernel, n_pairs=len(pairs), n_experts=n_experts),
        grid=(m // tm,), in_specs=in_specs, out_specs=out_specs, out_shape=out_shape,
        compiler_params=_cparams("parallel"), name=name)(*args)


def _swiglu(x, wg, wu, wd):
    a = _silu(_dot(x, wg)) * _dot(x, wu)
    return _dot(a.astype(BF16), wd)


FFN_WEIGHT_SLOTS = 4
FFN_PREFETCH = 2


def _ffn_ring_ok(n_chunks):
    slot = lambda c: (c % n_chunks) % FFN_WEIGHT_SLOTS
    return n_chunks >= FFN_PREFETCH and all(
        slot(c + FFN_PREFETCH) not in {slot(c + d) for d in range(FFN_PREFETCH)} for c in range(n_chunks))


def _ffn_ln_kernel(h_ref, g_ref, b_ref, wg_hbm, wu_hbm, wd_hbm, o_ref, x_ref, wg_buf, wu_buf, wd_buf, sem, *,
                   n_chunks):
    i, n = pl.program_id(0), pl.num_programs(0)
    tf = wg_buf.shape[2]

    def copies(c):
        slot, cols = c % FFN_WEIGHT_SLOTS, pl.ds(c * tf, tf)
        return (pltpu.make_async_copy(wg_hbm.at[:, cols], wg_buf.at[slot], sem.at[0, slot]),
                pltpu.make_async_copy(wu_hbm.at[:, cols], wu_buf.at[slot], sem.at[1, slot]),
                pltpu.make_async_copy(wd_hbm.at[cols, :], wd_buf.at[slot], sem.at[2, slot]))

    def start(c):
        for copy in copies(c):
            copy.start(priority=N_DMA_PRIORITIES - 1)

    @pl.when(i == 0)
    def _():
        for c in range(FFN_PREFETCH):
            start(c)

    x_ref[...] = h_ref[...].astype(BF16)
    for c in range(n_chunks):
        for copy in copies(c):
            copy.wait()
        ahead = c + FFN_PREFETCH
        if ahead < n_chunks:
            start(ahead)
        else:
            pl.when(i + 1 < n)(functools.partial(start, ahead - n_chunks))
        slot = c % FFN_WEIGHT_SLOTS
        y = _swiglu(x_ref[...], wg_buf[slot], wu_buf[slot], wd_buf[slot])
        if c == 0:
            o_ref[...] = y
        else:
            o_ref[...] += y
    o_ref[...] = _layer_norm(DEEPNORM_ALPHA * h_ref[...] + o_ref[...], g_ref[...], b_ref[...])


def _swiglu_residual_ln(h, wg, wu, wd, g, b):
    m, d = h.shape
    f = wg.shape[1]
    tm, tf = _tile(m, FFN_ROWS), _tile(f, FFN_COLS)
    while not _ffn_ring_ok(f // tf):
        tf //= 2
    assert tf % LANE == 0 and f % tf == 0
    row = lambda i: (i, 0)
    fixed = lambda i: (0, 0)
    hbm = pl.BlockSpec(memory_space=pl.ANY)
    return pl.pallas_call(
        functools.partial(_ffn_ln_kernel, n_chunks=f // tf), grid=(m // tm,),
        in_specs=[pl.BlockSpec((tm, d), row), pl.BlockSpec((1, d), fixed), pl.BlockSpec((1, d), fixed),
                  hbm, hbm, hbm],
        out_specs=pl.BlockSpec((tm, d), row),
        out_shape=jax.ShapeDtypeStruct((m, d), F32),
        scratch_shapes=[pltpu.VMEM((tm, d), BF16),
                        pltpu.VMEM((FFN_WEIGHT_SLOTS, d, tf), BF16), pltpu.VMEM((FFN_WEIGHT_SLOTS, d, tf), BF16),
                        pltpu.VMEM((FFN_WEIGHT_SLOTS, tf, d), BF16),
                        pltpu.SemaphoreType.DMA((3, FFN_WEIGHT_SLOTS))],
        compiler_params=_cparams("arbitrary"), name="ffn_ln")(
            h, g.reshape(1, d), b.reshape(1, d), wg, wu, wd)


def _rms_norm(c, g):
    ms = jnp.mean(c * c, axis=-1, keepdims=True)
    return c * lax.rsqrt(ms + RMS_EPS) * g


def _rope64(pair, tab):
    w = pair * tab
    lane = lax.broadcasted_iota(jnp.int32, w.shape, 1)
    return jnp.where(lane < MLA_ROPE_DIM, w + pltpu.roll(w, MLA_ROPE_DIM, 1), 0.0)


def _mla_proj_kernel(x_ref, win_ref, gq_ref, gkv_ref, tab_ref, wq_ref, wkv_ref, q_ref, k_ref, v_ref, *,
                     q_rank, kv_rank, n_heads, q_scale):
    acc = _dot(x_ref[...].astype(BF16), win_ref[...])
    cq = _rms_norm(acc[:, :q_rank], gq_ref[...]).astype(BF16)
    ckv = _rms_norm(acc[:, q_rank:q_rank + kv_rank], gkv_ref[...]).astype(BF16)
    tab = tab_ref[...]
    k_pe = _rope64(acc[:, q_rank + kv_rank:], tab).astype(BF16)
    wkv_w = MLA_NOPE_DIM + MLA_V_DIM
    for h in range(n_heads):
        q0 = h * MLA_QK_PAD
        qh = _dot(cq, wq_ref[:, q0:q0 + MLA_QK_PAD]) * q_scale
        q_ref[:, q0:q0 + MLA_NOPE_DIM] = qh[:, :MLA_NOPE_DIM].astype(BF16)
        q_ref[:, q0 + MLA_NOPE_DIM:q0 + MLA_QK_PAD] = _rope64(qh[:, MLA_NOPE_DIM:], tab).astype(BF16)
        kvh = _dot(ckv, wkv_ref[:, h * wkv_w:(h + 1) * wkv_w])
        k_ref[:, q0:q0 + MLA_NOPE_DIM] = kvh[:, :MLA_NOPE_DIM].astype(BF16)
        k_ref[:, q0 + MLA_NOPE_DIM:q0 + MLA_QK_PAD] = k_pe
        v_ref[:, h * MLA_V_DIM:(h + 1) * MLA_V_DIM] = kvh[:, MLA_NOPE_DIM:].astype(BF16)


def _mla_projections(x, w_in, gq, gkv, tab, w_q, w_kv, q_rank, kv_rank, n_heads, q_scale):
    m, d = x.shape
    tm = _tile(m, MLA_PROJ_ROWS)
    row = lambda i: (i, 0)
    fixed = lambda i: (0, 0)
    resident = lambda shape: pl.BlockSpec(shape, fixed, pipeline_mode=pl.Buffered(1))
    qk_w, v_w = n_heads * MLA_QK_PAD, n_heads * MLA_V_DIM
    return pl.pallas_call(
        functools.partial(_mla_proj_kernel, q_rank=q_rank, kv_rank=kv_rank, n_heads=n_heads, q_scale=q_scale),
        grid=(m // tm,),
        in_specs=[pl.BlockSpec((tm, d), row), resident(w_in.shape),
                  pl.BlockSpec((1, q_rank), fixed), pl.BlockSpec((1, kv_rank), fixed),
                  pl.BlockSpec((tm, LANE), row), resident(w_q.shape), resident(w_kv.shape)],
        out_specs=[pl.BlockSpec((tm, qk_w), row), pl.BlockSpec((tm, qk_w), row), pl.BlockSpec((tm, v_w), row)],
        out_shape=[jax.ShapeDtypeStruct((m, qk_w), BF16), jax.ShapeDtypeStruct((m, qk_w), BF16),
                   jax.ShapeDtypeStruct((m, v_w), BF16)],
        compiler_params=_cparams("parallel"), name="mla_proj")(
            x, w_in, gq.reshape(1, q_rank).astype(F32), gkv.reshape(1, kv_rank).astype(F32), tab, w_q, w_kv)


def _row_copy(src_hbm, src_row, group_buf, sublane, sem):
    return pltpu.make_async_copy(src_hbm.at[pl.ds(src_row, 1), :], group_buf.at[pl.ds(sublane, 1), :], sem)


def _wait_group(src_hbm, group_buf, sem):
    pltpu.make_async_copy(src_hbm.at[pl.ds(0, group_buf.shape[0]), :], group_buf, sem).wait()


def _gather_rows_kernel(tok_ref, x_hbm, o_ref, buf, sem):
    i, n = pl.program_id(0), pl.num_programs(0)
    groups, sub, d = buf.shape[1:]
    rows = groups * sub

    def issue(step, slot):
        def start(g, _):
            for s in range(sub):
                copy = _row_copy(x_hbm, tok_ref[step * rows + g * sub + s], buf.at[slot, g], s, sem.at[slot])
                copy.start(priority=s % N_DMA_PRIORITIES)
            return 0
        lax.fori_loop(0, groups, start, 0)

    @pl.when(i == 0)
    def _():
        issue(0, 0)

    @pl.when(i + 1 < n)
    def _():
        issue(i + 1, (i + 1) % 2)

    slot = i % 2

    def wait(g, _):
        _wait_group(x_hbm, buf.at[slot, g], sem.at[slot])
        return 0
    lax.fori_loop(0, groups, wait, 0)
    o_ref[...] = buf[slot].reshape(rows, d).astype(o_ref.dtype)


def _gather_rows(h, row_token, rows_per_step):
    n_rows = row_token.shape[0]
    d = h.shape[1]
    return pl.pallas_call(
        _gather_rows_kernel,
        grid_spec=pltpu.PrefetchScalarGridSpec(
            num_scalar_prefetch=1, grid=(n_rows // rows_per_step,),
            in_specs=[pl.BlockSpec(memory_space=pl.ANY)],
            out_specs=pl.BlockSpec((rows_per_step, d), lambda i, tok: (i, 0)),
            scratch_shapes=[pltpu.VMEM((2, rows_per_step // F32_SUBLANES, F32_SUBLANES, d), F32),
                            pltpu.SemaphoreType.DMA((2,))]),
        out_shape=jax.ShapeDtypeStruct((n_rows, d), BF16),
        compiler_params=_cparams("arbitrary"), name="moe_gather")(row_token, h)


def _moe_ffn_kernel(te_ref, tr_ref, x_ref, wg_ref, wu_ref, wd_ref, o_ref):
    i, j = pl.program_id(0), pl.program_id(1)
    part = x_ref.shape[0] // MOE_TILE_PARTS
    parts_used = (tr_ref[i] + part - 1) // part

    @pl.when(j == 0)
    def _():
        o_ref[...] = jnp.zeros_like(o_ref)

    for used in range(1, MOE_TILE_PARTS + 1):
        @pl.when(parts_used == used)
        def _():
            rows = slice(0, used * part)
            o_ref[rows, :] += _swiglu(x_ref[rows, :], wg_ref[0], wu_ref[0], wd_ref[0].astype(BF16))


def _moe_ffn(xg, wg, wu, wd, tile_expert, tile_rows, tm):
    n_rows, d = xg.shape
    f = wg.shape[2]
    tf = _tile(f, MOE_COLS)
    nj = f // tf
    col = lambda i, j, te, tr: jnp.where(tr[i] > 0, j, nj - 1)
    return pl.pallas_call(
        _moe_ffn_kernel,
        grid_spec=pltpu.PrefetchScalarGridSpec(
            num_scalar_prefetch=2, grid=(n_rows // tm, nj),
            in_specs=[pl.BlockSpec((tm, d), lambda i, j, te, tr: (i, 0)),
                      pl.BlockSpec((1, d, tf), lambda i, j, te, tr: (te[i], 0, col(i, j, te, tr))),
                      pl.BlockSpec((1, d, tf), lambda i, j, te, tr: (te[i], 0, col(i, j, te, tr))),
                      pl.BlockSpec((1, tf, d), lambda i, j, te, tr: (te[i], col(i, j, te, tr), 0))],
            out_specs=pl.BlockSpec((tm, d), lambda i, j, te, tr: (i, 0))),
        out_shape=jax.ShapeDtypeStruct((n_rows, d), F32),
        compiler_params=_cparams("arbitrary", "arbitrary"), name="moe_ffn")(
            tile_expert, tile_rows, xg, wg, wu, wd)


def _combine_ln_kernel(pos_ref, h_ref, gate_ref, g_ref, b_ref, y_hbm, o_ref, buf, sem):
    groups, sub, d = buf.shape[2:]
    rows = groups * sub
    i, n = pl.program_id(0), pl.num_programs(0)

    def issue(step, slot):
        def start(g, _):
            for s in range(sub):
                for k in range(TOP_K):
                    src = pos_ref[TOP_K * (step * rows + g * sub + s) + k]
                    _row_copy(y_hbm, src, buf.at[slot, k, g], s, sem.at[slot, k]).start(
                        priority=s % N_DMA_PRIORITIES)
            return 0
        lax.fori_loop(0, groups, start, 0)

    @pl.when(i == 0)
    def _():
        issue(0, 0)

    @pl.when(i + 1 < n)
    def _():
        issue(i + 1, (i + 1) % 2)

    slot = i % 2

    def wait(g, _):
        for k in range(TOP_K):
            _wait_group(y_hbm, buf.at[slot, k, g], sem.at[slot, k])
        return 0
    lax.fori_loop(0, groups, wait, 0)
    gate = gate_ref[...]
    y = None
    for k in range(TOP_K):
        term = buf[slot, k].reshape(rows, d) * gate[:, k:k + 1]
        y = term if y is None else y + term
    o_ref[...] = _layer_norm(DEEPNORM_ALPHA * h_ref[...] + y, g_ref[...], b_ref[...])


def _moe_combine_ln(y_rows, pos, h, gates, g, b):
    m, d = h.shape
    tm = _tile(m, MOE_COMBINE_ROWS)
    row = lambda i, p: (i, 0)
    fixed = lambda i, p: (0, 0)
    return pl.pallas_call(
        _combine_ln_kernel,
        grid_spec=pltpu.PrefetchScalarGridSpec(
            num_scalar_prefetch=1, grid=(m // tm,),
            in_specs=[pl.BlockSpec((tm, d), row), pl.BlockSpec((tm, LANE), row),
                      pl.BlockSpec((1, d), fixed), pl.BlockSpec((1, d), fixed),
                      pl.BlockSpec(memory_space=pl.ANY)],
            out_specs=pl.BlockSpec((tm, d), row),
            scratch_shapes=[pltpu.VMEM((2, TOP_K, tm // F32_SUBLANES, F32_SUBLANES, d), F32),
                            pltpu.SemaphoreType.DMA((2, TOP_K))]),
        out_shape=jax.ShapeDtypeStruct((m, d), F32),
        compiler_params=_cparams("arbitrary"), name="moe_combine_ln")(
            pos, h, gates, g.reshape(1, d), b.reshape(1, d), y_rows)


def _moe_routing(top_idx, n_experts, tm):
    n_assign = top_idx.size
    expert_of = top_idx.reshape(n_assign)
    onehot = (expert_of[:, None] == jnp.arange(n_experts, dtype=jnp.int32)[None, :]).astype(jnp.int32)
    csum = jnp.cumsum(onehot, axis=0)
    rank = jnp.sum(onehot * (csum - 1), axis=1)
    counts = csum[-1]
    padded = ((counts + tm - 1) // tm) * tm
    padded_end = jnp.cumsum(padded)
    padded_start = padded_end - padded
    pos = (padded_start[expert_of] + rank).astype(jnp.int32)
    n_tiles = -(-n_assign // tm) + n_experts
    row_token = jnp.zeros((n_tiles * tm,), jnp.int32).at[pos].set(
        jnp.arange(n_assign, dtype=jnp.int32) // TOP_K, unique_indices=True, mode="promise_in_bounds")
    tile_start = jnp.arange(n_tiles, dtype=jnp.int32) * tm
    tile_used = tile_start < padded_end[-1]
    tile_expert = jnp.minimum(jnp.searchsorted(padded_end, tile_start, side='right'), n_experts - 1)
    tile_rows = jnp.clip(counts[tile_expert] - (tile_start - padded_start[tile_expert]), 0, tm)
    tile_rows = jnp.where(tile_used, tile_rows, 0).astype(jnp.int32)
    last_expert = jnp.max(jnp.where(tile_used, tile_expert, 0))
    tile_expert = jnp.where(tile_used, tile_expert, last_expert).astype(jnp.int32)
    return pos, row_token, tile_expert, tile_rows


def _rotate_half_cols(w):
    half = w.shape[-1] // 2
    return jnp.concatenate([-w[..., half:], w[..., :half]], axis=-1)


def kernel(x, positions, w_in0, b_forget, sinks_b, w_out0, ln0_mix_g, ln0_mix_b, w_ffn_gate, w_ffn_up, w_ffn_down, ln0_ffn_g, ln0_ffn_b, w_in1, q_norm_g, w_uq, kv_norm_g, w_ukv, w_out1, ln1_mix_g, ln1_mix_b, w_router, w_moe_gate, w_moe_up, w_moe_down, ln1_ffn_g, ln1_ffn_b):
    batch, seq, d = x.shape
    n = batch * seq
    fox_heads = b_forget.shape[0]
    fox_w = fox_heads * HEAD_DIM
    swa_q = sinks_b.shape[0]
    swa_qw = swa_q * HEAD_DIM
    swa_kvw = (w_in0.shape[1] - 3 * fox_w - fox_heads - swa_qw) // 2
    swa_kv = swa_kvw // HEAD_DIM
    q_rank, kv_rank = q_norm_g.shape[0], kv_norm_g.shape[0]
    mla_heads = w_uq.shape[1] // (MLA_NOPE_DIM + MLA_ROPE_DIM)
    n_experts = w_router.shape[1]

    xf = x.reshape(n, d)
    cos_a, sin_a, tab_m = _rope_tables(positions)

    c0 = 3 * fox_w
    c1 = c0 + fox_heads
    w_qkv = jnp.concatenate([w_in0[:, :c0], w_in0[:, c1:]], axis=1).astype(BF16)
    group_kinds = ([SCALED] * fox_heads + [PLAIN] * (2 * fox_heads)
                   + [ROTARY] * (swa_q + swa_kv) + [PLAIN] * swa_kv)
    qkv = _project(xf, w_qkv, group_kinds, HEAD_DIM ** -0.5 * LOG2_E, cos_a, sin_a, PROJ_ROWS, PROJ_COLS,
                   "proj0")
    cum_log_f = _fox_cum_log_forget(xf, w_in0[:, c0:c1], b_forget, batch, seq)
    o_a, w_moe_up_bf = _causal_attention(qkv, qkv, qkv, 0, fox_heads, 2 * fox_heads, HEAD_DIM, HEAD_DIM,
                                         fox_heads, batch, seq, ck=cum_log_f, cast_rider=w_moe_up,
                                         name="fox_attn")
    o_b, ffn_w_bf = _swa_attention(qkv, c0, c0 + swa_qw, c0 + swa_qw + swa_kvw, sinks_b, swa_q, swa_kv,
                                   batch, seq, HEAD_DIM ** -0.5,
                                   cast_riders=[w_ffn_gate, w_ffn_up, w_ffn_down])
    w_out0_bf = w_out0.astype(BF16)
    (h1,) = _matmul_residual_ln([(o_a, w_out0_bf[:fox_w]), (o_b, w_out0_bf[fox_w:])], xf,
                                ln0_mix_g, ln0_mix_b, "out0_ln")
    h2 = _swiglu_residual_ln(h1, *ffn_w_bf, ln0_ffn_g, ln0_ffn_b)

    r0 = q_rank + kv_rank
    w_in1_x = jnp.concatenate([w_in1, _rotate_half_cols(w_in1[:, r0:])], axis=1).astype(BF16)
    w_uq_h = w_uq.reshape(q_rank, mla_heads, MLA_NOPE_DIM + MLA_ROPE_DIM)
    w_uq_x = jnp.concatenate([w_uq_h, _rotate_half_cols(w_uq_h[..., MLA_NOPE_DIM:])], axis=-1)
    q_m, k_m, v_m = _mla_projections(
        h2, w_in1_x, q_norm_g, kv_norm_g, tab_m,
        w_uq_x.reshape(q_rank, mla_heads * MLA_QK_PAD).astype(BF16), w_ukv.astype(BF16),
        q_rank, kv_rank, mla_heads, (MLA_NOPE_DIM + MLA_ROPE_DIM) ** -0.5 * LOG2_E)
    o_m, w_moe_gate_bf = _causal_attention(q_m, k_m, v_m, 0, 0, 0, MLA_QK_PAD, MLA_V_DIM, mla_heads, batch,
                                           seq, cast_rider=w_moe_gate, name="mla_attn")
    h3, idx, gates = _matmul_residual_ln([(o_m, w_out1.astype(BF16))], h2, ln1_mix_g, ln1_mix_b,
                                         "out1_ln_route", w_router=w_router)

    moe_tm = _tile(n * TOP_K, MOE_ROWS)
    pos, row_token, tile_expert, tile_rows = _moe_routing(idx[:, :TOP_K], n_experts, moe_tm)
    xg = _gather_rows(h3, row_token, _tile(moe_tm, MOE_GATHER_ROWS))
    y_rows = _moe_ffn(xg, w_moe_gate_bf, w_moe_up_bf, w_moe_down, tile_expert, tile_rows, moe_tm)
    out = _moe_combine_ln(y_rows, pos, h3, gates, ln1_ffn_g, ln1_ffn_b)
    return out.reshape(batch, seq, d)
```
